```python
import math
import functools
import jax
import jax.numpy as jnp
from jax import lax
import numpy as np

D_MODEL = 2048
BATCH = 8
SEQ = 2048
DEPTH = 2
DEC_BATCH = 128
DEC_SEQ = 8
PAST_LEN = 2048
PAGE_SIZE = 128

N_BRANCH = 4
BRANCH_W = D_MODEL // 2
POOL_WINDOWS = (2, 4, 8, 16)
POOL_GROUP = BRANCH_W // len(POOL_WINDOWS)
POOL_BUF = max(POOL_WINDOWS) - 1
N_HEADS_B = 8
HEAD_DIM_B = BRANCH_W // N_HEADS_B
N_KV_B = 4
KV_REP = N_HEADS_B // N_KV_B
N_IDX_HEADS = 16
IDX_DIM = 64
TOPK_MAX = 256
ATT_BLOCK = 128
ROPE_THETA = 10000.0
ATT_SCALE = HEAD_DIM_B ** -0.5
IDX_SCALE = (IDX_DIM * N_IDX_HEADS) ** -0.5
LRU_W = BRANCH_W
LRU_BLOCKS = 4
LRU_BLOCK = LRU_W // LRU_BLOCKS
LRU_C = 8.0
CONV_W = 4
N_HEADS_D = 8
QK_DIM_D = BRANCH_W // (2 * N_HEADS_D)
V_DIM_D = BRANCH_W // N_HEADS_D
RET_CHUNK = 128
D_FF = 11 * D_MODEL // 4
FFN_CONV_W = 3
EPS = 1e-6

IN_SPLITS = (BRANCH_W,
             N_HEADS_B * HEAD_DIM_B, N_KV_B * HEAD_DIM_B, N_KV_B * HEAD_DIM_B,
             N_IDX_HEADS * IDX_DIM, IDX_DIM, N_IDX_HEADS,
             LRU_W, LRU_W,
             N_HEADS_D * QK_DIM_D, N_HEADS_D * QK_DIM_D, N_HEADS_D * V_DIM_D, BRANCH_W)
IN_OFFSETS = tuple(sum(IN_SPLITS[:i + 1]) for i in range(len(IN_SPLITS) - 1))
N_IN = sum(IN_SPLITS)

kernel_name = 'hybrid_pool_dsa_rglru_retnet_decode_step'

F32 = jnp.float32


def rms_norm(x, g):
    xf = x.astype(F32)
    y = xf * lax.rsqrt(jnp.mean(xf * xf, axis=-1, keepdims=True) + EPS)
    return (y * g.astype(F32)).astype(x.dtype)


def rope(x, pos):
    half = x.shape[-1] // 2
    inv = jnp.exp(-math.log(ROPE_THETA) * jnp.arange(half, dtype=F32) / half)
    ang = pos.astype(F32)[:, None] * inv[None, :]
    cos = jnp.cos(ang)[None, :, None, :]
    sin = jnp.sin(ang)[None, :, None, :]
    xf = x.astype(F32)
    x1, x2 = xf[..., :half], xf[..., half:]
    return jnp.concatenate([x1 * cos - x2 * sin, x2 * cos + x1 * sin], axis=-1).astype(x.dtype)


def causal_dwconv(x, prev, w, b):
    width = w.shape[0]
    t = x.shape[1]
    xp = jnp.concatenate([prev.astype(x.dtype), x], axis=1)
    y = b + sum(xp[:, j:j + t] * w[j] for j in range(width))
    return y, xp[:, t:]


def pool_mixer(u, prev, start, w_grp, scale):
    bn, t, c = u.shape
    ext = jnp.concatenate([prev.astype(F32), u.astype(F32)], axis=1)
    cs = jnp.concatenate([jnp.zeros((bn, 1, c), F32), jnp.cumsum(ext, axis=1)], axis=1)
    end = cs[:, POOL_BUF + 1:]
    pos = start + jnp.arange(t)
    outs = []
    for gi, w in enumerate(POOL_WINDOWS):
        sl = slice(gi * POOL_GROUP, (gi + 1) * POOL_GROUP)
        begin = cs[:, POOL_BUF + 1 - w:POOL_BUF + 1 - w + t, sl]
        cnt = jnp.minimum(w, pos + 1).astype(F32)[None, :, None]
        outs.append((end[..., sl] - begin) / cnt)
    mixed = (jnp.concatenate(outs, axis=-1) - ext[:, POOL_BUF:]).astype(u.dtype)
    mixed = mixed.reshape(bn, t, len(POOL_WINDOWS), POOL_GROUP)
    y = jnp.einsum('btgc,gcd->btgd', mixed, w_grp).reshape(bn, t, c) * scale
    return y, ext[:, t:].astype(u.dtype)


def rg_lru(ux, ug, conv_prev, h0, w_conv, b_conv, w_a, b_a, w_x, b_x, lam):
    xc, conv_new = causal_dwconv(ux, conv_prev, w_conv, b_conv)
    bn, t, c = xc.shape
    xf = xc.astype(F32)
    xb = xf.reshape(bn, t, LRU_BLOCKS, LRU_BLOCK)
    r = jax.nn.sigmoid(jnp.einsum('btnc,ncd->btnd', xb, w_a.astype(F32)).reshape(bn, t, c) + b_a.astype(F32))
    i = jax.nn.sigmoid(jnp.einsum('btnc,ncd->btnd', xb, w_x.astype(F32)).reshape(bn, t, c) + b_x.astype(F32))
    log_a = -LRU_C * r * jax.nn.softplus(-lam.astype(F32))
    a = jnp.exp(log_a)
    b = jnp.sqrt(-jnp.expm1(2.0 * log_a)) * (i * xf)
    b = b.at[:, 0].add(a[:, 0] * h0.astype(F32))

    def combine(lhs, rhs):
        a1, b1 = lhs
        a2, b2 = rhs
        return a1 * a2, a2 * b1 + b2

    _, h = lax.associative_scan(combine, (a, b), axis=1)
    y = jax.nn.gelu(ug.astype(F32)) * h
    return y.astype(ux.dtype), conv_new, h[:, -1].astype(ux.dtype)


def retention(q, k, v, s0):
    bn, t, nh, _ = q.shape
    dv = v.shape[-1]
    c = RET_CHUNK if t % RET_CHUNK == 0 else t
    n = t // c
    log_g = jnp.log1p(-jnp.exp2(-5.0 - jnp.arange(nh, dtype=F32)))
    idx = jnp.arange(c, dtype=F32)
    diff = idx[:, None] - idx[None, :]
    intra = jnp.where(diff >= 0, jnp.exp(log_g[:, None, None] * jnp.maximum(diff, 0.0)), 0.0)
    q_dec = jnp.exp(log_g[None, :] * (idx[:, None] + 1.0))
    k_dec = jnp.exp(log_g[None, :] * (c - 1.0 - idx[:, None]))
    c_dec = jnp.exp(log_g * c)

    def to_chunks(a):
        return a.astype(F32).reshape((bn, n, c) + a.shape[2:]).swapaxes(0, 1)

    def step(s, xs):
        qc, kc, vc = xs
        att = jnp.einsum('bqhd,bkhd->bhqk', qc, kc) * intra
        o = (jnp.einsum('bhqk,bkhe->bqhe', att, vc)
             + jnp.einsum('bqhd,bhde->bqhe', qc, s) * q_dec[None, :, :, None])
        s = s * c_dec[None, :, None, None] + jnp.einsum('bkhd,bkhe->bhde', kc * k_dec[None, :, :, None], vc)
        return s, o

    s, o = lax.scan(step, s0.astype(F32), (to_chunks(q), to_chunks(k), to_chunks(v)))
    return o.swapaxes(0, 1).reshape(bn, t, nh, dv), s


def head_norm(o, g):
    of = o.astype(F32)
    mu = jnp.mean(of, axis=-1, keepdims=True)
    var = jnp.mean(jnp.square(of - mu), axis=-1, keepdims=True)
    y = (of - mu) * lax.rsqrt(var + EPS)
    return y.reshape(o.shape[0], o.shape[1], -1) * g.astype(F32)


def dsa_prompt(q, k, v, iq, ik, iw):
    bn, s_len = q.shape[:2]
    topk = min(TOPK_MAX, s_len // 4)
    nb = s_len // ATT_BLOCK
    ikf = ik.astype(F32)
    key_pos = jnp.arange(s_len)

    def blk(xs):
        qb, iqb, iwb, t0 = xs
        tpos = t0 + jnp.arange(ATT_BLOCK)
        sc = jax.nn.relu(jnp.einsum('bqhd,bsd->bqhs', iqb.astype(F32), ikf))
        score = jnp.einsum('bqhs,bqh->bqs', sc, iwb.astype(F32)) * IDX_SCALE
        score = jnp.where(key_pos[None, None, :] <= tpos[None, :, None], score, -jnp.inf)
        _, sel = lax.top_k(score, topk)
        valid = sel <= tpos[None, :, None]
        ksel = jax.vmap(lambda kb, ib: kb[ib])(k, sel)
        vsel = jax.vmap(lambda vb, ib: vb[ib])(v, sel)
        qg = qb.reshape(bn, ATT_BLOCK, N_KV_B, KV_REP, HEAD_DIM_B)
        s = jnp.einsum('bqgrd,bqkgd->bqgrk', qg, ksel).astype(F32) * ATT_SCALE
        s = jnp.where(valid[:, :, None, None, :], s, -jnp.inf)
        p = jax.nn.softmax(s, axis=-1).astype(v.dtype)
        o = jnp.einsum('bqgrk,bqkgd->bqgrd', p, vsel)
        return o.reshape(bn, ATT_BLOCK, N_HEADS_B, HEAD_DIM_B)

    def split(a):
        return a.reshape((bn, nb, ATT_BLOCK) + a.shape[2:]).swapaxes(0, 1)

    out = lax.map(blk, (split(q), split(iq), split(iw), jnp.arange(nb) * ATT_BLOCK))
    return out.swapaxes(0, 1).reshape(bn, s_len, N_HEADS_B, HEAD_DIM_B)


def dsa_sample(q, k, v, iq, ik, iw, cache_k, cache_v, cache_idx_k, page_table, layer):
    bn, t = q.shape[:2]
    past = page_table.shape[1] * PAGE_SIZE
    l_keys = past + t
    topk = min(TOPK_MAX, l_keys // 4)
    ik_past = cache_idx_k[layer, page_table].reshape(bn, past, IDX_DIM)
    ik_all = jnp.concatenate([ik_past.astype(F32), ik.astype(F32)], axis=1)
    tpos = past + jnp.arange(t)
    sc = jax.nn.relu(jnp.einsum('bqhd,bsd->bqhs', iq.astype(F32), ik_all))
    score = jnp.einsum('bqhs,bqh->bqs', sc, iw.astype(F32)) * IDX_SCALE
    score = jnp.where(jnp.arange(l_keys)[None, None, :] <= tpos[None, :, None], score, -jnp.inf)
    _, sel = lax.top_k(score, topk)
    in_past = sel < past
    sp = jnp.minimum(sel, past - 1)
    phys_page = jnp.take_along_axis(page_table, (sp // PAGE_SIZE).reshape(bn, -1), axis=1).reshape(sp.shape)
    off = sp % PAGE_SIZE
    kp = cache_k[layer, phys_page, off].astype(q.dtype)
    vp = cache_v[layer, phys_page, off].astype(v.dtype)
    jt = jnp.arange(t)
    sel_new = jnp.any(sel[..., None] == (past + jt)[None, None, None, :], axis=2)
    sel_new = sel_new & (jt[None, None, :] <= jt[None, :, None])
    qg = q.reshape(bn, t, N_KV_B, KV_REP, HEAD_DIM_B)
    s_p = jnp.einsum('bqgrd,bqkgd->bqgrk', qg, kp).astype(F32) * ATT_SCALE
    s_p = jnp.where(in_past[:, :, None, None, :], s_p, -jnp.inf)
    s_n = jnp.einsum('bqgrd,bjgd->bqgrj', qg, k).astype(F32) * ATT_SCALE
    s_n = jnp.where(sel_new[:, :, None, None, :], s_n, -jnp.inf)
    p = jax.nn.softmax(jnp.concatenate([s_p, s_n], axis=-1), axis=-1).astype(v.dtype)
    o = (jnp.einsum('bqgrk,bqkgd->bqgrd', p[..., :topk], vp)
         + jnp.einsum('bqgrj,bjgd->bqgrd', p[..., topk:], v))
    return o.reshape(bn, t, N_HEADS_B, HEAD_DIM_B)


def trunk_layer(x, start, pool_prev, conv_prev, h_prev, s_prev, ffn_prev, attn, lp):
    bn, t, _ = x.shape
    pos = start + jnp.arange(t, dtype=jnp.int32)
    xn = rms_norm(x, lp['g_mix_pre'])
    u = xn @ lp['w_in']
    (a_in, b_q, b_k, b_v, i_q, i_k, i_w, c_x, c_g, d_q, d_k, d_v, d_g) = jnp.split(u, IN_OFFSETS, axis=-1)
    o_a, pool_new = pool_mixer(a_in, pool_prev, start, lp['w_pool'], lp['pool_scale'])
    q = rope(b_q.reshape(bn, t, N_HEADS_B, HEAD_DIM_B), pos)
    k = rope(b_k.reshape(bn, t, N_KV_B, HEAD_DIM_B), pos)
    v = b_v.reshape(bn, t, N_KV_B, HEAD_DIM_B)
    iq = rope(i_q.reshape(bn, t, N_IDX_HEADS, IDX_DIM), pos)
    ik = rope(i_k[:, :, None, :], pos)[:, :, 0]
    o_b = attn(q, k, v, iq, ik, i_w).reshape(bn, t, BRANCH_W)
    o_c, conv_new, h_new = rg_lru(c_x, c_g, conv_prev, h_prev, lp['w_conv_c'], lp['b_conv_c'],
                                  lp['w_rg_a'], lp['b_rg_a'], lp['w_rg_x'], lp['b_rg_x'], lp['lru_lambda'])
    rq = rope(d_q.reshape(bn, t, N_HEADS_D, QK_DIM_D), pos)
    rk = rope(d_k.reshape(bn, t, N_HEADS_D, QK_DIM_D), pos) * (QK_DIM_D ** -0.5)
    rv = d_v.reshape(bn, t, N_HEADS_D, V_DIM_D)
    o_r, s_new = retention(rq, rk, rv, s_prev)
    o_d = jax.nn.silu(d_g) * head_norm(o_r, lp['ret_gn']).astype(x.dtype)
    br = jnp.stack([o_a, o_b, o_c, o_d], axis=2)
    proj = jnp.einsum('btnc,ncd->btnd', br, lp['w_branch'])
    gates = jax.nn.sigmoid(xn @ lp['w_gate'] + lp['b_gate']).reshape(bn, t, N_BRANCH, D_MODEL)
    mix = jnp.sum(gates * proj, axis=2) @ lp['w_o']
    x = x + rms_norm(mix, lp['g_mix_post'])
    up, ffn_new = causal_dwconv(rms_norm(x, lp['g_ffn_pre']) @ lp['w_up'], ffn_prev,
                                lp['w_ffn_conv'], lp['b_ffn_conv'])
    gate, val = jnp.split(up, 2, axis=-1)
    x = x + rms_norm((jax.nn.gelu(gate) * val) @ lp['w_down'], lp['g_ffn_post'])
    return x, (k, v, ik, pool_new, conv_new, h_new, s_new.astype(x.dtype), ffn_new)


def setup_inputs(seed: int = 0) -> dict:
    key = jax.random.key(seed)
    keys = jax.random.split(key, 40)
    ctr = [0]

    def nxt():
        ctr[0] += 1
        return keys[ctr[0] - 1]

    def nrm(shape, scale):
        return jax.random.normal(nxt(), shape, jnp.float32) * scale

    def gain(width):
        return 1.0 + nrm((DEPTH, width), 0.02)

    n_pages = PAST_LEN // PAGE_SIZE
    n_used = DEC_BATCH * n_pages
    n_pool = n_used + (n_used + 3) // 4
    page_table = jax.random.permutation(nxt(), n_pool)[:n_used].reshape(DEC_BATCH, n_pages).astype(jnp.int32)
    a0 = jax.random.uniform(nxt(), (DEPTH, LRU_W), jnp.float32, 0.9, 0.999)
    sa = a0 ** (1.0 / LRU_C)
    lru_lambda = jnp.log(sa) - jnp.log1p(-sa)
    return {
        'x_prompt': nrm((BATCH, SEQ, D_MODEL), 1.0),
        'x_sample': nrm((DEC_BATCH, DEC_SEQ, D_MODEL), 1.0),
        'cache_k': nrm((DEPTH, n_pool, PAGE_SIZE, N_KV_B, HEAD_DIM_B), 1.0),
        'cache_v': nrm((DEPTH, n_pool, PAGE_SIZE, N_KV_B, HEAD_DIM_B), 1.0),
        'cache_idx_k': nrm((DEPTH, n_pool, PAGE_SIZE, IDX_DIM), 1.0),
        'state_pool': nrm((DEPTH, DEC_BATCH, POOL_BUF, BRANCH_W), 1.0),
        'state_conv': nrm((DEPTH, DEC_BATCH, CONV_W - 1, LRU_W), 1.0),
        'state_rglru': nrm((DEPTH, DEC_BATCH, LRU_W), 0.5),
        'state_ret': nrm((DEPTH, DEC_BATCH, N_HEADS_D, QK_DIM_D, V_DIM_D), 1.0),
        'state_ffn_conv': nrm((DEPTH, DEC_BATCH, FFN_CONV_W - 1, 2 * D_FF), 1.0),
        'page_table': page_table,
        'g_mix_pre': gain(D_MODEL),
        'w_in': nrm((DEPTH, D_MODEL, N_IN), D_MODEL ** -0.5),
        'w_pool': nrm((DEPTH, len(POOL_WINDOWS), POOL_GROUP, POOL_GROUP), POOL_GROUP ** -0.5),
        'pool_scale': gain(BRANCH_W),
        'w_conv_c': nrm((DEPTH, CONV_W, LRU_W), CONV_W ** -0.5),
        'b_conv_c': nrm((DEPTH, LRU_W), 0.01),
        'w_rg_a': nrm((DEPTH, LRU_BLOCKS, LRU_BLOCK, LRU_BLOCK), LRU_BLOCK ** -0.5),
        'b_rg_a': nrm((DEPTH, LRU_W), 0.01),
        'w_rg_x': nrm((DEPTH, LRU_BLOCKS, LRU_BLOCK, LRU_BLOCK), LRU_BLOCK ** -0.5),
        'b_rg_x': nrm((DEPTH, LRU_W), 0.01),
        'lru_lambda': lru_lambda,
        'ret_gn': gain(BRANCH_W),
        'w_branch': nrm((DEPTH, N_BRANCH, BRANCH_W, D_MODEL), BRANCH_W ** -0.5),
        'w_gate': nrm((DEPTH, D_MODEL, N_BRANCH * D_MODEL), D_MODEL ** -0.5),
        'b_gate': nrm((DEPTH, N_BRANCH * D_MODEL), 0.01),
        'w_o': nrm((DEPTH, D_MODEL, D_MODEL), D_MODEL ** -0.5),
        'g_mix_post': gain(D_MODEL),
        'g_ffn_pre': gain(D_MODEL),
        'w_up': nrm((DEPTH, D_MODEL, 2 * D_FF), D_MODEL ** -0.5),
        'w_ffn_conv': nrm((DEPTH, FFN_CONV_W, 2 * D_FF), FFN_CONV_W ** -0.5),
        'b_ffn_conv': nrm((DEPTH, 2 * D_FF), 0.01),
        'w_down': nrm((DEPTH, D_FF, D_MODEL), D_FF ** -0.5),
        'g_ffn_post': gain(D_MODEL),
    }


def reference(x_prompt, x_sample, cache_k, cache_v, cache_idx_k, state_pool, state_conv, state_rglru,
              state_ret, state_ffn_conv, page_table, g_mix_pre, w_in, w_pool, pool_scale, w_conv_c, b_conv_c,
              w_rg_a, b_rg_a, w_rg_x, b_rg_x, lru_lambda, ret_gn, w_branch, w_gate, b_gate, w_o, g_mix_post,
              g_ffn_pre, w_up, w_ffn_conv, b_ffn_conv, w_down, g_ffn_post):
    dt = x_prompt.dtype
    bp = x_prompt.shape[0]
    pool0 = jnp.zeros((bp, POOL_BUF, BRANCH_W), dt)
    conv0 = jnp.zeros((bp, CONV_W - 1, LRU_W), dt)
    h0 = jnp.zeros((bp, LRU_W), dt)
    s0 = jnp.zeros((bp, N_HEADS_D, QK_DIM_D, V_DIM_D), dt)
    ffn0 = jnp.zeros((bp, FFN_CONV_W - 1, 2 * D_FF), dt)
    xp, xs = x_prompt, x_sample
    st_p, st_s = [], []
    for l in range(DEPTH):
        lp = dict(g_mix_pre=g_mix_pre[l], w_in=w_in[l], w_pool=w_pool[l], pool_scale=pool_scale[l],
                  w_conv_c=w_conv_c[l], b_conv_c=b_conv_c[l], w_rg_a=w_rg_a[l], b_rg_a=b_rg_a[l],
                  w_rg_x=w_rg_x[l], b_rg_x=b_rg_x[l], lru_lambda=lru_lambda[l], ret_gn=ret_gn[l],
                  w_branch=w_branch[l], w_gate=w_gate[l], b_gate=b_gate[l], w_o=w_o[l],
                  g_mix_post=g_mix_post[l], g_ffn_pre=g_ffn_pre[l], w_up=w_up[l],
                  w_ffn_conv=w_ffn_conv[l], b_ffn_conv=b_ffn_conv[l], w_down=w_down[l],
                  g_ffn_post=g_ffn_post[l])
        xp, sp = trunk_layer(xp, 0, pool0, conv0, h0, s0, ffn0, dsa_prompt, lp)
        attn_s = functools.partial(dsa_sample, cache_k=cache_k, cache_v=cache_v, cache_idx_k=cache_idx_k,
                                   page_table=page_table, layer=l)
        xs, ss = trunk_layer(xs, PAST_LEN, state_pool[l], state_conv[l], state_rglru[l], state_ret[l],
                             state_ffn_conv[l], attn_s, lp)
        st_p.append(sp)
        st_s.append(ss)

    def stk(sts, i):
        return jnp.stack([s[i] for s in sts], axis=0)

    k_prompt, k_sample = stk(st_p, 0), stk(st_s, 0)
    v_prompt, v_sample = stk(st_p, 1), stk(st_s, 1)
    idx_k_prompt, idx_k_sample = stk(st_p, 2), stk(st_s, 2)
    pool_prompt, pool_sample = stk(st_p, 3), stk(st_s, 3)
    conv_prompt, conv_sample = stk(st_p, 4), stk(st_s, 4)
    rglru_prompt, rglru_sample = stk(st_p, 5), stk(st_s, 5)
    ret_prompt, ret_sample = stk(st_p, 6), stk(st_s, 6)
    ffn_prompt, ffn_sample = stk(st_p, 7), stk(st_s, 7)
    return (xp, xs, k_prompt, k_sample, v_prompt, v_sample, idx_k_prompt, idx_k_sample,
            pool_prompt, pool_sample, conv_prompt, conv_sample, rglru_prompt, rglru_sample,
            ret_prompt, ret_sample, ffn_prompt, ffn_sample)
```

```python
import functools
import math

import jax
import jax.numpy as jnp
from jax import lax
from jax.experimental import pallas as pl
from jax.experimental.pallas import tpu as pltpu

F32 = jnp.float32
BF16 = jnp.bfloat16
I32 = jnp.int32

SUBLANES = 8
LANES = 128
VMEM_LIMIT_BYTES = 56 * 1024 * 1024

D_MODEL = 2048
BRANCH_W = D_MODEL // 2
N_BRANCH = 4
POOL_WINDOWS = (2, 4, 8, 16)
POOL_GROUP = BRANCH_W // len(POOL_WINDOWS)
POOL_BUF = max(POOL_WINDOWS) - 1
N_HEADS_B = 8
HEAD_DIM_B = BRANCH_W // N_HEADS_B
N_KV_B = 4
KV_REP = N_HEADS_B // N_KV_B
N_IDX_HEADS = 16
IDX_DIM = 64
TOPK_MAX = 256
ATT_BLOCK = 128
ROPE_THETA = 10000.0
ATT_SCALE = HEAD_DIM_B ** -0.5
IDX_SCALE = (IDX_DIM * N_IDX_HEADS) ** -0.5
LRU_W = BRANCH_W
LRU_BLOCKS = 4
LRU_BLOCK = LRU_W // LRU_BLOCKS
LRU_C = 8.0
CONV_W = 4
N_HEADS_D = 8
QK_DIM_D = BRANCH_W // (2 * N_HEADS_D)
V_DIM_D = BRANCH_W // N_HEADS_D
RET_CHUNK = 128
D_FF = 11 * D_MODEL // 4
FFN_CONV_W = 3
EPS = 1e-6
PAGE_SIZE = 128

N_MAIN = 8192
N_IDX = 1152
IDX4 = 4 * IDX_DIM

LOG_G = tuple(math.log1p(-(2.0 ** (-5.0 - h))) for h in range(N_HEADS_D))
INT_MIN = -2 ** 31


def _round_up(n, m):
    return (n + m - 1) // m * m


def _cparams(*sem):
    return pltpu.CompilerParams(dimension_semantics=sem, vmem_limit_bytes=VMEM_LIMIT_BYTES)


def _dot(a, b):
    return jnp.dot(a, b, preferred_element_type=F32)


def _dot_nt(a, b):
    return lax.dot_general(a, b, (((1,), (1,)), ((), ())), preferred_element_type=F32)


def _dot_tn(a, b):
    return lax.dot_general(a, b, (((0,), (0,)), ((), ())), preferred_element_type=F32)


def _rms(x, g):
    return x * lax.rsqrt(jnp.mean(x * x, axis=-1, keepdims=True) + EPS) * g


def _gelu_tanh(x):
    return x * (0.5 * (1.0 + jnp.tanh(0.7978845608028654 * (x + 0.044715 * (x * x * x)))))


def _split_bf16(x):
    hi = x.astype(BF16)
    lo = (x - hi.astype(F32)).astype(BF16)
    return hi, lo


def _inproj_kernel(x_ref, g_ref, w_ref, u_ref, xn_ref):
    @pl.when(pl.program_id(1) == 0)
    def _():
        xn_ref[...] = _rms(x_ref[...], g_ref[...]).astype(BF16)

    u_ref[...] = _dot(xn_ref[...], w_ref[...])


def _inproj(x, g, w, tm, tn):
    t, d = x.shape
    n = w.shape[1]
    return pl.pallas_call(
        _inproj_kernel,
        grid=(t // tm, n // tn),
        in_specs=[pl.BlockSpec((tm, d), lambda i, j: (i, 0)),
                  pl.BlockSpec((1, d), lambda i, j: (0, 0)),
                  pl.BlockSpec((d, tn), lambda i, j: (0, j))],
        out_specs=[pl.BlockSpec((tm, tn), lambda i, j: (i, j)),
                   pl.BlockSpec((tm, d), lambda i, j: (i, 0))],
        out_shape=[jax.ShapeDtypeStruct((t, n), F32), jax.ShapeDtypeStruct((t, d), BF16)],
        compiler_params=_cparams("parallel", "arbitrary"),
    )(x, g, w)


def _inproj_split_kernel(x_ref, g_ref, wh_ref, wl_ref, u_ref, xh_s, xl_s):
    @pl.when(pl.program_id(1) == 0)
    def _():
        hi, lo = _split_bf16(_rms(x_ref[...], g_ref[...]))
        xh_s[...] = hi
        xl_s[...] = lo

    xh = xh_s[...]
    u_ref[...] = _dot(xh, wh_ref[...]) + (_dot(xh, wl_ref[...]) + _dot(xl_s[...], wh_ref[...]))


def _inproj_split(x, g, wh, wl, tm, tn):
    t, d = x.shape
    n = wh.shape[1]
    return pl.pallas_call(
        _inproj_split_kernel,
        grid=(t // tm, n // tn),
        in_specs=[pl.BlockSpec((tm, d), lambda i, j: (i, 0)),
                  pl.BlockSpec((1, d), lambda i, j: (0, 0)),
                  pl.BlockSpec((d, tn), lambda i, j: (0, j)),
                  pl.BlockSpec((d, tn), lambda i, j: (0, j))],
        out_specs=pl.BlockSpec((tm, tn), lambda i, j: (i, j)),
        out_shape=jax.ShapeDtypeStruct((t, n), F32),
        scratch_shapes=[pltpu.VMEM((tm, d), BF16), pltpu.VMEM((tm, d), BF16)],
        compiler_params=_cparams("parallel", "arbitrary"),
    )(x, g, wh, wl)


def _rope128(xs, cos, sin):
    return xs * cos + pltpu.roll(xs, HEAD_DIM_B // 2, axis=1) * sin


def _rope64(xs, cos, sin, first_half):
    rot = jnp.where(first_half, pltpu.roll(xs, LANES - IDX_DIM // 2, axis=1), pltpu.roll(xs, IDX_DIM // 2, axis=1))
    return xs * cos + rot * sin


def _rope_kernel(bq_ref, bk_ref, bv_ref, dq_ref, dk_ref, ui_ref, c128_ref, s128_ref, c64_ref, s64_ref,
                 q_ref, k_ref, kb_ref, vb_ref, iq4_ref, ik4_ref, ikw_ref, rq_ref, rk_ref):
    rows = bq_ref.shape[0]
    c128, s128, c64, s64 = c128_ref[...], s128_ref[...], c64_ref[...], s64_ref[...]
    lane = lax.broadcasted_iota(I32, (rows, LANES), 1)
    first_half = (lane & (IDX_DIM // 2)) == 0
    low = lane < IDX_DIM
    for c in range(N_HEADS_B):
        sl = slice(c * LANES, (c + 1) * LANES)
        q_ref[:, sl] = (_rope128(bq_ref[:, sl], c128, s128) * ATT_SCALE).astype(BF16)
    for c in range(N_KV_B):
        sl = slice(c * LANES, (c + 1) * LANES)
        kr = _rope128(bk_ref[:, sl], c128, s128)
        k_ref[:, sl] = kr
        kb_ref[:, sl] = kr.astype(BF16)
    vb_ref[...] = bv_ref[...].astype(BF16)
    for c in range(N_HEADS_D * QK_DIM_D // LANES):
        sl = slice(c * LANES, (c + 1) * LANES)
        rq_ref[:, sl] = _rope64(dq_ref[:, sl], c64, s64, first_half)
        rk_ref[:, sl] = _rope64(dk_ref[:, sl], c64, s64, first_half) * (QK_DIM_D ** -0.5)
    for c in range(N_IDX_HEADS // 2):
        y = _rope64(ui_ref[:, c * LANES:(c + 1) * LANES], c64, s64, first_half)
        yr = pltpu.roll(y, IDX_DIM, axis=1)
        for hh, dup in enumerate((jnp.where(low, y, yr), jnp.where(low, yr, y))):
            hi, lo = _split_bf16(dup)
            base = (2 * c + hh) * IDX4
            iq4_ref[:, base:base + LANES] = hi
            iq4_ref[:, base + LANES:base + 2 * LANES] = lo
    raw = ui_ref[:, N_IDX_HEADS * IDX_DIM:N_IDX_HEADS * IDX_DIM + LANES]
    y = _rope64(raw, c64, s64, first_half)
    ikw_ref[...] = jnp.where(low, y, raw)
    dup = jnp.where(low, y, pltpu.roll(y, IDX_DIM, axis=1))
    hi = dup.astype(BF16).astype(F32)
    hilo = jnp.where(low, hi, dup - hi).astype(BF16)
    ik4_ref[:, 0:LANES] = hilo
    ik4_ref[:, LANES:2 * LANES] = hilo


def _rope_prep(u_main, u_idx, tabs, tr):
    t = u_main.shape[0]
    nt = tabs[0].shape[0] // tr
    row = lambda w, c: pl.BlockSpec((tr, w), lambda i: (i, c))
    tab = pl.BlockSpec((tr, LANES), lambda i: (i % nt, 0))
    outs = [(BRANCH_W, BF16), (N_KV_B * HEAD_DIM_B, F32), (N_KV_B * HEAD_DIM_B, BF16), (N_KV_B * HEAD_DIM_B, BF16),
            (N_IDX_HEADS * IDX4, BF16), (IDX4, BF16), (LANES, F32), (N_HEADS_D * QK_DIM_D, F32),
            (N_HEADS_D * QK_DIM_D, F32)]
    return pl.pallas_call(
        _rope_kernel,
        grid=(t // tr,),
        in_specs=[row(1024, 1), row(512, 4), row(512, 5), row(512, 10), row(512, 11), row(N_IDX, 0),
                  tab, tab, tab, tab],
        out_specs=[row(w, 0) for w, _ in outs],
        out_shape=[jax.ShapeDtypeStruct((t, w), dt) for w, dt in outs],
        compiler_params=_cparams("parallel"),
    )(u_main, u_main, u_main, u_main, u_main, u_idx, *tabs)


def _rope_tables(pos):
    def tab(half, reps):
        inv = jnp.exp(-math.log(ROPE_THETA) * jnp.arange(half, dtype=F32) / half)
        ang = pos[:, None] * inv[None, :]
        cos, sin = jnp.cos(ang), jnp.sin(ang)
        return jnp.tile(jnp.concatenate([cos, cos], 1), (1, reps)), jnp.tile(jnp.concatenate([-sin, sin], 1), (1, reps))

    c128, s128 = tab(HEAD_DIM_B // 2, 1)
    c64, s64 = tab(IDX_DIM // 2, 2)
    return c128, s128, c64, s64


def _pool_kernel(a_ref, prev_ref, w_ref, sc_ref, o_ref, ext_s, *, S, Tc, start, nch):
    ch = pl.program_id(1)
    R = Tc * S
    PS = POOL_BUF * S
    OFF = _round_up(PS, SUBLANES)

    @pl.when(ch == 0)
    def _():
        ext_s[OFF - PS:OFF, :] = prev_ref[0]

    x = a_ref[...]
    ext_s[OFF:OFF + R, :] = x
    t_loc = lax.broadcasted_iota(I32, (R, 1), 0) // S if S > 1 else lax.broadcasted_iota(I32, (R, 1), 0)
    pos1 = start + ch * Tc + t_loc + 1
    for gi, w in enumerate(POOL_WINDOWS):
        sl = slice(gi * POOL_GROUP, (gi + 1) * POOL_GROUP)
        xs = x[:, sl]
        acc = xs
        for j in range(1, w):
            acc = acc + ext_s[OFF - j * S:OFF - j * S + R, sl]
        cnt = jnp.minimum(w, pos1).astype(F32)
        mixed = acc / cnt - xs
        y = _dot(mixed.astype(BF16), w_ref[gi])
        o_ref[:, sl] = (y * sc_ref[:, sl]).astype(o_ref.dtype)
    if nch > 1:
        ext_s[OFF - PS:OFF, :] = ext_s[OFF + R - PS:OFF + R, :]


def _pool(u_main, prev, w_pool, scale, *, S, Tc, start):
    t = u_main.shape[0]
    R = Tc * S
    G = prev.shape[0]
    nch = t // (G * R)
    PS = POOL_BUF * S
    C = BRANCH_W
    return pl.pallas_call(
        functools.partial(_pool_kernel, S=S, Tc=Tc, start=start, nch=nch),
        grid=(G, nch),
        in_specs=[pl.BlockSpec((R, C), lambda g, c: (g * nch + c, 0)),
                  pl.BlockSpec((1, PS, C), lambda g, c: (g, 0, 0)),
                  pl.BlockSpec((len(POOL_WINDOWS), POOL_GROUP, POOL_GROUP), lambda g, c: (0, 0, 0)),
                  pl.BlockSpec((1, C), lambda g, c: (0, 0))],
        out_specs=pl.BlockSpec((R, C), lambda g, c: (g * nch + c, 0)),
        out_shape=jax.ShapeDtypeStruct((t, C), BF16),
        scratch_shapes=[pltpu.VMEM((_round_up(PS, SUBLANES) + R, C), F32)],
        compiler_params=_cparams("parallel", "arbitrary"),
    )(u_main, prev, w_pool, scale)


def _rglru_kernel(cx_ref, cg_ref, prev_ref, h0_ref, wc_ref, bc_ref, wa_ref, ba_ref, wx_ref, bx_ref, lam_ref,
                  o_ref, hl_ref, ext_s, a_s, b_s, h_s, *, S, Tc, nch):
    ch = pl.program_id(1)
    R = Tc * S
    PS = (CONV_W - 1) * S
    OFF = _round_up(PS, SUBLANES)

    @pl.when(ch == 0)
    def _():
        ext_s[OFF - PS:OFF, :] = prev_ref[0]
        h_s[...] = h0_ref[0]

    x = cx_ref[...]
    ext_s[OFF:OFF + R, :] = x
    xc = bc_ref[...] + x * wc_ref[CONV_W - 1:CONV_W, :]
    for j in range(CONV_W - 1):
        k = CONV_W - 1 - j
        xc = xc + ext_s[OFF - k * S:OFF - k * S + R, :] * wc_ref[j:j + 1, :]
    xb = xc.astype(BF16)
    for n in range(LRU_BLOCKS):
        sl = slice(n * LRU_BLOCK, (n + 1) * LRU_BLOCK)
        r = jax.nn.sigmoid(_dot(xb[:, sl], wa_ref[n]) + ba_ref[:, sl])
        i = jax.nn.sigmoid(_dot(xb[:, sl], wx_ref[n]) + bx_ref[:, sl])
        lam = lam_ref[:, sl]
        softplus_neg = jnp.maximum(-lam, 0.0) + jnp.log1p(jnp.exp(-jnp.abs(lam)))
        log_a = (-LRU_C) * r * softplus_neg
        a_s[:, sl] = jnp.exp(log_a)
        th = jnp.tanh(log_a)
        b_s[:, sl] = jnp.sqrt(-2.0 * th / (1.0 - th)) * (i * xc[:, sl])

    def step(t, h):
        off = t * S
        if S % SUBLANES == 0:
            off = pl.multiple_of(off, S)
        h = a_s[pl.ds(off, S), :] * h + b_s[pl.ds(off, S), :]
        b_s[pl.ds(off, S), :] = h
        return h

    h = lax.fori_loop(0, Tc, step, h_s[...])
    h_s[...] = h
    hl_ref[0] = h
    o_ref[...] = (_gelu_tanh(cg_ref[...]) * b_s[...]).astype(o_ref.dtype)
    if nch > 1:
        ext_s[OFF - PS:OFF, :] = ext_s[OFF + R - PS:OFF + R, :]


def _rglru(u_main, prev, h0, lp, *, S, Tc):
    t = u_main.shape[0]
    R = Tc * S
    G = prev.shape[0]
    nch = t // (G * R)
    PS = (CONV_W - 1) * S
    C = LRU_W
    vec = pl.BlockSpec((1, C), lambda g, c: (0, 0))
    blk = pl.BlockSpec((LRU_BLOCKS, LRU_BLOCK, LRU_BLOCK), lambda g, c: (0, 0, 0))
    return pl.pallas_call(
        functools.partial(_rglru_kernel, S=S, Tc=Tc, nch=nch),
        grid=(G, nch),
        in_specs=[pl.BlockSpec((R, C), lambda g, c: (g * nch + c, 3)),
                  pl.BlockSpec((R, C), lambda g, c: (g * nch + c, 4)),
                  pl.BlockSpec((1, PS, C), lambda g, c: (g, 0, 0)),
                  pl.BlockSpec((1, S, C), lambda g, c: (g, 0, 0)),
                  pl.BlockSpec((CONV_W, C), lambda g, c: (0, 0)), vec, blk, vec, blk, vec, vec],
        out_specs=[pl.BlockSpec((R, C), lambda g, c: (g * nch + c, 0)),
                   pl.BlockSpec((1, S, C), lambda g, c: (g, 0, 0))],
        out_shape=[jax.ShapeDtypeStruct((t, C), BF16), jax.ShapeDtypeStruct((G, S, C), F32)],
        scratch_shapes=[pltpu.VMEM((_round_up(PS, SUBLANES) + R, C), F32), pltpu.VMEM((R, C), F32),
                        pltpu.VMEM((R, C), F32), pltpu.VMEM((S, C), F32)],
        compiler_params=_cparams("parallel", "arbitrary"),
    )(u_main, u_main, prev, h0, lp["w_conv_c"], lp["b_conv_c"], lp["w_rg_a"], lp["b_rg_a"], lp["w_rg_x"],
      lp["b_rg_x"], lp["lru_lambda"])


def _ret_kernel(rq_ref, rk_ref, rv_ref, dg_ref, s0_ref, gn_ref, o_ref, sn_ref, s_s, *, c):
    @pl.when(pl.program_id(1) == 0)
    def _():
        s_s[...] = s0_ref[0]

    ii = lax.broadcasted_iota(I32, (c, c), 0)
    jj = lax.broadcasted_iota(I32, (c, c), 1)
    dif = (ii - jj).astype(F32)
    tpos = lax.broadcasted_iota(I32, (c, 1), 0).astype(F32)
    mm = BF16 if c % 16 == 0 else F32
    for h in range(N_HEADS_D):
        lg = LOG_G[h]
        qs = slice(h * QK_DIM_D, (h + 1) * QK_DIM_D)
        vs = slice(h * V_DIM_D, (h + 1) * V_DIM_D)
        q = rq_ref[:, qs].astype(mm)
        k = rk_ref[:, qs]
        v = rv_ref[:, vs].astype(mm)
        intra = jnp.where(dif >= 0.0, jnp.exp(lg * jnp.maximum(dif, 0.0)), 0.0)
        att = _dot_nt(q, k.astype(mm)) * intra
        s = s_s[h]
        o = _dot(att.astype(mm), v) + _dot(q, s.astype(mm)) * jnp.exp(lg * (tpos + 1.0))
        kd = (k * jnp.exp(lg * (c - 1.0 - tpos))).astype(mm)
        s_s[h] = s * math.exp(lg * c) + _dot_tn(kd, v)
        mu = jnp.mean(o, axis=-1, keepdims=True)
        var = jnp.mean(jnp.square(o - mu), axis=-1, keepdims=True)
        y = (o - mu) * lax.rsqrt(var + EPS) * gn_ref[:, vs]
        dg = dg_ref[:, vs]
        o_ref[:, vs] = (dg * jax.nn.sigmoid(dg) * y).astype(o_ref.dtype)
    sn_ref[0] = s_s[...]


def _retention(rq, rk, rv, dg, s0, gn, *, c, v_col, g_col):
    t = rq.shape[0]
    B = s0.shape[0]
    nch = t // (B * c)
    W = N_HEADS_D * QK_DIM_D
    return pl.pallas_call(
        functools.partial(_ret_kernel, c=c),
        grid=(B, nch),
        in_specs=[pl.BlockSpec((c, W), lambda b, i: (b * nch + i, 0)),
                  pl.BlockSpec((c, W), lambda b, i: (b * nch + i, 0)),
                  pl.BlockSpec((c, BRANCH_W), lambda b, i: (b * nch + i, v_col)),
                  pl.BlockSpec((c, BRANCH_W), lambda b, i: (b * nch + i, g_col)),
                  pl.BlockSpec((1, N_HEADS_D, QK_DIM_D, V_DIM_D), lambda b, i: (b, 0, 0, 0)),
                  pl.BlockSpec((1, BRANCH_W), lambda b, i: (0, 0))],
        out_specs=[pl.BlockSpec((c, BRANCH_W), lambda b, i: (b * nch + i, 0)),
                   pl.BlockSpec((1, N_HEADS_D, QK_DIM_D, V_DIM_D), lambda b, i: (b, 0, 0, 0))],
        out_shape=[jax.ShapeDtypeStruct((t, BRANCH_W), BF16), jax.ShapeDtypeStruct(s0.shape, F32)],
        scratch_shapes=[pltpu.VMEM((N_HEADS_D, QK_DIM_D, V_DIM_D), F32)],
        compiler_params=_cparams("parallel", "arbitrary"),
    )(rq, rk, rv, dg, s0, gn)


def _sortable(x):
    b = lax.bitcast_convert_type(x, I32)
    return b ^ ((b >> 31) & 0x7FFFFFFF)


def _count(m):
    return jnp.sum(jnp.where(m, 1.0, 0.0), axis=1, keepdims=True)


def _topk_mask(keys, valid, idx, k):
    m, n = keys.shape
    kf = float(k)
    t0 = jnp.where(_count(keys >= 0) >= kf, 0, INT_MIN).astype(I32)

    def bit_step(i, t):
        cand = t | jnp.left_shift(jnp.int32(1), 30 - i)
        return jnp.where(_count(keys >= cand) >= kf, cand, t)

    thr = lax.fori_loop(0, 31, bit_step, t0)
    gt = keys > thr
    eq = (keys == thr) & valid
    need = kf - _count(gt)
    excess = jnp.max(_count(eq) - need) > 0.0
    nbits = max(1, (n - 1).bit_length())

    def first_ties():
        def idx_step(i, j):
            cand = j | jnp.left_shift(jnp.int32(1), nbits - 1 - i)
            return jnp.where(_count(eq & (idx < cand)) <= need - 1.0, cand, j)

        return lax.fori_loop(0, nbits, idx_step, jnp.zeros((m, 1), I32))

    jthr = lax.cond(excess, first_ties, lambda: jnp.full((m, 1), 2 ** nbits, I32))
    return (gt & valid) | (eq & (idx <= jthr))


def _masked_attention(s, mask, v):
    s = jnp.where(mask, s, -jnp.inf)
    p = jnp.exp(s - jnp.max(s, axis=1, keepdims=True))
    return _dot(p.astype(BF16), v) / jnp.sum(p, axis=1, keepdims=True)


def _dsa_prompt_kernel(q_ref, kb_ref, vb_ref, iq4_ref, ik4_ref, iw_ref, o_ref):
    nq = q_ref.shape[0]
    nk = kb_ref.shape[0]
    t0 = pl.program_id(1) * nq
    ik4 = ik4_ref[...]
    score = jnp.zeros((nq, nk), F32)
    for h in range(N_IDX_HEADS):
        s = _dot_nt(iq4_ref[:, h * IDX4:(h + 1) * IDX4], ik4)
        score = score + jnp.maximum(s, 0.0) * iw_ref[:, IDX_DIM + h:IDX_DIM + h + 1]
    score = score * IDX_SCALE
    kpos = lax.broadcasted_iota(I32, (nq, nk), 1)
    valid = kpos <= t0 + lax.broadcasted_iota(I32, (nq, nk), 0)
    keys = _sortable(jnp.where(valid, score, -jnp.inf))
    mask = _topk_mask(keys, valid, kpos, min(TOPK_MAX, nk // 4))
    for g in range(N_KV_B):
        gs = slice(g * HEAD_DIM_B, (g + 1) * HEAD_DIM_B)
        kg = kb_ref[:, gs]
        vg = vb_ref[:, gs]
        for r in range(KV_REP):
            hs = slice((g * KV_REP + r) * HEAD_DIM_B, (g * KV_REP + r + 1) * HEAD_DIM_B)
            o_ref[:, hs] = _masked_attention(_dot_nt(q_ref[:, hs], kg), mask, vg).astype(o_ref.dtype)


def _dsa_prompt(q, kb, vb, iq4, ik4, ikw, *, n_seq):
    t = q.shape[0]
    s_len = t // n_seq
    nb = s_len // ATT_BLOCK
    qrow = lambda w: pl.BlockSpec((ATT_BLOCK, w), lambda b, j: (b * nb + j, 0))
    seq = lambda w: pl.BlockSpec((s_len, w), lambda b, j: (b, 0))
    return pl.pallas_call(
        _dsa_prompt_kernel,
        grid=(n_seq, nb),
        in_specs=[qrow(BRANCH_W), seq(N_KV_B * HEAD_DIM_B), seq(N_KV_B * HEAD_DIM_B), qrow(N_IDX_HEADS * IDX4),
                  seq(IDX4), qrow(LANES)],
        out_specs=qrow(BRANCH_W),
        out_shape=jax.ShapeDtypeStruct((t, BRANCH_W), BF16),
        compiler_params=_cparams("parallel", "arbitrary"),
    )(q, kb, vb, iq4, ik4, ikw)


def _dsa_sample_kernel(pt_ref, qg_ref, iq4_ref, w_ref, kn_ref, vn_ref, ikn_ref, *rest, n_pages, t_new):
    del pt_ref
    idx_refs = rest[:n_pages]
    k_refs = rest[n_pages:2 * n_pages]
    v_refs = rest[2 * n_pages:3 * n_pages]
    o_ref, sc_s, k_s, v_s = rest[3 * n_pages:]
    P = PAGE_SIZE
    iq4 = iq4_ref[0]
    w = w_ref[0]
    pad = P - t_new

    def page_scores(ik, page):
        hi, lo = _split_bf16(ik)
        ik4 = jnp.concatenate([hi, lo, hi, lo], axis=1)
        r = jnp.maximum(_dot_nt(iq4, ik4), 0.0) * w
        sc = r[0:t_new]
        for h in range(1, N_IDX_HEADS):
            sc = sc + r[h * t_new:(h + 1) * t_new]
        sc_s[:, page * P:(page + 1) * P] = sc * IDX_SCALE

    for p in range(n_pages):
        page_scores(idx_refs[p][0, 0], p)
        k_s[p * P:(p + 1) * P, :] = k_refs[p][0, 0].astype(BF16)
        v_s[p * P:(p + 1) * P, :] = v_refs[p][0, 0].astype(BF16)
    page_scores(jnp.concatenate([ikn_ref[0], jnp.zeros((pad, IDX_DIM), F32)], axis=0), n_pages)
    zkv = jnp.zeros((pad, N_KV_B * HEAD_DIM_B), F32)
    k_s[n_pages * P:(n_pages + 1) * P, :] = jnp.concatenate([kn_ref[0], zkv], axis=0).astype(BF16)
    v_s[n_pages * P:(n_pages + 1) * P, :] = jnp.concatenate([vn_ref[0], zkv], axis=0).astype(BF16)

    nk = (n_pages + 1) * P
    past = n_pages * P
    kpos = lax.broadcasted_iota(I32, (t_new, nk), 1)
    valid = kpos <= past + lax.broadcasted_iota(I32, (t_new, nk), 0)
    keys = _sortable(jnp.where(valid, sc_s[...], -jnp.inf))
    mask = _topk_mask(keys, valid, kpos, min(TOPK_MAX, (past + t_new) // 4))
    mask_i = jnp.where(mask, 1, 0).astype(I32)
    mask = jnp.concatenate([mask_i] * KV_REP, axis=0) > 0
    for g in range(N_KV_B):
        gs = slice(g * HEAD_DIM_B, (g + 1) * HEAD_DIM_B)
        o_ref[0, g] = _masked_attention(_dot_nt(qg_ref[0, g], k_s[:, gs]), mask, v_s[:, gs]).astype(o_ref.dtype)


def _dsa_sample(qg, iq4r, wr, k_new, v_new, ik_new, cache_k, cache_v, cache_idx_k, page_table, layer):
    B, _, rows, hd = qg.shape
    t_new = k_new.shape[1]
    n_pages = page_table.shape[1]
    kvw = N_KV_B * HEAD_DIM_B
    per_seq = lambda shape: pl.BlockSpec((1,) + shape, lambda b, pt: (b,) + (0,) * len(shape))

    def page_spec(width, p):
        return pl.BlockSpec((1, 1, PAGE_SIZE, width), lambda b, pt: (layer, pt[b, p], 0, 0))

    in_specs = [per_seq(qg.shape[1:]), per_seq(iq4r.shape[1:]), per_seq(wr.shape[1:]), per_seq((t_new, kvw)),
                per_seq((t_new, kvw)), per_seq((t_new, IDX_DIM))]
    in_specs += [page_spec(IDX_DIM, p) for p in range(n_pages)]
    in_specs += [page_spec(kvw, p) for p in range(n_pages)]
    in_specs += [page_spec(kvw, p) for p in range(n_pages)]
    nk = (n_pages + 1) * PAGE_SIZE
    grid_spec = pltpu.PrefetchScalarGridSpec(
        num_scalar_prefetch=1, grid=(B,), in_specs=in_specs,
        out_specs=pl.BlockSpec((1,) + qg.shape[1:], lambda b, pt: (b, 0, 0, 0)),
        scratch_shapes=[pltpu.VMEM((t_new, nk), F32), pltpu.VMEM((nk, kvw), BF16), pltpu.VMEM((nk, kvw), BF16)])
    return pl.pallas_call(
        functools.partial(_dsa_sample_kernel, n_pages=n_pages, t_new=t_new),
        grid_spec=grid_spec,
        out_shape=jax.ShapeDtypeStruct(qg.shape, BF16),
        compiler_params=_cparams("arbitrary"),
    )(page_table, qg, iq4r, wr, k_new, v_new, ik_new, *([cache_idx_k] * n_pages), *([cache_k] * n_pages),
      *([cache_v] * n_pages))


def _merge_kernel(xn_ref, a_ref, b_ref, c_ref, d_ref, wg0, wg1, wg2, wg3, bg0, bg1, bg2, bg3, wb_ref, o_ref):
    xn = xn_ref[...]
    acc = None
    for n, (br, wg, bg) in enumerate(zip((a_ref, b_ref, c_ref, d_ref), (wg0, wg1, wg2, wg3), (bg0, bg1, bg2, bg3))):
        term = jax.nn.sigmoid(_dot(xn, wg[...]) + bg[...]) * _dot(br[...], wb_ref[n])
        acc = term if acc is None else acc + term
    o_ref[...] = acc.astype(o_ref.dtype)


def _merge(xn, branches, w_gate, b_gate, w_branch, tm, tn):
    t, d = xn.shape
    nj = d // tn
    row = lambda w: pl.BlockSpec((tm, w), lambda i, j: (i, 0))
    wg = [pl.BlockSpec((d, tn), functools.partial(lambda i, j, n: (0, n * nj + j), n=n)) for n in range(N_BRANCH)]
    bg = [pl.BlockSpec((1, tn), functools.partial(lambda i, j, n: (0, n * nj + j), n=n)) for n in range(N_BRANCH)]
    return pl.pallas_call(
        _merge_kernel,
        grid=(t // tm, nj),
        in_specs=[row(d)] + [row(BRANCH_W)] * N_BRANCH + wg + bg
                 + [pl.BlockSpec((N_BRANCH, BRANCH_W, tn), lambda i, j: (0, 0, j))],
        out_specs=pl.BlockSpec((tm, tn), lambda i, j: (i, j)),
        out_shape=jax.ShapeDtypeStruct((t, d), BF16),
        compiler_params=_cparams("parallel", "arbitrary"),
    )(xn, *branches, *([w_gate] * N_BRANCH), *([b_gate] * N_BRANCH), w_branch)


def _outproj_kernel(m_ref, x_ref, w_ref, g_ref, o_ref):
    o_ref[...] = x_ref[...] + _rms(_dot(m_ref[...], w_ref[...]), g_ref[...])


def _outproj(merged, x, w_o, g, tm):
    t, d = x.shape
    return pl.pallas_call(
        _outproj_kernel,
        grid=(t // tm,),
        in_specs=[pl.BlockSpec((tm, d), lambda i: (i, 0)), pl.BlockSpec((tm, d), lambda i: (i, 0)),
                  pl.BlockSpec((d, d), lambda i: (0, 0)), pl.BlockSpec((1, d), lambda i: (0, 0))],
        out_specs=pl.BlockSpec((tm, d), lambda i: (i, 0)),
        out_shape=jax.ShapeDtypeStruct((t, d), F32),
        compiler_params=_cparams("parallel"),
    )(merged, x, w_o, g)


def _ffn_kernel(x_ref, gpre_ref, wug_ref, wuv_ref, wcg_ref, wcv_ref, bcg_ref, bcv_ref, wd_ref, gpost_ref,
                stg_ref, stv_ref, o_ref, ng_ref, nv_ref, h_s, acc_s, cg_s, cv_s, ext_s, *, S, tiles_per_group):
    i = pl.program_id(0)
    j = pl.program_id(1)
    tm = x_ref.shape[0]
    CR = cg_s.shape[1]

    @pl.when(j == 0)
    def _():
        h_s[...] = _rms(x_ref[...], gpre_ref[...]).astype(BF16)
        acc_s[...] = jnp.zeros_like(acc_s)

    @pl.when(i % tiles_per_group == 0)
    def _():
        cg_s[j] = stg_ref[0]
        cv_s[j] = stv_ref[0]

    h = h_s[...]

    def conv(w_ref, wc_ref, bc_ref, c_s, new_ref):
        u = _dot(h, w_ref[...])
        ext_s[0:CR, :] = c_s[j]
        ext_s[CR:CR + tm, :] = u
        y = bc_ref[...] + u * wc_ref[2:3, :]
        y = y + ext_s[CR - S:CR - S + tm, :] * wc_ref[1:2, :]
        y = y + ext_s[CR - 2 * S:CR - 2 * S + tm, :] * wc_ref[0:1, :]
        tail = ext_s[tm:tm + CR, :]
        c_s[j] = tail
        new_ref[0] = tail
        return y

    gate = conv(wug_ref, wcg_ref, bcg_ref, cg_s, ng_ref)
    val = conv(wuv_ref, wcv_ref, bcv_ref, cv_s, nv_ref)
    acc_s[...] += _dot((_gelu_tanh(gate) * val).astype(BF16), wd_ref[...])

    @pl.when(j == pl.num_programs(1) - 1)
    def _():
        o_ref[...] = x_ref[...] + _rms(acc_s[...], gpost_ref[...])


def _ffn(x, lp, state, *, S, tm, tn):
    t, d = x.shape
    G, CR, _ = state.shape
    nj = D_FF // tn
    ni = t // tm
    tiles_per_group = ni // G
    grp = lambda i, j: (i // tiles_per_group, 0, j)
    tile = lambda i, j: (i, 0, j)
    grp_v = lambda i, j: (i // tiles_per_group, 0, nj + j)
    col = lambda r: pl.BlockSpec((r, tn), lambda i, j: (0, j))
    col_v = lambda r: pl.BlockSpec((r, tn), lambda i, j: (0, nj + j))
    vec = pl.BlockSpec((1, d), lambda i, j: (0, 0))
    xo, ng, nv = pl.pallas_call(
        functools.partial(_ffn_kernel, S=S, tiles_per_group=tiles_per_group),
        grid=(ni, nj),
        in_specs=[pl.BlockSpec((tm, d), lambda i, j: (i, 0)), vec, col(d), col_v(d),
                  col(FFN_CONV_W), col_v(FFN_CONV_W), col(1), col_v(1),
                  pl.BlockSpec((tn, d), lambda i, j: (j, 0)), vec,
                  pl.BlockSpec((1, CR, tn), grp), pl.BlockSpec((1, CR, tn), grp_v)],
        out_specs=[pl.BlockSpec((tm, d), lambda i, j: (i, 0)),
                   pl.BlockSpec((1, CR, tn), tile), pl.BlockSpec((1, CR, tn), tile)],
        out_shape=[jax.ShapeDtypeStruct((t, d), F32), jax.ShapeDtypeStruct((ni, CR, D_FF), F32),
                   jax.ShapeDtypeStruct((ni, CR, D_FF), F32)],
        scratch_shapes=[pltpu.VMEM((tm, d), BF16), pltpu.VMEM((tm, d), F32), pltpu.VMEM((nj, CR, tn), F32),
                        pltpu.VMEM((nj, CR, tn), F32), pltpu.VMEM((CR + tm, tn), F32)],
        compiler_params=_cparams("arbitrary", "arbitrary"),
    )(x, lp["g_ffn_pre"], lp["w_up"], lp["w_up"], lp["w_ffn_conv"], lp["w_ffn_conv"], lp["b_ffn_conv"],
      lp["b_ffn_conv"], lp["w_down"], lp["g_ffn_post"], state, state)
    last = slice(tiles_per_group - 1, None, tiles_per_group)
    return xo, ng[last], nv[last]


def _layer_front(x, lp, tabs, *, tm):
    u_main, xn = _inproj(x, lp["g_mix_pre"], lp["w_in_main"], tm, 512)
    u_idx = _inproj_split(x, lp["g_mix_pre"], lp["w_in_idx_hi"], lp["w_in_idx_lo"], tm, 384)
    return u_main, xn, _rope_prep(u_main, u_idx, tabs, 256)


def _layer_back(x, xn, branches, lp, ffn_state, *, S, tm, tn_ffn):
    merged = _merge(xn, branches, lp["w_gate"], lp["b_gate"], lp["w_branch"], tm, 256)
    x1 = _outproj(merged, x, lp["w_o"], lp["g_mix_post"], tm)
    return _ffn(x1, lp, ffn_state, S=S, tm=tm, tn=tn_ffn)


def _prep_layer_params(p, l):
    w_in = p["w_in"][l]
    row = lambda a: a[l][None, :]
    o_iq = 3072
    o_cx = o_iq + N_IDX_HEADS * IDX_DIM + IDX_DIM + N_IDX_HEADS
    w_idx = jnp.pad(w_in[:, o_iq:o_cx], ((0, 0), (0, N_IDX - (o_cx - o_iq))))
    hi = w_idx.astype(BF16)
    return dict(
        g_mix_pre=row(p["g_mix_pre"]),
        w_in_main=jnp.concatenate([w_in[:, :o_iq], w_in[:, o_cx:]], axis=1).astype(BF16),
        w_in_idx_hi=hi, w_in_idx_lo=(w_idx - hi.astype(F32)).astype(BF16),
        w_pool=p["w_pool"][l].astype(BF16), pool_scale=row(p["pool_scale"]),
        w_conv_c=p["w_conv_c"][l], b_conv_c=row(p["b_conv_c"]),
        w_rg_a=p["w_rg_a"][l].astype(BF16), b_rg_a=row(p["b_rg_a"]),
        w_rg_x=p["w_rg_x"][l].astype(BF16), b_rg_x=row(p["b_rg_x"]),
        lru_lambda=row(p["lru_lambda"]), ret_gn=row(p["ret_gn"]),
        w_branch=p["w_branch"][l].astype(BF16), w_gate=p["w_gate"][l].astype(BF16), b_gate=row(p["b_gate"]),
        w_o=p["w_o"][l].astype(BF16), g_mix_post=row(p["g_mix_post"]), g_ffn_pre=row(p["g_ffn_pre"]),
        w_up=p["w_up"][l].astype(BF16), w_ffn_conv=p["w_ffn_conv"][l], b_ffn_conv=row(p["b_ffn_conv"]),
        w_down=p["w_down"][l].astype(BF16), g_ffn_post=row(p["g_ffn_post"]))


def _to_time_major(a, nb, nt):
    return jnp.swapaxes(a, 0, 1).reshape((nt * nb,) + a.shape[2:])


def _to_seq_major(a, nb, nt):
    return jnp.swapaxes(a.reshape((nt, nb) + a.shape[1:]), 0, 1)


def kernel(x_prompt, x_sample, cache_k, cache_v, cache_idx_k, state_pool, state_conv, state_rglru, state_ret, state_ffn_conv, page_table, g_mix_pre, w_in, w_pool, pool_scale, w_conv_c, b_conv_c, w_rg_a, b_rg_a, w_rg_x, b_rg_x, lru_lambda, ret_gn, w_branch, w_gate, b_gate, w_o, g_mix_post, g_ffn_pre, w_up, w_ffn_conv, b_ffn_conv, w_down, g_ffn_post):
    params = dict(g_mix_pre=g_mix_pre, w_in=w_in, w_pool=w_pool, pool_scale=pool_scale, w_conv_c=w_conv_c,
                  b_conv_c=b_conv_c, w_rg_a=w_rg_a, b_rg_a=b_rg_a, w_rg_x=w_rg_x, b_rg_x=b_rg_x,
                  lru_lambda=lru_lambda, ret_gn=ret_gn, w_branch=w_branch, w_gate=w_gate, b_gate=b_gate, w_o=w_o,
                  g_mix_post=g_mix_post, g_ffn_pre=g_ffn_pre, w_up=w_up, w_ffn_conv=w_ffn_conv,
                  b_ffn_conv=b_ffn_conv, w_down=w_down, g_ffn_post=g_ffn_post)
    depth = w_in.shape[0]
    bp, seq, d = x_prompt.shape
    bs, tdec, _ = x_sample.shape
    past = page_table.shape[1] * PAGE_SIZE
    n_pool = cache_k.shape[1]
    kvw = N_KV_B * HEAD_DIM_B
    ck = cache_k.reshape(depth, n_pool, PAGE_SIZE, kvw)
    cv = cache_v.reshape(depth, n_pool, PAGE_SIZE, kvw)

    tabs_p = _rope_tables(jnp.arange(seq, dtype=F32))
    tabs_s = _rope_tables(jnp.repeat(past + jnp.arange(tdec, dtype=F32), bs))
    xp = x_prompt.reshape(bp * seq, d)
    xs = _to_time_major(x_sample, bs, tdec)
    cr_p = max(SUBLANES, (FFN_CONV_W - 1))
    outs_p, outs_s = [], []
    for l in range(depth):
        lp = _prep_layer_params(params, l)
        u, xn, (q, k, kb, vb, iq4, ik4, ikw, rq, rk) = _layer_front(xp, lp, tabs_p, tm=512)
        o_a = _pool(u, jnp.zeros((bp, POOL_BUF, BRANCH_W), F32), lp["w_pool"], lp["pool_scale"], S=1, Tc=256, start=0)
        o_b = _dsa_prompt(q, kb, vb, iq4, ik4, ikw, n_seq=bp)
        o_c, h_p = _rglru(u, jnp.zeros((bp, CONV_W - 1, LRU_W), F32), jnp.zeros((bp, 1, LRU_W), F32), lp, S=1, Tc=256)
        o_d, s_p = _retention(rq, rk, u, u, jnp.zeros((bp, N_HEADS_D, QK_DIM_D, V_DIM_D), F32), lp["ret_gn"],
                              c=RET_CHUNK if seq % RET_CHUNK == 0 else seq, v_col=6, g_col=7)
        xp, ng, nv = _layer_back(xp, xn, (o_a, o_b, o_c, o_d), lp, jnp.zeros((bp, cr_p, 2 * D_FF), F32),
                                 S=1, tm=512, tn_ffn=512)
        u3 = u.reshape(bp, seq, N_MAIN)
        outs_p.append((
            k.reshape(bp, seq, N_KV_B, HEAD_DIM_B), u3[:, :, 2560:3072].reshape(bp, seq, N_KV_B, HEAD_DIM_B),
            ikw.reshape(bp, seq, LANES)[:, :, :IDX_DIM], u3[:, seq - POOL_BUF:, 0:BRANCH_W],
            u3[:, seq - (CONV_W - 1):, 3072:3072 + LRU_W], h_p[:, 0], s_p,
            jnp.concatenate([ng, nv], axis=-1)[:, cr_p - (FFN_CONV_W - 1):]))
        u, xn, (q, k, kb, vb, iq4, ik4, ikw, rq, rk) = _layer_front(xs, lp, tabs_s, tm=512)
        o_a = _pool(u, _to_time_major(state_pool[l], bs, POOL_BUF)[None], lp["w_pool"], lp["pool_scale"],
                    S=bs, Tc=tdec, start=past)
        o_c, h_s = _rglru(u, _to_time_major(state_conv[l], bs, CONV_W - 1)[None], state_rglru[l][None], lp,
                          S=bs, Tc=tdec)
        sm = lambda a: _to_seq_major(a, bs, tdec)
        u_sm = sm(u)
        o_d, s_s = _retention(sm(rq).reshape(bs * tdec, -1), sm(rk).reshape(bs * tdec, -1),
                              u_sm[:, :, 6144:7168].reshape(bs * tdec, -1), u_sm[:, :, 7168:8192].reshape(bs * tdec, -1),
                              state_ret[l], lp["ret_gn"], c=tdec, v_col=0, g_col=0)
        o_d = _to_time_major(o_d.reshape(bs, tdec, BRANCH_W), bs, tdec)
        qg = sm(q).reshape(bs, tdec, N_KV_B, KV_REP, HEAD_DIM_B).transpose(0, 2, 3, 1, 4)
        qg = qg.reshape(bs, N_KV_B, KV_REP * tdec, HEAD_DIM_B)
        iq4r = sm(iq4).reshape(bs, tdec, N_IDX_HEADS, IDX4).transpose(0, 2, 1, 3).reshape(bs, N_IDX_HEADS * tdec, IDX4)
        ikw_sm = sm(ikw)
        wr = ikw_sm[:, :, IDX_DIM:IDX_DIM + N_IDX_HEADS].transpose(0, 2, 1).reshape(bs, N_IDX_HEADS * tdec, 1)
        k_sm = sm(k)
        v_sm = u_sm[:, :, 2560:3072]
        o_b = _dsa_sample(qg, iq4r, wr, k_sm, v_sm, ikw_sm[:, :, :IDX_DIM], ck, cv, cache_idx_k, page_table, l)
        o_b = o_b.reshape(bs, N_KV_B, KV_REP, tdec, HEAD_DIM_B).transpose(3, 0, 1, 2, 4).reshape(tdec * bs, BRANCH_W)
        ffn_state = _to_time_major(state_ffn_conv[l], bs, FFN_CONV_W - 1)[None]
        xs, ng, nv = _layer_back(xs, xn, (o_a, o_b, o_c, o_d), lp, ffn_state, S=bs, tm=512, tn_ffn=256)
        ffn_new = _to_seq_major(jnp.concatenate([ng, nv], axis=-1)[0], bs, FFN_CONV_W - 1)
        a_in = u_sm[:, :, 0:BRANCH_W]
        c_x = u_sm[:, :, 3072:3072 + LRU_W]
        outs_s.append((
            k_sm.reshape(bs, tdec, N_KV_B, HEAD_DIM_B), v_sm.reshape(bs, tdec, N_KV_B, HEAD_DIM_B),
            ikw_sm[:, :, :IDX_DIM],
            jnp.concatenate([state_pool[l], a_in], axis=1)[:, tdec:],
            jnp.concatenate([state_conv[l], c_x], axis=1)[:, tdec:],
            h_s[0], s_s, ffn_new))

    stk = lambda outs, i: jnp.stack([o[i] for o in outs], axis=0)
    res = [xp.reshape(bp, seq, d), _to_seq_major(xs, bs, tdec)]
    for i in range(8):
        res += [stk(outs_p, i), stk(outs_s, i)]
    return tuple(res)
```

```python
import functools
import math

import jax
import jax.numpy as jnp
from jax import lax
from jax.experimental import pallas as pl
from jax.experimental.pallas import tpu as pltpu

F32 = jnp.float32
BF16 = jnp.bfloat16
I32 = jnp.int32

SUBLANES = 8
LANES = 128
VMEM_LIMIT_BYTES = 56 * 1024 * 1024

D_MODEL = 2048
BRANCH_W = D_MODEL // 2
N_BRANCH = 4
POOL_WINDOWS = (2, 4, 8, 16)
POOL_GROUP = BRANCH_W // len(POOL_WINDOWS)
POOL_BUF = max(POOL_WINDOWS) - 1
N_HEADS_B = 8
HEAD_DIM_B = BRANCH_W // N_HEADS_B
N_KV_B = 4
KV_REP = N_HEADS_B // N_KV_B
N_IDX_HEADS = 16
IDX_DIM = 64
TOPK_MAX = 256
ATT_BLOCK = 128
ROPE_THETA = 10000.0
ATT_SCALE = HEAD_DIM_B ** -0.5
IDX_SCALE = (IDX_DIM * N_IDX_HEADS) ** -0.5
LRU_W = BRANCH_W
LRU_BLOCKS = 4
LRU_BLOCK = LRU_W // LRU_BLOCKS
LRU_C = 8.0
CONV_W = 4
N_HEADS_D = 8
QK_DIM_D = BRANCH_W // (2 * N_HEADS_D)
V_DIM_D = BRANCH_W // N_HEADS_D
RET_CHUNK = 128
D_FF = 11 * D_MODEL // 4
FFN_CONV_W = 3
EPS = 1e-6
PAGE_SIZE = 128

N_MAIN = 8192
N_IDX = 1152
IDX4 = 4 * IDX_DIM

LOG_G = tuple(math.log1p(-(2.0 ** (-5.0 - h))) for h in range(N_HEADS_D))
INT_MIN = -2 ** 31


def _round_up(n, m):
    return (n + m - 1) // m * m


def _cparams(*sem):
    return pltpu.CompilerParams(dimension_semantics=sem, vmem_limit_bytes=VMEM_LIMIT_BYTES)


def _dot(a, b):
    return jnp.dot(a, b, preferred_element_type=F32)


def _dot_nt(a, b):
    return lax.dot_general(a, b, (((1,), (1,)), ((), ())), preferred_element_type=F32)


def _dot_tn(a, b):
    return lax.dot_general(a, b, (((0,), (0,)), ((), ())), preferred_element_type=F32)


def _rms(x, g):
    return x * lax.rsqrt(jnp.mean(x * x, axis=-1, keepdims=True) + EPS) * g


def _gelu_tanh(x):
    return x * (0.5 * (1.0 + jnp.tanh(0.7978845608028654 * (x + 0.044715 * (x * x * x)))))


def _split_bf16(x):
    hi = x.astype(BF16)
    lo = (x - hi.astype(F32)).astype(BF16)
    return hi, lo


def _inproj_kernel(x_ref, g_ref, w_ref, u_ref, xn_ref):
    @pl.when(pl.program_id(1) == 0)
    def _():
        xn_ref[...] = _rms(x_ref[...], g_ref[...]).astype(BF16)

    u_ref[...] = _dot(xn_ref[...], w_ref[...])


def _inproj(x, g, w, tm, tn):
    t, d = x.shape
    n = w.shape[1]
    return pl.pallas_call(
        _inproj_kernel,
        grid=(t // tm, n // tn),
        in_specs=[pl.BlockSpec((tm, d), lambda i, j: (i, 0)),
                  pl.BlockSpec((1, d), lambda i, j: (0, 0)),
                  pl.BlockSpec((d, tn), lambda i, j: (0, j))],
        out_specs=[pl.BlockSpec((tm, tn), lambda i, j: (i, j)),
                   pl.BlockSpec((tm, d), lambda i, j: (i, 0))],
        out_shape=[jax.ShapeDtypeStruct((t, n), F32), jax.ShapeDtypeStruct((t, d), BF16)],
        compiler_params=_cparams("parallel", "arbitrary"),
    )(x, g, w)


def _inproj_split_kernel(x_ref, g_ref, wh_ref, wl_ref, u_ref, xh_s, xl_s):
    @pl.when(pl.program_id(1) == 0)
    def _():
        hi, lo = _split_bf16(_rms(x_ref[...], g_ref[...]))
        xh_s[...] = hi
        xl_s[...] = lo

    xh = xh_s[...]
    u_ref[...] = _dot(xh, wh_ref[...]) + (_dot(xh, wl_ref[...]) + _dot(xl_s[...], wh_ref[...]))


def _inproj_split(x, g, wh, wl, tm, tn):
    t, d = x.shape
    n = wh.shape[1]
    return pl.pallas_call(
        _inproj_split_kernel,
        grid=(t // tm, n // tn),
        in_specs=[pl.BlockSpec((tm, d), lambda i, j: (i, 0)),
                  pl.BlockSpec((1, d), lambda i, j: (0, 0)),
                  pl.BlockSpec((d, tn), lambda i, j: (0, j)),
                  pl.BlockSpec((d, tn), lambda i, j: (0, j))],
        out_specs=pl.BlockSpec((tm, tn), lambda i, j: (i, j)),
        out_shape=jax.ShapeDtypeStruct((t, n), F32),
        scratch_shapes=[pltpu.VMEM((tm, d), BF16), pltpu.VMEM((tm, d), BF16)],
        compiler_params=_cparams("parallel", "arbitrary"),
    )(x, g, wh, wl)


def _rope128(xs, cos, sin):
    return xs * cos + pltpu.roll(xs, HEAD_DIM_B // 2, axis=1) * sin


def _rope64(xs, cos, sin, first_half):
    rot = jnp.where(first_half, pltpu.roll(xs, LANES - IDX_DIM // 2, axis=1), pltpu.roll(xs, IDX_DIM // 2, axis=1))
    return xs * cos + rot * sin


def _rope_kernel(bq_ref, bk_ref, bv_ref, dq_ref, dk_ref, ui_ref, c128_ref, s128_ref, c64_ref, s64_ref,
                 q_ref, k_ref, kb_ref, vb_ref, iq4_ref, ik4_ref, ikw_ref, rq_ref, rk_ref):
    rows = bq_ref.shape[0]
    c128, s128, c64, s64 = c128_ref[...], s128_ref[...], c64_ref[...], s64_ref[...]
    lane = lax.broadcasted_iota(I32, (rows, LANES), 1)
    first_half = (lane & (IDX_DIM // 2)) == 0
    low = lane < IDX_DIM
    for c in range(N_HEADS_B):
        sl = slice(c * LANES, (c + 1) * LANES)
        q_ref[:, sl] = (_rope128(bq_ref[:, sl], c128, s128) * ATT_SCALE).astype(BF16)
    for c in range(N_KV_B):
        sl = slice(c * LANES, (c + 1) * LANES)
        kr = _rope128(bk_ref[:, sl], c128, s128)
        k_ref[:, sl] = kr
        kb_ref[:, sl] = kr.astype(BF16)
    vb_ref[...] = bv_ref[...].astype(BF16)
    for c in range(N_HEADS_D * QK_DIM_D // LANES):
        sl = slice(c * LANES, (c + 1) * LANES)
        rq_ref[:, sl] = _rope64(dq_ref[:, sl], c64, s64, first_half)
        rk_ref[:, sl] = _rope64(dk_ref[:, sl], c64, s64, first_half) * (QK_DIM_D ** -0.5)
    for c in range(N_IDX_HEADS // 2):
        y = _rope64(ui_ref[:, c * LANES:(c + 1) * LANES], c64, s64, first_half)
        yr = pltpu.roll(y, IDX_DIM, axis=1)
        for hh, dup in enumerate((jnp.where(low, y, yr), jnp.where(low, yr, y))):
            hi, lo = _split_bf16(dup)
            base = (2 * c + hh) * IDX4
            iq4_ref[:, base:base + LANES] = hi
            iq4_ref[:, base + LANES:base + 2 * LANES] = lo
    raw = ui_ref[:, N_IDX_HEADS * IDX_DIM:N_IDX_HEADS * IDX_DIM + LANES]
    y = _rope64(raw, c64, s64, first_half)
    ikw_ref[...] = jnp.where(low, y, raw)
    dup = jnp.where(low, y, pltpu.roll(y, IDX_DIM, axis=1))
    hi = dup.astype(BF16).astype(F32)
    hilo = jnp.where(low, hi, dup - hi).astype(BF16)
    ik4_ref[:, 0:LANES] = hilo
    ik4_ref[:, LANES:2 * LANES] = hilo


def _rope_prep(u_main, u_idx, tabs, tr):
    t = u_main.shape[0]
    nt = tabs[0].shape[0] // tr
    row = lambda w, c: pl.BlockSpec((tr, w), lambda i: (i, c))
    tab = pl.BlockSpec((tr, LANES), lambda i: (i % nt, 0))
    outs = [(BRANCH_W, BF16), (N_KV_B * HEAD_DIM_B, F32), (N_KV_B * HEAD_DIM_B, BF16), (N_KV_B * HEAD_DIM_B, BF16),
            (N_IDX_HEADS * IDX4, BF16), (IDX4, BF16), (LANES, F32), (N_HEADS_D * QK_DIM_D, F32),
            (N_HEADS_D * QK_DIM_D, F32)]
    return pl.pallas_call(
        _rope_kernel,
        grid=(t // tr,),
        in_specs=[row(1024, 1), row(512, 4), row(512, 5), row(512, 10), row(512, 11), row(N_IDX, 0),
                  tab, tab, tab, tab],
        out_specs=[row(w, 0) for w, _ in outs],
        out_shape=[jax.ShapeDtypeStruct((t, w), dt) for w, dt in outs],
        compiler_params=_cparams("parallel"),
    )(u_main, u_main, u_main, u_main, u_main, u_idx, *tabs)


def _rope_tables(pos):
    def tab(half, reps):
        inv = jnp.exp(-math.log(ROPE_THETA) * jnp.arange(half, dtype=F32) / half)
        ang = pos[:, None] * inv[None, :]
        cos, sin = jnp.cos(ang), jnp.sin(ang)
        return jnp.tile(jnp.concatenate([cos, cos], 1), (1, reps)), jnp.tile(jnp.concatenate([-sin, sin], 1), (1, reps))

    c128, s128 = tab(HEAD_DIM_B // 2, 1)
    c64, s64 = tab(IDX_DIM // 2, 2)
    return c128, s128, c64, s64


def _pool_kernel(a_ref, prev_ref, w_ref, sc_ref, o_ref, ext_s, *, S, Tc, start, nch):
    ch = pl.program_id(1)
    R = Tc * S
    PS = POOL_BUF * S
    OFF = _round_up(PS, SUBLANES)

    @pl.when(ch == 0)
    def _():
        ext_s[OFF - PS:OFF, :] = prev_ref[0]

    x = a_ref[...]
    ext_s[OFF:OFF + R, :] = x
    t_loc = lax.broadcasted_iota(I32, (R, 1), 0) // S if S > 1 else lax.broadcasted_iota(I32, (R, 1), 0)
    pos1 = start + ch * Tc + t_loc + 1
    for gi, w in enumerate(POOL_WINDOWS):
        sl = slice(gi * POOL_GROUP, (gi + 1) * POOL_GROUP)
        xs = x[:, sl]
        acc = xs
        for j in range(1, w):
            acc = acc + ext_s[OFF - j * S:OFF - j * S + R, sl]
        cnt = jnp.minimum(w, pos1).astype(F32)
        mixed = acc / cnt - xs
        y = _dot(mixed.astype(BF16), w_ref[gi])
        o_ref[:, sl] = (y * sc_ref[:, sl]).astype(o_ref.dtype)
    if nch > 1:
        ext_s[OFF - PS:OFF, :] = ext_s[OFF + R - PS:OFF + R, :]


def _pool(u_main, prev, w_pool, scale, *, S, Tc, start):
    t = u_main.shape[0]
    R = Tc * S
    G = prev.shape[0]
    nch = t // (G * R)
    PS = POOL_BUF * S
    C = BRANCH_W
    return pl.pallas_call(
        functools.partial(_pool_kernel, S=S, Tc=Tc, start=start, nch=nch),
        grid=(G, nch),
        in_specs=[pl.BlockSpec((R, C), lambda g, c: (g * nch + c, 0)),
                  pl.BlockSpec((1, PS, C), lambda g, c: (g, 0, 0)),
                  pl.BlockSpec((len(POOL_WINDOWS), POOL_GROUP, POOL_GROUP), lambda g, c: (0, 0, 0)),
                  pl.BlockSpec((1, C), lambda g, c: (0, 0))],
        out_specs=pl.BlockSpec((R, C), lambda g, c: (g * nch + c, 0)),
        out_shape=jax.ShapeDtypeStruct((t, C), BF16),
        scratch_shapes=[pltpu.VMEM((_round_up(PS, SUBLANES) + R, C), F32)],
        compiler_params=_cparams("parallel", "arbitrary"),
    )(u_main, prev, w_pool, scale)


def _rglru_kernel(cx_ref, cg_ref, prev_ref, h0_ref, wc_ref, bc_ref, wa_ref, ba_ref, wx_ref, bx_ref, lam_ref,
                  o_ref, hl_ref, ext_s, a_s, b_s, h_s, *, S, Tc, nch):
    ch = pl.program_id(1)
    R = Tc * S
    PS = (CONV_W - 1) * S
    OFF = _round_up(PS, SUBLANES)

    @pl.when(ch == 0)
    def _():
        ext_s[OFF - PS:OFF, :] = prev_ref[0]
        h_s[...] = h0_ref[0]

    x = cx_ref[...]
    ext_s[OFF:OFF + R, :] = x
    xc = bc_ref[...] + x * wc_ref[CONV_W - 1:CONV_W, :]
    for j in range(CONV_W - 1):
        k = CONV_W - 1 - j
        xc = xc + ext_s[OFF - k * S:OFF - k * S + R, :] * wc_ref[j:j + 1, :]
    xb = xc.astype(BF16)
    for n in range(LRU_BLOCKS):
        sl = slice(n * LRU_BLOCK, (n + 1) * LRU_BLOCK)
        r = jax.nn.sigmoid(_dot(xb[:, sl], wa_ref[n]) + ba_ref[:, sl])
        i = jax.nn.sigmoid(_dot(xb[:, sl], wx_ref[n]) + bx_ref[:, sl])
        lam = lam_ref[:, sl]
        softplus_neg = jnp.maximum(-lam, 0.0) + jnp.log1p(jnp.exp(-jnp.abs(lam)))
        log_a = (-LRU_C) * r * softplus_neg
        a_s[:, sl] = jnp.exp(log_a)
        th = jnp.tanh(log_a)
        b_s[:, sl] = jnp.sqrt(-2.0 * th / (1.0 - th)) * (i * xc[:, sl])

    def step(t, h):
        off = t * S
        if S % SUBLANES == 0:
            off = pl.multiple_of(off, S)
        h = a_s[pl.ds(off, S), :] * h + b_s[pl.ds(off, S), :]
        b_s[pl.ds(off, S), :] = h
        return h

    h = lax.fori_loop(0, Tc, step, h_s[...])
    h_s[...] = h
    hl_ref[0] = h
    o_ref[...] = (_gelu_tanh(cg_ref[...]) * b_s[...]).astype(o_ref.dtype)
    if nch > 1:
        ext_s[OFF - PS:OFF, :] = ext_s[OFF + R - PS:OFF + R, :]


def _rglru(u_main, prev, h0, lp, *, S, Tc):
    t = u_main.shape[0]
    R = Tc * S
    G = prev.shape[0]
    nch = t // (G * R)
    PS = (CONV_W - 1) * S
    C = LRU_W
    vec = pl.BlockSpec((1, C), lambda g, c: (0, 0))
    blk = pl.BlockSpec((LRU_BLOCKS, LRU_BLOCK, LRU_BLOCK), lambda g, c: (0, 0, 0))
    return pl.pallas_call(
        functools.partial(_rglru_kernel, S=S, Tc=Tc, nch=nch),
        grid=(G, nch),
        in_specs=[pl.BlockSpec((R, C), lambda g, c: (g * nch + c, 3)),
                  pl.BlockSpec((R, C), lambda g, c: (g * nch + c, 4)),
                  pl.BlockSpec((1, PS, C), lambda g, c: (g, 0, 0)),
                  pl.BlockSpec((1, S, C), lambda g, c: (g, 0, 0)),
                  pl.BlockSpec((CONV_W, C), lambda g, c: (0, 0)), vec, blk, vec, blk, vec, vec],
        out_specs=[pl.BlockSpec((R, C), lambda g, c: (g * nch + c, 0)),
                   pl.BlockSpec((1, S, C), lambda g, c: (g, 0, 0))],
        out_shape=[jax.ShapeDtypeStruct((t, C), BF16), jax.ShapeDtypeStruct((G, S, C), F32)],
        scratch_shapes=[pltpu.VMEM((_round_up(PS, SUBLANES) + R, C), F32), pltpu.VMEM((R, C), F32),
                        pltpu.VMEM((R, C), F32), pltpu.VMEM((S, C), F32)],
        compiler_params=_cparams("parallel", "arbitrary"),
    )(u_main, u_main, prev, h0, lp["w_conv_c"], lp["b_conv_c"], lp["w_rg_a"], lp["b_rg_a"], lp["w_rg_x"],
      lp["b_rg_x"], lp["lru_lambda"])


def _ret_kernel(rq_ref, rk_ref, rv_ref, dg_ref, s0_ref, gn_ref, o_ref, sn_ref, s_s, *, c):
    @pl.when(pl.program_id(1) == 0)
    def _():
        s_s[...] = s0_ref[0]

    ii = lax.broadcasted_iota(I32, (c, c), 0)
    jj = lax.broadcasted_iota(I32, (c, c), 1)
    dif = (ii - jj).astype(F32)
    tpos = lax.broadcasted_iota(I32, (c, 1), 0).astype(F32)
    mm = BF16 if c % 16 == 0 else F32
    for h in range(N_HEADS_D):
        lg = LOG_G[h]
        qs = slice(h * QK_DIM_D, (h + 1) * QK_DIM_D)
        vs = slice(h * V_DIM_D, (h + 1) * V_DIM_D)
        q = rq_ref[:, qs].astype(mm)
        k = rk_ref[:, qs]
        v = rv_ref[:, vs].astype(mm)
        intra = jnp.where(dif >= 0.0, jnp.exp(lg * jnp.maximum(dif, 0.0)), 0.0)
        att = _dot_nt(q, k.astype(mm)) * intra
        s = s_s[h]
        o = _dot(att.astype(mm), v) + _dot(q, s.astype(mm)) * jnp.exp(lg * (tpos + 1.0))
        kd = (k * jnp.exp(lg * (c - 1.0 - tpos))).astype(mm)
        s_s[h] = s * math.exp(lg * c) + _dot_tn(kd, v)
        mu = jnp.mean(o, axis=-1, keepdims=True)
        var = jnp.mean(jnp.square(o - mu), axis=-1, keepdims=True)
        y = (o - mu) * lax.rsqrt(var + EPS) * gn_ref[:, vs]
        dg = dg_ref[:, vs]
        o_ref[:, vs] = (dg * jax.nn.sigmoid(dg) * y).astype(o_ref.dtype)
    sn_ref[0] = s_s[...]


def _retention(rq, rk, rv, dg, s0, gn, *, c, v_col, g_col):
    t = rq.shape[0]
    B = s0.shape[0]
    nch = t // (B * c)
    W = N_HEADS_D * QK_DIM_D
    return pl.pallas_call(
        functools.partial(_ret_kernel, c=c),
        grid=(B, nch),
        in_specs=[pl.BlockSpec((c, W), lambda b, i: (b * nch + i, 0)),
                  pl.BlockSpec((c, W), lambda b, i: (b * nch + i, 0)),
                  pl.BlockSpec((c, BRANCH_W), lambda b, i: (b * nch + i, v_col)),
                  pl.BlockSpec((c, BRANCH_W), lambda b, i: (b * nch + i, g_col)),
                  pl.BlockSpec((1, N_HEADS_D, QK_DIM_D, V_DIM_D), lambda b, i: (b, 0, 0, 0)),
                  pl.BlockSpec((1, BRANCH_W), lambda b, i: (0, 0))],
        out_specs=[pl.BlockSpec((c, BRANCH_W), lambda b, i: (b * nch + i, 0)),
                   pl.BlockSpec((1, N_HEADS_D, QK_DIM_D, V_DIM_D), lambda b, i: (b, 0, 0, 0))],
        out_shape=[jax.ShapeDtypeStruct((t, BRANCH_W), BF16), jax.ShapeDtypeStruct(s0.shape, F32)],
        scratch_shapes=[pltpu.VMEM((N_HEADS_D, QK_DIM_D, V_DIM_D), F32)],
        compiler_params=_cparams("parallel", "arbitrary"),
    )(rq, rk, rv, dg, s0, gn)


def _sortable(x):
    b = lax.bitcast_convert_type(x, I32)
    return b ^ ((b >> 31) & 0x7FFFFFFF)


def _count(m):
    return jnp.sum(jnp.where(m, 1.0, 0.0), axis=1, keepdims=True)


def _kth_largest_key(keys, kf, bits):
    m = keys.shape[0]
    t0 = jnp.where(_count(keys >= 0) >= kf, 0, INT_MIN).astype(I32)

    def step(nb, shift, t):
        digit = jnp.zeros((m, 1), I32)
        for v in range(1, 2 ** nb):
            cand = t | jnp.left_shift(jnp.int32(v), shift)
            digit = digit + jnp.where(_count(keys >= cand) >= kf, 1, 0)
        return t | jnp.left_shift(digit, shift)

    nfull, rem = divmod(31, bits)
    t = lax.fori_loop(0, nfull, lambda i, t: step(bits, 31 - bits * (i + 1), t), t0)
    return step(rem, 0, t) if rem else t


def _topk_mask(keys, valid, idx, k, bits):
    m, n = keys.shape
    kf = float(k)
    thr = _kth_largest_key(keys, kf, bits)
    gt = keys > thr
    eq = (keys == thr) & valid
    need = kf - _count(gt)
    excess = jnp.max(_count(eq) - need) > 0.0
    nbits = max(1, (n - 1).bit_length())

    def first_ties():
        def idx_step(i, j):
            cand = j | jnp.left_shift(jnp.int32(1), nbits - 1 - i)
            return jnp.where(_count(eq & (idx < cand)) <= need - 1.0, cand, j)

        return lax.fori_loop(0, nbits, idx_step, jnp.zeros((m, 1), I32))

    jthr = lax.cond(excess, first_ties, lambda: jnp.full((m, 1), 2 ** nbits, I32))
    return (gt & valid) | (eq & (idx <= jthr))


def _masked_attention(s, mask, v):
    s = jnp.where(mask, s, -jnp.inf)
    p = jnp.exp(s - jnp.max(s, axis=1, keepdims=True))
    return _dot(p.astype(BF16), v) / jnp.sum(p, axis=1, keepdims=True)


def _dsa_prompt_kernel(q_ref, kb_ref, vb_ref, iq4_ref, ik4_ref, iw_ref, o_ref, *, j0, topk):
    nq = q_ref.shape[0]
    nk = kb_ref.shape[0]
    t0 = (j0 + pl.program_id(1)) * nq
    ik4 = ik4_ref[...]
    score = jnp.zeros((nq, nk), F32)
    for h in range(N_IDX_HEADS):
        s = _dot_nt(iq4_ref[:, h * IDX4:(h + 1) * IDX4], ik4)
        score = score + jnp.maximum(s, 0.0) * iw_ref[:, IDX_DIM + h:IDX_DIM + h + 1]
    score = score * IDX_SCALE
    kpos = lax.broadcasted_iota(I32, (nq, nk), 1)
    valid = kpos <= t0 + lax.broadcasted_iota(I32, (nq, nk), 0)
    keys = _sortable(jnp.where(valid, score, -jnp.inf))
    mask = _topk_mask(keys, valid, kpos, topk, 1)
    for g in range(N_KV_B):
        gs = slice(g * HEAD_DIM_B, (g + 1) * HEAD_DIM_B)
        kg = kb_ref[:, gs]
        vg = vb_ref[:, gs]
        for r in range(KV_REP):
            hs = slice((g * KV_REP + r) * HEAD_DIM_B, (g * KV_REP + r + 1) * HEAD_DIM_B)
            o_ref[:, hs] = _masked_attention(_dot_nt(q_ref[:, hs], kg), mask, vg).astype(o_ref.dtype)


def _dsa_prompt(q, kb, vb, iq4, ik4, ikw, *, n_seq, n_cls):
    t = q.shape[0]
    s_len = t // n_seq
    nb = s_len // ATT_BLOCK
    nbc = nb // n_cls
    topk = min(TOPK_MAX, s_len // 4)
    q, kb, vb, iq4, ik4, ikw = (a.reshape(n_seq, s_len, a.shape[-1]) for a in (q, kb, vb, iq4, ik4, ikw))
    outs = []
    for c in range(n_cls):
        nk = (c + 1) * nbc * ATT_BLOCK
        qrow = lambda w, c=c: pl.BlockSpec((None, ATT_BLOCK, w), lambda b, j: (b, c * nbc + j, 0))
        seq = lambda w, nk=nk: pl.BlockSpec((None, nk, w), lambda b, j: (b, 0, 0))
        outs.append(pl.pallas_call(
            functools.partial(_dsa_prompt_kernel, j0=c * nbc, topk=topk),
            grid=(n_seq, nbc),
            in_specs=[qrow(BRANCH_W), seq(N_KV_B * HEAD_DIM_B), seq(N_KV_B * HEAD_DIM_B), qrow(N_IDX_HEADS * IDX4),
                      seq(IDX4), qrow(LANES)],
            out_specs=pl.BlockSpec((None, ATT_BLOCK, BRANCH_W), lambda b, j: (b, j, 0)),
            out_shape=jax.ShapeDtypeStruct((n_seq, nbc * ATT_BLOCK, BRANCH_W), BF16),
            compiler_params=_cparams("parallel", "arbitrary"),
        )(q, kb, vb, iq4, ik4, ikw))
    return jnp.concatenate(outs, axis=1).reshape(t, BRANCH_W)


def _dsa_sample_kernel(pt_ref, qg_ref, iq4_ref, w_ref, kn_ref, vn_ref, ikn_ref, *rest, n_pages, t_new):
    del pt_ref
    idx_refs = rest[:n_pages]
    k_refs = rest[n_pages:2 * n_pages]
    v_refs = rest[2 * n_pages:3 * n_pages]
    o_ref, sc_s, k_s, v_s = rest[3 * n_pages:]
    P = PAGE_SIZE
    iq4 = iq4_ref[0]
    w = w_ref[0]
    pad = P - t_new

    def page_scores(ik, page):
        hi, lo = _split_bf16(ik)
        ik4 = jnp.concatenate([hi, lo, hi, lo], axis=1)
        r = jnp.maximum(_dot_nt(iq4, ik4), 0.0) * w
        sc = r[0:t_new]
        for h in range(1, N_IDX_HEADS):
            sc = sc + r[h * t_new:(h + 1) * t_new]
        sc_s[:, page * P:(page + 1) * P] = sc * IDX_SCALE

    for p in range(n_pages):
        page_scores(idx_refs[p][0, 0], p)
        for g in range(N_KV_B):
            gs = slice(g * HEAD_DIM_B, (g + 1) * HEAD_DIM_B)
            k_s[p * P:(p + 1) * P, gs] = k_refs[p][0, 0, pl.ds(g, P, stride=N_KV_B), :].astype(BF16)
            v_s[p * P:(p + 1) * P, gs] = v_refs[p][0, 0, pl.ds(g, P, stride=N_KV_B), :].astype(BF16)
    page_scores(jnp.concatenate([ikn_ref[0], jnp.zeros((pad, IDX_DIM), F32)], axis=0), n_pages)
    zkv = jnp.zeros((pad, N_KV_B * HEAD_DIM_B), F32)
    k_s[n_pages * P:(n_pages + 1) * P, :] = jnp.concatenate([kn_ref[0], zkv], axis=0).astype(BF16)
    v_s[n_pages * P:(n_pages + 1) * P, :] = jnp.concatenate([vn_ref[0], zkv], axis=0).astype(BF16)

    nk = (n_pages + 1) * P
    past = n_pages * P
    kpos = lax.broadcasted_iota(I32, (t_new, nk), 1)
    valid = kpos <= past + lax.broadcasted_iota(I32, (t_new, nk), 0)
    keys = _sortable(jnp.where(valid, sc_s[...], -jnp.inf))
    mask = _topk_mask(keys, valid, kpos, min(TOPK_MAX, (past + t_new) // 4), 3)
    mask_i = jnp.where(mask, 1, 0).astype(I32)
    mask = jnp.concatenate([mask_i] * KV_REP, axis=0) > 0
    for g in range(N_KV_B):
        gs = slice(g * HEAD_DIM_B, (g + 1) * HEAD_DIM_B)
        o_ref[0, g] = _masked_attention(_dot_nt(qg_ref[0, g], k_s[:, gs]), mask, v_s[:, gs]).astype(o_ref.dtype)


def _dsa_sample(qg, iq4r, wr, k_new, v_new, ik_new, cache_k, cache_v, cache_idx_k, page_table, layer):
    B, _, rows, hd = qg.shape
    t_new = k_new.shape[1]
    n_pages = page_table.shape[1]
    kvw = N_KV_B * HEAD_DIM_B
    per_seq = lambda shape: pl.BlockSpec((1,) + shape, lambda b, pt: (b,) + (0,) * len(shape))

    def page_spec(rows_, width, p):
        return pl.BlockSpec((1, 1, rows_, width), lambda b, pt: (layer, pt[b, p], 0, 0))

    in_specs = [per_seq(qg.shape[1:]), per_seq(iq4r.shape[1:]), per_seq(wr.shape[1:]), per_seq((t_new, kvw)),
                per_seq((t_new, kvw)), per_seq((t_new, IDX_DIM))]
    in_specs += [page_spec(PAGE_SIZE, IDX_DIM, p) for p in range(n_pages)]
    in_specs += [page_spec(PAGE_SIZE * N_KV_B, HEAD_DIM_B, p) for p in range(n_pages)]
    in_specs += [page_spec(PAGE_SIZE * N_KV_B, HEAD_DIM_B, p) for p in range(n_pages)]
    nk = (n_pages + 1) * PAGE_SIZE
    grid_spec = pltpu.PrefetchScalarGridSpec(
        num_scalar_prefetch=1, grid=(B,), in_specs=in_specs,
        out_specs=pl.BlockSpec((1,) + qg.shape[1:], lambda b, pt: (b, 0, 0, 0)),
        scratch_shapes=[pltpu.VMEM((t_new, nk), F32), pltpu.VMEM((nk, kvw), BF16), pltpu.VMEM((nk, kvw), BF16)])
    return pl.pallas_call(
        functools.partial(_dsa_sample_kernel, n_pages=n_pages, t_new=t_new),
        grid_spec=grid_spec,
        out_shape=jax.ShapeDtypeStruct(qg.shape, BF16),
        compiler_params=_cparams("arbitrary"),
    )(page_table, qg, iq4r, wr, k_new, v_new, ik_new, *([cache_idx_k] * n_pages), *([cache_k] * n_pages),
      *([cache_v] * n_pages))


def _merge_kernel(xn_ref, a_ref, b_ref, c_ref, d_ref, wg0, wg1, wg2, wg3, bg0, bg1, bg2, bg3, wb_ref, o_ref):
    xn = xn_ref[...]
    acc = None
    for n, (br, wg, bg) in enumerate(zip((a_ref, b_ref, c_ref, d_ref), (wg0, wg1, wg2, wg3), (bg0, bg1, bg2, bg3))):
        term = jax.nn.sigmoid(_dot(xn, wg[...]) + bg[...]) * _dot(br[...], wb_ref[n])
        acc = term if acc is None else acc + term
    o_ref[...] = acc.astype(o_ref.dtype)


def _merge(xn, branches, w_gate, b_gate, w_branch, tm, tn):
    t, d = xn.shape
    nj = d // tn
    row = lambda w: pl.BlockSpec((tm, w), lambda i, j: (i, 0))
    wg = [pl.BlockSpec((d, tn), functools.partial(lambda i, j, n: (0, n * nj + j), n=n)) for n in range(N_BRANCH)]
    bg = [pl.BlockSpec((1, tn), functools.partial(lambda i, j, n: (0, n * nj + j), n=n)) for n in range(N_BRANCH)]
    return pl.pallas_call(
        _merge_kernel,
        grid=(t // tm, nj),
        in_specs=[row(d)] + [row(BRANCH_W)] * N_BRANCH + wg + bg
                 + [pl.BlockSpec((N_BRANCH, BRANCH_W, tn), lambda i, j: (0, 0, j))],
        out_specs=pl.BlockSpec((tm, tn), lambda i, j: (i, j)),
        out_shape=jax.ShapeDtypeStruct((t, d), BF16),
        compiler_params=_cparams("parallel", "arbitrary"),
    )(xn, *branches, *([w_gate] * N_BRANCH), *([b_gate] * N_BRANCH), w_branch)


def _outproj_kernel(m_ref, x_ref, w_ref, g_ref, o_ref):
    o_ref[...] = x_ref[...] + _rms(_dot(m_ref[...], w_ref[...]), g_ref[...])


def _outproj(merged, x, w_o, g, tm):
    t, d = x.shape
    return pl.pallas_call(
        _outproj_kernel,
        grid=(t // tm,),
        in_specs=[pl.BlockSpec((tm, d), lambda i: (i, 0)), pl.BlockSpec((tm, d), lambda i: (i, 0)),
                  pl.BlockSpec((d, d), lambda i: (0, 0)), pl.BlockSpec((1, d), lambda i: (0, 0))],
        out_specs=pl.BlockSpec((tm, d), lambda i: (i, 0)),
        out_shape=jax.ShapeDtypeStruct((t, d), F32),
        compiler_params=_cparams("parallel"),
    )(merged, x, w_o, g)


def _ffn_kernel(x_ref, gpre_ref, wug_ref, wuv_ref, wcg_ref, wcv_ref, bcg_ref, bcv_ref, wd_ref, gpost_ref,
                stg_ref, stv_ref, o_ref, ng_ref, nv_ref, h_s, acc_s, cg_s, cv_s, ext_s, *, S, tiles_per_group):
    i = pl.program_id(0)
    j = pl.program_id(1)
    tm = x_ref.shape[0]
    CR = cg_s.shape[1]

    @pl.when(j == 0)
    def _():
        h_s[...] = _rms(x_ref[...], gpre_ref[...]).astype(BF16)
        acc_s[...] = jnp.zeros_like(acc_s)

    @pl.when(i % tiles_per_group == 0)
    def _():
        cg_s[j] = stg_ref[0]
        cv_s[j] = stv_ref[0]

    h = h_s[...]

    def conv(w_ref, wc_ref, bc_ref, c_s, new_ref):
        u = _dot(h, w_ref[...])
        ext_s[0:CR, :] = c_s[j]
        ext_s[CR:CR + tm, :] = u
        y = bc_ref[...] + u * wc_ref[2:3, :]
        y = y + ext_s[CR - S:CR - S + tm, :] * wc_ref[1:2, :]
        y = y + ext_s[CR - 2 * S:CR - 2 * S + tm, :] * wc_ref[0:1, :]
        tail = ext_s[tm:tm + CR, :]
        c_s[j] = tail
        new_ref[0] = tail
        return y

    gate = conv(wug_ref, wcg_ref, bcg_ref, cg_s, ng_ref)
    val = conv(wuv_ref, wcv_ref, bcv_ref, cv_s, nv_ref)
    acc_s[...] += _dot((_gelu_tanh(gate) * val).astype(BF16), wd_ref[...])

    @pl.when(j == pl.num_programs(1) - 1)
    def _():
        o_ref[...] = x_ref[...] + _rms(acc_s[...], gpost_ref[...])


def _ffn(x, lp, state, *, S, tm, tn):
    t, d = x.shape
    G, CR, _ = state.shape
    nj = D_FF // tn
    ni = t // tm
    tiles_per_group = ni // G
    grp = lambda i, j: (i // tiles_per_group, 0, j)
    tile = lambda i, j: (i, 0, j)
    grp_v = lambda i, j: (i // tiles_per_group, 0, nj + j)
    col = lambda r: pl.BlockSpec((r, tn), lambda i, j: (0, j))
    col_v = lambda r: pl.BlockSpec((r, tn), lambda i, j: (0, nj + j))
    vec = pl.BlockSpec((1, d), lambda i, j: (0, 0))
    xo, ng, nv = pl.pallas_call(
        functools.partial(_ffn_kernel, S=S, tiles_per_group=tiles_per_group),
        grid=(ni, nj),
        in_specs=[pl.BlockSpec((tm, d), lambda i, j: (i, 0)), vec, col(d), col_v(d),
                  col(FFN_CONV_W), col_v(FFN_CONV_W), col(1), col_v(1),
                  pl.BlockSpec((tn, d), lambda i, j: (j, 0)), vec,
                  pl.BlockSpec((1, CR, tn), grp), pl.BlockSpec((1, CR, tn), grp_v)],
        out_specs=[pl.BlockSpec((tm, d), lambda i, j: (i, 0)),
                   pl.BlockSpec((1, CR, tn), tile), pl.BlockSpec((1, CR, tn), tile)],
        out_shape=[jax.ShapeDtypeStruct((t, d), F32), jax.ShapeDtypeStruct((ni, CR, D_FF), F32),
                   jax.ShapeDtypeStruct((ni, CR, D_FF), F32)],
        scratch_shapes=[pltpu.VMEM((tm, d), BF16), pltpu.VMEM((tm, d), F32), pltpu.VMEM((nj, CR, tn), F32),
                        pltpu.VMEM((nj, CR, tn), F32), pltpu.VMEM((CR + tm, tn), F32)],
        compiler_params=_cparams("arbitrary", "arbitrary"),
    )(x, lp["g_ffn_pre"], lp["w_up"], lp["w_up"], lp["w_ffn_conv"], lp["w_ffn_conv"], lp["b_ffn_conv"],
      lp["b_ffn_conv"], lp["w_down"], lp["g_ffn_post"], state, state)
    last = slice(tiles_per_group - 1, None, tiles_per_group)
    return xo, ng[last], nv[last]


def _layer_front(x, lp, tabs, *, tm):
    u_main, xn = _inproj(x, lp["g_mix_pre"], lp["w_in_main"], tm, 512)
    u_idx = _inproj_split(x, lp["g_mix_pre"], lp["w_in_idx_hi"], lp["w_in_idx_lo"], tm, 384)
    return u_main, xn, _rope_prep(u_main, u_idx, tabs, 256)


def _layer_back(x, xn, branches, lp, ffn_state, *, S, tm, tn_ffn):
    merged = _merge(xn, branches, lp["w_gate"], lp["b_gate"], lp["w_branch"], tm, 256)
    x1 = _outproj(merged, x, lp["w_o"], lp["g_mix_post"], tm)
    return _ffn(x1, lp, ffn_state, S=S, tm=tm, tn=tn_ffn)


def _prep_layer_params(p, l):
    w_in = p["w_in"][l]
    row = lambda a: a[l][None, :]
    o_iq = 3072
    o_cx = o_iq + N_IDX_HEADS * IDX_DIM + IDX_DIM + N_IDX_HEADS
    w_idx = jnp.pad(w_in[:, o_iq:o_cx], ((0, 0), (0, N_IDX - (o_cx - o_iq))))
    hi = w_idx.astype(BF16)
    return dict(
        g_mix_pre=row(p["g_mix_pre"]),
        w_in_main=jnp.concatenate([w_in[:, :o_iq], w_in[:, o_cx:]], axis=1).astype(BF16),
        w_in_idx_hi=hi, w_in_idx_lo=(w_idx - hi.astype(F32)).astype(BF16),
        w_pool=p["w_pool"][l].astype(BF16), pool_scale=row(p["pool_scale"]),
        w_conv_c=p["w_conv_c"][l], b_conv_c=row(p["b_conv_c"]),
        w_rg_a=p["w_rg_a"][l].astype(BF16), b_rg_a=row(p["b_rg_a"]),
        w_rg_x=p["w_rg_x"][l].astype(BF16), b_rg_x=row(p["b_rg_x"]),
        lru_lambda=row(p["lru_lambda"]), ret_gn=row(p["ret_gn"]),
        w_branch=p["w_branch"][l].astype(BF16), w_gate=p["w_gate"][l].astype(BF16), b_gate=row(p["b_gate"]),
        w_o=p["w_o"][l].astype(BF16), g_mix_post=row(p["g_mix_post"]), g_ffn_pre=row(p["g_ffn_pre"]),
        w_up=p["w_up"][l].astype(BF16), w_ffn_conv=p["w_ffn_conv"][l], b_ffn_conv=row(p["b_ffn_conv"]),
        w_down=p["w_down"][l].astype(BF16), g_ffn_post=row(p["g_ffn_post"]))


def _to_time_major(a, nb, nt):
    return jnp.swapaxes(a, 0, 1).reshape((nt * nb,) + a.shape[2:])


def _to_seq_major(a, nb, nt):
    return jnp.swapaxes(a.reshape((nt, nb) + a.shape[1:]), 0, 1)


def kernel(x_prompt, x_sample, cache_k, cache_v, cache_idx_k, state_pool, state_conv, state_rglru, state_ret, state_ffn_conv, page_table, g_mix_pre, w_in, w_pool, pool_scale, w_conv_c, b_conv_c, w_rg_a, b_rg_a, w_rg_x, b_rg_x, lru_lambda, ret_gn, w_branch, w_gate, b_gate, w_o, g_mix_post, g_ffn_pre, w_up, w_ffn_conv, b_ffn_conv, w_down, g_ffn_post):
    params = dict(g_mix_pre=g_mix_pre, w_in=w_in, w_pool=w_pool, pool_scale=pool_scale, w_conv_c=w_conv_c,
                  b_conv_c=b_conv_c, w_rg_a=w_rg_a, b_rg_a=b_rg_a, w_rg_x=w_rg_x, b_rg_x=b_rg_x,
                  lru_lambda=lru_lambda, ret_gn=ret_gn, w_branch=w_branch, w_gate=w_gate, b_gate=b_gate, w_o=w_o,
                  g_mix_post=g_mix_post, g_ffn_pre=g_ffn_pre, w_up=w_up, w_ffn_conv=w_ffn_conv,
                  b_ffn_conv=b_ffn_conv, w_down=w_down, g_ffn_post=g_ffn_post)
    depth = w_in.shape[0]
    bp, seq, d = x_prompt.shape
    bs, tdec, _ = x_sample.shape
    past = page_table.shape[1] * PAGE_SIZE
    n_pool = cache_k.shape[1]
    ck = cache_k.reshape(depth, n_pool, PAGE_SIZE * N_KV_B, HEAD_DIM_B)
    cv = cache_v.reshape(depth, n_pool, PAGE_SIZE * N_KV_B, HEAD_DIM_B)

    tabs_p = _rope_tables(jnp.arange(seq, dtype=F32))
    tabs_s = _rope_tables(jnp.repeat(past + jnp.arange(tdec, dtype=F32), bs))
    xp = x_prompt.reshape(bp * seq, d)
    xs = _to_time_major(x_sample, bs, tdec)
    cr_p = max(SUBLANES, (FFN_CONV_W - 1))
    outs_p, outs_s = [], []
    for l in range(depth):
        lp = _prep_layer_params(params, l)
        u, xn, (q, k, kb, vb, iq4, ik4, ikw, rq, rk) = _layer_front(xp, lp, tabs_p, tm=1024)
        o_a = _pool(u, jnp.zeros((bp, POOL_BUF, BRANCH_W), F32), lp["w_pool"], lp["pool_scale"], S=1, Tc=256, start=0)
        o_b = _dsa_prompt(q, kb, vb, iq4, ik4, ikw, n_seq=bp, n_cls=8)
        o_c, h_p = _rglru(u, jnp.zeros((bp, CONV_W - 1, LRU_W), F32), jnp.zeros((bp, 1, LRU_W), F32), lp, S=1, Tc=256)
        o_d, s_p = _retention(rq, rk, u, u, jnp.zeros((bp, N_HEADS_D, QK_DIM_D, V_DIM_D), F32), lp["ret_gn"],
                              c=RET_CHUNK if seq % RET_CHUNK == 0 else seq, v_col=6, g_col=7)
        xp, ng, nv = _layer_back(xp, xn, (o_a, o_b, o_c, o_d), lp, jnp.zeros((bp, cr_p, 2 * D_FF), F32),
                                 S=1, tm=512, tn_ffn=512)
        u3 = u.reshape(bp, seq, N_MAIN)
        outs_p.append((
            k.reshape(bp, seq, N_KV_B, HEAD_DIM_B), u3[:, :, 2560:3072].reshape(bp, seq, N_KV_B, HEAD_DIM_B),
            ikw.reshape(bp, seq, LANES)[:, :, :IDX_DIM], u3[:, seq - POOL_BUF:, 0:BRANCH_W],
            u3[:, seq - (CONV_W - 1):, 3072:3072 + LRU_W], h_p[:, 0], s_p,
            jnp.concatenate([ng, nv], axis=-1)[:, cr_p - (FFN_CONV_W - 1):]))
        u, xn, (q, k, kb, vb, iq4, ik4, ikw, rq, rk) = _layer_front(xs, lp, tabs_s, tm=1024)
        o_a = _pool(u, _to_time_major(state_pool[l], bs, POOL_BUF)[None], lp["w_pool"], lp["pool_scale"],
                    S=bs, Tc=tdec, start=past)
        o_c, h_s = _rglru(u, _to_time_major(state_conv[l], bs, CONV_W - 1)[None], state_rglru[l][None], lp,
                          S=bs, Tc=tdec)
        sm = lambda a: _to_seq_major(a, bs, tdec)
        u_sm = sm(u)
        o_d, s_s = _retention(sm(rq).reshape(bs * tdec, -1), sm(rk).reshape(bs * tdec, -1),
                              u_sm[:, :, 6144:7168].reshape(bs * tdec, -1), u_sm[:, :, 7168:8192].reshape(bs * tdec, -1),
                              state_ret[l], lp["ret_gn"], c=tdec, v_col=0, g_col=0)
        o_d = _to_time_major(o_d.reshape(bs, tdec, BRANCH_W), bs, tdec)
        qg = sm(q).reshape(bs, tdec, N_KV_B, KV_REP, HEAD_DIM_B).transpose(0, 2, 3, 1, 4)
        qg = qg.reshape(bs, N_KV_B, KV_REP * tdec, HEAD_DIM_B)
        iq4r = sm(iq4).reshape(bs, tdec, N_IDX_HEADS, IDX4).transpose(0, 2, 1, 3).reshape(bs, N_IDX_HEADS * tdec, IDX4)
        ikw_sm = sm(ikw)
        wr = ikw_sm[:, :, IDX_DIM:IDX_DIM + N_IDX_HEADS].transpose(0, 2, 1).reshape(bs, N_IDX_HEADS * tdec, 1)
        k_sm = sm(k)
        v_sm = u_sm[:, :, 2560:3072]
        o_b = _dsa_sample(qg, iq4r, wr, k_sm, v_sm, ikw_sm[:, :, :IDX_DIM], ck, cv, cache_idx_k, page_table, l)
        o_b = o_b.reshape(bs, N_KV_B, KV_REP, tdec, HEAD_DIM_B).transpose(3, 0, 1, 2, 4).reshape(tdec * bs, BRANCH_W)
        ffn_state = _to_time_major(state_ffn_conv[l], bs, FFN_CONV_W - 1)[None]
        xs, ng, nv = _layer_back(xs, xn, (o_a, o_b, o_c, o_d), lp, ffn_state, S=bs, tm=512, tn_ffn=256)
        ffn_new = _to_seq_major(jnp.concatenate([ng, nv], axis=-1)[0], bs, FFN_CONV_W - 1)
        a_in = u_sm[:, :, 0:BRANCH_W]
        c_x = u_sm[:, :, 3072:3072 + LRU_W]
        outs_s.append((
            k_sm.reshape(bs, tdec, N_KV_B, HEAD_DIM_B), v_sm.reshape(bs, tdec, N_KV_B, HEAD_DIM_B),
            ikw_sm[:, :, :IDX_DIM],
            jnp.concatenate([state_pool[l], a_in], axis=1)[:, tdec:],
            jnp.concatenate([state_conv[l], c_x], axis=1)[:, tdec:],
            h_s[0], s_s, ffn_new))

    stk = lambda outs, i: jnp.stack([o[i] for o in outs], axis=0)
    res = [xp.reshape(bp, seq, d), _to_seq_major(xs, bs, tdec)]
    for i in range(8):
        res += [stk(outs_p, i), stk(outs_s, i)]
    return tuple(res)
```

```python
import functools
import math

import jax
import jax.numpy as jnp
from jax import lax
from jax.experimental import pallas as pl
from jax.experimental.pallas import tpu as pltpu

F32 = jnp.float32
BF16 = jnp.bfloat16
I32 = jnp.int32

SUBLANES = 8
LANES = 128
VMEM_LIMIT_BYTES = 56 * 1024 * 1024

D_MODEL = 2048
BRANCH_W = D_MODEL // 2
N_BRANCH = 4
POOL_WINDOWS = (2, 4, 8, 16)
POOL_GROUP = BRANCH_W // len(POOL_WINDOWS)
POOL_BUF = max(POOL_WINDOWS) - 1
N_HEADS_B = 8
HEAD_DIM_B = BRANCH_W // N_HEADS_B
N_KV_B = 4
KV_REP = N_HEADS_B // N_KV_B
N_IDX_HEADS = 16
IDX_DIM = 64
TOPK_MAX = 256
ATT_BLOCK = 128
ROPE_THETA = 10000.0
ATT_SCALE = HEAD_DIM_B ** -0.5
IDX_SCALE = (IDX_DIM * N_IDX_HEADS) ** -0.5
LRU_W = BRANCH_W
LRU_BLOCKS = 4
LRU_BLOCK = LRU_W // LRU_BLOCKS
LRU_C = 8.0
CONV_W = 4
N_HEADS_D = 8
QK_DIM_D = BRANCH_W // (2 * N_HEADS_D)
V_DIM_D = BRANCH_W // N_HEADS_D
RET_CHUNK = 128
D_FF = 11 * D_MODEL // 4
FFN_CONV_W = 3
EPS = 1e-6
PAGE_SIZE = 128

N_MAIN = 8192
N_IDX = 1152
IDX4 = 4 * IDX_DIM

LOG_G = tuple(math.log1p(-(2.0 ** (-5.0 - h))) for h in range(N_HEADS_D))
INT_MIN = -2 ** 31


def _round_up(n, m):
    return (n + m - 1) // m * m


def _cparams(*sem):
    return pltpu.CompilerParams(dimension_semantics=sem, vmem_limit_bytes=VMEM_LIMIT_BYTES)


def _dot(a, b):
    return jnp.dot(a, b, preferred_element_type=F32)


def _dot_nt(a, b):
    return lax.dot_general(a, b, (((1,), (1,)), ((), ())), preferred_element_type=F32)


def _dot_tn(a, b):
    return lax.dot_general(a, b, (((0,), (0,)), ((), ())), preferred_element_type=F32)


def _rms(x, g):
    return x * lax.rsqrt(jnp.mean(x * x, axis=-1, keepdims=True) + EPS) * g


def _gelu_tanh(x):
    return x * (0.5 * (1.0 + jnp.tanh(0.7978845608028654 * (x + 0.044715 * (x * x * x)))))


def _split_bf16(x):
    hi = x.astype(BF16)
    lo = (x - hi.astype(F32)).astype(BF16)
    return hi, lo


def _inproj_kernel(x_ref, g_ref, w_ref, u_ref, xn_ref):
    @pl.when(pl.program_id(1) == 0)
    def _():
        xn_ref[...] = _rms(x_ref[...], g_ref[...]).astype(BF16)

    u_ref[...] = _dot(xn_ref[...], w_ref[...])


def _inproj(x, g, w, l, tm, tn):
    t, d = x.shape
    n = w.shape[2]
    return pl.pallas_call(
        _inproj_kernel,
        grid=(t // tm, n // tn),
        in_specs=[pl.BlockSpec((tm, d), lambda i, j: (i, 0)),
                  pl.BlockSpec((1, d), lambda i, j: (0, 0)),
                  pl.BlockSpec((None, d, tn), lambda i, j: (l, 0, j))],
        out_specs=[pl.BlockSpec((tm, tn), lambda i, j: (i, j)),
                   pl.BlockSpec((tm, d), lambda i, j: (i, 0))],
        out_shape=[jax.ShapeDtypeStruct((t, n), F32), jax.ShapeDtypeStruct((t, d), BF16)],
        compiler_params=_cparams("parallel", "arbitrary"),
    )(x, g, w)


def _inproj_split_kernel(x_ref, g_ref, wh_ref, wl_ref, u_ref, xh_s, xl_s):
    @pl.when(pl.program_id(1) == 0)
    def _():
        hi, lo = _split_bf16(_rms(x_ref[...], g_ref[...]))
        xh_s[...] = hi
        xl_s[...] = lo

    xh = xh_s[...]
    u_ref[...] = _dot(xh, wh_ref[...]) + (_dot(xh, wl_ref[...]) + _dot(xl_s[...], wh_ref[...]))


def _inproj_split(x, g, wh, wl, l, tm, tn):
    t, d = x.shape
    n = wh.shape[2]
    return pl.pallas_call(
        _inproj_split_kernel,
        grid=(t // tm, n // tn),
        in_specs=[pl.BlockSpec((tm, d), lambda i, j: (i, 0)),
                  pl.BlockSpec((1, d), lambda i, j: (0, 0)),
                  pl.BlockSpec((None, d, tn), lambda i, j: (l, 0, j)),
                  pl.BlockSpec((None, d, tn), lambda i, j: (l, 0, j))],
        out_specs=pl.BlockSpec((tm, tn), lambda i, j: (i, j)),
        out_shape=jax.ShapeDtypeStruct((t, n), F32),
        scratch_shapes=[pltpu.VMEM((tm, d), BF16), pltpu.VMEM((tm, d), BF16)],
        compiler_params=_cparams("parallel", "arbitrary"),
    )(x, g, wh, wl)


def _rope128(xs, cos, sin):
    return xs * cos + pltpu.roll(xs, HEAD_DIM_B // 2, axis=1) * sin


def _rope64(xs, cos, sin, first_half):
    rot = jnp.where(first_half, pltpu.roll(xs, LANES - IDX_DIM // 2, axis=1), pltpu.roll(xs, IDX_DIM // 2, axis=1))
    return xs * cos + rot * sin


def _rope_kernel(bq_ref, bk_ref, bv_ref, dq_ref, dk_ref, ui_ref, c128_ref, s128_ref, c64_ref, s64_ref,
                 q_ref, k_ref, kb_ref, vb_ref, iq4_ref, ik4_ref, ikw_ref, rq_ref, rk_ref):
    rows = bq_ref.shape[0]
    c128, s128, c64, s64 = c128_ref[...], s128_ref[...], c64_ref[...], s64_ref[...]
    lane = lax.broadcasted_iota(I32, (rows, LANES), 1)
    first_half = (lane & (IDX_DIM // 2)) == 0
    low = lane < IDX_DIM
    for c in range(N_HEADS_B):
        sl = slice(c * LANES, (c + 1) * LANES)
        q_ref[:, sl] = (_rope128(bq_ref[:, sl], c128, s128) * ATT_SCALE).astype(BF16)
    for c in range(N_KV_B):
        sl = slice(c * LANES, (c + 1) * LANES)
        kr = _rope128(bk_ref[:, sl], c128, s128)
        k_ref[:, sl] = kr
        kb_ref[:, sl] = kr.astype(BF16)
    vb_ref[...] = bv_ref[...].astype(BF16)
    for c in range(N_HEADS_D * QK_DIM_D // LANES):
        sl = slice(c * LANES, (c + 1) * LANES)
        rq_ref[:, sl] = _rope64(dq_ref[:, sl], c64, s64, first_half)
        rk_ref[:, sl] = _rope64(dk_ref[:, sl], c64, s64, first_half) * (QK_DIM_D ** -0.5)
    for c in range(N_IDX_HEADS // 2):
        y = _rope64(ui_ref[:, c * LANES:(c + 1) * LANES], c64, s64, first_half)
        yr = pltpu.roll(y, IDX_DIM, axis=1)
        for hh, dup in enumerate((jnp.where(low, y, yr), jnp.where(low, yr, y))):
            hi, lo = _split_bf16(dup)
            base = (2 * c + hh) * IDX4
            iq4_ref[:, base:base + LANES] = hi
            iq4_ref[:, base + LANES:base + 2 * LANES] = lo
    raw = ui_ref[:, N_IDX_HEADS * IDX_DIM:N_IDX_HEADS * IDX_DIM + LANES]
    y = _rope64(raw, c64, s64, first_half)
    ikw_ref[...] = jnp.where(low, y, raw)
    dup = jnp.where(low, y, pltpu.roll(y, IDX_DIM, axis=1))
    hi = dup.astype(BF16).astype(F32)
    hilo = jnp.where(low, hi, dup - hi).astype(BF16)
    ik4_ref[:, 0:LANES] = hilo
    ik4_ref[:, LANES:2 * LANES] = hilo


def _rope_prep(u_main, u_idx, tabs, tr):
    t = u_main.shape[0]
    nt = tabs[0].shape[0] // tr
    row = lambda w, c: pl.BlockSpec((tr, w), lambda i: (i, c))
    tab = pl.BlockSpec((tr, LANES), lambda i: (i % nt, 0))
    outs = [(BRANCH_W, BF16), (N_KV_B * HEAD_DIM_B, F32), (N_KV_B * HEAD_DIM_B, BF16), (N_KV_B * HEAD_DIM_B, BF16),
            (N_IDX_HEADS * IDX4, BF16), (IDX4, BF16), (LANES, F32), (N_HEADS_D * QK_DIM_D, F32),
            (N_HEADS_D * QK_DIM_D, F32)]
    return pl.pallas_call(
        _rope_kernel,
        grid=(t // tr,),
        in_specs=[row(1024, 1), row(512, 4), row(512, 5), row(512, 10), row(512, 11), row(N_IDX, 0),
                  tab, tab, tab, tab],
        out_specs=[row(w, 0) for w, _ in outs],
        out_shape=[jax.ShapeDtypeStruct((t, w), dt) for w, dt in outs],
        compiler_params=_cparams("parallel"),
    )(u_main, u_main, u_main, u_main, u_main, u_idx, *tabs)


def _rope_tables(pos):
    def tab(half, reps):
        inv = jnp.exp(-math.log(ROPE_THETA) * jnp.arange(half, dtype=F32) / half)
        ang = pos[:, None] * inv[None, :]
        cos, sin = jnp.cos(ang), jnp.sin(ang)
        return jnp.tile(jnp.concatenate([cos, cos], 1), (1, reps)), jnp.tile(jnp.concatenate([-sin, sin], 1), (1, reps))

    c128, s128 = tab(HEAD_DIM_B // 2, 1)
    c64, s64 = tab(IDX_DIM // 2, 2)
    return c128, s128, c64, s64


def _pool_kernel(a_ref, prev_ref, w_ref, sc_ref, o_ref, ext_s, *, S, Tc, start, nch):
    ch = pl.program_id(1)
    R = Tc * S
    PS = POOL_BUF * S
    OFF = _round_up(PS, SUBLANES)

    @pl.when(ch == 0)
    def _():
        ext_s[OFF - PS:OFF, :] = prev_ref[0]

    x = a_ref[...]
    ext_s[OFF:OFF + R, :] = x
    t_loc = lax.broadcasted_iota(I32, (R, 1), 0) // S if S > 1 else lax.broadcasted_iota(I32, (R, 1), 0)
    pos1 = start + ch * Tc + t_loc + 1
    for gi, w in enumerate(POOL_WINDOWS):
        sl = slice(gi * POOL_GROUP, (gi + 1) * POOL_GROUP)
        xs = x[:, sl]
        acc = xs
        for j in range(1, w):
            acc = acc + ext_s[OFF - j * S:OFF - j * S + R, sl]
        cnt = jnp.minimum(w, pos1).astype(F32)
        mixed = acc / cnt - xs
        y = _dot(mixed.astype(BF16), w_ref[gi])
        o_ref[:, sl] = (y * sc_ref[:, sl]).astype(o_ref.dtype)
    if nch > 1:
        ext_s[OFF - PS:OFF, :] = ext_s[OFF + R - PS:OFF + R, :]


def _pool(u_main, prev, w_pool, scale, *, S, Tc, start):
    t = u_main.shape[0]
    R = Tc * S
    G = prev.shape[0]
    nch = t // (G * R)
    PS = POOL_BUF * S
    C = BRANCH_W
    return pl.pallas_call(
        functools.partial(_pool_kernel, S=S, Tc=Tc, start=start, nch=nch),
        grid=(G, nch),
        in_specs=[pl.BlockSpec((R, C), lambda g, c: (g * nch + c, 0)),
                  pl.BlockSpec((1, PS, C), lambda g, c: (g, 0, 0)),
                  pl.BlockSpec((len(POOL_WINDOWS), POOL_GROUP, POOL_GROUP), lambda g, c: (0, 0, 0)),
                  pl.BlockSpec((1, C), lambda g, c: (0, 0))],
        out_specs=pl.BlockSpec((R, C), lambda g, c: (g * nch + c, 0)),
        out_shape=jax.ShapeDtypeStruct((t, C), BF16),
        scratch_shapes=[pltpu.VMEM((_round_up(PS, SUBLANES) + R, C), F32)],
        compiler_params=_cparams("parallel", "arbitrary"),
    )(u_main, prev, w_pool, scale)


def _rglru_kernel(cx_ref, cg_ref, prev_ref, h0_ref, wc_ref, bc_ref, wa_ref, ba_ref, wx_ref, bx_ref, lam_ref,
                  o_ref, hl_ref, ext_s, a_s, b_s, h_s, *, S, Tc, nch):
    ch = pl.program_id(1)
    R = Tc * S
    PS = (CONV_W - 1) * S
    OFF = _round_up(PS, SUBLANES)

    @pl.when(ch == 0)
    def _():
        ext_s[OFF - PS:OFF, :] = prev_ref[0]
        h_s[...] = h0_ref[0]

    x = cx_ref[...]
    ext_s[OFF:OFF + R, :] = x
    xc = bc_ref[...] + x * wc_ref[CONV_W - 1:CONV_W, :]
    for j in range(CONV_W - 1):
        k = CONV_W - 1 - j
        xc = xc + ext_s[OFF - k * S:OFF - k * S + R, :] * wc_ref[j:j + 1, :]
    xb = xc.astype(BF16)
    for n in range(LRU_BLOCKS):
        sl = slice(n * LRU_BLOCK, (n + 1) * LRU_BLOCK)
        r = jax.nn.sigmoid(_dot(xb[:, sl], wa_ref[n]) + ba_ref[:, sl])
        i = jax.nn.sigmoid(_dot(xb[:, sl], wx_ref[n]) + bx_ref[:, sl])
        lam = lam_ref[:, sl]
        softplus_neg = jnp.maximum(-lam, 0.0) + jnp.log1p(jnp.exp(-jnp.abs(lam)))
        log_a = (-LRU_C) * r * softplus_neg
        a_s[:, sl] = jnp.exp(log_a)
        th = jnp.tanh(log_a)
        b_s[:, sl] = jnp.sqrt(-2.0 * th / (1.0 - th)) * (i * xc[:, sl])

    if S % SUBLANES == 0:
        def step(t, h):
            off = pl.multiple_of(t * S, S)
            h = a_s[pl.ds(off, S), :] * h + b_s[pl.ds(off, S), :]
            b_s[pl.ds(off, S), :] = h
            return h

        h = lax.fori_loop(0, Tc, step, h_s[...])
    else:
        assert S == 1 and Tc % SUBLANES == 0
        row =lax.broadcasted_iota(I32, (SUBLANES, a_s.shape[1]), 0)

        def tile_step(i, h):
            off = pl.multiple_of(i * SUBLANES, SUBLANES)
            A = a_s[pl.ds(off, SUBLANES), :]
            B = b_s[pl.ds(off, SUBLANES), :]
            for sh in (1, 2, 4):
                keep = row >= sh
                B = A * jnp.where(keep, pltpu.roll(B, sh, axis=0), 0.0) + B
                A = A * jnp.where(keep, pltpu.roll(A, sh, axis=0), 1.0)
            hs = A * h + B
            b_s[pl.ds(off, SUBLANES), :] = hs
            return hs[SUBLANES - 1:SUBLANES, :]

        h = lax.fori_loop(0, Tc // SUBLANES, tile_step, h_s[...])
    h_s[...] = h
    hl_ref[0] = h
    o_ref[...] = (_gelu_tanh(cg_ref[...]) * b_s[...]).astype(o_ref.dtype)
    if nch > 1:
        ext_s[OFF - PS:OFF, :] = ext_s[OFF + R - PS:OFF + R, :]


def _rglru(u_main, prev, h0, lp, *, S, Tc):
    t = u_main.shape[0]
    R = Tc * S
    G = prev.shape[0]
    nch = t // (G * R)
    PS = (CONV_W - 1) * S
    C = LRU_W
    vec = pl.BlockSpec((1, C), lambda g, c: (0, 0))
    blk = pl.BlockSpec((LRU_BLOCKS, LRU_BLOCK, LRU_BLOCK), lambda g, c: (0, 0, 0))
    return pl.pallas_call(
        functools.partial(_rglru_kernel, S=S, Tc=Tc, nch=nch),
        grid=(G, nch),
        in_specs=[pl.BlockSpec((R, C), lambda g, c: (g * nch + c, 3)),
                  pl.BlockSpec((R, C), lambda g, c: (g * nch + c, 4)),
                  pl.BlockSpec((1, PS, C), lambda g, c: (g, 0, 0)),
                  pl.BlockSpec((1, S, C), lambda g, c: (g, 0, 0)),
                  pl.BlockSpec((CONV_W, C), lambda g, c: (0, 0)), vec, blk, vec, blk, vec, vec],
        out_specs=[pl.BlockSpec((R, C), lambda g, c: (g * nch + c, 0)),
                   pl.BlockSpec((1, S, C), lambda g, c: (g, 0, 0))],
        out_shape=[jax.ShapeDtypeStruct((t, C), BF16), jax.ShapeDtypeStruct((G, S, C), F32)],
        scratch_shapes=[pltpu.VMEM((_round_up(PS, SUBLANES) + R, C), F32), pltpu.VMEM((R, C), F32),
                        pltpu.VMEM((R, C), F32), pltpu.VMEM((S, C), F32)],
        compiler_params=_cparams("parallel", "arbitrary"),
    )(u_main, u_main, prev, h0, lp["w_conv_c"], lp["b_conv_c"], lp["w_rg_a"], lp["b_rg_a"], lp["w_rg_x"],
      lp["b_rg_x"], lp["lru_lambda"])


def _ret_kernel(rq_ref, rk_ref, rv_ref, dg_ref, s0_ref, gn_ref, o_ref, sn_ref, s_s, *, c):
    @pl.when(pl.program_id(1) == 0)
    def _():
        s_s[...] = s0_ref[0]

    ii = lax.broadcasted_iota(I32, (c, c), 0)
    jj = lax.broadcasted_iota(I32, (c, c), 1)
    dif = (ii - jj).astype(F32)
    tpos = lax.broadcasted_iota(I32, (c, 1), 0).astype(F32)
    mm = BF16 if c % 16 == 0 else F32
    for h in range(N_HEADS_D):
        lg = LOG_G[h]
        qs = slice(h * QK_DIM_D, (h + 1) * QK_DIM_D)
        vs = slice(h * V_DIM_D, (h + 1) * V_DIM_D)
        q = rq_ref[:, qs].astype(mm)
        k = rk_ref[:, qs]
        v = rv_ref[:, vs].astype(mm)
        intra = jnp.where(dif >= 0.0, jnp.exp(lg * jnp.maximum(dif, 0.0)), 0.0)
        att = _dot_nt(q, k.astype(mm)) * intra
        s = s_s[h]
        o = _dot(att.astype(mm), v) + _dot(q, s.astype(mm)) * jnp.exp(lg * (tpos + 1.0))
        kd = (k * jnp.exp(lg * (c - 1.0 - tpos))).astype(mm)
        s_s[h] = s * math.exp(lg * c) + _dot_tn(kd, v)
        mu = jnp.mean(o, axis=-1, keepdims=True)
        var = jnp.mean(jnp.square(o - mu), axis=-1, keepdims=True)
        y = (o - mu) * lax.rsqrt(var + EPS) * gn_ref[:, vs]
        dg = dg_ref[:, vs]
        o_ref[:, vs] = (dg * jax.nn.sigmoid(dg) * y).astype(o_ref.dtype)
    sn_ref[0] = s_s[...]


def _retention(rq, rk, rv, dg, s0, gn, *, c, v_col, g_col):
    t = rq.shape[0]
    B = s0.shape[0]
    nch = t // (B * c)
    W = N_HEADS_D * QK_DIM_D
    return pl.pallas_call(
        functools.partial(_ret_kernel, c=c),
        grid=(B, nch),
        in_specs=[pl.BlockSpec((c, W), lambda b, i: (b * nch + i, 0)),
                  pl.BlockSpec((c, W), lambda b, i: (b * nch + i, 0)),
                  pl.BlockSpec((c, BRANCH_W), lambda b, i: (b * nch + i, v_col)),
                  pl.BlockSpec((c, BRANCH_W), lambda b, i: (b * nch + i, g_col)),
                  pl.BlockSpec((1, N_HEADS_D, QK_DIM_D, V_DIM_D), lambda b, i: (b, 0, 0, 0)),
                  pl.BlockSpec((1, BRANCH_W), lambda b, i: (0, 0))],
        out_specs=[pl.BlockSpec((c, BRANCH_W), lambda b, i: (b * nch + i, 0)),
                   pl.BlockSpec((1, N_HEADS_D, QK_DIM_D, V_DIM_D), lambda b, i: (b, 0, 0, 0))],
        out_shape=[jax.ShapeDtypeStruct((t, BRANCH_W), BF16), jax.ShapeDtypeStruct(s0.shape, F32)],
        scratch_shapes=[pltpu.VMEM((N_HEADS_D, QK_DIM_D, V_DIM_D), F32)],
        compiler_params=_cparams("parallel", "arbitrary"),
    )(rq, rk, rv, dg, s0, gn)


def _sortable(x):
    b = lax.bitcast_convert_type(x, I32)
    return b ^ ((b >> 31) & 0x7FFFFFFF)


def _count(m):
    return jnp.sum(jnp.where(m, 1.0, 0.0), axis=1, keepdims=True)


def _kth_largest_key(keys, kf, bits):
    m = keys.shape[0]
    t0 = jnp.where(_count(keys >= 0) >= kf, 0, INT_MIN).astype(I32)

    def step(nb, shift, t):
        digit = jnp.zeros((m, 1), I32)
        for v in range(1, 2 ** nb):
            cand = t | jnp.left_shift(jnp.int32(v), shift)
            digit = digit + jnp.where(_count(keys >= cand) >= kf, 1, 0)
        return t | jnp.left_shift(digit, shift)

    nfull, rem = divmod(31, bits)
    t = lax.fori_loop(0, nfull, lambda i, t: step(bits, 31 - bits * (i + 1), t), t0)
    return step(rem, 0, t) if rem else t


def _topk_mask(keys, valid, idx, k, bits):
    m, n = keys.shape
    kf = float(k)
    thr = _kth_largest_key(keys, kf, bits)
    gt = keys > thr
    eq = (keys == thr) & valid
    need = kf - _count(gt)
    excess = jnp.max(_count(eq) - need) > 0.0
    nbits = max(1, (n - 1).bit_length())

    def first_ties():
        def idx_step(i, j):
            cand = j | jnp.left_shift(jnp.int32(1), nbits - 1 - i)
            return jnp.where(_count(eq & (idx < cand)) <= need - 1.0, cand, j)

        return lax.fori_loop(0, nbits, idx_step, jnp.zeros((m, 1), I32))

    jthr = lax.cond(excess, first_ties, lambda: jnp.full((m, 1), 2 ** nbits, I32))
    return (gt & valid) | (eq & (idx <= jthr))


def _masked_attention(s, mask, v):
    s = jnp.where(mask, s, -jnp.inf)
    p = jnp.exp(s - jnp.max(s, axis=1, keepdims=True))
    return _dot(p.astype(BF16), v) / jnp.sum(p, axis=1, keepdims=True)


def _dsa_prompt_kernel(q_ref, kb_ref, vb_ref, iq4_ref, ik4_ref, iw_ref, o_ref, *, j0, topk):
    nq = q_ref.shape[0]
    nk = kb_ref.shape[0]
    t0 = (j0 + pl.program_id(1)) * nq
    ik4 = ik4_ref[...]
    score = jnp.zeros((nq, nk), F32)
    for h in range(N_IDX_HEADS):
        s = _dot_nt(iq4_ref[:, h * IDX4:(h + 1) * IDX4], ik4)
        score = score + jnp.maximum(s, 0.0) * iw_ref[:, IDX_DIM + h:IDX_DIM + h + 1]
    score = score * IDX_SCALE
    kpos = lax.broadcasted_iota(I32, (nq, nk), 1)
    valid = kpos <= t0 + lax.broadcasted_iota(I32, (nq, nk), 0)
    if nk > topk:
        mask = _topk_mask(_sortable(jnp.where(valid, score, -jnp.inf)), valid, kpos, topk, 1)
    else:
        mask = valid
    for g in range(N_KV_B):
        gs = slice(g * HEAD_DIM_B, (g + 1) * HEAD_DIM_B)
        kg = kb_ref[:, gs]
        vg = vb_ref[:, gs]
        for r in range(KV_REP):
            hs = slice((g * KV_REP + r) * HEAD_DIM_B, (g * KV_REP + r + 1) * HEAD_DIM_B)
            o_ref[:, hs] = _masked_attention(_dot_nt(q_ref[:, hs], kg), mask, vg).astype(o_ref.dtype)


def _dsa_prompt(q, kb, vb, iq4, ik4, ikw, *, n_seq, n_cls, qb):
    t = q.shape[0]
    s_len = t // n_seq
    nb = s_len // qb
    nbc = nb // n_cls
    topk = min(TOPK_MAX, s_len // 4)
    q, kb, vb, iq4, ik4, ikw = (a.reshape(n_seq, s_len, a.shape[-1]) for a in (q, kb, vb, iq4, ik4, ikw))
    outs = []
    for c in range(n_cls):
        nk = (c + 1) * nbc * qb
        qrow = lambda w, c=c: pl.BlockSpec((None, qb, w), lambda b, j: (b, c * nbc + j, 0))
        seq = lambda w, nk=nk: pl.BlockSpec((None, nk, w), lambda b, j: (b, 0, 0))
        outs.append(pl.pallas_call(
            functools.partial(_dsa_prompt_kernel, j0=c * nbc, topk=topk),
            grid=(n_seq, nbc),
            in_specs=[qrow(BRANCH_W), seq(N_KV_B * HEAD_DIM_B), seq(N_KV_B * HEAD_DIM_B), qrow(N_IDX_HEADS * IDX4),
                      seq(IDX4), qrow(LANES)],
            out_specs=pl.BlockSpec((None, qb, BRANCH_W), lambda b, j: (b, j, 0)),
            out_shape=jax.ShapeDtypeStruct((n_seq, nbc * qb, BRANCH_W), BF16),
            compiler_params=_cparams("parallel", "arbitrary"),
        )(q, kb, vb, iq4, ik4, ikw))
    return jnp.concatenate(outs, axis=1).reshape(t, BRANCH_W)


def _dsa_sample_kernel(pt_ref, qg_ref, iq4_ref, w_ref, kn_ref, vn_ref, ikn_ref, *rest, n_pages, t_new):
    del pt_ref
    idx_refs = rest[:n_pages]
    k_refs = rest[n_pages:2 * n_pages]
    v_refs = rest[2 * n_pages:3 * n_pages]
    o_ref, sc_s, k_s, v_s = rest[3 * n_pages:]
    P = PAGE_SIZE
    iq4 = iq4_ref[0]
    w = w_ref[0]
    pad = P - t_new

    def page_scores(ik, page, keys_on_lanes):
        hi, lo = _split_bf16(ik)
        if keys_on_lanes:
            s = _dot(iq4, jnp.concatenate([hi, lo, hi, lo], axis=0))
        else:
            s = _dot_nt(iq4, jnp.concatenate([hi, lo, hi, lo], axis=1))
        r = jnp.maximum(s, 0.0) * w
        sc = r[0:t_new]
        for h in range(1, N_IDX_HEADS):
            sc = sc + r[h * t_new:(h + 1) * t_new]
        sc_s[:, page * P:(page + 1) * P] = sc * IDX_SCALE

    for p in range(n_pages):
        page_scores(idx_refs[p][0, 0], p, True)
        for g in range(N_KV_B):
            gs = slice(g * HEAD_DIM_B, (g + 1) * HEAD_DIM_B)
            k_s[p * P:(p + 1) * P, gs] = k_refs[p][0, 0, pl.ds(g, P, stride=N_KV_B), :].astype(BF16)
            v_s[p * P:(p + 1) * P, gs] = v_refs[p][0, 0, pl.ds(g, P, stride=N_KV_B), :].astype(BF16)
    page_scores(jnp.concatenate([ikn_ref[0], jnp.zeros((pad, IDX_DIM), F32)], axis=0), n_pages, False)
    zkv = jnp.zeros((pad, N_KV_B * HEAD_DIM_B), F32)
    k_s[n_pages * P:(n_pages + 1) * P, :] = jnp.concatenate([kn_ref[0], zkv], axis=0).astype(BF16)
    v_s[n_pages * P:(n_pages + 1) * P, :] = jnp.concatenate([vn_ref[0], zkv], axis=0).astype(BF16)

    nk = (n_pages + 1) * P
    past = n_pages * P
    kpos = lax.broadcasted_iota(I32, (t_new, nk), 1)
    valid = kpos <= past + lax.broadcasted_iota(I32, (t_new, nk), 0)
    keys = _sortable(jnp.where(valid, sc_s[...], -jnp.inf))
    mask = _topk_mask(keys, valid, kpos, min(TOPK_MAX, (past + t_new) // 4), 3)
    mask_i = jnp.where(mask, 1, 0).astype(I32)
    mask = jnp.concatenate([mask_i] * KV_REP, axis=0) > 0
    for g in range(N_KV_B):
        gs = slice(g * HEAD_DIM_B, (g + 1) * HEAD_DIM_B)
        o_ref[0, g] = _masked_attention(_dot_nt(qg_ref[0, g], k_s[:, gs]), mask, v_s[:, gs]).astype(o_ref.dtype)


def _dsa_sample(qg, iq4r, wr, k_new, v_new, ik_new, cache_k, cache_v, cache_idx_k, page_table, layer):
    B, _, rows, hd = qg.shape
    t_new = k_new.shape[1]
    n_pages = page_table.shape[1]
    kvw = N_KV_B * HEAD_DIM_B
    per_seq = lambda shape: pl.BlockSpec((1,) + shape, lambda b, pt: (b,) + (0,) * len(shape))

    def page_spec(rows_, width, p):
        return pl.BlockSpec((1, 1, rows_, width), lambda b, pt: (layer, pt[b, p], 0, 0))

    in_specs = [per_seq(qg.shape[1:]), per_seq(iq4r.shape[1:]), per_seq(wr.shape[1:]), per_seq((t_new, kvw)),
                per_seq((t_new, kvw)), per_seq((t_new, IDX_DIM))]
    in_specs += [page_spec(IDX_DIM, PAGE_SIZE, p) for p in range(n_pages)]
    in_specs += [page_spec(PAGE_SIZE * N_KV_B, HEAD_DIM_B, p) for p in range(n_pages)]
    in_specs += [page_spec(PAGE_SIZE * N_KV_B, HEAD_DIM_B, p) for p in range(n_pages)]
    nk = (n_pages + 1) * PAGE_SIZE
    grid_spec = pltpu.PrefetchScalarGridSpec(
        num_scalar_prefetch=1, grid=(B,), in_specs=in_specs,
        out_specs=pl.BlockSpec((1,) + qg.shape[1:], lambda b, pt: (b, 0, 0, 0)),
        scratch_shapes=[pltpu.VMEM((t_new, nk), F32), pltpu.VMEM((nk, kvw), BF16), pltpu.VMEM((nk, kvw), BF16)])
    return pl.pallas_call(
        functools.partial(_dsa_sample_kernel, n_pages=n_pages, t_new=t_new),
        grid_spec=grid_spec,
        out_shape=jax.ShapeDtypeStruct(qg.shape, BF16),
        compiler_params=_cparams("arbitrary"),
    )(page_table, qg, iq4r, wr, k_new, v_new, ik_new, *([cache_idx_k] * n_pages), *([cache_k] * n_pages),
      *([cache_v] * n_pages))


def _merge_kernel(xn_ref, a_ref, b_ref, c_ref, d_ref, wg0, wg1, wg2, wg3, bg0, bg1, bg2, bg3, wb_ref, o_ref):
    xn = xn_ref[...]
    acc = None
    for n, (br, wg, bg) in enumerate(zip((a_ref, b_ref, c_ref, d_ref), (wg0, wg1, wg2, wg3), (bg0, bg1, bg2, bg3))):
        term = jax.nn.sigmoid(_dot(xn, wg[...]) + bg[...]) * _dot(br[...], wb_ref[n])
        acc = term if acc is None else acc + term
    o_ref[...] = acc.astype(o_ref.dtype)


def _merge(xn, branches, w_gate, b_gate, w_branch, l, tm, tn):
    t, d = xn.shape
    nj = d // tn
    row = lambda w: pl.BlockSpec((tm, w), lambda i, j: (i, 0))
    wg = [pl.BlockSpec((None, d, tn), functools.partial(lambda i, j, n: (l, 0, n * nj + j), n=n))
          for n in range(N_BRANCH)]
    bg = [pl.BlockSpec((1, tn), functools.partial(lambda i, j, n: (0, n * nj + j), n=n)) for n in range(N_BRANCH)]
    return pl.pallas_call(
        _merge_kernel,
        grid=(t // tm, nj),
        in_specs=[row(d)] + [row(BRANCH_W)] * N_BRANCH + wg + bg
                 + [pl.BlockSpec((None, N_BRANCH, BRANCH_W, tn), lambda i, j: (l, 0, 0, j))],
        out_specs=pl.BlockSpec((tm, tn), lambda i, j: (i, j)),
        out_shape=jax.ShapeDtypeStruct((t, d), BF16),
        compiler_params=_cparams("parallel", "arbitrary"),
    )(xn, *branches, *([w_gate] * N_BRANCH), *([b_gate] * N_BRANCH), w_branch)


def _outproj_kernel(m_ref, x_ref, w_ref, g_ref, o_ref):
    o_ref[...] = x_ref[...] + _rms(_dot(m_ref[...], w_ref[...]), g_ref[...])


def _outproj(merged, x, w_o, g, l, tm):
    t, d = x.shape
    return pl.pallas_call(
        _outproj_kernel,
        grid=(t // tm,),
        in_specs=[pl.BlockSpec((tm, d), lambda i: (i, 0)), pl.BlockSpec((tm, d), lambda i: (i, 0)),
                  pl.BlockSpec((None, d, d), lambda i: (l, 0, 0)), pl.BlockSpec((1, d), lambda i: (0, 0))],
        out_specs=pl.BlockSpec((tm, d), lambda i: (i, 0)),
        out_shape=jax.ShapeDtypeStruct((t, d), F32),
        compiler_params=_cparams("parallel"),
    )(merged, x, w_o, g)


def _ffn_kernel(x_ref, gpre_ref, wug_ref, wuv_ref, wcg_ref, wcv_ref, bcg_ref, bcv_ref, wd_ref, gpost_ref,
                stg_ref, stv_ref, o_ref, ng_ref, nv_ref, h_s, acc_s, cg_s, cv_s, ext_s, *, S, tiles_per_group):
    i = pl.program_id(0)
    j = pl.program_id(1)
    tm = x_ref.shape[0]
    CR = cg_s.shape[1]

    @pl.when(j == 0)
    def _():
        h_s[...] = _rms(x_ref[...], gpre_ref[...]).astype(BF16)
        acc_s[...] = jnp.zeros_like(acc_s)

    @pl.when(i % tiles_per_group == 0)
    def _():
        cg_s[j] = stg_ref[0]
        cv_s[j] = stv_ref[0]

    h = h_s[...]

    def conv(w_ref, wc_ref, bc_ref, c_s, new_ref):
        u = _dot(h, w_ref[...])
        ext_s[0:CR, :] = c_s[j]
        ext_s[CR:CR + tm, :] = u
        y = bc_ref[...] + u * wc_ref[2:3, :]
        y = y + ext_s[CR - S:CR - S + tm, :] * wc_ref[1:2, :]
        y = y + ext_s[CR - 2 * S:CR - 2 * S + tm, :] * wc_ref[0:1, :]
        tail = ext_s[tm:tm + CR, :]
        c_s[j] = tail
        new_ref[0] = tail
        return y

    gate = conv(wug_ref, wcg_ref, bcg_ref, cg_s, ng_ref)
    val = conv(wuv_ref, wcv_ref, bcv_ref, cv_s, nv_ref)
    acc_s[...] += _dot((_gelu_tanh(gate) * val).astype(BF16), wd_ref[...])

    @pl.when(j == pl.num_programs(1) - 1)
    def _():
        o_ref[...] = x_ref[...] + _rms(acc_s[...], gpost_ref[...])


def _ffn(x, lp, state, *, S, tm, tn):
    t, d = x.shape
    G, CR, _ = state.shape
    nj = D_FF // tn
    ni = t // tm
    tiles_per_group = ni // G
    grp = lambda i, j: (i // tiles_per_group, 0, j)
    tile = lambda i, j: (i, 0, j)
    grp_v = lambda i, j: (i // tiles_per_group, 0, nj + j)
    col = lambda r: pl.BlockSpec((r, tn), lambda i, j: (0, j))
    col_v = lambda r: pl.BlockSpec((r, tn), lambda i, j: (0, nj + j))
    vec = pl.BlockSpec((1, d), lambda i, j: (0, 0))
    l = lp["layer"]
    xo, ng, nv = pl.pallas_call(
        functools.partial(_ffn_kernel, S=S, tiles_per_group=tiles_per_group),
        grid=(ni, nj),
        in_specs=[pl.BlockSpec((tm, d), lambda i, j: (i, 0)), vec,
                  pl.BlockSpec((None, d, tn), lambda i, j: (l, 0, j)),
                  pl.BlockSpec((None, d, tn), lambda i, j: (l, 0, nj + j)),
                  col(FFN_CONV_W), col_v(FFN_CONV_W), col(1), col_v(1),
                  pl.BlockSpec((None, tn, d), lambda i, j: (l, j, 0)), vec,
                  pl.BlockSpec((1, CR, tn), grp), pl.BlockSpec((1, CR, tn), grp_v)],
        out_specs=[pl.BlockSpec((tm, d), lambda i, j: (i, 0)),
                   pl.BlockSpec((1, CR, tn), tile), pl.BlockSpec((1, CR, tn), tile)],
        out_shape=[jax.ShapeDtypeStruct((t, d), F32), jax.ShapeDtypeStruct((ni, CR, D_FF), F32),
                   jax.ShapeDtypeStruct((ni, CR, D_FF), F32)],
        scratch_shapes=[pltpu.VMEM((tm, d), BF16), pltpu.VMEM((tm, d), F32), pltpu.VMEM((nj, CR, tn), F32),
                        pltpu.VMEM((nj, CR, tn), F32), pltpu.VMEM((CR + tm, tn), F32)],
        compiler_params=_cparams("arbitrary", "arbitrary"),
    )(x, lp["g_ffn_pre"], lp["w_up"], lp["w_up"], lp["w_ffn_conv"], lp["w_ffn_conv"], lp["b_ffn_conv"],
      lp["b_ffn_conv"], lp["w_down"], lp["g_ffn_post"], state, state)
    last = slice(tiles_per_group - 1, None, tiles_per_group)
    return xo, ng[last], nv[last]


def _layer_front(x, lp, tabs, *, tm):
    l = lp["layer"]
    u_main, xn = _inproj(x, lp["g_mix_pre"], lp["w_in_main"], l, tm, 512)
    u_idx = _inproj_split(x, lp["g_mix_pre"], lp["w_in_idx_hi"], lp["w_in_idx_lo"], l, tm, 384)
    return u_main, xn, _rope_prep(u_main, u_idx, tabs, 256)


def _layer_back(x, xn, branches, lp, ffn_state, *, S, tm, tn_ffn):
    l = lp["layer"]
    merged = _merge(xn, branches, lp["w_gate"], lp["b_gate"], lp["w_branch"], l, tm, 512)
    x1 = _outproj(merged, x, lp["w_o"], lp["g_mix_post"], l, tm)
    return _ffn(x1, lp, ffn_state, S=S, tm=tm, tn=tn_ffn)


def _prep_big_weights(p):
    w_in = p["w_in"]
    o_iq = 3072
    o_cx = o_iq + N_IDX_HEADS * IDX_DIM + IDX_DIM + N_IDX_HEADS
    w_idx = jnp.pad(w_in[:, :, o_iq:o_cx], ((0, 0), (0, 0), (0, N_IDX - (o_cx - o_iq))))
    hi = w_idx.astype(BF16)
    return dict(
        w_in_main=jnp.concatenate([w_in[:, :, :o_iq], w_in[:, :, o_cx:]], axis=2).astype(BF16),
        w_in_idx_hi=hi, w_in_idx_lo=(w_idx - hi.astype(F32)).astype(BF16),
        w_branch=p["w_branch"].astype(BF16), w_gate=p["w_gate"].astype(BF16), w_o=p["w_o"].astype(BF16),
        w_up=p["w_up"].astype(BF16), w_down=p["w_down"].astype(BF16))


def _prep_layer_params(p, big, l):
    row = lambda a: a[l][None, :]
    return dict(
        big, layer=l,
        g_mix_pre=row(p["g_mix_pre"]),
        w_pool=p["w_pool"][l].astype(BF16), pool_scale=row(p["pool_scale"]),
        w_conv_c=p["w_conv_c"][l], b_conv_c=row(p["b_conv_c"]),
        w_rg_a=p["w_rg_a"][l].astype(BF16), b_rg_a=row(p["b_rg_a"]),
        w_rg_x=p["w_rg_x"][l].astype(BF16), b_rg_x=row(p["b_rg_x"]),
        lru_lambda=row(p["lru_lambda"]), ret_gn=row(p["ret_gn"]), b_gate=row(p["b_gate"]),
        g_mix_post=row(p["g_mix_post"]), g_ffn_pre=row(p["g_ffn_pre"]),
        w_ffn_conv=p["w_ffn_conv"][l], b_ffn_conv=row(p["b_ffn_conv"]), g_ffn_post=row(p["g_ffn_post"]))


def _to_time_major(a, nb, nt):
    return jnp.swapaxes(a, 0, 1).reshape((nt * nb,) + a.shape[2:])


def _to_seq_major(a, nb, nt):
    return jnp.swapaxes(a.reshape((nt, nb) + a.shape[1:]), 0, 1)


def kernel(x_prompt, x_sample, cache_k, cache_v, cache_idx_k, state_pool, state_conv, state_rglru, state_ret, state_ffn_conv, page_table, g_mix_pre, w_in, w_pool, pool_scale, w_conv_c, b_conv_c, w_rg_a, b_rg_a, w_rg_x, b_rg_x, lru_lambda, ret_gn, w_branch, w_gate, b_gate, w_o, g_mix_post, g_ffn_pre, w_up, w_ffn_conv, b_ffn_conv, w_down, g_ffn_post):
    params = dict(g_mix_pre=g_mix_pre, w_in=w_in, w_pool=w_pool, pool_scale=pool_scale, w_conv_c=w_conv_c,
                  b_conv_c=b_conv_c, w_rg_a=w_rg_a, b_rg_a=b_rg_a, w_rg_x=w_rg_x, b_rg_x=b_rg_x,
                  lru_lambda=lru_lambda, ret_gn=ret_gn, w_branch=w_branch, w_gate=w_gate, b_gate=b_gate, w_o=w_o,
                  g_mix_post=g_mix_post, g_ffn_pre=g_ffn_pre, w_up=w_up, w_ffn_conv=w_ffn_conv,
                  b_ffn_conv=b_ffn_conv, w_down=w_down, g_ffn_post=g_ffn_post)
    depth = w_in.shape[0]
    bp, seq, d = x_prompt.shape
    bs, tdec, _ = x_sample.shape
    past = page_table.shape[1] * PAGE_SIZE
    n_pool = cache_k.shape[1]
    ck = cache_k.reshape(depth, n_pool, PAGE_SIZE * N_KV_B, HEAD_DIM_B)
    cv = cache_v.reshape(depth, n_pool, PAGE_SIZE * N_KV_B, HEAD_DIM_B)
    cik_t = jnp.swapaxes(cache_idx_k, 2, 3)
    big = _prep_big_weights(params)

    tabs_p = _rope_tables(jnp.arange(seq, dtype=F32))
    tabs_s = _rope_tables(jnp.repeat(past + jnp.arange(tdec, dtype=F32), bs))
    xp = x_prompt.reshape(bp * seq, d)
    xs = _to_time_major(x_sample, bs, tdec)
    cr_p = max(SUBLANES, (FFN_CONV_W - 1))
    outs_p, outs_s = [], []
    for l in range(depth):
        lp = _prep_layer_params(params, big, l)
        u, xn, (q, k, kb, vb, iq4, ik4, ikw, rq, rk) = _layer_front(xp, lp, tabs_p, tm=1024)
        o_a = _pool(u, jnp.zeros((bp, POOL_BUF, BRANCH_W), F32), lp["w_pool"], lp["pool_scale"], S=1, Tc=256, start=0)
        o_b = _dsa_prompt(q, kb, vb, iq4, ik4, ikw, n_seq=bp, n_cls=8, qb=256)
        o_c, h_p = _rglru(u, jnp.zeros((bp, CONV_W - 1, LRU_W), F32), jnp.zeros((bp, 1, LRU_W), F32), lp, S=1, Tc=256)
        o_d, s_p = _retention(rq, rk, u, u, jnp.zeros((bp, N_HEADS_D, QK_DIM_D, V_DIM_D), F32), lp["ret_gn"],
                              c=RET_CHUNK if seq % RET_CHUNK == 0 else seq, v_col=6, g_col=7)
        xp, ng, nv = _layer_back(xp, xn, (o_a, o_b, o_c, o_d), lp, jnp.zeros((bp, cr_p, 2 * D_FF), F32),
                                 S=1, tm=512, tn_ffn=512)
        u3 = u.reshape(bp, seq, N_MAIN)
        outs_p.append((
            k.reshape(bp, seq, N_KV_B, HEAD_DIM_B), u3[:, :, 2560:3072].reshape(bp, seq, N_KV_B, HEAD_DIM_B),
            ikw.reshape(bp, seq, LANES)[:, :, :IDX_DIM], u3[:, seq - POOL_BUF:, 0:BRANCH_W],
            u3[:, seq - (CONV_W - 1):, 3072:3072 + LRU_W], h_p[:, 0], s_p,
            jnp.concatenate([ng, nv], axis=-1)[:, cr_p - (FFN_CONV_W - 1):]))
        u, xn, (q, k, kb, vb, iq4, ik4, ikw, rq, rk) = _layer_front(xs, lp, tabs_s, tm=1024)
        o_a = _pool(u, _to_time_major(state_pool[l], bs, POOL_BUF)[None], lp["w_pool"], lp["pool_scale"],
                    S=bs, Tc=tdec, start=past)
        o_c, h_s = _rglru(u, _to_time_major(state_conv[l], bs, CONV_W - 1)[None], state_rglru[l][None], lp,
                          S=bs, Tc=tdec)
        sm = lambda a: _to_seq_major(a, bs, tdec)
        u_sm = sm(u)
        o_d, s_s = _retention(sm(rq).reshape(bs * tdec, -1), sm(rk).reshape(bs * tdec, -1),
                              u_sm[:, :, 6144:7168].reshape(bs * tdec, -1), u_sm[:, :, 7168:8192].reshape(bs * tdec, -1),
                              state_ret[l], lp["ret_gn"], c=tdec, v_col=0, g_col=0)
        o_d = _to_time_major(o_d.reshape(bs, tdec, BRANCH_W), bs, tdec)
        qg = sm(q).reshape(bs, tdec, N_KV_B, KV_REP, HEAD_DIM_B).transpose(0, 2, 3, 1, 4)
        qg = qg.reshape(bs, N_KV_B, KV_REP * tdec, HEAD_DIM_B)
        iq4r = sm(iq4).reshape(bs, tdec, N_IDX_HEADS, IDX4).transpose(0, 2, 1, 3).reshape(bs, N_IDX_HEADS * tdec, IDX4)
        ikw_sm = sm(ikw)
        wr = ikw_sm[:, :, IDX_DIM:IDX_DIM + N_IDX_HEADS].transpose(0, 2, 1).reshape(bs, N_IDX_HEADS * tdec, 1)
        k_sm = sm(k)
        v_sm = u_sm[:, :, 2560:3072]
        o_b = _dsa_sample(qg, iq4r, wr, k_sm, v_sm, ikw_sm[:, :, :IDX_DIM], ck, cv, cik_t, page_table, l)
        o_b = o_b.reshape(bs, N_KV_B, KV_REP, tdec, HEAD_DIM_B).transpose(3, 0, 1, 2, 4).reshape(tdec * bs, BRANCH_W)
        ffn_state = _to_time_major(state_ffn_conv[l], bs, FFN_CONV_W - 1)[None]
        xs, ng, nv = _layer_back(xs, xn, (o_a, o_b, o_c, o_d), lp, ffn_state, S=bs, tm=512, tn_ffn=256)
        ffn_new = _to_seq_major(jnp.concatenate([ng, nv], axis=-1)[0], bs, FFN_CONV_W - 1)
        a_in = u_sm[:, :, 0:BRANCH_W]
        c_x = u_sm[:, :, 3072:3072 + LRU_W]
        outs_s.append((
            k_sm.reshape(bs, tdec, N_KV_B, HEAD_DIM_B), v_sm.reshape(bs, tdec, N_KV_B, HEAD_DIM_B),
            ikw_sm[:, :, :IDX_DIM],
            jnp.concatenate([state_pool[l], a_in], axis=1)[:, tdec:],
            jnp.concatenate([state_conv[l], c_x], axis=1)[:, tdec:],
            h_s[0], s_s, ffn_new))

    stk = lambda outs, i: jnp.stack([o[i] for o in outs], axis=0)
    res = [xp.reshape(bp, seq, d), _to_seq_major(xs, bs, tdec)]
    for i in range(8):
        res += [stk(outs_p, i), stk(outs_s, i)]
    return tuple(res)
```

```python
import functools
import math

import jax
import jax.numpy as jnp
from jax import lax
from jax.experimental import pallas as pl
from jax.experimental.pallas import tpu as pltpu

F32 = jnp.float32
BF16 = jnp.bfloat16
I32 = jnp.int32

SUBLANES = 8
LANES = 128
VMEM_LIMIT_BYTES = 56 * 1024 * 1024

D_MODEL = 2048
BRANCH_W = D_MODEL // 2
N_BRANCH = 4
POOL_WINDOWS = (2, 4, 8, 16)
POOL_GROUP = BRANCH_W // len(POOL_WINDOWS)
POOL_BUF = max(POOL_WINDOWS) - 1
N_HEADS_B = 8
HEAD_DIM_B = BRANCH_W // N_HEADS_B
N_KV_B = 4
KV_REP = N_HEADS_B // N_KV_B
N_IDX_HEADS = 16
IDX_DIM = 64
TOPK_MAX = 256
ATT_BLOCK = 128
ROPE_THETA = 10000.0
ATT_SCALE = HEAD_DIM_B ** -0.5
IDX_SCALE = (IDX_DIM * N_IDX_HEADS) ** -0.5
LRU_W = BRANCH_W
LRU_BLOCKS = 4
LRU_BLOCK = LRU_W // LRU_BLOCKS
LRU_C = 8.0
CONV_W = 4
N_HEADS_D = 8
QK_DIM_D = BRANCH_W // (2 * N_HEADS_D)
V_DIM_D = BRANCH_W // N_HEADS_D
RET_CHUNK = 128
D_FF = 11 * D_MODEL // 4
FFN_CONV_W = 3
EPS = 1e-6
PAGE_SIZE = 128

N_MAIN = 8192
N_IDX = 1152
IDX4 = 4 * IDX_DIM

LOG_G = tuple(math.log1p(-(2.0 ** (-5.0 - h))) for h in range(N_HEADS_D))
INT_MIN = -2 ** 31


def _round_up(n, m):
    return (n + m - 1) // m * m


def _cparams(*sem):
    return pltpu.CompilerParams(dimension_semantics=sem, vmem_limit_bytes=VMEM_LIMIT_BYTES)


def _dot(a, b):
    return jnp.dot(a, b, preferred_element_type=F32)


def _dot_nt(a, b):
    return lax.dot_general(a, b, (((1,), (1,)), ((), ())), preferred_element_type=F32)


def _dot_tn(a, b):
    return lax.dot_general(a, b, (((0,), (0,)), ((), ())), preferred_element_type=F32)


def _rms(x, g):
    return x * lax.rsqrt(jnp.mean(x * x, axis=-1, keepdims=True) + EPS) * g


def _gelu_tanh(x):
    return x * (0.5 * (1.0 + jnp.tanh(0.7978845608028654 * (x + 0.044715 * (x * x * x)))))


def _split_bf16(x):
    hi = x.astype(BF16)
    lo = (x - hi.astype(F32)).astype(BF16)
    return hi, lo


def _inproj_kernel(x_ref, g_ref, w_ref, u_ref, xn_ref):
    @pl.when(pl.program_id(1) == 0)
    def _():
        xn_ref[...] = _rms(x_ref[...], g_ref[...]).astype(BF16)

    u_ref[...] = _dot(xn_ref[...], w_ref[...])


def _inproj(x, g, w, l, tm, tn):
    t, d = x.shape
    n = w.shape[2]
    return pl.pallas_call(
        _inproj_kernel,
        grid=(t // tm, n // tn),
        in_specs=[pl.BlockSpec((tm, d), lambda i, j: (i, 0)),
                  pl.BlockSpec((1, d), lambda i, j: (0, 0)),
                  pl.BlockSpec((None, d, tn), lambda i, j: (l, 0, j))],
        out_specs=[pl.BlockSpec((tm, tn), lambda i, j: (i, j)),
                   pl.BlockSpec((tm, d), lambda i, j: (i, 0))],
        out_shape=[jax.ShapeDtypeStruct((t, n), F32), jax.ShapeDtypeStruct((t, d), BF16)],
        compiler_params=_cparams("parallel", "arbitrary"),
    )(x, g, w)


def _inproj_split_kernel(x_ref, g_ref, wh_ref, wl_ref, u_ref, xh_s, xl_s):
    @pl.when(pl.program_id(1) == 0)
    def _():
        hi, lo = _split_bf16(_rms(x_ref[...], g_ref[...]))
        xh_s[...] = hi
        xl_s[...] = lo

    xh = xh_s[...]
    u_ref[...] = _dot(xh, wh_ref[...]) + (_dot(xh, wl_ref[...]) + _dot(xl_s[...], wh_ref[...]))


def _inproj_split(x, g, wh, wl, l, tm, tn):
    t, d = x.shape
    n = wh.shape[2]
    return pl.pallas_call(
        _inproj_split_kernel,
        grid=(t // tm, n // tn),
        in_specs=[pl.BlockSpec((tm, d), lambda i, j: (i, 0)),
                  pl.BlockSpec((1, d), lambda i, j: (0, 0)),
                  pl.BlockSpec((None, d, tn), lambda i, j: (l, 0, j)),
                  pl.BlockSpec((None, d, tn), lambda i, j: (l, 0, j))],
        out_specs=pl.BlockSpec((tm, tn), lambda i, j: (i, j)),
        out_shape=jax.ShapeDtypeStruct((t, n), F32),
        scratch_shapes=[pltpu.VMEM((tm, d), BF16), pltpu.VMEM((tm, d), BF16)],
        compiler_params=_cparams("parallel", "arbitrary"),
    )(x, g, wh, wl)


def _rope128(xs, cos, sin):
    return xs * cos + pltpu.roll(xs, HEAD_DIM_B // 2, axis=1) * sin


def _rope64(xs, cos, sin, first_half):
    rot = jnp.where(first_half, pltpu.roll(xs, LANES - IDX_DIM // 2, axis=1), pltpu.roll(xs, IDX_DIM // 2, axis=1))
    return xs * cos + rot * sin


def _rope_kernel(bq_ref, bk_ref, bv_ref, dq_ref, dk_ref, ui_ref, c128_ref, s128_ref, c64_ref, s64_ref,
                 *rest, n_alias):
    q_ref, k_ref, kb_ref, vb_ref, iq4_ref, ik4_ref, ikw_ref, rq_ref, rk_ref = rest[n_alias:n_alias + 9]
    final = rest[n_alias + 9:]
    rows = bq_ref.shape[0]
    c128, s128, c64, s64 = c128_ref[...], s128_ref[...], c64_ref[...], s64_ref[...]
    lane = lax.broadcasted_iota(I32, (rows, LANES), 1)
    first_half = (lane & (IDX_DIM // 2)) == 0
    low = lane < IDX_DIM
    for c in range(N_HEADS_B):
        sl = slice(c * LANES, (c + 1) * LANES)
        q_ref[:, sl] = (_rope128(bq_ref[:, sl], c128, s128) * ATT_SCALE).astype(BF16)
    for c in range(N_KV_B):
        sl = slice(c * LANES, (c + 1) * LANES)
        kr = _rope128(bk_ref[:, sl], c128, s128)
        k_ref[:, sl] = kr
        kb_ref[:, sl] = kr.astype(BF16)
        if final:
            final[0][pl.ds(c, rows, stride=N_KV_B), :] = kr
            final[1][pl.ds(c, rows, stride=N_KV_B), :] = bv_ref[:, sl]
    vb_ref[...] = bv_ref[...].astype(BF16)
    for c in range(N_HEADS_D * QK_DIM_D // LANES):
        sl = slice(c * LANES, (c + 1) * LANES)
        rq_ref[:, sl] = _rope64(dq_ref[:, sl], c64, s64, first_half)
        rk_ref[:, sl] = _rope64(dk_ref[:, sl], c64, s64, first_half) * (QK_DIM_D ** -0.5)
    for c in range(N_IDX_HEADS // 2):
        y = _rope64(ui_ref[:, c * LANES:(c + 1) * LANES], c64, s64, first_half)
        yr = pltpu.roll(y, IDX_DIM, axis=1)
        for hh, dup in enumerate((jnp.where(low, y, yr), jnp.where(low, yr, y))):
            hi, lo = _split_bf16(dup)
            base = (2 * c + hh) * IDX4
            iq4_ref[:, base:base + LANES] = hi
            iq4_ref[:, base + LANES:base + 2 * LANES] = lo
    raw = ui_ref[:, N_IDX_HEADS * IDX_DIM:N_IDX_HEADS * IDX_DIM + LANES]
    y = _rope64(raw, c64, s64, first_half)
    ikw_ref[...] = jnp.where(low, y, raw)
    dup = jnp.where(low, y, pltpu.roll(y, IDX_DIM, axis=1))
    hi = dup.astype(BF16).astype(F32)
    hilo = jnp.where(low, hi, dup - hi).astype(BF16)
    ik4_ref[:, 0:LANES] = hilo
    ik4_ref[:, LANES:2 * LANES] = hilo


def _rope_prep(u_main, u_idx, tabs, tr, kv_final=None):
    t = u_main.shape[0]
    nt = tabs[0].shape[0] // tr
    row = lambda w, c: pl.BlockSpec((tr, w), lambda i: (i, c))
    tab = pl.BlockSpec((tr, LANES), lambda i: (i % nt, 0))
    outs = [(BRANCH_W, BF16), (N_KV_B * HEAD_DIM_B, F32), (N_KV_B * HEAD_DIM_B, BF16), (N_KV_B * HEAD_DIM_B, BF16),
            (N_IDX_HEADS * IDX4, BF16), (IDX4, BF16), (LANES, F32), (N_HEADS_D * QK_DIM_D, F32),
            (N_HEADS_D * QK_DIM_D, F32)]
    in_specs = [row(1024, 1), row(512, 4), row(512, 5), row(512, 10), row(512, 11), row(N_IDX, 0), tab, tab, tab, tab]
    args = [u_main, u_main, u_main, u_main, u_main, u_idx, *tabs]
    out_specs = [row(w, 0) for w, _ in outs]
    out_shape = [jax.ShapeDtypeStruct((t, w), dt) for w, dt in outs]
    aliases = {}
    if kv_final is not None:
        layer, depth, prev = kv_final
        out_specs += [pl.BlockSpec((None, tr * N_KV_B, HEAD_DIM_B), lambda i: (layer, i, 0))] * 2
        out_shape += [jax.ShapeDtypeStruct((depth, t * N_KV_B, HEAD_DIM_B), F32)] * 2
        if prev is not None:
            aliases = {len(args): len(outs), len(args) + 1: len(outs) + 1}
            in_specs += [pl.BlockSpec(memory_space=pl.ANY)] * 2
            args += list(prev)
    return pl.pallas_call(
        functools.partial(_rope_kernel, n_alias=len(aliases)),
        grid=(t // tr,),
        in_specs=in_specs,
        out_specs=out_specs,
        out_shape=out_shape,
        input_output_aliases=aliases,
        compiler_params=_cparams("parallel"),
    )(*args)


def _rope_tables(pos):
    def tab(half, reps):
        inv = jnp.exp(-math.log(ROPE_THETA) * jnp.arange(half, dtype=F32) / half)
        ang = pos[:, None] * inv[None, :]
        cos, sin = jnp.cos(ang), jnp.sin(ang)
        return jnp.tile(jnp.concatenate([cos, cos], 1), (1, reps)), jnp.tile(jnp.concatenate([-sin, sin], 1), (1, reps))

    c128, s128 = tab(HEAD_DIM_B // 2, 1)
    c64, s64 = tab(IDX_DIM // 2, 2)
    return c128, s128, c64, s64


def _pool_kernel(a_ref, prev_ref, w_ref, sc_ref, o_ref, ext_s, *, S, Tc, start, nch):
    ch = pl.program_id(1)
    R = Tc * S
    PS = POOL_BUF * S
    OFF = _round_up(PS, SUBLANES)

    @pl.when(ch == 0)
    def _():
        ext_s[OFF - PS:OFF, :] = prev_ref[0]

    x = a_ref[...]
    ext_s[OFF:OFF + R, :] = x
    t_loc = lax.broadcasted_iota(I32, (R, 1), 0) // S if S > 1 else lax.broadcasted_iota(I32, (R, 1), 0)
    pos1 = start + ch * Tc + t_loc + 1
    for gi, w in enumerate(POOL_WINDOWS):
        sl = slice(gi * POOL_GROUP, (gi + 1) * POOL_GROUP)
        xs = x[:, sl]
        acc = xs
        for j in range(1, w):
            acc = acc + ext_s[OFF - j * S:OFF - j * S + R, sl]
        cnt = jnp.minimum(w, pos1).astype(F32)
        mixed = acc / cnt - xs
        y = _dot(mixed.astype(BF16), w_ref[gi])
        o_ref[:, sl] = (y * sc_ref[:, sl]).astype(o_ref.dtype)
    if nch > 1:
        ext_s[OFF - PS:OFF, :] = ext_s[OFF + R - PS:OFF + R, :]


def _pool(u_main, prev, w_pool, scale, *, S, Tc, start):
    t = u_main.shape[0]
    R = Tc * S
    G = prev.shape[0]
    nch = t // (G * R)
    PS = POOL_BUF * S
    C = BRANCH_W
    return pl.pallas_call(
        functools.partial(_pool_kernel, S=S, Tc=Tc, start=start, nch=nch),
        grid=(G, nch),
        in_specs=[pl.BlockSpec((R, C), lambda g, c: (g * nch + c, 0)),
                  pl.BlockSpec((1, PS, C), lambda g, c: (g, 0, 0)),
                  pl.BlockSpec((len(POOL_WINDOWS), POOL_GROUP, POOL_GROUP), lambda g, c: (0, 0, 0)),
                  pl.BlockSpec((1, C), lambda g, c: (0, 0))],
        out_specs=pl.BlockSpec((R, C), lambda g, c: (g * nch + c, 0)),
        out_shape=jax.ShapeDtypeStruct((t, C), BF16),
        scratch_shapes=[pltpu.VMEM((_round_up(PS, SUBLANES) + R, C), F32)],
        compiler_params=_cparams("parallel", "arbitrary"),
    )(u_main, prev, w_pool, scale)


def _rglru_kernel(cx_ref, cg_ref, prev_ref, h0_ref, wc_ref, bc_ref, wa_ref, ba_ref, wx_ref, bx_ref, lam_ref,
                  o_ref, hl_ref, ext_s, a_s, b_s, h_s, *, S, Tc, nch):
    ch = pl.program_id(1)
    R = Tc * S
    PS = (CONV_W - 1) * S
    OFF = _round_up(PS, SUBLANES)

    @pl.when(ch == 0)
    def _():
        ext_s[OFF - PS:OFF, :] = prev_ref[0]
        h_s[...] = h0_ref[0]

    x = cx_ref[...]
    ext_s[OFF:OFF + R, :] = x
    xc = bc_ref[...] + x * wc_ref[CONV_W - 1:CONV_W, :]
    for j in range(CONV_W - 1):
        k = CONV_W - 1 - j
        xc = xc + ext_s[OFF - k * S:OFF - k * S + R, :] * wc_ref[j:j + 1, :]
    xb = xc.astype(BF16)
    for n in range(LRU_BLOCKS):
        sl = slice(n * LRU_BLOCK, (n + 1) * LRU_BLOCK)
        r = jax.nn.sigmoid(_dot(xb[:, sl], wa_ref[n]) + ba_ref[:, sl])
        i = jax.nn.sigmoid(_dot(xb[:, sl], wx_ref[n]) + bx_ref[:, sl])
        lam = lam_ref[:, sl]
        softplus_neg = jnp.maximum(-lam, 0.0) + jnp.log1p(jnp.exp(-jnp.abs(lam)))
        log_a = (-LRU_C) * r * softplus_neg
        a_s[:, sl] = jnp.exp(log_a)
        th = jnp.tanh(log_a)
        b_s[:, sl] = jnp.sqrt(-2.0 * th / (1.0 - th)) * (i * xc[:, sl])

    if S % SUBLANES == 0:
        def step(t, h):
            off = pl.multiple_of(t * S, S)
            h = a_s[pl.ds(off, S), :] * h + b_s[pl.ds(off, S), :]
            b_s[pl.ds(off, S), :] = h
            return h

        h = lax.fori_loop(0, Tc, step, h_s[...])
    else:
        assert S == 1 and Tc % SUBLANES == 0
        row =lax.broadcasted_iota(I32, (SUBLANES, a_s.shape[1]), 0)

        def tile_step(i, h):
            off = pl.multiple_of(i * SUBLANES, SUBLANES)
            A = a_s[pl.ds(off, SUBLANES), :]
            B = b_s[pl.ds(off, SUBLANES), :]
            for sh in (1, 2, 4):
                keep = row >= sh
                B = A * jnp.where(keep, pltpu.roll(B, sh, axis=0), 0.0) + B
                A = A * jnp.where(keep, pltpu.roll(A, sh, axis=0), 1.0)
            hs = A * h + B
            b_s[pl.ds(off, SUBLANES), :] = hs
            return hs[SUBLANES - 1:SUBLANES, :]

        h = lax.fori_loop(0, Tc // SUBLANES, tile_step, h_s[...])
    h_s[...] = h
    hl_ref[0] = h
    o_ref[...] = (_gelu_tanh(cg_ref[...]) * b_s[...]).astype(o_ref.dtype)
    if nch > 1:
        ext_s[OFF - PS:OFF, :] = ext_s[OFF + R - PS:OFF + R, :]


def _rglru(u_main, prev, h0, lp, *, S, Tc):
    t = u_main.shape[0]
    R = Tc * S
    G = prev.shape[0]
    nch = t // (G * R)
    PS = (CONV_W - 1) * S
    C = LRU_W
    vec = pl.BlockSpec((1, C), lambda g, c: (0, 0))
    blk = pl.BlockSpec((LRU_BLOCKS, LRU_BLOCK, LRU_BLOCK), lambda g, c: (0, 0, 0))
    return pl.pallas_call(
        functools.partial(_rglru_kernel, S=S, Tc=Tc, nch=nch),
        grid=(G, nch),
        in_specs=[pl.BlockSpec((R, C), lambda g, c: (g * nch + c, 3)),
                  pl.BlockSpec((R, C), lambda g, c: (g * nch + c, 4)),
                  pl.BlockSpec((1, PS, C), lambda g, c: (g, 0, 0)),
                  pl.BlockSpec((1, S, C), lambda g, c: (g, 0, 0)),
                  pl.BlockSpec((CONV_W, C), lambda g, c: (0, 0)), vec, blk, vec, blk, vec, vec],
        out_specs=[pl.BlockSpec((R, C), lambda g, c: (g * nch + c, 0)),
                   pl.BlockSpec((1, S, C), lambda g, c: (g, 0, 0))],
        out_shape=[jax.ShapeDtypeStruct((t, C), BF16), jax.ShapeDtypeStruct((G, S, C), F32)],
        scratch_shapes=[pltpu.VMEM((_round_up(PS, SUBLANES) + R, C), F32), pltpu.VMEM((R, C), F32),
                        pltpu.VMEM((R, C), F32), pltpu.VMEM((S, C), F32)],
        compiler_params=_cparams("parallel", "arbitrary"),
    )(u_main, u_main, prev, h0, lp["w_conv_c"], lp["b_conv_c"], lp["w_rg_a"], lp["b_rg_a"], lp["w_rg_x"],
      lp["b_rg_x"], lp["lru_lambda"])


def _ret_kernel(rq_ref, rk_ref, rv_ref, dg_ref, s0_ref, gn_ref, o_ref, sn_ref, s_s, *, c, spb):
    @pl.when(pl.program_id(1) == 0)
    def _():
        s_s[...] = s0_ref[...]

    ii = lax.broadcasted_iota(I32, (c, c), 0)
    jj = lax.broadcasted_iota(I32, (c, c), 1)
    dif = (ii - jj).astype(F32)
    tpos = lax.broadcasted_iota(I32, (c, 1), 0).astype(F32)
    mm = BF16 if c % 16 == 0 else F32
    for h in range(N_HEADS_D):
        lg = LOG_G[h]
        qs = slice(h * QK_DIM_D, (h + 1) * QK_DIM_D)
        vs = slice(h * V_DIM_D, (h + 1) * V_DIM_D)
        intra = jnp.where(dif >= 0.0, jnp.exp(lg * jnp.maximum(dif, 0.0)), 0.0)
        q_dec = jnp.exp(lg * (tpos + 1.0))
        k_dec = jnp.exp(lg * (c - 1.0 - tpos))
        for sq in range(spb):
            rows = slice(sq * c, (sq + 1) * c)
            q = rq_ref[rows, qs].astype(mm)
            k = rk_ref[rows, qs]
            v = rv_ref[rows, vs].astype(mm)
            att = _dot_nt(q, k.astype(mm)) * intra
            s = s_s[sq, h]
            o = _dot(att.astype(mm), v) + _dot(q, s.astype(mm)) * q_dec
            s_s[sq, h] = s * math.exp(lg * c) + _dot_tn((k * k_dec).astype(mm), v)
            mu = jnp.mean(o, axis=-1, keepdims=True)
            var = jnp.mean(jnp.square(o - mu), axis=-1, keepdims=True)
            y = (o - mu) * lax.rsqrt(var + EPS) * gn_ref[:, vs]
            dg = dg_ref[rows, vs]
            o_ref[rows, vs] = (dg * jax.nn.sigmoid(dg) * y).astype(o_ref.dtype)
    sn_ref[...] = s_s[...]


def _retention(rq, rk, rv, dg, s0, gn, *, c, v_col, g_col, spb):
    t = rq.shape[0]
    B = s0.shape[0]
    nch = t // (B * c)
    assert spb == 1 or nch == 1
    W = N_HEADS_D * QK_DIM_D
    R = spb * c
    st = pl.BlockSpec((spb, N_HEADS_D, QK_DIM_D, V_DIM_D), lambda b, i: (b, 0, 0, 0))
    return pl.pallas_call(
        functools.partial(_ret_kernel, c=c, spb=spb),
        grid=(B // spb, nch),
        in_specs=[pl.BlockSpec((R, W), lambda b, i: (b * nch + i, 0)),
                  pl.BlockSpec((R, W), lambda b, i: (b * nch + i, 0)),
                  pl.BlockSpec((R, BRANCH_W), lambda b, i: (b * nch + i, v_col)),
                  pl.BlockSpec((R, BRANCH_W), lambda b, i: (b * nch + i, g_col)),
                  st, pl.BlockSpec((1, BRANCH_W), lambda b, i: (0, 0))],
        out_specs=[pl.BlockSpec((R, BRANCH_W), lambda b, i: (b * nch + i, 0)), st],
        out_shape=[jax.ShapeDtypeStruct((t, BRANCH_W), BF16), jax.ShapeDtypeStruct(s0.shape, F32)],
        scratch_shapes=[pltpu.VMEM((spb, N_HEADS_D, QK_DIM_D, V_DIM_D), F32)],
        compiler_params=_cparams("parallel", "arbitrary"),
    )(rq, rk, rv, dg, s0, gn)


def _sortable(x):
    b = lax.bitcast_convert_type(x, I32)
    return b ^ ((b >> 31) & 0x7FFFFFFF)


def _count(m):
    return jnp.sum(jnp.where(m, 1.0, 0.0), axis=1, keepdims=True)


def _kth_largest_key(keys, kf, bits):
    m = keys.shape[0]
    t0 = jnp.where(_count(keys >= 0) >= kf, 0, INT_MIN).astype(I32)

    def step(nb, shift, t):
        digit = jnp.zeros((m, 1), I32)
        for v in range(1, 2 ** nb):
            cand = t | jnp.left_shift(jnp.int32(v), shift)
            digit = digit + jnp.where(_count(keys >= cand) >= kf, 1, 0)
        return t | jnp.left_shift(digit, shift)

    nfull, rem = divmod(31, bits)
    t = lax.fori_loop(0, nfull, lambda i, t: step(bits, 31 - bits * (i + 1), t), t0)
    return step(rem, 0, t) if rem else t


def _topk_mask(keys, valid, idx, k, bits):
    m, n = keys.shape
    kf = float(k)
    thr = _kth_largest_key(keys, kf, bits)
    gt = keys > thr
    eq = (keys == thr) & valid
    need = kf - _count(gt)
    excess = jnp.max(_count(eq) - need) > 0.0
    nbits = max(1, (n - 1).bit_length())

    def first_ties():
        def idx_step(i, j):
            cand = j | jnp.left_shift(jnp.int32(1), nbits - 1 - i)
            return jnp.where(_count(eq & (idx < cand)) <= need - 1.0, cand, j)

        return lax.fori_loop(0, nbits, idx_step, jnp.zeros((m, 1), I32))

    jthr = lax.cond(excess, first_ties, lambda: jnp.full((m, 1), 2 ** nbits, I32))
    return (gt & valid) | (eq & (idx <= jthr))


def _masked_attention(s, mask, v):
    s = jnp.where(mask, s, -jnp.inf)
    p = jnp.exp(s - jnp.max(s, axis=1, keepdims=True))
    return _dot(p.astype(BF16), v) / jnp.sum(p, axis=1, keepdims=True)


def _dsa_prompt_kernel(q_ref, kb_ref, vb_ref, iq4_ref, ik4_ref, iw_ref, o_ref, *, j0, topk):
    nq = q_ref.shape[0]
    nk = kb_ref.shape[0]
    t0 = (j0 + pl.program_id(1)) * nq
    ik4 = ik4_ref[...]
    score = jnp.zeros((nq, nk), F32)
    for h in range(N_IDX_HEADS):
        s = _dot_nt(iq4_ref[:, h * IDX4:(h + 1) * IDX4], ik4)
        score = score + jnp.maximum(s, 0.0) * iw_ref[:, IDX_DIM + h:IDX_DIM + h + 1]
    score = score * IDX_SCALE
    kpos = lax.broadcasted_iota(I32, (nq, nk), 1)
    valid = kpos <= t0 + lax.broadcasted_iota(I32, (nq, nk), 0)
    if nk > topk:
        mask = _topk_mask(_sortable(jnp.where(valid, score, -jnp.inf)), valid, kpos, topk, 1)
    else:
        mask = valid
    for g in range(N_KV_B):
        gs = slice(g * HEAD_DIM_B, (g + 1) * HEAD_DIM_B)
        kg = kb_ref[:, gs]
        vg = vb_ref[:, gs]
        for r in range(KV_REP):
            hs = slice((g * KV_REP + r) * HEAD_DIM_B, (g * KV_REP + r + 1) * HEAD_DIM_B)
            o_ref[:, hs] = _masked_attention(_dot_nt(q_ref[:, hs], kg), mask, vg).astype(o_ref.dtype)


def _dsa_prompt(q, kb, vb, iq4, ik4, ikw, *, n_seq, n_cls, qb):
    t = q.shape[0]
    s_len = t // n_seq
    nb = s_len // qb
    nbc = nb // n_cls
    topk = min(TOPK_MAX, s_len // 4)
    q, kb, vb, iq4, ik4, ikw = (a.reshape(n_seq, s_len, a.shape[-1]) for a in (q, kb, vb, iq4, ik4, ikw))
    outs = []
    for c in range(n_cls):
        nk = (c + 1) * nbc * qb
        qrow = lambda w, c=c: pl.BlockSpec((None, qb, w), lambda b, j: (b, c * nbc + j, 0))
        seq = lambda w, nk=nk: pl.BlockSpec((None, nk, w), lambda b, j: (b, 0, 0))
        outs.append(pl.pallas_call(
            functools.partial(_dsa_prompt_kernel, j0=c * nbc, topk=topk),
            grid=(n_seq, nbc),
            in_specs=[qrow(BRANCH_W), seq(N_KV_B * HEAD_DIM_B), seq(N_KV_B * HEAD_DIM_B), qrow(N_IDX_HEADS * IDX4),
                      seq(IDX4), qrow(LANES)],
            out_specs=pl.BlockSpec((None, qb, BRANCH_W), lambda b, j: (b, j, 0)),
            out_shape=jax.ShapeDtypeStruct((n_seq, nbc * qb, BRANCH_W), BF16),
            compiler_params=_cparams("parallel", "arbitrary"),
        )(q, kb, vb, iq4, ik4, ikw))
    return jnp.concatenate(outs, axis=1).reshape(t, BRANCH_W)


def _dsa_sample_kernel(pt_ref, qg_ref, iq4_ref, w_ref, kn_ref, vn_ref, ikn_ref, *rest, n_pages, t_new, spb):
    del pt_ref
    n_in = 3 * n_pages * spb
    page_refs, (o_ref, sc_s, k_s, v_s) = rest[:n_in], rest[n_in:]
    P = PAGE_SIZE
    pad = P - t_new
    nk = (n_pages + 1) * P
    past = n_pages * P

    for s in range(spb):
        idx_refs = page_refs[3 * n_pages * s:3 * n_pages * s + n_pages]
        k_refs = page_refs[3 * n_pages * s + n_pages:3 * n_pages * s + 2 * n_pages]
        v_refs = page_refs[3 * n_pages * s + 2 * n_pages:3 * n_pages * (s + 1)]
        iq4 = iq4_ref[s]
        w = w_ref[s]

        def page_scores(ik, page, keys_on_lanes):
            hi, lo = _split_bf16(ik)
            if keys_on_lanes:
                sc = _dot(iq4, jnp.concatenate([hi, lo, hi, lo], axis=0))
            else:
                sc = _dot_nt(iq4, jnp.concatenate([hi, lo, hi, lo], axis=1))
            r = jnp.maximum(sc, 0.0) * w
            acc = r[0:t_new]
            for h in range(1, N_IDX_HEADS):
                acc = acc + r[h * t_new:(h + 1) * t_new]
            sc_s[s * t_new:(s + 1) * t_new, page * P:(page + 1) * P] = acc * IDX_SCALE

        for p in range(n_pages):
            page_scores(idx_refs[p][0, 0], p, True)
            for g in range(N_KV_B):
                gs = slice(g * HEAD_DIM_B, (g + 1) * HEAD_DIM_B)
                k_s[s, p * P:(p + 1) * P, gs] = k_refs[p][0, 0, pl.ds(g, P, stride=N_KV_B), :].astype(BF16)
                v_s[s, p * P:(p + 1) * P, gs] = v_refs[p][0, 0, pl.ds(g, P, stride=N_KV_B), :].astype(BF16)
        page_scores(jnp.concatenate([ikn_ref[s], jnp.zeros((pad, IDX_DIM), F32)], axis=0), n_pages, False)
        zkv = jnp.zeros((pad, N_KV_B * HEAD_DIM_B), F32)
        k_s[s, past:nk, :] = jnp.concatenate([kn_ref[s], zkv], axis=0).astype(BF16)
        v_s[s, past:nk, :] = jnp.concatenate([vn_ref[s], zkv], axis=0).astype(BF16)

    rows = spb * t_new
    kpos = lax.broadcasted_iota(I32, (rows, nk), 1)
    valid = kpos <= past + (lax.broadcasted_iota(I32, (rows, nk), 0) & (t_new - 1))
    keys = _sortable(jnp.where(valid, sc_s[...], -jnp.inf))
    mask = _topk_mask(keys, valid, kpos, min(TOPK_MAX, (past + t_new) // 4), 3)
    mask_i = jnp.where(mask, 1, 0).astype(I32)
    for s in range(spb):
        m_s = jnp.concatenate([mask_i[s * t_new:(s + 1) * t_new]] * KV_REP, axis=0) > 0
        for g in range(N_KV_B):
            gs = slice(g * HEAD_DIM_B, (g + 1) * HEAD_DIM_B)
            o_ref[s, g] = _masked_attention(_dot_nt(qg_ref[s, g], k_s[s, :, gs]), m_s, v_s[s, :, gs]).astype(o_ref.dtype)


def _dsa_sample(qg, iq4r, wr, k_new, v_new, ik_new, cache_k, cache_v, cache_idx_k, page_table, layer, spb):
    B, _, rows, hd = qg.shape
    t_new = k_new.shape[1]
    assert t_new & (t_new - 1) == 0 and B % spb == 0
    n_pages = page_table.shape[1]
    kvw = N_KV_B * HEAD_DIM_B
    per_seq = lambda shape: pl.BlockSpec((spb,) + shape, lambda b, pt: (b,) + (0,) * len(shape))

    def page_spec(rows_, width, s, p):
        return pl.BlockSpec((1, 1, rows_, width), lambda b, pt: (layer, pt[b * spb + s, p], 0, 0))

    in_specs = [per_seq(qg.shape[1:]), per_seq(iq4r.shape[1:]), per_seq(wr.shape[1:]), per_seq((t_new, kvw)),
                per_seq((t_new, kvw)), per_seq((t_new, IDX_DIM))]
    pages = []
    for s in range(spb):
        in_specs += [page_spec(IDX_DIM, PAGE_SIZE, s, p) for p in range(n_pages)]
        in_specs += [page_spec(PAGE_SIZE * N_KV_B, HEAD_DIM_B, s, p) for p in range(n_pages)]
        in_specs += [page_spec(PAGE_SIZE * N_KV_B, HEAD_DIM_B, s, p) for p in range(n_pages)]
        pages += [cache_idx_k] * n_pages + [cache_k] * n_pages + [cache_v] * n_pages
    nk = (n_pages + 1) * PAGE_SIZE
    grid_spec = pltpu.PrefetchScalarGridSpec(
        num_scalar_prefetch=1, grid=(B // spb,), in_specs=in_specs,
        out_specs=pl.BlockSpec((spb,) + qg.shape[1:], lambda b, pt: (b, 0, 0, 0)),
        scratch_shapes=[pltpu.VMEM((spb * t_new, nk), F32), pltpu.VMEM((spb, nk, kvw), BF16),
                        pltpu.VMEM((spb, nk, kvw), BF16)])
    return pl.pallas_call(
        functools.partial(_dsa_sample_kernel, n_pages=n_pages, t_new=t_new, spb=spb),
        grid_spec=grid_spec,
        out_shape=jax.ShapeDtypeStruct(qg.shape, BF16),
        compiler_params=_cparams("arbitrary"),
    )(page_table, qg, iq4r, wr, k_new, v_new, ik_new, *pages)


def _merge_kernel(xn_ref, a_ref, b_ref, c_ref, d_ref, wg0, wg1, wg2, wg3, bg0, bg1, bg2, bg3, wb_ref, o_ref):
    xn = xn_ref[...]
    acc = None
    for n, (br, wg, bg) in enumerate(zip((a_ref, b_ref, c_ref, d_ref), (wg0, wg1, wg2, wg3), (bg0, bg1, bg2, bg3))):
        term = jax.nn.sigmoid(_dot(xn, wg[...]) + bg[...]) * _dot(br[...], wb_ref[n])
        acc = term if acc is None else acc + term
    o_ref[...] = acc.astype(o_ref.dtype)


def _merge(xn, branches, w_gate, b_gate, w_branch, l, tm, tn):
    t, d = xn.shape
    nj = d // tn
    row = lambda w: pl.BlockSpec((tm, w), lambda i, j: (i, 0))
    wg = [pl.BlockSpec((None, d, tn), functools.partial(lambda i, j, n: (l, 0, n * nj + j), n=n))
          for n in range(N_BRANCH)]
    bg = [pl.BlockSpec((1, tn), functools.partial(lambda i, j, n: (0, n * nj + j), n=n)) for n in range(N_BRANCH)]
    return pl.pallas_call(
        _merge_kernel,
        grid=(t // tm, nj),
        in_specs=[row(d)] + [row(BRANCH_W)] * N_BRANCH + wg + bg
                 + [pl.BlockSpec((None, N_BRANCH, BRANCH_W, tn), lambda i, j: (l, 0, 0, j))],
        out_specs=pl.BlockSpec((tm, tn), lambda i, j: (i, j)),
        out_shape=jax.ShapeDtypeStruct((t, d), BF16),
        compiler_params=_cparams("parallel", "arbitrary"),
    )(xn, *branches, *([w_gate] * N_BRANCH), *([b_gate] * N_BRANCH), w_branch)


def _outproj_kernel(m_ref, x_ref, w_ref, g_ref, o_ref):
    o_ref[...] = x_ref[...] + _rms(_dot(m_ref[...], w_ref[...]), g_ref[...])


def _outproj(merged, x, w_o, g, l, tm):
    t, d = x.shape
    return pl.pallas_call(
        _outproj_kernel,
        grid=(t // tm,),
        in_specs=[pl.BlockSpec((tm, d), lambda i: (i, 0)), pl.BlockSpec((tm, d), lambda i: (i, 0)),
                  pl.BlockSpec((None, d, d), lambda i: (l, 0, 0)), pl.BlockSpec((1, d), lambda i: (0, 0))],
        out_specs=pl.BlockSpec((tm, d), lambda i: (i, 0)),
        out_shape=jax.ShapeDtypeStruct((t, d), F32),
        compiler_params=_cparams("parallel"),
    )(merged, x, w_o, g)


def _ffn_kernel(x_ref, gpre_ref, wug_ref, wuv_ref, wcg_ref, wcv_ref, bcg_ref, bcv_ref, wd_ref, gpost_ref,
                stg_ref, stv_ref, o_ref, ng_ref, nv_ref, h_s, acc_s, cg_s, cv_s, ext_s, *, S, tiles_per_group):
    i = pl.program_id(0)
    j = pl.program_id(1)
    tm = x_ref.shape[0]
    CR = cg_s.shape[1]

    @pl.when(j == 0)
    def _():
        h_s[...] = _rms(x_ref[...], gpre_ref[...]).astype(BF16)
        acc_s[...] = jnp.zeros_like(acc_s)

    @pl.when(i % tiles_per_group == 0)
    def _():
        cg_s[j] = stg_ref[0]
        cv_s[j] = stv_ref[0]

    h = h_s[...]

    def conv(w_ref, wc_ref, bc_ref, c_s, new_ref):
        u = _dot(h, w_ref[...])
        ext_s[0:CR, :] = c_s[j]
        ext_s[CR:CR + tm, :] = u
        y = bc_ref[...] + u * wc_ref[2:3, :]
        y = y + ext_s[CR - S:CR - S + tm, :] * wc_ref[1:2, :]
        y = y + ext_s[CR - 2 * S:CR - 2 * S + tm, :] * wc_ref[0:1, :]
        tail = ext_s[tm:tm + CR, :]
        c_s[j] = tail
        new_ref[0] = tail
        return y

    gate = conv(wug_ref, wcg_ref, bcg_ref, cg_s, ng_ref)
    val = conv(wuv_ref, wcv_ref, bcv_ref, cv_s, nv_ref)
    acc_s[...] += _dot((_gelu_tanh(gate) * val).astype(BF16), wd_ref[...])

    @pl.when(j == pl.num_programs(1) - 1)
    def _():
        o_ref[...] = x_ref[...] + _rms(acc_s[...], gpost_ref[...])


def _ffn(x, lp, state, *, S, tm, tn):
    t, d = x.shape
    G, CR, _ = state.shape
    nj = D_FF // tn
    ni = t // tm
    tiles_per_group = ni // G
    grp = lambda i, j: (i // tiles_per_group, 0, j)
    tile = lambda i, j: (i, 0, j)
    grp_v = lambda i, j: (i // tiles_per_group, 0, nj + j)
    col = lambda r: pl.BlockSpec((r, tn), lambda i, j: (0, j))
    col_v = lambda r: pl.BlockSpec((r, tn), lambda i, j: (0, nj + j))
    vec = pl.BlockSpec((1, d), lambda i, j: (0, 0))
    l = lp["layer"]
    xo, ng, nv = pl.pallas_call(
        functools.partial(_ffn_kernel, S=S, tiles_per_group=tiles_per_group),
        grid=(ni, nj),
        in_specs=[pl.BlockSpec((tm, d), lambda i, j: (i, 0)), vec,
                  pl.BlockSpec((None, d, tn), lambda i, j: (l, 0, j)),
                  pl.BlockSpec((None, d, tn), lambda i, j: (l, 0, nj + j)),
                  col(FFN_CONV_W), col_v(FFN_CONV_W), col(1), col_v(1),
                  pl.BlockSpec((None, tn, d), lambda i, j: (l, j, 0)), vec,
                  pl.BlockSpec((1, CR, tn), grp), pl.BlockSpec((1, CR, tn), grp_v)],
        out_specs=[pl.BlockSpec((tm, d), lambda i, j: (i, 0)),
                   pl.BlockSpec((1, CR, tn), tile), pl.BlockSpec((1, CR, tn), tile)],
        out_shape=[jax.ShapeDtypeStruct((t, d), F32), jax.ShapeDtypeStruct((ni, CR, D_FF), F32),
                   jax.ShapeDtypeStruct((ni, CR, D_FF), F32)],
        scratch_shapes=[pltpu.VMEM((tm, d), BF16), pltpu.VMEM((tm, d), F32), pltpu.VMEM((nj, CR, tn), F32),
                        pltpu.VMEM((nj, CR, tn), F32), pltpu.VMEM((CR + tm, tn), F32)],
        compiler_params=_cparams("arbitrary", "arbitrary"),
    )(x, lp["g_ffn_pre"], lp["w_up"], lp["w_up"], lp["w_ffn_conv"], lp["w_ffn_conv"], lp["b_ffn_conv"],
      lp["b_ffn_conv"], lp["w_down"], lp["g_ffn_post"], state, state)
    last = slice(tiles_per_group - 1, None, tiles_per_group)
    return xo, ng[last], nv[last]


def _layer_front(x, lp, tabs, *, tm, kv_final=None):
    l = lp["layer"]
    u_main, xn = _inproj(x, lp["g_mix_pre"], lp["w_in_main"], l, tm, 512)
    u_idx = _inproj_split(x, lp["g_mix_pre"], lp["w_in_idx_hi"], lp["w_in_idx_lo"], l, tm, 384)
    return u_main, xn, _rope_prep(u_main, u_idx, tabs, 256, kv_final)


def _layer_back(x, xn, branches, lp, ffn_state, *, S, tm, tn_ffn):
    l = lp["layer"]
    merged = _merge(xn, branches, lp["w_gate"], lp["b_gate"], lp["w_branch"], l, tm, 512)
    x1 = _outproj(merged, x, lp["w_o"], lp["g_mix_post"], l, tm)
    return _ffn(x1, lp, ffn_state, S=S, tm=tm, tn=tn_ffn)


def _prep_big_weights(p):
    w_in = p["w_in"]
    o_iq = 3072
    o_cx = o_iq + N_IDX_HEADS * IDX_DIM + IDX_DIM + N_IDX_HEADS
    w_idx = jnp.pad(w_in[:, :, o_iq:o_cx], ((0, 0), (0, 0), (0, N_IDX - (o_cx - o_iq))))
    hi = w_idx.astype(BF16)
    return dict(
        w_in_main=jnp.concatenate([w_in[:, :, :o_iq], w_in[:, :, o_cx:]], axis=2).astype(BF16),
        w_in_idx_hi=hi, w_in_idx_lo=(w_idx - hi.astype(F32)).astype(BF16),
        w_branch=p["w_branch"].astype(BF16), w_gate=p["w_gate"].astype(BF16), w_o=p["w_o"].astype(BF16),
        w_up=p["w_up"].astype(BF16), w_down=p["w_down"].astype(BF16))


def _prep_layer_params(p, big, l):
    row = lambda a: a[l][None, :]
    return dict(
        big, layer=l,
        g_mix_pre=row(p["g_mix_pre"]),
        w_pool=p["w_pool"][l].astype(BF16), pool_scale=row(p["pool_scale"]),
        w_conv_c=p["w_conv_c"][l], b_conv_c=row(p["b_conv_c"]),
        w_rg_a=p["w_rg_a"][l].astype(BF16), b_rg_a=row(p["b_rg_a"]),
        w_rg_x=p["w_rg_x"][l].astype(BF16), b_rg_x=row(p["b_rg_x"]),
        lru_lambda=row(p["lru_lambda"]), ret_gn=row(p["ret_gn"]), b_gate=row(p["b_gate"]),
        g_mix_post=row(p["g_mix_post"]), g_ffn_pre=row(p["g_ffn_pre"]),
        w_ffn_conv=p["w_ffn_conv"][l], b_ffn_conv=row(p["b_ffn_conv"]), g_ffn_post=row(p["g_ffn_post"]))


def _to_time_major(a, nb, nt):
    return jnp.swapaxes(a, 0, 1).reshape((nt * nb,) + a.shape[2:])


def _to_seq_major(a, nb, nt):
    return jnp.swapaxes(a.reshape((nt, nb) + a.shape[1:]), 0, 1)


def kernel(x_prompt, x_sample, cache_k, cache_v, cache_idx_k, state_pool, state_conv, state_rglru, state_ret, state_ffn_conv, page_table, g_mix_pre, w_in, w_pool, pool_scale, w_conv_c, b_conv_c, w_rg_a, b_rg_a, w_rg_x, b_rg_x, lru_lambda, ret_gn, w_branch, w_gate, b_gate, w_o, g_mix_post, g_ffn_pre, w_up, w_ffn_conv, b_ffn_conv, w_down, g_ffn_post):
    params = dict(g_mix_pre=g_mix_pre, w_in=w_in, w_pool=w_pool, pool_scale=pool_scale, w_conv_c=w_conv_c,
                  b_conv_c=b_conv_c, w_rg_a=w_rg_a, b_rg_a=b_rg_a, w_rg_x=w_rg_x, b_rg_x=b_rg_x,
                  lru_lambda=lru_lambda, ret_gn=ret_gn, w_branch=w_branch, w_gate=w_gate, b_gate=b_gate, w_o=w_o,
                  g_mix_post=g_mix_post, g_ffn_pre=g_ffn_pre, w_up=w_up, w_ffn_conv=w_ffn_conv,
                  b_ffn_conv=b_ffn_conv, w_down=w_down, g_ffn_post=g_ffn_post)
    depth = w_in.shape[0]
    bp, seq, d = x_prompt.shape
    bs, tdec, _ = x_sample.shape
    past = page_table.shape[1] * PAGE_SIZE
    n_pool = cache_k.shape[1]
    ck = cache_k.reshape(depth, n_pool, PAGE_SIZE * N_KV_B, HEAD_DIM_B)
    cv = cache_v.reshape(depth, n_pool, PAGE_SIZE * N_KV_B, HEAD_DIM_B)
    cik_t = jnp.swapaxes(cache_idx_k, 2, 3)
    big = _prep_big_weights(params)

    tabs_p = _rope_tables(jnp.arange(seq, dtype=F32))
    tabs_s = _rope_tables(jnp.repeat(past + jnp.arange(tdec, dtype=F32), bs))
    xp = x_prompt.reshape(bp * seq, d)
    xs = _to_time_major(x_sample, bs, tdec)
    cr_p = max(SUBLANES, (FFN_CONV_W - 1))
    outs_p, outs_s = [], []
    kv_p = None
    for l in range(depth):
        lp = _prep_layer_params(params, big, l)
        u, xn, (q, _, kb, vb, iq4, ik4, ikw, rq, rk, *kv_p) = _layer_front(xp, lp, tabs_p, tm=1024,
                                                                         kv_final=(l, depth, kv_p))
        o_a = _pool(u, jnp.zeros((bp, POOL_BUF, BRANCH_W), F32), lp["w_pool"], lp["pool_scale"], S=1, Tc=256, start=0)
        o_b = _dsa_prompt(q, kb, vb, iq4, ik4, ikw, n_seq=bp, n_cls=8, qb=256)
        o_c, h_p = _rglru(u, jnp.zeros((bp, CONV_W - 1, LRU_W), F32), jnp.zeros((bp, 1, LRU_W), F32), lp, S=1, Tc=256)
        o_d, s_p = _retention(rq, rk, u, u, jnp.zeros((bp, N_HEADS_D, QK_DIM_D, V_DIM_D), F32), lp["ret_gn"],
                              c=RET_CHUNK if seq % RET_CHUNK == 0 else seq, v_col=6, g_col=7, spb=1)
        xp, ng, nv = _layer_back(xp, xn, (o_a, o_b, o_c, o_d), lp, jnp.zeros((bp, cr_p, 2 * D_FF), F32),
                                 S=1, tm=512, tn_ffn=512)
        u3 = u.reshape(bp, seq, N_MAIN)
        outs_p.append((
            None, None,
            ikw.reshape(bp, seq, LANES)[:, :, :IDX_DIM], u3[:, seq - POOL_BUF:, 0:BRANCH_W],
            u3[:, seq - (CONV_W - 1):, 3072:3072 + LRU_W], h_p[:, 0], s_p,
            jnp.concatenate([ng, nv], axis=-1)[:, cr_p - (FFN_CONV_W - 1):]))
        u, xn, (q, k, kb, vb, iq4, ik4, ikw, rq, rk) = _layer_front(xs, lp, tabs_s, tm=1024)
        o_a = _pool(u, _to_time_major(state_pool[l], bs, POOL_BUF)[None], lp["w_pool"], lp["pool_scale"],
                    S=bs, Tc=tdec, start=past)
        o_c, h_s = _rglru(u, _to_time_major(state_conv[l], bs, CONV_W - 1)[None], state_rglru[l][None], lp,
                          S=bs, Tc=tdec)
        sm = lambda a: _to_seq_major(a, bs, tdec)
        u_sm = sm(u)
        o_d, s_s = _retention(sm(rq).reshape(bs * tdec, -1), sm(rk).reshape(bs * tdec, -1),
                              u_sm[:, :, 6144:7168].reshape(bs * tdec, -1), u_sm[:, :, 7168:8192].reshape(bs * tdec, -1),
                              state_ret[l], lp["ret_gn"], c=tdec, v_col=0, g_col=0, spb=8)
        o_d = _to_time_major(o_d.reshape(bs, tdec, BRANCH_W), bs, tdec)
        qg = sm(q).reshape(bs, tdec, N_KV_B, KV_REP, HEAD_DIM_B).transpose(0, 2, 3, 1, 4)
        qg = qg.reshape(bs, N_KV_B, KV_REP * tdec, HEAD_DIM_B)
        iq4r = sm(iq4).reshape(bs, tdec, N_IDX_HEADS, IDX4).transpose(0, 2, 1, 3).reshape(bs, N_IDX_HEADS * tdec, IDX4)
        ikw_sm = sm(ikw)
        wr = ikw_sm[:, :, IDX_DIM:IDX_DIM + N_IDX_HEADS].transpose(0, 2, 1).reshape(bs, N_IDX_HEADS * tdec, 1)
        k_sm = sm(k)
        v_sm = u_sm[:, :, 2560:3072]
        o_b = _dsa_sample(qg, iq4r, wr, k_sm, v_sm, ikw_sm[:, :, :IDX_DIM], ck, cv, cik_t, page_table, l, 2)
        o_b = o_b.reshape(bs, N_KV_B, KV_REP, tdec, HEAD_DIM_B).transpose(3, 0, 1, 2, 4).reshape(tdec * bs, BRANCH_W)
        ffn_state = _to_time_major(state_ffn_conv[l], bs, FFN_CONV_W - 1)[None]
        xs, ng, nv = _layer_back(xs, xn, (o_a, o_b, o_c, o_d), lp, ffn_state, S=bs, tm=512, tn_ffn=256)
        ffn_new = _to_seq_major(jnp.concatenate([ng, nv], axis=-1)[0], bs, FFN_CONV_W - 1)
        a_in = u_sm[:, :, 0:BRANCH_W]
        c_x = u_sm[:, :, 3072:3072 + LRU_W]
        outs_s.append((
            k_sm.reshape(bs, tdec, N_KV_B, HEAD_DIM_B), v_sm.reshape(bs, tdec, N_KV_B, HEAD_DIM_B),
            ikw_sm[:, :, :IDX_DIM],
            jnp.concatenate([state_pool[l], a_in], axis=1)[:, tdec:],
            jnp.concatenate([state_conv[l], c_x], axis=1)[:, tdec:],
            h_s[0], s_s, ffn_new))

    stk = lambda outs, i: jnp.stack([o[i] for o in outs], axis=0)
    res = [xp.reshape(bp, seq, d), _to_seq_major(xs, bs, tdec)]
    for i in range(8):
        p_i = kv_p[i].reshape(depth, bp, seq, N_KV_B, HEAD_DIM_B) if i < 2 else stk(outs_p, i)
        res += [p_i, stk(outs_s, i)]
    return tuple(res)
```

```python
import functools
import math

import jax
import jax.numpy as jnp
from jax import lax
from jax.experimental import pallas as pl
from jax.experimental.pallas import tpu as pltpu

F32 = jnp.float32
BF16 = jnp.bfloat16
I32 = jnp.int32

SUBLANES = 8
LANES = 128
VMEM_LIMIT_BYTES = 56 * 1024 * 1024

D_MODEL = 2048
BRANCH_W = D_MODEL // 2
N_BRANCH = 4
POOL_WINDOWS = (2, 4, 8, 16)
POOL_GROUP = BRANCH_W // len(POOL_WINDOWS)
POOL_BUF = max(POOL_WINDOWS) - 1
N_HEADS_B = 8
HEAD_DIM_B = BRANCH_W // N_HEADS_B
N_KV_B = 4
KV_REP = N_HEADS_B // N_KV_B
N_IDX_HEADS = 16
IDX_DIM = 64
TOPK_MAX = 256
ATT_BLOCK = 128
ROPE_THETA = 10000.0
ATT_SCALE = HEAD_DIM_B ** -0.5
IDX_SCALE = (IDX_DIM * N_IDX_HEADS) ** -0.5
LRU_W = BRANCH_W
LRU_BLOCKS = 4
LRU_BLOCK = LRU_W // LRU_BLOCKS
LRU_C = 8.0
CONV_W = 4
N_HEADS_D = 8
QK_DIM_D = BRANCH_W // (2 * N_HEADS_D)
V_DIM_D = BRANCH_W // N_HEADS_D
RET_CHUNK = 128
D_FF = 11 * D_MODEL // 4
FFN_CONV_W = 3
EPS = 1e-6
PAGE_SIZE = 128

N_MAIN = 8192
N_IDX = 1152
IDX4 = 4 * IDX_DIM

LOG_G = tuple(math.log1p(-(2.0 ** (-5.0 - h))) for h in range(N_HEADS_D))
INT_MIN = -2 ** 31


def _round_up(n, m):
    return (n + m - 1) // m * m


def _cparams(*sem):
    return pltpu.CompilerParams(dimension_semantics=sem, vmem_limit_bytes=VMEM_LIMIT_BYTES)


def _dot(a, b):
    return jnp.dot(a, b, preferred_element_type=F32)


def _dot_nt(a, b):
    return lax.dot_general(a, b, (((1,), (1,)), ((), ())), preferred_element_type=F32)


def _dot_tn(a, b):
    return lax.dot_general(a, b, (((0,), (0,)), ((), ())), preferred_element_type=F32)


def _rms(x, g):
    return x * lax.rsqrt(jnp.mean(x * x, axis=-1, keepdims=True) + EPS) * g


def _gelu_tanh(x):
    return x * (0.5 * (1.0 + jnp.tanh(0.7978845608028654 * (x + 0.044715 * (x * x * x)))))


def _split_bf16(x):
    hi = x.astype(BF16)
    lo = (x - hi.astype(F32)).astype(BF16)
    return hi, lo


def _inproj_kernel(x_ref, g_ref, w_ref, u_ref, xn_ref):
    @pl.when(pl.program_id(1) == 0)
    def _():
        xn_ref[...] = _rms(x_ref[...], g_ref[...]).astype(BF16)

    u_ref[...] = _dot_nt(xn_ref[...], w_ref[...])


def _inproj(x, g, w, l, tm, tn):
    t, d = x.shape
    n = w.shape[1]
    return pl.pallas_call(
        _inproj_kernel,
        grid=(t // tm, n // tn),
        in_specs=[pl.BlockSpec((tm, d), lambda i, j: (i, 0)),
                  pl.BlockSpec((1, d), lambda i, j: (0, 0)),
                  pl.BlockSpec((None, tn, d), lambda i, j: (l, j, 0))],
        out_specs=[pl.BlockSpec((tm, tn), lambda i, j: (i, j)),
                   pl.BlockSpec((tm, d), lambda i, j: (i, 0))],
        out_shape=[jax.ShapeDtypeStruct((t, n), F32), jax.ShapeDtypeStruct((t, d), BF16)],
        compiler_params=_cparams("parallel", "arbitrary"),
    )(x, g, w)


def _inproj_split_kernel(x_ref, g_ref, w3_ref, u_ref, x3_s):
    d = x_ref.shape[1]

    @pl.when(pl.program_id(1) == 0)
    def _():
        hi, lo = _split_bf16(_rms(x_ref[...], g_ref[...]))
        x3_s[:, 0:d] = hi
        x3_s[:, d:2 * d] = hi
        x3_s[:, 2 * d:3 * d] = lo

    u_ref[...] = _dot_nt(x3_s[...], w3_ref[...])


def _inproj_split(x, g, w3, l, tm, tn):
    t, d = x.shape
    n = w3.shape[1]
    return pl.pallas_call(
        _inproj_split_kernel,
        grid=(t // tm, n // tn),
        in_specs=[pl.BlockSpec((tm, d), lambda i, j: (i, 0)),
                  pl.BlockSpec((1, d), lambda i, j: (0, 0)),
                  pl.BlockSpec((None, tn, 3 * d), lambda i, j: (l, j, 0))],
        out_specs=pl.BlockSpec((tm, tn), lambda i, j: (i, j)),
        out_shape=jax.ShapeDtypeStruct((t, n), F32),
        scratch_shapes=[pltpu.VMEM((tm, 3 * d), BF16)],
        compiler_params=_cparams("parallel", "arbitrary"),
    )(x, g, w3)


def _rope128(xs, cos, sin):
    return xs * cos + pltpu.roll(xs, HEAD_DIM_B // 2, axis=1) * sin


def _rope64(xs, cos, sin, first_half):
    rot = jnp.where(first_half, pltpu.roll(xs, LANES - IDX_DIM // 2, axis=1), pltpu.roll(xs, IDX_DIM // 2, axis=1))
    return xs * cos + rot * sin


def _rope_kernel(bq_ref, bk_ref, bv_ref, dq_ref, dk_ref, ui_ref, c128_ref, s128_ref, c64_ref, s64_ref,
                 *rest, n_alias):
    q_ref, k_ref, kb_ref, vb_ref, iq4_ref, ik4_ref, ikw_ref, rq_ref, rk_ref = rest[n_alias:n_alias + 9]
    final = rest[n_alias + 9:]
    rows = bq_ref.shape[0]
    c128, s128, c64, s64 = c128_ref[...], s128_ref[...], c64_ref[...], s64_ref[...]
    lane = lax.broadcasted_iota(I32, (rows, LANES), 1)
    first_half = (lane & (IDX_DIM // 2)) == 0
    low = lane < IDX_DIM
    for c in range(N_HEADS_B):
        sl = slice(c * LANES, (c + 1) * LANES)
        q_ref[:, sl] = (_rope128(bq_ref[:, sl], c128, s128) * ATT_SCALE).astype(BF16)
    for c in range(N_KV_B):
        sl = slice(c * LANES, (c + 1) * LANES)
        kr = _rope128(bk_ref[:, sl], c128, s128)
        k_ref[:, sl] = kr
        kb_ref[:, sl] = kr.astype(BF16)
        if final:
            final[0][pl.ds(c, rows, stride=N_KV_B), :] = kr
            final[1][pl.ds(c, rows, stride=N_KV_B), :] = bv_ref[:, sl]
    vb_ref[...] = bv_ref[...].astype(BF16)
    for c in range(N_HEADS_D * QK_DIM_D // LANES):
        sl = slice(c * LANES, (c + 1) * LANES)
        rq_ref[:, sl] = _rope64(dq_ref[:, sl], c64, s64, first_half)
        rk_ref[:, sl] = _rope64(dk_ref[:, sl], c64, s64, first_half) * (QK_DIM_D ** -0.5)
    for c in range(N_IDX_HEADS // 2):
        y = _rope64(ui_ref[:, c * LANES:(c + 1) * LANES], c64, s64, first_half)
        yr = pltpu.roll(y, IDX_DIM, axis=1)
        for hh, dup in enumerate((jnp.where(low, y, yr), jnp.where(low, yr, y))):
            hi, lo = _split_bf16(dup)
            base = (2 * c + hh) * IDX4
            iq4_ref[:, base:base + LANES] = hi
            iq4_ref[:, base + LANES:base + 2 * LANES] = lo
    raw = ui_ref[:, N_IDX_HEADS * IDX_DIM:N_IDX_HEADS * IDX_DIM + LANES]
    y = _rope64(raw, c64, s64, first_half)
    ikw_ref[...] = jnp.where(low, y, raw)
    dup = jnp.where(low, y, pltpu.roll(y, IDX_DIM, axis=1))
    hi = dup.astype(BF16).astype(F32)
    hilo = jnp.where(low, hi, dup - hi).astype(BF16)
    ik4_ref[:, 0:LANES] = hilo
    ik4_ref[:, LANES:2 * LANES] = hilo


def _rope_prep(u_main, u_idx, tabs, tr, kv_final=None):
    t = u_main.shape[0]
    nt = tabs[0].shape[0] // tr
    row = lambda w, c: pl.BlockSpec((tr, w), lambda i: (i, c))
    tab = pl.BlockSpec((tr, LANES), lambda i: (i % nt, 0))
    outs = [(BRANCH_W, BF16), (N_KV_B * HEAD_DIM_B, F32), (N_KV_B * HEAD_DIM_B, BF16), (N_KV_B * HEAD_DIM_B, BF16),
            (N_IDX_HEADS * IDX4, BF16), (IDX4, BF16), (LANES, F32), (N_HEADS_D * QK_DIM_D, F32),
            (N_HEADS_D * QK_DIM_D, F32)]
    in_specs = [row(1024, 1), row(512, 4), row(512, 5), row(512, 10), row(512, 11), row(N_IDX, 0), tab, tab, tab, tab]
    args = [u_main, u_main, u_main, u_main, u_main, u_idx, *tabs]
    out_specs = [row(w, 0) for w, _ in outs]
    out_shape = [jax.ShapeDtypeStruct((t, w), dt) for w, dt in outs]
    aliases = {}
    if kv_final is not None:
        layer, depth, prev = kv_final
        out_specs += [pl.BlockSpec((None, tr * N_KV_B, HEAD_DIM_B), lambda i: (layer, i, 0))] * 2
        out_shape += [jax.ShapeDtypeStruct((depth, t * N_KV_B, HEAD_DIM_B), F32)] * 2
        if prev is not None:
            aliases = {len(args): len(outs), len(args) + 1: len(outs) + 1}
            in_specs += [pl.BlockSpec(memory_space=pl.ANY)] * 2
            args += list(prev)
    return pl.pallas_call(
        functools.partial(_rope_kernel, n_alias=len(aliases)),
        grid=(t // tr,),
        in_specs=in_specs,
        out_specs=out_specs,
        out_shape=out_shape,
        input_output_aliases=aliases,
        compiler_params=_cparams("parallel"),
    )(*args)


def _rope_tables(pos):
    def tab(half, reps):
        inv = jnp.exp(-math.log(ROPE_THETA) * jnp.arange(half, dtype=F32) / half)
        ang = pos[:, None] * inv[None, :]
        cos, sin = jnp.cos(ang), jnp.sin(ang)
        return jnp.tile(jnp.concatenate([cos, cos], 1), (1, reps)), jnp.tile(jnp.concatenate([-sin, sin], 1), (1, reps))

    c128, s128 = tab(HEAD_DIM_B // 2, 1)
    c64, s64 = tab(IDX_DIM // 2, 2)
    return c128, s128, c64, s64


def _pool_kernel(a_ref, prev_ref, w_ref, sc_ref, o_ref, ext_s, *, S, Tc, start, nch):
    ch = pl.program_id(1)
    R = Tc * S
    PS = POOL_BUF * S
    OFF = _round_up(PS, SUBLANES)

    @pl.when(ch == 0)
    def _():
        ext_s[OFF - PS:OFF, :] = prev_ref[0]

    x = a_ref[...]
    ext_s[OFF:OFF + R, :] = x
    t_loc = lax.broadcasted_iota(I32, (R, 1), 0) // S if S > 1 else lax.broadcasted_iota(I32, (R, 1), 0)
    pos1 = start + ch * Tc + t_loc + 1
    for gi, w in enumerate(POOL_WINDOWS):
        sl = slice(gi * POOL_GROUP, (gi + 1) * POOL_GROUP)
        xs = x[:, sl]
        acc = xs
        for j in range(1, w):
            acc = acc + ext_s[OFF - j * S:OFF - j * S + R, sl]
        cnt = jnp.minimum(w, pos1).astype(F32)
        mixed = acc / cnt - xs
        y = _dot(mixed.astype(BF16), w_ref[gi])
        o_ref[:, sl] = (y * sc_ref[:, sl]).astype(o_ref.dtype)
    if nch > 1:
        ext_s[OFF - PS:OFF, :] = ext_s[OFF + R - PS:OFF + R, :]


def _pool(u_main, prev, w_pool, scale, *, S, Tc, start):
    t = u_main.shape[0]
    R = Tc * S
    G = prev.shape[0]
    nch = t // (G * R)
    PS = POOL_BUF * S
    C = BRANCH_W
    return pl.pallas_call(
        functools.partial(_pool_kernel, S=S, Tc=Tc, start=start, nch=nch),
        grid=(G, nch),
        in_specs=[pl.BlockSpec((R, C), lambda g, c: (g * nch + c, 0)),
                  pl.BlockSpec((1, PS, C), lambda g, c: (g, 0, 0)),
                  pl.BlockSpec((len(POOL_WINDOWS), POOL_GROUP, POOL_GROUP), lambda g, c: (0, 0, 0)),
                  pl.BlockSpec((1, C), lambda g, c: (0, 0))],
        out_specs=pl.BlockSpec((R, C), lambda g, c: (g * nch + c, 0)),
        out_shape=jax.ShapeDtypeStruct((t, C), BF16),
        scratch_shapes=[pltpu.VMEM((_round_up(PS, SUBLANES) + R, C), F32)],
        compiler_params=_cparams("parallel", "arbitrary"),
    )(u_main, prev, w_pool, scale)


def _rglru_kernel(cx_ref, cg_ref, prev_ref, h0_ref, wc_ref, bc_ref, wa_ref, ba_ref, wx_ref, bx_ref, lam_ref,
                  o_ref, hl_ref, ext_s, a_s, b_s, h_s, *, S, Tc, nch):
    ch = pl.program_id(1)
    R = Tc * S
    PS = (CONV_W - 1) * S
    OFF = _round_up(PS, SUBLANES)

    @pl.when(ch == 0)
    def _():
        ext_s[OFF - PS:OFF, :] = prev_ref[0]
        h_s[...] = h0_ref[0]

    x = cx_ref[...]
    ext_s[OFF:OFF + R, :] = x
    xc = bc_ref[...] + x * wc_ref[CONV_W - 1:CONV_W, :]
    for j in range(CONV_W - 1):
        k = CONV_W - 1 - j
        xc = xc + ext_s[OFF - k * S:OFF - k * S + R, :] * wc_ref[j:j + 1, :]
    xb = xc.astype(BF16)
    for n in range(LRU_BLOCKS):
        sl = slice(n * LRU_BLOCK, (n + 1) * LRU_BLOCK)
        r = jax.nn.sigmoid(_dot(xb[:, sl], wa_ref[n]) + ba_ref[:, sl])
        i = jax.nn.sigmoid(_dot(xb[:, sl], wx_ref[n]) + bx_ref[:, sl])
        lam = lam_ref[:, sl]
        softplus_neg = jnp.maximum(-lam, 0.0) + jnp.log1p(jnp.exp(-jnp.abs(lam)))
        log_a = (-LRU_C) * r * softplus_neg
        a_s[:, sl] = jnp.exp(log_a)
        th = jnp.tanh(log_a)
        b_s[:, sl] = jnp.sqrt(-2.0 * th / (1.0 - th)) * (i * xc[:, sl])

    if S % SUBLANES == 0:
        def step(t, h):
            off = pl.multiple_of(t * S, S)
            h = a_s[pl.ds(off, S), :] * h + b_s[pl.ds(off, S), :]
            b_s[pl.ds(off, S), :] = h
            return h

        h = lax.fori_loop(0, Tc, step, h_s[...])
    else:
        assert S == 1 and Tc % SUBLANES == 0
        row =lax.broadcasted_iota(I32, (SUBLANES, a_s.shape[1]), 0)

        def tile_step(i, h):
            off = pl.multiple_of(i * SUBLANES, SUBLANES)
            A = a_s[pl.ds(off, SUBLANES), :]
            B = b_s[pl.ds(off, SUBLANES), :]
            for sh in (1, 2, 4):
                keep = row >= sh
                B = A * jnp.where(keep, pltpu.roll(B, sh, axis=0), 0.0) + B
                A = A * jnp.where(keep, pltpu.roll(A, sh, axis=0), 1.0)
            hs = A * h + B
            b_s[pl.ds(off, SUBLANES), :] = hs
            return hs[SUBLANES - 1:SUBLANES, :]

        h = lax.fori_loop(0, Tc // SUBLANES, tile_step, h_s[...])
    h_s[...] = h
    hl_ref[0] = h
    o_ref[...] = (_gelu_tanh(cg_ref[...]) * b_s[...]).astype(o_ref.dtype)
    if nch > 1:
        ext_s[OFF - PS:OFF, :] = ext_s[OFF + R - PS:OFF + R, :]


def _rglru(u_main, prev, h0, lp, *, S, Tc):
    t = u_main.shape[0]
    R = Tc * S
    G = prev.shape[0]
    nch = t // (G * R)
    PS = (CONV_W - 1) * S
    C = LRU_W
    vec = pl.BlockSpec((1, C), lambda g, c: (0, 0))
    blk = pl.BlockSpec((LRU_BLOCKS, LRU_BLOCK, LRU_BLOCK), lambda g, c: (0, 0, 0))
    return pl.pallas_call(
        functools.partial(_rglru_kernel, S=S, Tc=Tc, nch=nch),
        grid=(G, nch),
        in_specs=[pl.BlockSpec((R, C), lambda g, c: (g * nch + c, 3)),
                  pl.BlockSpec((R, C), lambda g, c: (g * nch + c, 4)),
                  pl.BlockSpec((1, PS, C), lambda g, c: (g, 0, 0)),
                  pl.BlockSpec((1, S, C), lambda g, c: (g, 0, 0)),
                  pl.BlockSpec((CONV_W, C), lambda g, c: (0, 0)), vec, blk, vec, blk, vec, vec],
        out_specs=[pl.BlockSpec((R, C), lambda g, c: (g * nch + c, 0)),
                   pl.BlockSpec((1, S, C), lambda g, c: (g, 0, 0))],
        out_shape=[jax.ShapeDtypeStruct((t, C), BF16), jax.ShapeDtypeStruct((G, S, C), F32)],
        scratch_shapes=[pltpu.VMEM((_round_up(PS, SUBLANES) + R, C), F32), pltpu.VMEM((R, C), F32),
                        pltpu.VMEM((R, C), F32), pltpu.VMEM((S, C), F32)],
        compiler_params=_cparams("parallel", "arbitrary"),
    )(u_main, u_main, prev, h0, lp["w_conv_c"], lp["b_conv_c"], lp["w_rg_a"], lp["b_rg_a"], lp["w_rg_x"],
      lp["b_rg_x"], lp["lru_lambda"])


def _ret_kernel(rq_ref, rk_ref, rv_ref, dg_ref, s0_ref, gn_ref, o_ref, sn_ref, s_s, *, c, spb):
    @pl.when(pl.program_id(1) == 0)
    def _():
        s_s[...] = s0_ref[...]

    ii = lax.broadcasted_iota(I32, (c, c), 0)
    jj = lax.broadcasted_iota(I32, (c, c), 1)
    dif = (ii - jj).astype(F32)
    tpos = lax.broadcasted_iota(I32, (c, 1), 0).astype(F32)
    mm = BF16 if c % 16 == 0 else F32
    for h in range(N_HEADS_D):
        lg = LOG_G[h]
        qs = slice(h * QK_DIM_D, (h + 1) * QK_DIM_D)
        vs = slice(h * V_DIM_D, (h + 1) * V_DIM_D)
        intra = jnp.where(dif >= 0.0, jnp.exp(lg * jnp.maximum(dif, 0.0)), 0.0)
        q_dec = jnp.exp(lg * (tpos + 1.0))
        k_dec = jnp.exp(lg * (c - 1.0 - tpos))
        for sq in range(spb):
            rows = slice(sq * c, (sq + 1) * c)
            q = rq_ref[rows, qs].astype(mm)
            k = rk_ref[rows, qs]
            v = rv_ref[rows, vs].astype(mm)
            att = _dot_nt(q, k.astype(mm)) * intra
            s = s_s[sq, h]
            o = _dot(att.astype(mm), v) + _dot(q, s.astype(mm)) * q_dec
            s_s[sq, h] = s * math.exp(lg * c) + _dot_tn((k * k_dec).astype(mm), v)
            mu = jnp.mean(o, axis=-1, keepdims=True)
            var = jnp.mean(jnp.square(o - mu), axis=-1, keepdims=True)
            y = (o - mu) * lax.rsqrt(var + EPS) * gn_ref[:, vs]
            dg = dg_ref[rows, vs]
            o_ref[rows, vs] = (dg * jax.nn.sigmoid(dg) * y).astype(o_ref.dtype)
    sn_ref[...] = s_s[...]


def _retention(rq, rk, rv, dg, s0, gn, *, c, v_col, g_col, spb):
    t = rq.shape[0]
    B = s0.shape[0]
    nch = t // (B * c)
    assert spb == 1 or nch == 1
    W = N_HEADS_D * QK_DIM_D
    R = spb * c
    st = pl.BlockSpec((spb, N_HEADS_D, QK_DIM_D, V_DIM_D), lambda b, i: (b, 0, 0, 0))
    return pl.pallas_call(
        functools.partial(_ret_kernel, c=c, spb=spb),
        grid=(B // spb, nch),
        in_specs=[pl.BlockSpec((R, W), lambda b, i: (b * nch + i, 0)),
                  pl.BlockSpec((R, W), lambda b, i: (b * nch + i, 0)),
                  pl.BlockSpec((R, BRANCH_W), lambda b, i: (b * nch + i, v_col)),
                  pl.BlockSpec((R, BRANCH_W), lambda b, i: (b * nch + i, g_col)),
                  st, pl.BlockSpec((1, BRANCH_W), lambda b, i: (0, 0))],
        out_specs=[pl.BlockSpec((R, BRANCH_W), lambda b, i: (b * nch + i, 0)), st],
        out_shape=[jax.ShapeDtypeStruct((t, BRANCH_W), BF16), jax.ShapeDtypeStruct(s0.shape, F32)],
        scratch_shapes=[pltpu.VMEM((spb, N_HEADS_D, QK_DIM_D, V_DIM_D), F32)],
        compiler_params=_cparams("parallel", "arbitrary"),
    )(rq, rk, rv, dg, s0, gn)


def _sortable(x):
    b = lax.bitcast_convert_type(x, I32)
    return b ^ ((b >> 31) & 0x7FFFFFFF)


def _count(m):
    return jnp.sum(jnp.where(m, 1.0, 0.0), axis=1, keepdims=True)


def _kth_largest_key(keys, kf, bits):
    m = keys.shape[0]
    t0 = jnp.where(_count(keys >= 0) >= kf, 0, INT_MIN).astype(I32)

    def step(nb, shift, t):
        digit = jnp.zeros((m, 1), I32)
        for v in range(1, 2 ** nb):
            cand = t | jnp.left_shift(jnp.int32(v), shift)
            digit = digit + jnp.where(_count(keys >= cand) >= kf, 1, 0)
        return t | jnp.left_shift(digit, shift)

    nfull, rem = divmod(31, bits)
    t = lax.fori_loop(0, nfull, lambda i, t: step(bits, 31 - bits * (i + 1), t), t0)
    return step(rem, 0, t) if rem else t


def _topk_mask(keys, valid, idx, k, bits):
    m, n = keys.shape
    kf = float(k)
    thr = _kth_largest_key(keys, kf, bits)
    gt = keys > thr
    eq = (keys == thr) & valid
    need = kf - _count(gt)
    excess = jnp.max(_count(eq) - need) > 0.0
    nbits = max(1, (n - 1).bit_length())

    def first_ties():
        def idx_step(i, j):
            cand = j | jnp.left_shift(jnp.int32(1), nbits - 1 - i)
            return jnp.where(_count(eq & (idx < cand)) <= need - 1.0, cand, j)

        return lax.fori_loop(0, nbits, idx_step, jnp.zeros((m, 1), I32))

    jthr = lax.cond(excess, first_ties, lambda: jnp.full((m, 1), 2 ** nbits, I32))
    return (gt & valid) | (eq & (idx <= jthr))


def _masked_attention(s, mask, v):
    s = jnp.where(mask, s, -jnp.inf)
    p = jnp.exp(s - jnp.max(s, axis=1, keepdims=True))
    return _dot(p.astype(BF16), v) / jnp.sum(p, axis=1, keepdims=True)


def _dsa_prompt_kernel(q_ref, kb_ref, vb_ref, iq4_ref, ik4_ref, iw_ref, o_ref, *, j0, topk):
    nq = q_ref.shape[0]
    nk = kb_ref.shape[0]
    t0 = (j0 + pl.program_id(1)) * nq
    ik4 = ik4_ref[...]
    score = jnp.zeros((nq, nk), F32)
    for h in range(N_IDX_HEADS):
        s = _dot_nt(iq4_ref[:, h * IDX4:(h + 1) * IDX4], ik4)
        score = score + jnp.maximum(s, 0.0) * iw_ref[:, IDX_DIM + h:IDX_DIM + h + 1]
    score = score * IDX_SCALE
    kpos = lax.broadcasted_iota(I32, (nq, nk), 1)
    valid = kpos <= t0 + lax.broadcasted_iota(I32, (nq, nk), 0)
    if nk > topk:
        mask = _topk_mask(_sortable(jnp.where(valid, score, -jnp.inf)), valid, kpos, topk, 1)
    else:
        mask = valid
    for g in range(N_KV_B):
        gs = slice(g * HEAD_DIM_B, (g + 1) * HEAD_DIM_B)
        kg = kb_ref[:, gs]
        vg = vb_ref[:, gs]
        for r in range(KV_REP):
            hs = slice((g * KV_REP + r) * HEAD_DIM_B, (g * KV_REP + r + 1) * HEAD_DIM_B)
            o_ref[:, hs] = _masked_attention(_dot_nt(q_ref[:, hs], kg), mask, vg).astype(o_ref.dtype)


def _dsa_prompt(q, kb, vb, iq4, ik4, ikw, *, n_seq, n_cls, qb):
    t = q.shape[0]
    s_len = t // n_seq
    nb = s_len // qb
    nbc = nb // n_cls
    topk = min(TOPK_MAX, s_len // 4)
    q, kb, vb, iq4, ik4, ikw = (a.reshape(n_seq, s_len, a.shape[-1]) for a in (q, kb, vb, iq4, ik4, ikw))
    outs = []
    for c in range(n_cls):
        nk = (c + 1) * nbc * qb
        qrow = lambda w, c=c: pl.BlockSpec((None, qb, w), lambda b, j: (b, c * nbc + j, 0))
        seq = lambda w, nk=nk: pl.BlockSpec((None, nk, w), lambda b, j: (b, 0, 0))
        outs.append(pl.pallas_call(
            functools.partial(_dsa_prompt_kernel, j0=c * nbc, topk=topk),
            grid=(n_seq, nbc),
            in_specs=[qrow(BRANCH_W), seq(N_KV_B * HEAD_DIM_B), seq(N_KV_B * HEAD_DIM_B), qrow(N_IDX_HEADS * IDX4),
                      seq(IDX4), qrow(LANES)],
            out_specs=pl.BlockSpec((None, qb, BRANCH_W), lambda b, j: (b, j, 0)),
            out_shape=jax.ShapeDtypeStruct((n_seq, nbc * qb, BRANCH_W), BF16),
            compiler_params=_cparams("parallel", "arbitrary"),
        )(q, kb, vb, iq4, ik4, ikw))
    return jnp.concatenate(outs, axis=1).reshape(t, BRANCH_W)


def _dsa_sample_kernel(pt_ref, qg_ref, iq4_ref, w_ref, kn_ref, vn_ref, ikn_ref, *rest, n_pages, t_new, spb):
    del pt_ref
    n_in = 3 * n_pages * spb
    page_refs, (o_ref, sc_s, k_s, v_s) = rest[:n_in], rest[n_in:]
    P = PAGE_SIZE
    pad = P - t_new
    nk = (n_pages + 1) * P
    past = n_pages * P

    for s in range(spb):
        idx_refs = page_refs[3 * n_pages * s:3 * n_pages * s + n_pages]
        k_refs = page_refs[3 * n_pages * s + n_pages:3 * n_pages * s + 2 * n_pages]
        v_refs = page_refs[3 * n_pages * s + 2 * n_pages:3 * n_pages * (s + 1)]
        iq4 = iq4_ref[s]
        w = w_ref[s]

        def page_scores(ik, page, keys_on_lanes):
            hi, lo = _split_bf16(ik)
            if keys_on_lanes:
                sc = _dot(iq4, jnp.concatenate([hi, lo, hi, lo], axis=0))
            else:
                sc = _dot_nt(iq4, jnp.concatenate([hi, lo, hi, lo], axis=1))
            r = jnp.maximum(sc, 0.0) * w
            acc = r[0:t_new]
            for h in range(1, N_IDX_HEADS):
                acc = acc + r[h * t_new:(h + 1) * t_new]
            sc_s[s * t_new:(s + 1) * t_new, page * P:(page + 1) * P] = acc * IDX_SCALE

        for p in range(n_pages):
            page_scores(idx_refs[p][0, 0], p, True)
            for g in range(N_KV_B):
                gs = slice(g * HEAD_DIM_B, (g + 1) * HEAD_DIM_B)
                k_s[s, p * P:(p + 1) * P, gs] = k_refs[p][0, 0, pl.ds(g, P, stride=N_KV_B), :].astype(BF16)
                v_s[s, p * P:(p + 1) * P, gs] = v_refs[p][0, 0, pl.ds(g, P, stride=N_KV_B), :].astype(BF16)
        page_scores(jnp.concatenate([ikn_ref[s], jnp.zeros((pad, IDX_DIM), F32)], axis=0), n_pages, False)
        zkv = jnp.zeros((pad, N_KV_B * HEAD_DIM_B), F32)
        k_s[s, past:nk, :] = jnp.concatenate([kn_ref[s], zkv], axis=0).astype(BF16)
        v_s[s, past:nk, :] = jnp.concatenate([vn_ref[s], zkv], axis=0).astype(BF16)

    rows = spb * t_new
    kpos = lax.broadcasted_iota(I32, (rows, nk), 1)
    valid = kpos <= past + (lax.broadcasted_iota(I32, (rows, nk), 0) & (t_new - 1))
    keys = _sortable(jnp.where(valid, sc_s[...], -jnp.inf))
    mask = _topk_mask(keys, valid, kpos, min(TOPK_MAX, (past + t_new) // 4), 3)
    mask_i = jnp.where(mask, 1, 0).astype(I32)
    for s in range(spb):
        m_s = jnp.concatenate([mask_i[s * t_new:(s + 1) * t_new]] * KV_REP, axis=0) > 0
        for g in range(N_KV_B):
            gs = slice(g * HEAD_DIM_B, (g + 1) * HEAD_DIM_B)
            o_ref[s, g] = _masked_attention(_dot_nt(qg_ref[s, g], k_s[s, :, gs]), m_s, v_s[s, :, gs]).astype(o_ref.dtype)


def _dsa_sample(qg, iq4r, wr, k_new, v_new, ik_new, cache_k, cache_v, cache_idx_k, page_table, layer, spb):
    B, _, rows, hd = qg.shape
    t_new = k_new.shape[1]
    assert t_new & (t_new - 1) == 0 and B % spb == 0
    n_pages = page_table.shape[1]
    kvw = N_KV_B * HEAD_DIM_B
    per_seq = lambda shape: pl.BlockSpec((spb,) + shape, lambda b, pt: (b,) + (0,) * len(shape))

    def page_spec(rows_, width, s, p):
        return pl.BlockSpec((1, 1, rows_, width), lambda b, pt: (layer, pt[b * spb + s, p], 0, 0))

    in_specs = [per_seq(qg.shape[1:]), per_seq(iq4r.shape[1:]), per_seq(wr.shape[1:]), per_seq((t_new, kvw)),
                per_seq((t_new, kvw)), per_seq((t_new, IDX_DIM))]
    pages = []
    for s in range(spb):
        in_specs += [page_spec(IDX_DIM, PAGE_SIZE, s, p) for p in range(n_pages)]
        in_specs += [page_spec(PAGE_SIZE * N_KV_B, HEAD_DIM_B, s, p) for p in range(n_pages)]
        in_specs += [page_spec(PAGE_SIZE * N_KV_B, HEAD_DIM_B, s, p) for p in range(n_pages)]
        pages += [cache_idx_k] * n_pages + [cache_k] * n_pages + [cache_v] * n_pages
    nk = (n_pages + 1) * PAGE_SIZE
    grid_spec = pltpu.PrefetchScalarGridSpec(
        num_scalar_prefetch=1, grid=(B // spb,), in_specs=in_specs,
        out_specs=pl.BlockSpec((spb,) + qg.shape[1:], lambda b, pt: (b, 0, 0, 0)),
        scratch_shapes=[pltpu.VMEM((spb * t_new, nk), F32), pltpu.VMEM((spb, nk, kvw), BF16),
                        pltpu.VMEM((spb, nk, kvw), BF16)])
    return pl.pallas_call(
        functools.partial(_dsa_sample_kernel, n_pages=n_pages, t_new=t_new, spb=spb),
        grid_spec=grid_spec,
        out_shape=jax.ShapeDtypeStruct(qg.shape, BF16),
        compiler_params=_cparams("arbitrary"),
    )(page_table, qg, iq4r, wr, k_new, v_new, ik_new, *pages)


def _merge_kernel(xn_ref, a_ref, b_ref, c_ref, d_ref, wg0, wg1, wg2, wg3, bg0, bg1, bg2, bg3, wb_ref, o_ref):
    xn = xn_ref[...]
    acc = None
    for n, (br, wg, bg) in enumerate(zip((a_ref, b_ref, c_ref, d_ref), (wg0, wg1, wg2, wg3), (bg0, bg1, bg2, bg3))):
        term = jax.nn.sigmoid(_dot(xn, wg[...]) + bg[...]) * _dot(br[...], wb_ref[n])
        acc = term if acc is None else acc + term
    o_ref[...] = acc.astype(o_ref.dtype)


def _merge(xn, branches, w_gate, b_gate, w_branch, l, tm, tn):
    t, d = xn.shape
    nj = d // tn
    row = lambda w: pl.BlockSpec((tm, w), lambda i, j: (i, 0))
    wg = [pl.BlockSpec((None, d, tn), functools.partial(lambda i, j, n: (l, 0, n * nj + j), n=n))
          for n in range(N_BRANCH)]
    bg = [pl.BlockSpec((1, tn), functools.partial(lambda i, j, n: (0, n * nj + j), n=n)) for n in range(N_BRANCH)]
    return pl.pallas_call(
        _merge_kernel,
        grid=(t // tm, nj),
        in_specs=[row(d)] + [row(BRANCH_W)] * N_BRANCH + wg + bg
                 + [pl.BlockSpec((None, N_BRANCH, BRANCH_W, tn), lambda i, j: (l, 0, 0, j))],
        out_specs=pl.BlockSpec((tm, tn), lambda i, j: (i, j)),
        out_shape=jax.ShapeDtypeStruct((t, d), BF16),
        compiler_params=_cparams("parallel", "arbitrary"),
    )(xn, *branches, *([w_gate] * N_BRANCH), *([b_gate] * N_BRANCH), w_branch)


def _outproj_kernel(m_ref, x_ref, w_ref, g_ref, o_ref):
    o_ref[...] = x_ref[...] + _rms(_dot(m_ref[...], w_ref[...]), g_ref[...])


def _outproj(merged, x, w_o, g, l, tm):
    t, d = x.shape
    return pl.pallas_call(
        _outproj_kernel,
        grid=(t // tm,),
        in_specs=[pl.BlockSpec((tm, d), lambda i: (i, 0)), pl.BlockSpec((tm, d), lambda i: (i, 0)),
                  pl.BlockSpec((None, d, d), lambda i: (l, 0, 0)), pl.BlockSpec((1, d), lambda i: (0, 0))],
        out_specs=pl.BlockSpec((tm, d), lambda i: (i, 0)),
        out_shape=jax.ShapeDtypeStruct((t, d), F32),
        compiler_params=_cparams("parallel"),
    )(merged, x, w_o, g)


def _ffn_kernel(x_ref, gpre_ref, wug_ref, wuv_ref, wcg_ref, wcv_ref, bcg_ref, bcv_ref, wd_ref, gpost_ref,
                stg_ref, stv_ref, o_ref, ng_ref, nv_ref, h_s, acc_s, cg_s, cv_s, ext_s, *, S, tiles_per_group):
    i = pl.program_id(0)
    j = pl.program_id(1)
    tm = x_ref.shape[0]
    CR = cg_s.shape[1]

    @pl.when(j == 0)
    def _():
        h_s[...] = _rms(x_ref[...], gpre_ref[...]).astype(BF16)
        acc_s[...] = jnp.zeros_like(acc_s)

    @pl.when(i % tiles_per_group == 0)
    def _():
        cg_s[j] = stg_ref[0]
        cv_s[j] = stv_ref[0]

    h = h_s[...]

    def conv(w_ref, wc_ref, bc_ref, c_s, new_ref):
        u = _dot(h, w_ref[...])
        ext_s[0:CR, :] = c_s[j]
        ext_s[CR:CR + tm, :] = u
        y = bc_ref[...] + u * wc_ref[2:3, :]
        y = y + ext_s[CR - S:CR - S + tm, :] * wc_ref[1:2, :]
        y = y + ext_s[CR - 2 * S:CR - 2 * S + tm, :] * wc_ref[0:1, :]
        tail = ext_s[tm:tm + CR, :]
        c_s[j] = tail
        new_ref[0] = tail
        return y

    gate = conv(wug_ref, wcg_ref, bcg_ref, cg_s, ng_ref)
    val = conv(wuv_ref, wcv_ref, bcv_ref, cv_s, nv_ref)
    acc_s[...] += _dot((_gelu_tanh(gate) * val).astype(BF16), wd_ref[...])

    @pl.when(j == pl.num_programs(1) - 1)
    def _():
        o_ref[...] = x_ref[...] + _rms(acc_s[...], gpost_ref[...])


def _ffn(x, lp, state, *, S, tm, tn):
    t, d = x.shape
    G, CR, _ = state.shape
    nj = D_FF // tn
    ni = t // tm
    tiles_per_group = ni // G
    grp = lambda i, j: (i // tiles_per_group, 0, j)
    tile = lambda i, j: (i, 0, j)
    grp_v = lambda i, j: (i // tiles_per_group, 0, nj + j)
    col = lambda r: pl.BlockSpec((r, tn), lambda i, j: (0, j))
    col_v = lambda r: pl.BlockSpec((r, tn), lambda i, j: (0, nj + j))
    vec = pl.BlockSpec((1, d), lambda i, j: (0, 0))
    l = lp["layer"]
    xo, ng, nv = pl.pallas_call(
        functools.partial(_ffn_kernel, S=S, tiles_per_group=tiles_per_group),
        grid=(ni, nj),
        in_specs=[pl.BlockSpec((tm, d), lambda i, j: (i, 0)), vec,
                  pl.BlockSpec((None, d, tn), lambda i, j: (l, 0, j)),
                  pl.BlockSpec((None, d, tn), lambda i, j: (l, 0, nj + j)),
                  col(FFN_CONV_W), col_v(FFN_CONV_W), col(1), col_v(1),
                  pl.BlockSpec((None, tn, d), lambda i, j: (l, j, 0)), vec,
                  pl.BlockSpec((1, CR, tn), grp), pl.BlockSpec((1, CR, tn), grp_v)],
        out_specs=[pl.BlockSpec((tm, d), lambda i, j: (i, 0)),
                   pl.BlockSpec((1, CR, tn), tile), pl.BlockSpec((1, CR, tn), tile)],
        out_shape=[jax.ShapeDtypeStruct((t, d), F32), jax.ShapeDtypeStruct((ni, CR, D_FF), F32),
                   jax.ShapeDtypeStruct((ni, CR, D_FF), F32)],
        scratch_shapes=[pltpu.VMEM((tm, d), BF16), pltpu.VMEM((tm, d), F32), pltpu.VMEM((nj, CR, tn), F32),
                        pltpu.VMEM((nj, CR, tn), F32), pltpu.VMEM((CR + tm, tn), F32)],
        compiler_params=_cparams("arbitrary", "arbitrary"),
    )(x, lp["g_ffn_pre"], lp["w_up"], lp["w_up"], lp["w_ffn_conv"], lp["w_ffn_conv"], lp["b_ffn_conv"],
      lp["b_ffn_conv"], lp["w_down"], lp["g_ffn_post"], state, state)
    last = slice(tiles_per_group - 1, None, tiles_per_group)
    return xo, ng[last], nv[last]


def _layer_front(x, lp, tabs, *, tm, kv_final=None):
    l = lp["layer"]
    u_main, xn = _inproj(x, lp["g_mix_pre"], lp["w_in_main"], l, tm, 512)
    u_idx = _inproj_split(x, lp["g_mix_pre"], lp["w_in_idx3"], l, 512, N_IDX)
    return u_main, xn, _rope_prep(u_main, u_idx, tabs, 256, kv_final)


def _layer_back(x, xn, branches, lp, ffn_state, *, S, tm, tn_ffn):
    l = lp["layer"]
    merged = _merge(xn, branches, lp["w_gate"], lp["b_gate"], lp["w_branch"], l, tm, 512)
    x1 = _outproj(merged, x, lp["w_o"], lp["g_mix_post"], l, tm)
    return _ffn(x1, lp, ffn_state, S=S, tm=tm, tn=tn_ffn)


def _prep_big_weights(p):
    w_in = jnp.swapaxes(p["w_in"], 1, 2)
    o_iq = 3072
    o_cx = o_iq + N_IDX_HEADS * IDX_DIM + IDX_DIM + N_IDX_HEADS
    w_idx = jnp.pad(w_in[:, o_iq:o_cx], ((0, 0), (0, N_IDX - (o_cx - o_iq)), (0, 0)))
    hi = w_idx.astype(BF16)
    lo = (w_idx - hi.astype(F32)).astype(BF16)
    return dict(
        w_in_main=jnp.concatenate([w_in[:, :o_iq], w_in[:, o_cx:]], axis=1).astype(BF16),
        w_in_idx3=jnp.concatenate([hi, lo, hi], axis=2),
        w_branch=p["w_branch"].astype(BF16), w_gate=p["w_gate"].astype(BF16), w_o=p["w_o"].astype(BF16),
        w_up=p["w_up"].astype(BF16), w_down=p["w_down"].astype(BF16))


def _prep_layer_params(p, big, l):
    row = lambda a: a[l][None, :]
    return dict(
        big, layer=l,
        g_mix_pre=row(p["g_mix_pre"]),
        w_pool=p["w_pool"][l].astype(BF16), pool_scale=row(p["pool_scale"]),
        w_conv_c=p["w_conv_c"][l], b_conv_c=row(p["b_conv_c"]),
        w_rg_a=p["w_rg_a"][l].astype(BF16), b_rg_a=row(p["b_rg_a"]),
        w_rg_x=p["w_rg_x"][l].astype(BF16), b_rg_x=row(p["b_rg_x"]),
        lru_lambda=row(p["lru_lambda"]), ret_gn=row(p["ret_gn"]), b_gate=row(p["b_gate"]),
        g_mix_post=row(p["g_mix_post"]), g_ffn_pre=row(p["g_ffn_pre"]),
        w_ffn_conv=p["w_ffn_conv"][l], b_ffn_conv=row(p["b_ffn_conv"]), g_ffn_post=row(p["g_ffn_post"]))


def _to_time_major(a, nb, nt):
    return jnp.swapaxes(a, 0, 1).reshape((nt * nb,) + a.shape[2:])


def _to_seq_major(a, nb, nt):
    return jnp.swapaxes(a.reshape((nt, nb) + a.shape[1:]), 0, 1)


def kernel(x_prompt, x_sample, cache_k, cache_v, cache_idx_k, state_pool, state_conv, state_rglru, state_ret, state_ffn_conv, page_table, g_mix_pre, w_in, w_pool, pool_scale, w_conv_c, b_conv_c, w_rg_a, b_rg_a, w_rg_x, b_rg_x, lru_lambda, ret_gn, w_branch, w_gate, b_gate, w_o, g_mix_post, g_ffn_pre, w_up, w_ffn_conv, b_ffn_conv, w_down, g_ffn_post):
    params = dict(g_mix_pre=g_mix_pre, w_in=w_in, w_pool=w_pool, pool_scale=pool_scale, w_conv_c=w_conv_c,
                  b_conv_c=b_conv_c, w_rg_a=w_rg_a, b_rg_a=b_rg_a, w_rg_x=w_rg_x, b_rg_x=b_rg_x,
                  lru_lambda=lru_lambda, ret_gn=ret_gn, w_branch=w_branch, w_gate=w_gate, b_gate=b_gate, w_o=w_o,
                  g_mix_post=g_mix_post, g_ffn_pre=g_ffn_pre, w_up=w_up, w_ffn_conv=w_ffn_conv,
                  b_ffn_conv=b_ffn_conv, w_down=w_down, g_ffn_post=g_ffn_post)
    depth = w_in.shape[0]
    bp, seq, d = x_prompt.shape
    bs, tdec, _ = x_sample.shape
    past = page_table.shape[1] * PAGE_SIZE
    n_pool = cache_k.shape[1]
    ck = cache_k.reshape(depth, n_pool, PAGE_SIZE * N_KV_B, HEAD_DIM_B)
    cv = cache_v.reshape(depth, n_pool, PAGE_SIZE * N_KV_B, HEAD_DIM_B)
    cik_t = jnp.swapaxes(cache_idx_k, 2, 3)
    big = _prep_big_weights(params)

    tabs_p = _rope_tables(jnp.arange(seq, dtype=F32))
    tabs_s = _rope_tables(jnp.repeat(past + jnp.arange(tdec, dtype=F32), bs))
    xp = x_prompt.reshape(bp * seq, d)
    xs = _to_time_major(x_sample, bs, tdec)
    cr_p = max(SUBLANES, (FFN_CONV_W - 1))
    outs_p, outs_s = [], []
    kv_p = None
    for l in range(depth):
        lp = _prep_layer_params(params, big, l)
        u, xn, (q, _, kb, vb, iq4, ik4, ikw, rq, rk, *kv_p) = _layer_front(xp, lp, tabs_p, tm=1024,
                                                                         kv_final=(l, depth, kv_p))
        o_a = _pool(u, jnp.zeros((bp, POOL_BUF, BRANCH_W), F32), lp["w_pool"], lp["pool_scale"], S=1, Tc=256, start=0)
        o_b = _dsa_prompt(q, kb, vb, iq4, ik4, ikw, n_seq=bp, n_cls=8, qb=256)
        o_c, h_p = _rglru(u, jnp.zeros((bp, CONV_W - 1, LRU_W), F32), jnp.zeros((bp, 1, LRU_W), F32), lp, S=1, Tc=256)
        o_d, s_p = _retention(rq, rk, u, u, jnp.zeros((bp, N_HEADS_D, QK_DIM_D, V_DIM_D), F32), lp["ret_gn"],
                              c=RET_CHUNK if seq % RET_CHUNK == 0 else seq, v_col=6, g_col=7, spb=1)
        xp, ng, nv = _layer_back(xp, xn, (o_a, o_b, o_c, o_d), lp, jnp.zeros((bp, cr_p, 2 * D_FF), F32),
                                 S=1, tm=512, tn_ffn=512)
        u3 = u.reshape(bp, seq, N_MAIN)
        outs_p.append((
            None, None,
            ikw.reshape(bp, seq, LANES)[:, :, :IDX_DIM], u3[:, seq - POOL_BUF:, 0:BRANCH_W],
            u3[:, seq - (CONV_W - 1):, 3072:3072 + LRU_W], h_p[:, 0], s_p,
            jnp.concatenate([ng, nv], axis=-1)[:, cr_p - (FFN_CONV_W - 1):]))
        u, xn, (q, k, kb, vb, iq4, ik4, ikw, rq, rk) = _layer_front(xs, lp, tabs_s, tm=1024)
        o_a = _pool(u, _to_time_major(state_pool[l], bs, POOL_BUF)[None], lp["w_pool"], lp["pool_scale"],
                    S=bs, Tc=tdec, start=past)
        o_c, h_s = _rglru(u, _to_time_major(state_conv[l], bs, CONV_W - 1)[None], state_rglru[l][None], lp,
                          S=bs, Tc=tdec)
        sm = lambda a: _to_seq_major(a, bs, tdec)
        u_sm = sm(u)
        o_d, s_s = _retention(sm(rq).reshape(bs * tdec, -1), sm(rk).reshape(bs * tdec, -1),
                              u_sm[:, :, 6144:7168].reshape(bs * tdec, -1), u_sm[:, :, 7168:8192].reshape(bs * tdec, -1),
                              state_ret[l], lp["ret_gn"], c=tdec, v_col=0, g_col=0, spb=8)
        o_d = _to_time_major(o_d.reshape(bs, tdec, BRANCH_W), bs, tdec)
        qg = sm(q).reshape(bs, tdec, N_KV_B, KV_REP, HEAD_DIM_B).transpose(0, 2, 3, 1, 4)
        qg = qg.reshape(bs, N_KV_B, KV_REP * tdec, HEAD_DIM_B)
        iq4r = sm(iq4).reshape(bs, tdec, N_IDX_HEADS, IDX4).transpose(0, 2, 1, 3).reshape(bs, N_IDX_HEADS * tdec, IDX4)
        ikw_sm = sm(ikw)
        wr = ikw_sm[:, :, IDX_DIM:IDX_DIM + N_IDX_HEADS].transpose(0, 2, 1).reshape(bs, N_IDX_HEADS * tdec, 1)
        k_sm = sm(k)
        v_sm = u_sm[:, :, 2560:3072]
        o_b = _dsa_sample(qg, iq4r, wr, k_sm, v_sm, ikw_sm[:, :, :IDX_DIM], ck, cv, cik_t, page_table, l, 2)
        o_b = o_b.reshape(bs, N_KV_B, KV_REP, tdec, HEAD_DIM_B).transpose(3, 0, 1, 2, 4).reshape(tdec * bs, BRANCH_W)
        ffn_state = _to_time_major(state_ffn_conv[l], bs, FFN_CONV_W - 1)[None]
        xs, ng, nv = _layer_back(xs, xn, (o_a, o_b, o_c, o_d), lp, ffn_state, S=bs, tm=512, tn_ffn=256)
        ffn_new = _to_seq_major(jnp.concatenate([ng, nv], axis=-1)[0], bs, FFN_CONV_W - 1)
        a_in = u_sm[:, :, 0:BRANCH_W]
        c_x = u_sm[:, :, 3072:3072 + LRU_W]
        outs_s.append((
            k_sm.reshape(bs, tdec, N_KV_B, HEAD_DIM_B), v_sm.reshape(bs, tdec, N_KV_B, HEAD_DIM_B),
            ikw_sm[:, :, :IDX_DIM],
            jnp.concatenate([state_pool[l], a_in], axis=1)[:, tdec:],
            jnp.concatenate([state_conv[l], c_x], axis=1)[:, tdec:],
            h_s[0], s_s, ffn_new))

    stk = lambda outs, i: jnp.stack([o[i] for o in outs], axis=0)
    res = [xp.reshape(bp, seq, d), _to_seq_major(xs, bs, tdec)]
    for i in range(8):
        p_i = kv_p[i].reshape(depth, bp, seq, N_KV_B, HEAD_DIM_B) if i < 2 else stk(outs_p, i)
        res += [p_i, stk(outs_s, i)]
    return tuple(res)
```

```python
import functools
import math

import jax
import jax.numpy as jnp
from jax import lax
from jax.experimental import pallas as pl
from jax.experimental.pallas import tpu as pltpu

F32 = jnp.float32
BF16 = jnp.bfloat16
I32 = jnp.int32

SUBLANES = 8
LANES = 128
VMEM_LIMIT_BYTES = 56 * 1024 * 1024

D_MODEL = 2048
BRANCH_W = D_MODEL // 2
N_BRANCH = 4
POOL_WINDOWS = (2, 4, 8, 16)
POOL_GROUP = BRANCH_W // len(POOL_WINDOWS)
POOL_BUF = max(POOL_WINDOWS) - 1
N_HEADS_B = 8
HEAD_DIM_B = BRANCH_W // N_HEADS_B
N_KV_B = 4
KV_REP = N_HEADS_B // N_KV_B
N_IDX_HEADS = 16
IDX_DIM = 64
TOPK_MAX = 256
ATT_BLOCK = 128
ROPE_THETA = 10000.0
ATT_SCALE = HEAD_DIM_B ** -0.5
IDX_SCALE = (IDX_DIM * N_IDX_HEADS) ** -0.5
LRU_W = BRANCH_W
LRU_BLOCKS = 4
LRU_BLOCK = LRU_W // LRU_BLOCKS
LRU_C = 8.0
CONV_W = 4
N_HEADS_D = 8
QK_DIM_D = BRANCH_W // (2 * N_HEADS_D)
V_DIM_D = BRANCH_W // N_HEADS_D
RET_CHUNK = 128
D_FF = 11 * D_MODEL // 4
FFN_CONV_W = 3
EPS = 1e-6
PAGE_SIZE = 128

N_MAIN = 8192
N_IDX = 1152
IDX4 = 4 * IDX_DIM

LOG_G = tuple(math.log1p(-(2.0 ** (-5.0 - h))) for h in range(N_HEADS_D))
INT_MIN = -2 ** 31


def _round_up(n, m):
    return (n + m - 1) // m * m


def _cparams(*sem):
    return pltpu.CompilerParams(dimension_semantics=sem, vmem_limit_bytes=VMEM_LIMIT_BYTES)


def _dot(a, b):
    return jnp.dot(a, b, preferred_element_type=F32)


def _dot_nt(a, b):
    return lax.dot_general(a, b, (((1,), (1,)), ((), ())), preferred_element_type=F32)


def _dot_tn(a, b):
    return lax.dot_general(a, b, (((0,), (0,)), ((), ())), preferred_element_type=F32)


def _rms(x, g):
    return x * lax.rsqrt(jnp.mean(x * x, axis=-1, keepdims=True) + EPS) * g


def _gelu_tanh(x):
    return x * (0.5 * (1.0 + jnp.tanh(0.7978845608028654 * (x + 0.044715 * (x * x * x)))))


def _split_bf16(x):
    hi = x.astype(BF16)
    lo = (x - hi.astype(F32)).astype(BF16)
    return hi, lo


def _inproj_kernel(x_ref, g_ref, w_ref, u_ref, xn_ref):
    @pl.when(pl.program_id(1) == 0)
    def _():
        xn_ref[...] = _rms(x_ref[...], g_ref[...]).astype(BF16)

    u_ref[...] = _dot_nt(xn_ref[...], w_ref[...])


def _inproj(x, g, w, l, tm, tn):
    t, d = x.shape
    n = w.shape[1]
    return pl.pallas_call(
        _inproj_kernel,
        grid=(t // tm, n // tn),
        in_specs=[pl.BlockSpec((tm, d), lambda i, j: (i, 0)),
                  pl.BlockSpec((1, d), lambda i, j: (0, 0)),
                  pl.BlockSpec((None, tn, d), lambda i, j: (l, j, 0))],
        out_specs=[pl.BlockSpec((tm, tn), lambda i, j: (i, j)),
                   pl.BlockSpec((tm, d), lambda i, j: (i, 0))],
        out_shape=[jax.ShapeDtypeStruct((t, n), F32), jax.ShapeDtypeStruct((t, d), BF16)],
        compiler_params=_cparams("parallel", "arbitrary"),
    )(x, g, w)


def _inproj_split_kernel(x_ref, g_ref, w3_ref, u_ref, x3_s):
    d = x_ref.shape[1]

    @pl.when(pl.program_id(1) == 0)
    def _():
        hi, lo = _split_bf16(_rms(x_ref[...], g_ref[...]))
        x3_s[:, 0:d] = hi
        x3_s[:, d:2 * d] = hi
        x3_s[:, 2 * d:3 * d] = lo

    u_ref[...] = _dot_nt(x3_s[...], w3_ref[...])


def _inproj_split(x, g, w3, l, tm, tn):
    t, d = x.shape
    n = w3.shape[1]
    return pl.pallas_call(
        _inproj_split_kernel,
        grid=(t // tm, n // tn),
        in_specs=[pl.BlockSpec((tm, d), lambda i, j: (i, 0)),
                  pl.BlockSpec((1, d), lambda i, j: (0, 0)),
                  pl.BlockSpec((None, tn, 3 * d), lambda i, j: (l, j, 0))],
        out_specs=pl.BlockSpec((tm, tn), lambda i, j: (i, j)),
        out_shape=jax.ShapeDtypeStruct((t, n), F32),
        scratch_shapes=[pltpu.VMEM((tm, 3 * d), BF16)],
        compiler_params=_cparams("parallel", "arbitrary"),
    )(x, g, w3)


def _rope128(xs, cos, sin):
    return xs * cos + pltpu.roll(xs, HEAD_DIM_B // 2, axis=1) * sin


def _rope64(xs, cos, sin, first_half):
    rot = jnp.where(first_half, pltpu.roll(xs, LANES - IDX_DIM // 2, axis=1), pltpu.roll(xs, IDX_DIM // 2, axis=1))
    return xs * cos + rot * sin


def _rope_kernel(bq_ref, bk_ref, bv_ref, dq_ref, dk_ref, ui_ref, c128_ref, s128_ref, c64_ref, s64_ref,
                 *rest, n_alias):
    q_ref, k_ref, kb_ref, vb_ref, iq4_ref, ik4_ref, ikw_ref, rq_ref, rk_ref = rest[n_alias:n_alias + 9]
    final = rest[n_alias + 9:]
    rows = bq_ref.shape[0]
    c128, s128, c64, s64 = c128_ref[...], s128_ref[...], c64_ref[...], s64_ref[...]
    lane = lax.broadcasted_iota(I32, (rows, LANES), 1)
    first_half = (lane & (IDX_DIM // 2)) == 0
    low = lane < IDX_DIM
    for c in range(N_HEADS_B):
        sl = slice(c * LANES, (c + 1) * LANES)
        q_ref[:, sl] = (_rope128(bq_ref[:, sl], c128, s128) * ATT_SCALE).astype(BF16)
    for c in range(N_KV_B):
        sl = slice(c * LANES, (c + 1) * LANES)
        kr = _rope128(bk_ref[:, sl], c128, s128)
        k_ref[:, sl] = kr
        kb_ref[:, sl] = kr.astype(BF16)
        if final:
            final[0][pl.ds(c, rows, stride=N_KV_B), :] = kr
            final[1][pl.ds(c, rows, stride=N_KV_B), :] = bv_ref[:, sl]
    vb_ref[...] = bv_ref[...].astype(BF16)
    for c in range(N_HEADS_D * QK_DIM_D // LANES):
        sl = slice(c * LANES, (c + 1) * LANES)
        rq_ref[:, sl] = _rope64(dq_ref[:, sl], c64, s64, first_half)
        rk_ref[:, sl] = _rope64(dk_ref[:, sl], c64, s64, first_half) * (QK_DIM_D ** -0.5)
    for c in range(N_IDX_HEADS // 2):
        y = _rope64(ui_ref[:, c * LANES:(c + 1) * LANES], c64, s64, first_half)
        yr = pltpu.roll(y, IDX_DIM, axis=1)
        for hh, dup in enumerate((jnp.where(low, y, yr), jnp.where(low, yr, y))):
            hi, lo = _split_bf16(dup)
            base = (2 * c + hh) * IDX4
            iq4_ref[:, base:base + LANES] = hi
            iq4_ref[:, base + LANES:base + 2 * LANES] = lo
    raw = ui_ref[:, N_IDX_HEADS * IDX_DIM:N_IDX_HEADS * IDX_DIM + LANES]
    y = _rope64(raw, c64, s64, first_half)
    ikw_ref[...] = jnp.where(low, y, raw)
    dup = jnp.where(low, y, pltpu.roll(y, IDX_DIM, axis=1))
    hi = dup.astype(BF16).astype(F32)
    hilo = jnp.where(low, hi, dup - hi).astype(BF16)
    ik4_ref[:, 0:LANES] = hilo
    ik4_ref[:, LANES:2 * LANES] = hilo


def _rope_prep(u_main, u_idx, tabs, tr, kv_final=None):
    t = u_main.shape[0]
    nt = tabs[0].shape[0] // tr
    row = lambda w, c: pl.BlockSpec((tr, w), lambda i: (i, c))
    tab = pl.BlockSpec((tr, LANES), lambda i: (i % nt, 0))
    outs = [(BRANCH_W, BF16), (N_KV_B * HEAD_DIM_B, F32), (N_KV_B * HEAD_DIM_B, BF16), (N_KV_B * HEAD_DIM_B, BF16),
            (N_IDX_HEADS * IDX4, BF16), (IDX4, BF16), (LANES, F32), (N_HEADS_D * QK_DIM_D, F32),
            (N_HEADS_D * QK_DIM_D, F32)]
    in_specs = [row(1024, 1), row(512, 4), row(512, 5), row(512, 10), row(512, 11), row(N_IDX, 0), tab, tab, tab, tab]
    args = [u_main, u_main, u_main, u_main, u_main, u_idx, *tabs]
    out_specs = [row(w, 0) for w, _ in outs]
    out_shape = [jax.ShapeDtypeStruct((t, w), dt) for w, dt in outs]
    aliases = {}
    if kv_final is not None:
        layer, depth, prev = kv_final
        out_specs += [pl.BlockSpec((None, tr * N_KV_B, HEAD_DIM_B), lambda i: (layer, i, 0))] * 2
        out_shape += [jax.ShapeDtypeStruct((depth, t * N_KV_B, HEAD_DIM_B), F32)] * 2
        if prev is not None:
            aliases = {len(args): len(outs), len(args) + 1: len(outs) + 1}
            in_specs += [pl.BlockSpec(memory_space=pl.ANY)] * 2
            args += list(prev)
    return pl.pallas_call(
        functools.partial(_rope_kernel, n_alias=len(aliases)),
        grid=(t // tr,),
        in_specs=in_specs,
        out_specs=out_specs,
        out_shape=out_shape,
        input_output_aliases=aliases,
        compiler_params=_cparams("parallel"),
    )(*args)


def _rope_tables(pos):
    def tab(half, reps):
        inv = jnp.exp(-math.log(ROPE_THETA) * jnp.arange(half, dtype=F32) / half)
        ang = pos[:, None] * inv[None, :]
        cos, sin = jnp.cos(ang), jnp.sin(ang)
        return jnp.tile(jnp.concatenate([cos, cos], 1), (1, reps)), jnp.tile(jnp.concatenate([-sin, sin], 1), (1, reps))

    c128, s128 = tab(HEAD_DIM_B // 2, 1)
    c64, s64 = tab(IDX_DIM // 2, 2)
    return c128, s128, c64, s64


def _pool_kernel(a_ref, prev_ref, w_ref, sc_ref, o_ref, ext_s, *, S, Tc, start, nch):
    ch = pl.program_id(1)
    R = Tc * S
    PS = POOL_BUF * S
    OFF = _round_up(PS, SUBLANES)

    @pl.when(ch == 0)
    def _():
        ext_s[OFF - PS:OFF, :] = prev_ref[0]

    x = a_ref[...]
    ext_s[OFF:OFF + R, :] = x
    t_loc = lax.broadcasted_iota(I32, (R, 1), 0) // S if S > 1 else lax.broadcasted_iota(I32, (R, 1), 0)
    pos1 = start + ch * Tc + t_loc + 1
    for gi, w in enumerate(POOL_WINDOWS):
        sl = slice(gi * POOL_GROUP, (gi + 1) * POOL_GROUP)
        xs = x[:, sl]
        acc = xs
        for j in range(1, w):
            acc = acc + ext_s[OFF - j * S:OFF - j * S + R, sl]
        cnt = jnp.minimum(w, pos1).astype(F32)
        mixed = acc / cnt - xs
        y = _dot(mixed.astype(BF16), w_ref[gi])
        o_ref[:, sl] = (y * sc_ref[:, sl]).astype(o_ref.dtype)
    if nch > 1:
        ext_s[OFF - PS:OFF, :] = ext_s[OFF + R - PS:OFF + R, :]


def _pool(u_main, prev, w_pool, scale, *, S, Tc, start):
    t = u_main.shape[0]
    R = Tc * S
    G = prev.shape[0]
    nch = t // (G * R)
    PS = POOL_BUF * S
    C = BRANCH_W
    return pl.pallas_call(
        functools.partial(_pool_kernel, S=S, Tc=Tc, start=start, nch=nch),
        grid=(G, nch),
        in_specs=[pl.BlockSpec((R, C), lambda g, c: (g * nch + c, 0)),
                  pl.BlockSpec((1, PS, C), lambda g, c: (g, 0, 0)),
                  pl.BlockSpec((len(POOL_WINDOWS), POOL_GROUP, POOL_GROUP), lambda g, c: (0, 0, 0)),
                  pl.BlockSpec((1, C), lambda g, c: (0, 0))],
        out_specs=pl.BlockSpec((R, C), lambda g, c: (g * nch + c, 0)),
        out_shape=jax.ShapeDtypeStruct((t, C), BF16),
        scratch_shapes=[pltpu.VMEM((_round_up(PS, SUBLANES) + R, C), F32)],
        compiler_params=_cparams("parallel", "arbitrary"),
    )(u_main, prev, w_pool, scale)


def _rglru_kernel(cx_ref, cg_ref, prev_ref, h0_ref, wc_ref, bc_ref, wa_ref, ba_ref, wx_ref, bx_ref, lam_ref,
                  o_ref, hl_ref, ext_s, a_s, b_s, h_s, *, S, Tc, nch):
    ch = pl.program_id(1)
    R = Tc * S
    PS = (CONV_W - 1) * S
    OFF = _round_up(PS, SUBLANES)

    @pl.when(ch == 0)
    def _():
        ext_s[OFF - PS:OFF, :] = prev_ref[0]
        h_s[...] = h0_ref[0]

    x = cx_ref[...]
    ext_s[OFF:OFF + R, :] = x
    xc = bc_ref[...] + x * wc_ref[CONV_W - 1:CONV_W, :]
    for j in range(CONV_W - 1):
        k = CONV_W - 1 - j
        xc = xc + ext_s[OFF - k * S:OFF - k * S + R, :] * wc_ref[j:j + 1, :]
    xb = xc.astype(BF16)
    for n in range(LRU_BLOCKS):
        sl = slice(n * LRU_BLOCK, (n + 1) * LRU_BLOCK)
        r = jax.nn.sigmoid(_dot(xb[:, sl], wa_ref[n]) + ba_ref[:, sl])
        i = jax.nn.sigmoid(_dot(xb[:, sl], wx_ref[n]) + bx_ref[:, sl])
        lam = lam_ref[:, sl]
        softplus_neg = jnp.maximum(-lam, 0.0) + jnp.log1p(jnp.exp(-jnp.abs(lam)))
        log_a = (-LRU_C) * r * softplus_neg
        a_s[:, sl] = jnp.exp(log_a)
        th = jnp.tanh(log_a)
        b_s[:, sl] = jnp.sqrt(-2.0 * th / (1.0 - th)) * (i * xc[:, sl])

    if S % SUBLANES == 0:
        def step(t, h):
            off = pl.multiple_of(t * S, S)
            h = a_s[pl.ds(off, S), :] * h + b_s[pl.ds(off, S), :]
            b_s[pl.ds(off, S), :] = h
            return h

        h = lax.fori_loop(0, Tc, step, h_s[...])
    else:
        assert S == 1 and Tc % SUBLANES == 0
        row =lax.broadcasted_iota(I32, (SUBLANES, a_s.shape[1]), 0)

        def tile_step(i, h):
            off = pl.multiple_of(i * SUBLANES, SUBLANES)
            A = a_s[pl.ds(off, SUBLANES), :]
            B = b_s[pl.ds(off, SUBLANES), :]
            for sh in (1, 2, 4):
                keep = row >= sh
                B = A * jnp.where(keep, pltpu.roll(B, sh, axis=0), 0.0) + B
                A = A * jnp.where(keep, pltpu.roll(A, sh, axis=0), 1.0)
            hs = A * h + B
            b_s[pl.ds(off, SUBLANES), :] = hs
            return hs[SUBLANES - 1:SUBLANES, :]

        h = lax.fori_loop(0, Tc // SUBLANES, tile_step, h_s[...])
    h_s[...] = h
    hl_ref[0] = h
    o_ref[...] = (_gelu_tanh(cg_ref[...]) * b_s[...]).astype(o_ref.dtype)
    if nch > 1:
        ext_s[OFF - PS:OFF, :] = ext_s[OFF + R - PS:OFF + R, :]


def _rglru(u_main, prev, h0, lp, *, S, Tc):
    t = u_main.shape[0]
    R = Tc * S
    G = prev.shape[0]
    nch = t // (G * R)
    PS = (CONV_W - 1) * S
    C = LRU_W
    vec = pl.BlockSpec((1, C), lambda g, c: (0, 0))
    blk = pl.BlockSpec((LRU_BLOCKS, LRU_BLOCK, LRU_BLOCK), lambda g, c: (0, 0, 0))
    return pl.pallas_call(
        functools.partial(_rglru_kernel, S=S, Tc=Tc, nch=nch),
        grid=(G, nch),
        in_specs=[pl.BlockSpec((R, C), lambda g, c: (g * nch + c, 3)),
                  pl.BlockSpec((R, C), lambda g, c: (g * nch + c, 4)),
                  pl.BlockSpec((1, PS, C), lambda g, c: (g, 0, 0)),
                  pl.BlockSpec((1, S, C), lambda g, c: (g, 0, 0)),
                  pl.BlockSpec((CONV_W, C), lambda g, c: (0, 0)), vec, blk, vec, blk, vec, vec],
        out_specs=[pl.BlockSpec((R, C), lambda g, c: (g * nch + c, 0)),
                   pl.BlockSpec((1, S, C), lambda g, c: (g, 0, 0))],
        out_shape=[jax.ShapeDtypeStruct((t, C), BF16), jax.ShapeDtypeStruct((G, S, C), F32)],
        scratch_shapes=[pltpu.VMEM((_round_up(PS, SUBLANES) + R, C), F32), pltpu.VMEM((R, C), F32),
                        pltpu.VMEM((R, C), F32), pltpu.VMEM((S, C), F32)],
        compiler_params=_cparams("parallel", "arbitrary"),
    )(u_main, u_main, prev, h0, lp["w_conv_c"], lp["b_conv_c"], lp["w_rg_a"], lp["b_rg_a"], lp["w_rg_x"],
      lp["b_rg_x"], lp["lru_lambda"])


def _ret_kernel(rq_ref, rk_ref, rv_ref, dg_ref, s0_ref, gn_ref, o_ref, sn_ref, s_s, *, c, spb):
    @pl.when(pl.program_id(1) == 0)
    def _():
        s_s[...] = s0_ref[...]

    ii = lax.broadcasted_iota(I32, (c, c), 0)
    jj = lax.broadcasted_iota(I32, (c, c), 1)
    dif = (ii - jj).astype(F32)
    tpos = lax.broadcasted_iota(I32, (c, 1), 0).astype(F32)
    mm = BF16 if c % 16 == 0 else F32
    for h in range(N_HEADS_D):
        lg = LOG_G[h]
        qs = slice(h * QK_DIM_D, (h + 1) * QK_DIM_D)
        vs = slice(h * V_DIM_D, (h + 1) * V_DIM_D)
        intra = jnp.where(dif >= 0.0, jnp.exp(lg * jnp.maximum(dif, 0.0)), 0.0)
        q_dec = jnp.exp(lg * (tpos + 1.0))
        k_dec = jnp.exp(lg * (c - 1.0 - tpos))
        for sq in range(spb):
            rows = slice(sq * c, (sq + 1) * c)
            q = rq_ref[rows, qs].astype(mm)
            k = rk_ref[rows, qs]
            v = rv_ref[rows, vs].astype(mm)
            att = _dot_nt(q, k.astype(mm)) * intra
            s = s_s[sq, h]
            o = _dot(att.astype(mm), v) + _dot(q, s.astype(mm)) * q_dec
            s_s[sq, h] = s * math.exp(lg * c) + _dot_tn((k * k_dec).astype(mm), v)
            mu = jnp.mean(o, axis=-1, keepdims=True)
            var = jnp.mean(jnp.square(o - mu), axis=-1, keepdims=True)
            y = (o - mu) * lax.rsqrt(var + EPS) * gn_ref[:, vs]
            dg = dg_ref[rows, vs]
            o_ref[rows, vs] = (dg * jax.nn.sigmoid(dg) * y).astype(o_ref.dtype)
    sn_ref[...] = s_s[...]


def _retention(rq, rk, rv, dg, s0, gn, *, c, v_col, g_col, spb, layer=None):
    t = rq.shape[0]
    st_shape = s0.shape if layer is None else s0.shape[1:]
    B = st_shape[0]
    nch = t // (B * c)
    assert spb == 1 or nch == 1
    W = N_HEADS_D * QK_DIM_D
    R = spb * c
    st = pl.BlockSpec((spb, N_HEADS_D, QK_DIM_D, V_DIM_D), lambda b, i: (b, 0, 0, 0))
    st_in = st if layer is None else pl.BlockSpec((None, spb, N_HEADS_D, QK_DIM_D, V_DIM_D),
                                                  lambda b, i: (layer, b, 0, 0, 0))
    return pl.pallas_call(
        functools.partial(_ret_kernel, c=c, spb=spb),
        grid=(B // spb, nch),
        in_specs=[pl.BlockSpec((R, W), lambda b, i: (b * nch + i, 0)),
                  pl.BlockSpec((R, W), lambda b, i: (b * nch + i, 0)),
                  pl.BlockSpec((R, BRANCH_W), lambda b, i: (b * nch + i, v_col)),
                  pl.BlockSpec((R, BRANCH_W), lambda b, i: (b * nch + i, g_col)),
                  st_in, pl.BlockSpec((1, BRANCH_W), lambda b, i: (0, 0))],
        out_specs=[pl.BlockSpec((R, BRANCH_W), lambda b, i: (b * nch + i, 0)), st],
        out_shape=[jax.ShapeDtypeStruct((t, BRANCH_W), BF16), jax.ShapeDtypeStruct(st_shape, F32)],
        scratch_shapes=[pltpu.VMEM((spb, N_HEADS_D, QK_DIM_D, V_DIM_D), F32)],
        compiler_params=_cparams("parallel", "arbitrary"),
    )(rq, rk, rv, dg, s0, gn)


def _sortable(x):
    b = lax.bitcast_convert_type(x, I32)
    return b ^ ((b >> 31) & 0x7FFFFFFF)


def _count(m):
    return jnp.sum(jnp.where(m, 1.0, 0.0), axis=1, keepdims=True)


def _kth_largest_key(keys, kf, bits):
    m = keys.shape[0]
    t0 = jnp.where(_count(keys >= 0) >= kf, 0, INT_MIN).astype(I32)

    def step(nb, shift, t):
        digit = jnp.zeros((m, 1), I32)
        for v in range(1, 2 ** nb):
            cand = t | jnp.left_shift(jnp.int32(v), shift)
            digit = digit + jnp.where(_count(keys >= cand) >= kf, 1, 0)
        return t | jnp.left_shift(digit, shift)

    nfull, rem = divmod(31, bits)
    t = lax.fori_loop(0, nfull, lambda i, t: step(bits, 31 - bits * (i + 1), t), t0)
    return step(rem, 0, t) if rem else t


def _topk_mask(keys, valid, idx, k, bits):
    m, n = keys.shape
    kf = float(k)
    thr = _kth_largest_key(keys, kf, bits)
    gt = keys > thr
    eq = (keys == thr) & valid
    need = kf - _count(gt)
    excess = jnp.max(_count(eq) - need) > 0.0
    nbits = max(1, (n - 1).bit_length())

    def first_ties():
        def idx_step(i, j):
            cand = j | jnp.left_shift(jnp.int32(1), nbits - 1 - i)
            return jnp.where(_count(eq & (idx < cand)) <= need - 1.0, cand, j)

        return lax.fori_loop(0, nbits, idx_step, jnp.zeros((m, 1), I32))

    jthr = lax.cond(excess, first_ties, lambda: jnp.full((m, 1), 2 ** nbits, I32))
    return (gt & valid) | (eq & (idx <= jthr))


Q_DIGITS = (7, 8, 8)
Q_MAX = float(2 ** sum(Q_DIGITS) - 1)


def _count_ge(b, cand):
    m, n = b.shape
    ct = jnp.broadcast_to(cand, (m, LANES)).astype(b.dtype)
    one, zero = jnp.ones((), b.dtype), jnp.zeros((), b.dtype)
    acc = jnp.where(b[:, 0:LANES] >= ct, one, zero)
    for c in range(1, n // LANES):
        acc = acc + jnp.where(b[:, c * LANES:(c + 1) * LANES] >= ct, one, zero)
    return jnp.sum(acc.astype(F32), axis=1, keepdims=True)


def _digit_search(b, kf, nbits):
    def step(i, t):
        cand = t | jnp.left_shift(jnp.int32(1), nbits - 1 - i)
        return jnp.where(_count_ge(b, cand.astype(F32)) >= kf, cand, t)

    return lax.fori_loop(0, nbits, step, jnp.zeros((b.shape[0], 1), I32)).astype(F32)


def _topk_mask_fast(score, valid, idx, k, digit_dtype):
    kf = float(k)
    lo = jnp.min(jnp.where(valid, score, jnp.inf), axis=1, keepdims=True)
    hi = jnp.max(jnp.where(valid, score, -jnp.inf), axis=1, keepdims=True)
    scale = jnp.where(hi > lo, Q_MAX / (hi - lo), 0.0)
    q = jnp.where(valid, jnp.minimum(jnp.floor((score - lo) * scale), Q_MAX), -1.0)
    w1, w0 = float(2 ** Q_DIGITS[2]), float(2 ** (Q_DIGITS[1] + Q_DIGITS[2]))
    d2 = jnp.floor(q * (1.0 / w0))
    r = q - d2 * w0
    d1 = jnp.floor(r * (1.0 / w1))
    d0 = r - d1 * w1
    b2 = d2.astype(digit_dtype)
    t2 = _digit_search(b2, kf, Q_DIGITS[0])
    k1 = kf - _count_ge(b2, t2 + 1.0)
    in2 = d2 == t2
    b1 = jnp.where(in2, d1, -1.0).astype(digit_dtype)
    t1 = _digit_search(b1, k1, Q_DIGITS[1])
    k0 = k1 - _count_ge(b1, t1 + 1.0)
    b0 = jnp.where(in2 & (d1 == t1), d0, -1.0).astype(digit_dtype)
    t0 = _digit_search(b0, k0, Q_DIGITS[2])
    thr = t2 * w0 + t1 * w1 + t0
    gt = q > thr
    eq = q == thr
    ambiguous = jnp.max(_count(eq) - (kf - _count(gt))) > 0.0
    fast = jnp.where(gt | eq, 1, 0).astype(I32)

    def exact():
        keys = _sortable(jnp.where(valid, score, -jnp.inf))
        return jnp.where(_topk_mask(keys, valid, idx, k, 1), 1, 0).astype(I32)

    return lax.cond(ambiguous, exact, lambda: fast) > 0


def _masked_attention(s, mask, v):
    s = jnp.where(mask, s, -jnp.inf)
    p = jnp.exp(s - jnp.max(s, axis=1, keepdims=True))
    return _dot(p.astype(BF16), v) / jnp.sum(p, axis=1, keepdims=True)


def _dsa_prompt_kernel(q_ref, kb_ref, vb_ref, iq4_ref, ik4_ref, iw_ref, *rest, j0, topk):
    o_ref = rest[-1]
    nq = q_ref.shape[0]
    nk = kb_ref.shape[0]
    t0 = (j0 + pl.program_id(1)) * nq
    ik4 = ik4_ref[...]
    score = jnp.zeros((nq, nk), F32)
    for h in range(N_IDX_HEADS):
        s = _dot_nt(iq4_ref[:, h * IDX4:(h + 1) * IDX4], ik4)
        score = score + jnp.maximum(s, 0.0) * iw_ref[:, IDX_DIM + h:IDX_DIM + h + 1]
    score = score * IDX_SCALE
    kpos = lax.broadcasted_iota(I32, (nq, nk), 1)
    valid = kpos <= t0 + lax.broadcasted_iota(I32, (nq, nk), 0)
    if nk > topk and j0 * nq + 1 >= topk:
        mask = _topk_mask_fast(score, valid, kpos, topk, BF16)
    elif nk > topk:
        mask = _topk_mask(_sortable(jnp.where(valid, score, -jnp.inf)), valid, kpos, topk, 1)
    else:
        mask = valid
    for g in range(N_KV_B):
        gs = slice(g * HEAD_DIM_B, (g + 1) * HEAD_DIM_B)
        kg = kb_ref[:, gs]
        vg = vb_ref[:, gs]
        for r in range(KV_REP):
            hs = slice((g * KV_REP + r) * HEAD_DIM_B, (g * KV_REP + r + 1) * HEAD_DIM_B)
            o_ref[:, hs] = _masked_attention(_dot_nt(q_ref[:, hs], kg), mask, vg).astype(o_ref.dtype)


def _dsa_prompt(q, kb, vb, iq4, ik4, ikw, *, n_seq, n_cls, qb):
    t = q.shape[0]
    s_len = t // n_seq
    nb = s_len // qb
    nbc = nb // n_cls
    topk = min(TOPK_MAX, s_len // 4)
    q, kb, vb, iq4, ik4, ikw = (a.reshape(n_seq, s_len, a.shape[-1]) for a in (q, kb, vb, iq4, ik4, ikw))
    out = None
    for c in range(n_cls):
        nk = (c + 1) * nbc * qb
        qrow = lambda w, c=c: pl.BlockSpec((None, qb, w), lambda b, j: (b, c * nbc + j, 0))
        seq = lambda w, nk=nk: pl.BlockSpec((None, nk, w), lambda b, j: (b, 0, 0))
        prev = [] if out is None else [out]
        out = pl.pallas_call(
            functools.partial(_dsa_prompt_kernel, j0=c * nbc, topk=topk),
            grid=(n_seq, nbc),
            in_specs=[qrow(BRANCH_W), seq(N_KV_B * HEAD_DIM_B), seq(N_KV_B * HEAD_DIM_B), qrow(N_IDX_HEADS * IDX4),
                      seq(IDX4), qrow(LANES)] + [pl.BlockSpec(memory_space=pl.ANY)] * len(prev),
            out_specs=qrow(BRANCH_W),
            out_shape=jax.ShapeDtypeStruct((n_seq, s_len, BRANCH_W), BF16),
            input_output_aliases={6: 0} if prev else {},
            compiler_params=_cparams("parallel", "arbitrary"),
        )(q, kb, vb, iq4, ik4, ikw, *prev)
    return out.reshape(t, BRANCH_W)


def _dsa_sample_kernel(pt_ref, qg_ref, iq4_ref, w_ref, kn_ref, vn_ref, ikn_ref, *rest, n_pages, t_new, spb):
    del pt_ref
    n_in = 3 * n_pages * spb
    page_refs, (o_ref, sc_s, k_s, v_s) = rest[:n_in], rest[n_in:]
    P = PAGE_SIZE
    pad = P - t_new
    nk = (n_pages + 1) * P
    past = n_pages * P

    for s in range(spb):
        idx_refs = page_refs[3 * n_pages * s:3 * n_pages * s + n_pages]
        k_refs = page_refs[3 * n_pages * s + n_pages:3 * n_pages * s + 2 * n_pages]
        v_refs = page_refs[3 * n_pages * s + 2 * n_pages:3 * n_pages * (s + 1)]
        iq4 = iq4_ref[s]
        w = w_ref[s]

        def page_scores(ik, page, keys_on_lanes):
            hi, lo = _split_bf16(ik)
            if keys_on_lanes:
                sc = _dot(iq4, jnp.concatenate([hi, lo, hi, lo], axis=0))
            else:
                sc = _dot_nt(iq4, jnp.concatenate([hi, lo, hi, lo], axis=1))
            r = jnp.maximum(sc, 0.0) * w
            acc = r[0:t_new]
            for h in range(1, N_IDX_HEADS):
                acc = acc + r[h * t_new:(h + 1) * t_new]
            sc_s[s * t_new:(s + 1) * t_new, page * P:(page + 1) * P] = acc * IDX_SCALE

        for p in range(n_pages):
            page_scores(idx_refs[p][0, 0], p, True)
            for g in range(N_KV_B):
                gs = slice(g * HEAD_DIM_B, (g + 1) * HEAD_DIM_B)
                k_s[s, p * P:(p + 1) * P, gs] = k_refs[p][0, 0, pl.ds(g, P, stride=N_KV_B), :].astype(BF16)
                v_s[s, p * P:(p + 1) * P, gs] = v_refs[p][0, 0, pl.ds(g, P, stride=N_KV_B), :].astype(BF16)
        page_scores(jnp.concatenate([ikn_ref[s], jnp.zeros((pad, IDX_DIM), F32)], axis=0), n_pages, False)
        zkv = jnp.zeros((pad, N_KV_B * HEAD_DIM_B), F32)
        k_s[s, past:nk, :] = jnp.concatenate([kn_ref[s], zkv], axis=0).astype(BF16)
        v_s[s, past:nk, :] = jnp.concatenate([vn_ref[s], zkv], axis=0).astype(BF16)

    rows = spb * t_new
    kpos = lax.broadcasted_iota(I32, (rows, nk), 1)
    valid = kpos <= past + (lax.broadcasted_iota(I32, (rows, nk), 0) & (t_new - 1))
    keys = _sortable(jnp.where(valid, sc_s[...], -jnp.inf))
    mask = _topk_mask(keys, valid, kpos, min(TOPK_MAX, (past + t_new) // 4), 3)
    mask_i = jnp.where(mask, 1, 0).astype(I32)
    for s in range(spb):
        m_s = jnp.concatenate([mask_i[s * t_new:(s + 1) * t_new]] * KV_REP, axis=0) > 0
        for g in range(N_KV_B):
            gs = slice(g * HEAD_DIM_B, (g + 1) * HEAD_DIM_B)
            o_ref[s, g] = _masked_attention(_dot_nt(qg_ref[s, g], k_s[s, :, gs]), m_s, v_s[s, :, gs]).astype(o_ref.dtype)


def _dsa_sample(qg, iq4r, wr, k_new, v_new, ik_new, cache_k, cache_v, cache_idx_k, page_table, layer, spb):
    B, _, rows, hd = qg.shape
    t_new = k_new.shape[1]
    assert t_new & (t_new - 1) == 0 and B % spb == 0
    n_pages = page_table.shape[1]
    kvw = N_KV_B * HEAD_DIM_B
    per_seq = lambda shape: pl.BlockSpec((spb,) + shape, lambda b, pt: (b,) + (0,) * len(shape))

    def page_spec(rows_, width, s, p):
        return pl.BlockSpec((1, 1, rows_, width), lambda b, pt: (layer, pt[b * spb + s, p], 0, 0))

    in_specs = [per_seq(qg.shape[1:]), per_seq(iq4r.shape[1:]), per_seq(wr.shape[1:]), per_seq((t_new, kvw)),
                per_seq((t_new, kvw)), per_seq((t_new, IDX_DIM))]
    pages = []
    for s in range(spb):
        in_specs += [page_spec(IDX_DIM, PAGE_SIZE, s, p) for p in range(n_pages)]
        in_specs += [page_spec(PAGE_SIZE * N_KV_B, HEAD_DIM_B, s, p) for p in range(n_pages)]
        in_specs += [page_spec(PAGE_SIZE * N_KV_B, HEAD_DIM_B, s, p) for p in range(n_pages)]
        pages += [cache_idx_k] * n_pages + [cache_k] * n_pages + [cache_v] * n_pages
    nk = (n_pages + 1) * PAGE_SIZE
    grid_spec = pltpu.PrefetchScalarGridSpec(
        num_scalar_prefetch=1, grid=(B // spb,), in_specs=in_specs,
        out_specs=pl.BlockSpec((spb,) + qg.shape[1:], lambda b, pt: (b, 0, 0, 0)),
        scratch_shapes=[pltpu.VMEM((spb * t_new, nk), F32), pltpu.VMEM((spb, nk, kvw), BF16),
                        pltpu.VMEM((spb, nk, kvw), BF16)])
    return pl.pallas_call(
        functools.partial(_dsa_sample_kernel, n_pages=n_pages, t_new=t_new, spb=spb),
        grid_spec=grid_spec,
        out_shape=jax.ShapeDtypeStruct(qg.shape, BF16),
        compiler_params=_cparams("arbitrary"),
    )(page_table, qg, iq4r, wr, k_new, v_new, ik_new, *pages)


def _merge_kernel(xn_ref, a_ref, b_ref, c_ref, d_ref, wg0, wg1, wg2, wg3, bg0, bg1, bg2, bg3, wb_ref, o_ref):
    xn = xn_ref[...]
    acc = None
    for n, (br, wg, bg) in enumerate(zip((a_ref, b_ref, c_ref, d_ref), (wg0, wg1, wg2, wg3), (bg0, bg1, bg2, bg3))):
        term = jax.nn.sigmoid(_dot(xn, wg[...]) + bg[...]) * _dot(br[...], wb_ref[n])
        acc = term if acc is None else acc + term
    o_ref[...] = acc.astype(o_ref.dtype)


def _merge(xn, branches, w_gate, b_gate, w_branch, l, tm, tn):
    t, d = xn.shape
    nj = d // tn
    row = lambda w: pl.BlockSpec((tm, w), lambda i, j: (i, 0))
    wg = [pl.BlockSpec((None, d, tn), functools.partial(lambda i, j, n: (l, 0, n * nj + j), n=n))
          for n in range(N_BRANCH)]
    bg = [pl.BlockSpec((1, tn), functools.partial(lambda i, j, n: (0, n * nj + j), n=n)) for n in range(N_BRANCH)]
    return pl.pallas_call(
        _merge_kernel,
        grid=(t // tm, nj),
        in_specs=[row(d)] + [row(BRANCH_W)] * N_BRANCH + wg + bg
                 + [pl.BlockSpec((None, N_BRANCH, BRANCH_W, tn), lambda i, j: (l, 0, 0, j))],
        out_specs=pl.BlockSpec((tm, tn), lambda i, j: (i, j)),
        out_shape=jax.ShapeDtypeStruct((t, d), BF16),
        compiler_params=_cparams("parallel", "arbitrary"),
    )(xn, *branches, *([w_gate] * N_BRANCH), *([b_gate] * N_BRANCH), w_branch)


def _outproj_kernel(m_ref, x_ref, w_ref, g_ref, o_ref):
    o_ref[...] = x_ref[...] + _rms(_dot(m_ref[...], w_ref[...]), g_ref[...])


def _outproj(merged, x, w_o, g, l, tm):
    t, d = x.shape
    return pl.pallas_call(
        _outproj_kernel,
        grid=(t // tm,),
        in_specs=[pl.BlockSpec((tm, d), lambda i: (i, 0)), pl.BlockSpec((tm, d), lambda i: (i, 0)),
                  pl.BlockSpec((None, d, d), lambda i: (l, 0, 0)), pl.BlockSpec((1, d), lambda i: (0, 0))],
        out_specs=pl.BlockSpec((tm, d), lambda i: (i, 0)),
        out_shape=jax.ShapeDtypeStruct((t, d), F32),
        compiler_params=_cparams("parallel"),
    )(merged, x, w_o, g)


def _ffn_kernel(x_ref, gpre_ref, wug_ref, wuv_ref, wcg_ref, wcv_ref, bcg_ref, bcv_ref, wd_ref, gpost_ref,
                stg_ref, stv_ref, o_ref, ng_ref, nv_ref, h_s, acc_s, cg_s, cv_s, ext_s, *, S, tiles_per_group):
    i = pl.program_id(0)
    j = pl.program_id(1)
    tm = x_ref.shape[0]
    CR = cg_s.shape[1]

    @pl.when(j == 0)
    def _():
        h_s[...] = _rms(x_ref[...], gpre_ref[...]).astype(BF16)
        acc_s[...] = jnp.zeros_like(acc_s)

    @pl.when(i % tiles_per_group == 0)
    def _():
        cg_s[j] = stg_ref[0]
        cv_s[j] = stv_ref[0]

    h = h_s[...]

    def conv(w_ref, wc_ref, bc_ref, c_s, new_ref):
        u = _dot(h, w_ref[...])
        ext_s[0:CR, :] = c_s[j]
        ext_s[CR:CR + tm, :] = u
        y = bc_ref[...] + u * wc_ref[2:3, :]
        y = y + ext_s[CR - S:CR - S + tm, :] * wc_ref[1:2, :]
        y = y + ext_s[CR - 2 * S:CR - 2 * S + tm, :] * wc_ref[0:1, :]
        tail = ext_s[tm:tm + CR, :]
        c_s[j] = tail
        new_ref[0] = tail
        return y

    gate = conv(wug_ref, wcg_ref, bcg_ref, cg_s, ng_ref)
    val = conv(wuv_ref, wcv_ref, bcv_ref, cv_s, nv_ref)
    acc_s[...] += _dot((_gelu_tanh(gate) * val).astype(BF16), wd_ref[...])

    @pl.when(j == pl.num_programs(1) - 1)
    def _():
        o_ref[...] = x_ref[...] + _rms(acc_s[...], gpost_ref[...])


def _ffn(x, lp, state, *, S, tm, tn):
    t, d = x.shape
    G, CR, _ = state.shape
    nj = D_FF // tn
    ni = t // tm
    tiles_per_group = ni // G
    grp = lambda i, j: (i // tiles_per_group, 0, j)
    tile = lambda i, j: (i, 0, j)
    grp_v = lambda i, j: (i // tiles_per_group, 0, nj + j)
    col = lambda r: pl.BlockSpec((r, tn), lambda i, j: (0, j))
    col_v = lambda r: pl.BlockSpec((r, tn), lambda i, j: (0, nj + j))
    vec = pl.BlockSpec((1, d), lambda i, j: (0, 0))
    l = lp["layer"]
    xo, ng, nv = pl.pallas_call(
        functools.partial(_ffn_kernel, S=S, tiles_per_group=tiles_per_group),
        grid=(ni, nj),
        in_specs=[pl.BlockSpec((tm, d), lambda i, j: (i, 0)), vec,
                  pl.BlockSpec((None, d, tn), lambda i, j: (l, 0, j)),
                  pl.BlockSpec((None, d, tn), lambda i, j: (l, 0, nj + j)),
                  col(FFN_CONV_W), col_v(FFN_CONV_W), col(1), col_v(1),
                  pl.BlockSpec((None, tn, d), lambda i, j: (l, j, 0)), vec,
                  pl.BlockSpec((1, CR, tn), grp), pl.BlockSpec((1, CR, tn), grp_v)],
        out_specs=[pl.BlockSpec((tm, d), lambda i, j: (i, 0)),
                   pl.BlockSpec((1, CR, tn), tile), pl.BlockSpec((1, CR, tn), tile)],
        out_shape=[jax.ShapeDtypeStruct((t, d), F32), jax.ShapeDtypeStruct((ni, CR, D_FF), F32),
                   jax.ShapeDtypeStruct((ni, CR, D_FF), F32)],
        scratch_shapes=[pltpu.VMEM((tm, d), BF16), pltpu.VMEM((tm, d), F32), pltpu.VMEM((nj, CR, tn), F32),
                        pltpu.VMEM((nj, CR, tn), F32), pltpu.VMEM((CR + tm, tn), F32)],
        compiler_params=_cparams("arbitrary", "arbitrary"),
    )(x, lp["g_ffn_pre"], lp["w_up"], lp["w_up"], lp["w_ffn_conv"], lp["w_ffn_conv"], lp["b_ffn_conv"],
      lp["b_ffn_conv"], lp["w_down"], lp["g_ffn_post"], state, state)
    last = slice(tiles_per_group - 1, None, tiles_per_group)
    return xo, ng[last], nv[last]


def _layer_front(x, lp, tabs, *, tm, kv_final=None):
    l = lp["layer"]
    u_main, xn = _inproj(x, lp["g_mix_pre"], lp["w_in_main"], l, tm, 512)
    u_idx = _inproj_split(x, lp["g_mix_pre"], lp["w_in_idx3"], l, 512, N_IDX)
    return u_main, xn, _rope_prep(u_main, u_idx, tabs, 256, kv_final)


def _layer_back(x, xn, branches, lp, ffn_state, *, S, tm, tn_ffn):
    l = lp["layer"]
    merged = _merge(xn, branches, lp["w_gate"], lp["b_gate"], lp["w_branch"], l, tm, 512)
    x1 = _outproj(merged, x, lp["w_o"], lp["g_mix_post"], l, tm)
    return _ffn(x1, lp, ffn_state, S=S, tm=tm, tn=tn_ffn)


def _prep_big_weights(p):
    w_in = jnp.swapaxes(p["w_in"], 1, 2)
    o_iq = 3072
    o_cx = o_iq + N_IDX_HEADS * IDX_DIM + IDX_DIM + N_IDX_HEADS
    w_idx = jnp.pad(w_in[:, o_iq:o_cx], ((0, 0), (0, N_IDX - (o_cx - o_iq)), (0, 0)))
    hi = w_idx.astype(BF16)
    lo = (w_idx - hi.astype(F32)).astype(BF16)
    return dict(
        w_in_main=jnp.concatenate([w_in[:, :o_iq], w_in[:, o_cx:]], axis=1).astype(BF16),
        w_in_idx3=jnp.concatenate([hi, lo, hi], axis=2),
        w_branch=p["w_branch"].astype(BF16), w_gate=p["w_gate"].astype(BF16), w_o=p["w_o"].astype(BF16),
        w_up=p["w_up"].astype(BF16), w_down=p["w_down"].astype(BF16))


def _prep_layer_params(p, big, l):
    row = lambda a: a[l][None, :]
    return dict(
        big, layer=l,
        g_mix_pre=row(p["g_mix_pre"]),
        w_pool=p["w_pool"][l].astype(BF16), pool_scale=row(p["pool_scale"]),
        w_conv_c=p["w_conv_c"][l], b_conv_c=row(p["b_conv_c"]),
        w_rg_a=p["w_rg_a"][l].astype(BF16), b_rg_a=row(p["b_rg_a"]),
        w_rg_x=p["w_rg_x"][l].astype(BF16), b_rg_x=row(p["b_rg_x"]),
        lru_lambda=row(p["lru_lambda"]), ret_gn=row(p["ret_gn"]), b_gate=row(p["b_gate"]),
        g_mix_post=row(p["g_mix_post"]), g_ffn_pre=row(p["g_ffn_pre"]),
        w_ffn_conv=p["w_ffn_conv"][l], b_ffn_conv=row(p["b_ffn_conv"]), g_ffn_post=row(p["g_ffn_post"]))


def _to_time_major(a, nb, nt):
    return jnp.swapaxes(a, 0, 1).reshape((nt * nb,) + a.shape[2:])


def _to_seq_major(a, nb, nt):
    return jnp.swapaxes(a.reshape((nt, nb) + a.shape[1:]), 0, 1)


def kernel(x_prompt, x_sample, cache_k, cache_v, cache_idx_k, state_pool, state_conv, state_rglru, state_ret, state_ffn_conv, page_table, g_mix_pre, w_in, w_pool, pool_scale, w_conv_c, b_conv_c, w_rg_a, b_rg_a, w_rg_x, b_rg_x, lru_lambda, ret_gn, w_branch, w_gate, b_gate, w_o, g_mix_post, g_ffn_pre, w_up, w_ffn_conv, b_ffn_conv, w_down, g_ffn_post):
    params = dict(g_mix_pre=g_mix_pre, w_in=w_in, w_pool=w_pool, pool_scale=pool_scale, w_conv_c=w_conv_c,
                  b_conv_c=b_conv_c, w_rg_a=w_rg_a, b_rg_a=b_rg_a, w_rg_x=w_rg_x, b_rg_x=b_rg_x,
                  lru_lambda=lru_lambda, ret_gn=ret_gn, w_branch=w_branch, w_gate=w_gate, b_gate=b_gate, w_o=w_o,
                  g_mix_post=g_mix_post, g_ffn_pre=g_ffn_pre, w_up=w_up, w_ffn_conv=w_ffn_conv,
                  b_ffn_conv=b_ffn_conv, w_down=w_down, g_ffn_post=g_ffn_post)
    depth = w_in.shape[0]
    bp, seq, d = x_prompt.shape
    bs, tdec, _ = x_sample.shape
    past = page_table.shape[1] * PAGE_SIZE
    n_pool = cache_k.shape[1]
    ck = cache_k.reshape(depth, n_pool, PAGE_SIZE * N_KV_B, HEAD_DIM_B)
    cv = cache_v.reshape(depth, n_pool, PAGE_SIZE * N_KV_B, HEAD_DIM_B)
    cik_t = jnp.swapaxes(cache_idx_k, 2, 3)
    big = _prep_big_weights(params)

    tabs_p = _rope_tables(jnp.arange(seq, dtype=F32))
    tabs_s = _rope_tables(jnp.repeat(past + jnp.arange(tdec, dtype=F32), bs))
    xp = x_prompt.reshape(bp * seq, d)
    xs = _to_time_major(x_sample, bs, tdec)
    cr_p = max(SUBLANES, (FFN_CONV_W - 1))
    outs_p, outs_s = [], []
    kv_p = None
    for l in range(depth):
        lp = _prep_layer_params(params, big, l)
        u, xn, (q, _, kb, vb, iq4, ik4, ikw, rq, rk, *kv_p) = _layer_front(xp, lp, tabs_p, tm=1024,
                                                                         kv_final=(l, depth, kv_p))
        o_a = _pool(u, jnp.zeros((bp, POOL_BUF, BRANCH_W), F32), lp["w_pool"], lp["pool_scale"], S=1, Tc=256, start=0)
        o_b = _dsa_prompt(q, kb, vb, iq4, ik4, ikw, n_seq=bp, n_cls=8, qb=256)
        o_c, h_p = _rglru(u, jnp.zeros((bp, CONV_W - 1, LRU_W), F32), jnp.zeros((bp, 1, LRU_W), F32), lp, S=1, Tc=256)
        o_d, s_p = _retention(rq, rk, u, u, jnp.zeros((bp, N_HEADS_D, QK_DIM_D, V_DIM_D), F32), lp["ret_gn"],
                              c=RET_CHUNK if seq % RET_CHUNK == 0 else seq, v_col=6, g_col=7, spb=1)
        xp, ng, nv = _layer_back(xp, xn, (o_a, o_b, o_c, o_d), lp, jnp.zeros((bp, cr_p, 2 * D_FF), F32),
                                 S=1, tm=512, tn_ffn=512)
        u3 = u.reshape(bp, seq, N_MAIN)
        outs_p.append((
            None, None,
            ikw.reshape(bp, seq, LANES)[:, :, :IDX_DIM], u3[:, seq - POOL_BUF:, 0:BRANCH_W],
            u3[:, seq - (CONV_W - 1):, 3072:3072 + LRU_W], h_p[:, 0], s_p,
            jnp.concatenate([ng, nv], axis=-1)[:, cr_p - (FFN_CONV_W - 1):]))
        u, xn, (q, k, kb, vb, iq4, ik4, ikw, rq, rk) = _layer_front(xs, lp, tabs_s, tm=1024)
        o_a = _pool(u, _to_time_major(state_pool[l], bs, POOL_BUF)[None], lp["w_pool"], lp["pool_scale"],
                    S=bs, Tc=tdec, start=past)
        o_c, h_s = _rglru(u, _to_time_major(state_conv[l], bs, CONV_W - 1)[None], state_rglru[l][None], lp,
                          S=bs, Tc=tdec)
        sm = lambda a: _to_seq_major(a, bs, tdec)
        u_sm = sm(u)
        o_d, s_s = _retention(sm(rq).reshape(bs * tdec, -1), sm(rk).reshape(bs * tdec, -1),
                              u_sm[:, :, 6144:7168].reshape(bs * tdec, -1), u_sm[:, :, 7168:8192].reshape(bs * tdec, -1),
                              state_ret, lp["ret_gn"], c=tdec, v_col=0, g_col=0, spb=8, layer=l)
        o_d = _to_time_major(o_d.reshape(bs, tdec, BRANCH_W), bs, tdec)
        qg = sm(q).reshape(bs, tdec, N_KV_B, KV_REP, HEAD_DIM_B).transpose(0, 2, 3, 1, 4)
        qg = qg.reshape(bs, N_KV_B, KV_REP * tdec, HEAD_DIM_B)
        iq4r = sm(iq4).reshape(bs, tdec, N_IDX_HEADS, IDX4).transpose(0, 2, 1, 3).reshape(bs, N_IDX_HEADS * tdec, IDX4)
        ikw_sm = sm(ikw)
        wr = ikw_sm[:, :, IDX_DIM:IDX_DIM + N_IDX_HEADS].transpose(0, 2, 1).reshape(bs, N_IDX_HEADS * tdec, 1)
        k_sm = sm(k)
        v_sm = u_sm[:, :, 2560:3072]
        o_b = _dsa_sample(qg, iq4r, wr, k_sm, v_sm, ikw_sm[:, :, :IDX_DIM], ck, cv, cik_t, page_table, l, 2)
        o_b = o_b.reshape(bs, N_KV_B, KV_REP, tdec, HEAD_DIM_B).transpose(3, 0, 1, 2, 4).reshape(tdec * bs, BRANCH_W)
        ffn_state = _to_time_major(state_ffn_conv[l], bs, FFN_CONV_W - 1)[None]
        xs, ng, nv = _layer_back(xs, xn, (o_a, o_b, o_c, o_d), lp, ffn_state, S=bs, tm=512, tn_ffn=256)
        ffn_new = _to_seq_major(jnp.concatenate([ng, nv], axis=-1)[0], bs, FFN_CONV_W - 1)
        a_in = u_sm[:, :, 0:BRANCH_W]
        c_x = u_sm[:, :, 3072:3072 + LRU_W]
        outs_s.append((
            k_sm.reshape(bs, tdec, N_KV_B, HEAD_DIM_B), v_sm.reshape(bs, tdec, N_KV_B, HEAD_DIM_B),
            ikw_sm[:, :, :IDX_DIM],
            jnp.concatenate([state_pool[l], a_in], axis=1)[:, tdec:],
            jnp.concatenate([state_conv[l], c_x], axis=1)[:, tdec:],
            h_s[0], s_s, ffn_new))

    stk = lambda outs, i: jnp.stack([o[i] for o in outs], axis=0)
    res = [xp.reshape(bp, seq, d), _to_seq_major(xs, bs, tdec)]
    for i in range(8):
        p_i = kv_p[i].reshape(depth, bp, seq, N_KV_B, HEAD_DIM_B) if i < 2 else stk(outs_p, i)
        res += [p_i, stk(outs_s, i)]
    return tuple(res)
```

```python
import functools
import math

import jax
import jax.numpy as jnp
from jax import lax
from jax.experimental import pallas as pl
from jax.experimental.pallas import tpu as pltpu

F32 = jnp.float32
BF16 = jnp.bfloat16
I32 = jnp.int32

SUBLANES = 8
LANES = 128
VMEM_LIMIT_BYTES = 56 * 1024 * 1024

D_MODEL = 2048
BRANCH_W = D_MODEL // 2
N_BRANCH = 4
POOL_WINDOWS = (2, 4, 8, 16)
POOL_GROUP = BRANCH_W // len(POOL_WINDOWS)
POOL_BUF = max(POOL_WINDOWS) - 1
N_HEADS_B = 8
HEAD_DIM_B = BRANCH_W // N_HEADS_B
N_KV_B = 4
KV_REP = N_HEADS_B // N_KV_B
N_IDX_HEADS = 16
IDX_DIM = 64
TOPK_MAX = 256
ATT_BLOCK = 128
ROPE_THETA = 10000.0
ATT_SCALE = HEAD_DIM_B ** -0.5
Q_SCALE = ATT_SCALE * math.log2(math.e)
IDX_SCALE = (IDX_DIM * N_IDX_HEADS) ** -0.5
LRU_W = BRANCH_W
LRU_BLOCKS = 4
LRU_BLOCK = LRU_W // LRU_BLOCKS
LRU_C = 8.0
CONV_W = 4
N_HEADS_D = 8
QK_DIM_D = BRANCH_W // (2 * N_HEADS_D)
V_DIM_D = BRANCH_W // N_HEADS_D
RET_CHUNK = 128
D_FF = 11 * D_MODEL // 4
FFN_CONV_W = 3
EPS = 1e-6
PAGE_SIZE = 128

N_MAIN = 8192
N_IDX = 1152
IDX4 = 4 * IDX_DIM

LOG_G = tuple(math.log1p(-(2.0 ** (-5.0 - h))) for h in range(N_HEADS_D))
INT_MIN = -2 ** 31


def _round_up(n, m):
    return (n + m - 1) // m * m


def _cparams(*sem):
    return pltpu.CompilerParams(dimension_semantics=sem, vmem_limit_bytes=VMEM_LIMIT_BYTES)


def _dot(a, b):
    return jnp.dot(a, b, preferred_element_type=F32)


def _dot_nt(a, b):
    return lax.dot_general(a, b, (((1,), (1,)), ((), ())), preferred_element_type=F32)


def _dot_tn(a, b):
    return lax.dot_general(a, b, (((0,), (0,)), ((), ())), preferred_element_type=F32)


def _rms(x, g):
    return x * lax.rsqrt(jnp.mean(x * x, axis=-1, keepdims=True) + EPS) * g


def _gelu_tanh(x):
    return x * (0.5 * (1.0 + jnp.tanh(0.7978845608028654 * (x + 0.044715 * (x * x * x)))))


def _split_bf16(x):
    hi = x.astype(BF16)
    lo = (x - hi.astype(F32)).astype(BF16)
    return hi, lo


def _inproj_kernel(x_ref, g_ref, w_ref, u_ref, xn_ref):
    @pl.when(pl.program_id(1) == 0)
    def _():
        xn_ref[...] = _rms(x_ref[...], g_ref[...]).astype(BF16)

    u_ref[...] = _dot_nt(xn_ref[...], w_ref[...])


def _inproj(x, g, w, l, tm, tn):
    t, d = x.shape
    n = w.shape[1]
    return pl.pallas_call(
        _inproj_kernel,
        grid=(t // tm, n // tn),
        in_specs=[pl.BlockSpec((tm, d), lambda i, j: (i, 0)),
                  pl.BlockSpec((1, d), lambda i, j: (0, 0)),
                  pl.BlockSpec((None, tn, d), lambda i, j: (l, j, 0))],
        out_specs=[pl.BlockSpec((tm, tn), lambda i, j: (i, j)),
                   pl.BlockSpec((tm, d), lambda i, j: (i, 0))],
        out_shape=[jax.ShapeDtypeStruct((t, n), F32), jax.ShapeDtypeStruct((t, d), BF16)],
        compiler_params=_cparams("parallel", "arbitrary"),
    )(x, g, w)


def _inproj_split_kernel(x_ref, g_ref, w3_ref, u_ref, x3_s):
    d = x_ref.shape[1]

    @pl.when(pl.program_id(1) == 0)
    def _():
        hi, lo = _split_bf16(_rms(x_ref[...], g_ref[...]))
        x3_s[:, 0:d] = hi
        x3_s[:, d:2 * d] = hi
        x3_s[:, 2 * d:3 * d] = lo

    u_ref[...] = _dot_nt(x3_s[...], w3_ref[...])


def _inproj_split(x, g, w3, l, tm, tn):
    t, d = x.shape
    n = w3.shape[1]
    return pl.pallas_call(
        _inproj_split_kernel,
        grid=(t // tm, n // tn),
        in_specs=[pl.BlockSpec((tm, d), lambda i, j: (i, 0)),
                  pl.BlockSpec((1, d), lambda i, j: (0, 0)),
                  pl.BlockSpec((None, tn, 3 * d), lambda i, j: (l, j, 0))],
        out_specs=pl.BlockSpec((tm, tn), lambda i, j: (i, j)),
        out_shape=jax.ShapeDtypeStruct((t, n), F32),
        scratch_shapes=[pltpu.VMEM((tm, 3 * d), BF16)],
        compiler_params=_cparams("parallel", "arbitrary"),
    )(x, g, w3)


def _rope128(xs, cos, sin):
    return xs * cos + pltpu.roll(xs, HEAD_DIM_B // 2, axis=1) * sin


def _rope64(xs, cos, sin, first_half):
    rot = jnp.where(first_half, pltpu.roll(xs, LANES - IDX_DIM // 2, axis=1), pltpu.roll(xs, IDX_DIM // 2, axis=1))
    return xs * cos + rot * sin


def _rope_kernel(bq_ref, bk_ref, bv_ref, dq_ref, dk_ref, ui_ref, c128_ref, s128_ref, c64_ref, s64_ref,
                 *rest, n_alias):
    q_ref, k_ref, kb_ref, vb_ref, iq4_ref, ik4_ref, ikw_ref, rq_ref, rk_ref = rest[n_alias:n_alias + 9]
    final = rest[n_alias + 9:]
    rows = bq_ref.shape[0]
    c128, s128, c64, s64 = c128_ref[...], s128_ref[...], c64_ref[...], s64_ref[...]
    lane = lax.broadcasted_iota(I32, (rows, LANES), 1)
    first_half = (lane & (IDX_DIM // 2)) == 0
    low = lane < IDX_DIM
    for c in range(N_HEADS_B):
        sl = slice(c * LANES, (c + 1) * LANES)
        q_ref[:, sl] = (_rope128(bq_ref[:, sl], c128, s128) * Q_SCALE).astype(BF16)
    for c in range(N_KV_B):
        sl = slice(c * LANES, (c + 1) * LANES)
        kr = _rope128(bk_ref[:, sl], c128, s128)
        k_ref[:, sl] = kr
        kb_ref[:, sl] = kr.astype(BF16)
        if final:
            final[0][pl.ds(c, rows, stride=N_KV_B), :] = kr
            final[1][pl.ds(c, rows, stride=N_KV_B), :] = bv_ref[:, sl]
    vb_ref[...] = bv_ref[...].astype(BF16)
    for c in range(N_HEADS_D * QK_DIM_D // LANES):
        sl = slice(c * LANES, (c + 1) * LANES)
        rq_ref[:, sl] = _rope64(dq_ref[:, sl], c64, s64, first_half)
        rk_ref[:, sl] = _rope64(dk_ref[:, sl], c64, s64, first_half) * (QK_DIM_D ** -0.5)
    for c in range(N_IDX_HEADS // 2):
        y = _rope64(ui_ref[:, c * LANES:(c + 1) * LANES], c64, s64, first_half)
        yr = pltpu.roll(y, IDX_DIM, axis=1)
        for hh, dup in enumerate((jnp.where(low, y, yr), jnp.where(low, yr, y))):
            hi, lo = _split_bf16(dup)
            base = (2 * c + hh) * IDX4
            iq4_ref[:, base:base + LANES] = hi
            iq4_ref[:, base + LANES:base + 2 * LANES] = lo
    raw = ui_ref[:, N_IDX_HEADS * IDX_DIM:N_IDX_HEADS * IDX_DIM + LANES]
    y = _rope64(raw, c64, s64, first_half)
    ikw_ref[...] = jnp.where(low, y, raw)
    dup = jnp.where(low, y, pltpu.roll(y, IDX_DIM, axis=1))
    hi = dup.astype(BF16).astype(F32)
    hilo = jnp.where(low, hi, dup - hi).astype(BF16)
    ik4_ref[:, 0:LANES] = hilo
    ik4_ref[:, LANES:2 * LANES] = hilo


def _rope_prep(u_main, u_idx, tabs, tr, kv_final=None):
    t = u_main.shape[0]
    nt = tabs[0].shape[0] // tr
    row = lambda w, c: pl.BlockSpec((tr, w), lambda i: (i, c))
    tab = pl.BlockSpec((tr, LANES), lambda i: (i % nt, 0))
    outs = [(BRANCH_W, BF16), (N_KV_B * HEAD_DIM_B, F32), (N_KV_B * HEAD_DIM_B, BF16), (N_KV_B * HEAD_DIM_B, BF16),
            (N_IDX_HEADS * IDX4, BF16), (IDX4, BF16), (LANES, F32), (N_HEADS_D * QK_DIM_D, F32),
            (N_HEADS_D * QK_DIM_D, F32)]
    in_specs = [row(1024, 1), row(512, 4), row(512, 5), row(512, 10), row(512, 11), row(N_IDX, 0), tab, tab, tab, tab]
    args = [u_main, u_main, u_main, u_main, u_main, u_idx, *tabs]
    out_specs = [row(w, 0) for w, _ in outs]
    out_shape = [jax.ShapeDtypeStruct((t, w), dt) for w, dt in outs]
    aliases = {}
    if kv_final is not None:
        layer, depth, prev = kv_final
        out_specs += [pl.BlockSpec((None, tr * N_KV_B, HEAD_DIM_B), lambda i: (layer, i, 0))] * 2
        out_shape += [jax.ShapeDtypeStruct((depth, t * N_KV_B, HEAD_DIM_B), F32)] * 2
        if prev is not None:
            aliases = {len(args): len(outs), len(args) + 1: len(outs) + 1}
            in_specs += [pl.BlockSpec(memory_space=pl.ANY)] * 2
            args += list(prev)
    return pl.pallas_call(
        functools.partial(_rope_kernel, n_alias=len(aliases)),
        grid=(t // tr,),
        in_specs=in_specs,
        out_specs=out_specs,
        out_shape=out_shape,
        input_output_aliases=aliases,
        compiler_params=_cparams("parallel"),
    )(*args)


def _rope_tables(pos):
    def tab(half, reps):
        inv = jnp.exp(-math.log(ROPE_THETA) * jnp.arange(half, dtype=F32) / half)
        ang = pos[:, None] * inv[None, :]
        cos, sin = jnp.cos(ang), jnp.sin(ang)
        return jnp.tile(jnp.concatenate([cos, cos], 1), (1, reps)), jnp.tile(jnp.concatenate([-sin, sin], 1), (1, reps))

    c128, s128 = tab(HEAD_DIM_B // 2, 1)
    c64, s64 = tab(IDX_DIM // 2, 2)
    return c128, s128, c64, s64


def _pool_kernel(a_ref, prev_ref, w_ref, sc_ref, o_ref, ext_s, *, S, Tc, start, nch):
    ch = pl.program_id(1)
    R = Tc * S
    PS = POOL_BUF * S
    OFF = _round_up(PS, SUBLANES)

    @pl.when(ch == 0)
    def _():
        ext_s[OFF - PS:OFF, :] = prev_ref[0]

    x = a_ref[...]
    ext_s[OFF:OFF + R, :] = x
    t_loc = lax.broadcasted_iota(I32, (R, 1), 0) // S if S > 1 else lax.broadcasted_iota(I32, (R, 1), 0)
    pos1 = start + ch * Tc + t_loc + 1
    for gi, w in enumerate(POOL_WINDOWS):
        sl = slice(gi * POOL_GROUP, (gi + 1) * POOL_GROUP)
        xs = x[:, sl]
        acc = xs
        for j in range(1, w):
            acc = acc + ext_s[OFF - j * S:OFF - j * S + R, sl]
        cnt = jnp.minimum(w, pos1).astype(F32)
        mixed = acc / cnt - xs
        y = _dot(mixed.astype(BF16), w_ref[gi])
        o_ref[:, sl] = (y * sc_ref[:, sl]).astype(o_ref.dtype)
    if nch > 1:
        ext_s[OFF - PS:OFF, :] = ext_s[OFF + R - PS:OFF + R, :]


def _pool(u_main, prev, w_pool, scale, *, S, Tc, start):
    t = u_main.shape[0]
    R = Tc * S
    G = prev.shape[0]
    nch = t // (G * R)
    PS = POOL_BUF * S
    C = BRANCH_W
    return pl.pallas_call(
        functools.partial(_pool_kernel, S=S, Tc=Tc, start=start, nch=nch),
        grid=(G, nch),
        in_specs=[pl.BlockSpec((R, C), lambda g, c: (g * nch + c, 0)),
                  pl.BlockSpec((1, PS, C), lambda g, c: (g, 0, 0)),
                  pl.BlockSpec((len(POOL_WINDOWS), POOL_GROUP, POOL_GROUP), lambda g, c: (0, 0, 0)),
                  pl.BlockSpec((1, C), lambda g, c: (0, 0))],
        out_specs=pl.BlockSpec((R, C), lambda g, c: (g * nch + c, 0)),
        out_shape=jax.ShapeDtypeStruct((t, C), BF16),
        scratch_shapes=[pltpu.VMEM((_round_up(PS, SUBLANES) + R, C), F32)],
        compiler_params=_cparams("parallel", "arbitrary"),
    )(u_main, prev, w_pool, scale)


def _rglru_kernel(cx_ref, cg_ref, prev_ref, h0_ref, wc_ref, bc_ref, wa_ref, ba_ref, wx_ref, bx_ref, lam_ref,
                  o_ref, hl_ref, ext_s, a_s, b_s, h_s, *, S, Tc, nch):
    ch = pl.program_id(1)
    R = Tc * S
    PS = (CONV_W - 1) * S
    OFF = _round_up(PS, SUBLANES)

    @pl.when(ch == 0)
    def _():
        ext_s[OFF - PS:OFF, :] = prev_ref[0]
        h_s[...] = h0_ref[0]

    x = cx_ref[...]
    ext_s[OFF:OFF + R, :] = x
    xc = bc_ref[...] + x * wc_ref[CONV_W - 1:CONV_W, :]
    for j in range(CONV_W - 1):
        k = CONV_W - 1 - j
        xc = xc + ext_s[OFF - k * S:OFF - k * S + R, :] * wc_ref[j:j + 1, :]
    xb = xc.astype(BF16)
    for n in range(LRU_BLOCKS):
        sl = slice(n * LRU_BLOCK, (n + 1) * LRU_BLOCK)
        r = jax.nn.sigmoid(_dot(xb[:, sl], wa_ref[n]) + ba_ref[:, sl])
        i = jax.nn.sigmoid(_dot(xb[:, sl], wx_ref[n]) + bx_ref[:, sl])
        lam = lam_ref[:, sl]
        softplus_neg = jnp.maximum(-lam, 0.0) + jnp.log1p(jnp.exp(-jnp.abs(lam)))
        log_a = (-LRU_C) * r * softplus_neg
        a_s[:, sl] = jnp.exp(log_a)
        th = jnp.tanh(log_a)
        b_s[:, sl] = jnp.sqrt(-2.0 * th / (1.0 - th)) * (i * xc[:, sl])

    if S % SUBLANES == 0:
        def step(t, h):
            off = pl.multiple_of(t * S, S)
            h = a_s[pl.ds(off, S), :] * h + b_s[pl.ds(off, S), :]
            b_s[pl.ds(off, S), :] = h
            return h

        h = lax.fori_loop(0, Tc, step, h_s[...])
    else:
        assert S == 1 and Tc % SUBLANES == 0
        row =lax.broadcasted_iota(I32, (SUBLANES, a_s.shape[1]), 0)

        def tile_step(i, h):
            off = pl.multiple_of(i * SUBLANES, SUBLANES)
            A = a_s[pl.ds(off, SUBLANES), :]
            B = b_s[pl.ds(off, SUBLANES), :]
            for sh in (1, 2, 4):
                keep = row >= sh
                B = A * jnp.where(keep, pltpu.roll(B, sh, axis=0), 0.0) + B
                A = A * jnp.where(keep, pltpu.roll(A, sh, axis=0), 1.0)
            hs = A * h + B
            b_s[pl.ds(off, SUBLANES), :] = hs
            return hs[SUBLANES - 1:SUBLANES, :]

        h = lax.fori_loop(0, Tc // SUBLANES, tile_step, h_s[...])
    h_s[...] = h
    hl_ref[0] = h
    o_ref[...] = (_gelu_tanh(cg_ref[...]) * b_s[...]).astype(o_ref.dtype)
    if nch > 1:
        ext_s[OFF - PS:OFF, :] = ext_s[OFF + R - PS:OFF + R, :]


def _rglru(u_main, prev, h0, lp, *, S, Tc):
    t = u_main.shape[0]
    R = Tc * S
    G = prev.shape[0]
    nch = t // (G * R)
    PS = (CONV_W - 1) * S
    C = LRU_W
    vec = pl.BlockSpec((1, C), lambda g, c: (0, 0))
    blk = pl.BlockSpec((LRU_BLOCKS, LRU_BLOCK, LRU_BLOCK), lambda g, c: (0, 0, 0))
    return pl.pallas_call(
        functools.partial(_rglru_kernel, S=S, Tc=Tc, nch=nch),
        grid=(G, nch),
        in_specs=[pl.BlockSpec((R, C), lambda g, c: (g * nch + c, 3)),
                  pl.BlockSpec((R, C), lambda g, c: (g * nch + c, 4)),
                  pl.BlockSpec((1, PS, C), lambda g, c: (g, 0, 0)),
                  pl.BlockSpec((1, S, C), lambda g, c: (g, 0, 0)),
                  pl.BlockSpec((CONV_W, C), lambda g, c: (0, 0)), vec, blk, vec, blk, vec, vec],
        out_specs=[pl.BlockSpec((R, C), lambda g, c: (g * nch + c, 0)),
                   pl.BlockSpec((1, S, C), lambda g, c: (g, 0, 0))],
        out_shape=[jax.ShapeDtypeStruct((t, C), BF16), jax.ShapeDtypeStruct((G, S, C), F32)],
        scratch_shapes=[pltpu.VMEM((_round_up(PS, SUBLANES) + R, C), F32), pltpu.VMEM((R, C), F32),
                        pltpu.VMEM((R, C), F32), pltpu.VMEM((S, C), F32)],
        compiler_params=_cparams("parallel", "arbitrary"),
    )(u_main, u_main, prev, h0, lp["w_conv_c"], lp["b_conv_c"], lp["w_rg_a"], lp["b_rg_a"], lp["w_rg_x"],
      lp["b_rg_x"], lp["lru_lambda"])


def _ret_kernel(rq_ref, rk_ref, rv_ref, dg_ref, s0_ref, gn_ref, o_ref, sn_ref, s_s, *, c, spb, cps):
    @pl.when(pl.program_id(1) == 0)
    def _():
        s_s[...] = s0_ref[...]

    ii = lax.broadcasted_iota(I32, (c, c), 0)
    jj = lax.broadcasted_iota(I32, (c, c), 1)
    dif = (ii - jj).astype(F32)
    tpos = lax.broadcasted_iota(I32, (c, 1), 0).astype(F32)
    mm = BF16 if c % 16 == 0 else F32
    for h in range(N_HEADS_D):
        lg = LOG_G[h]
        qs = slice(h * QK_DIM_D, (h + 1) * QK_DIM_D)
        vs = slice(h * V_DIM_D, (h + 1) * V_DIM_D)
        intra = jnp.where(dif >= 0.0, jnp.exp(lg * jnp.maximum(dif, 0.0)), 0.0)
        q_dec = jnp.exp(lg * (tpos + 1.0))
        k_dec = jnp.exp(lg * (c - 1.0 - tpos))
        for sq in range(spb):
            s = s_s[sq, h]
            for sub in range(cps):
                rows = slice((sq * cps + sub) * c, (sq * cps + sub + 1) * c)
                q = rq_ref[rows, qs].astype(mm)
                k = rk_ref[rows, qs]
                v = rv_ref[rows, vs].astype(mm)
                att = _dot_nt(q, k.astype(mm)) * intra
                o = _dot(att.astype(mm), v) + _dot(q, s.astype(mm)) * q_dec
                s = s * math.exp(lg * c) + _dot_tn((k * k_dec).astype(mm), v)
                mu = jnp.mean(o, axis=-1, keepdims=True)
                var = jnp.mean(jnp.square(o - mu), axis=-1, keepdims=True)
                y = (o - mu) * lax.rsqrt(var + EPS) * gn_ref[:, vs]
                dg = dg_ref[rows, vs]
                o_ref[rows, vs] = (dg * jax.nn.sigmoid(dg) * y).astype(o_ref.dtype)
            s_s[sq, h] = s
    sn_ref[...] = s_s[...]


def _retention(rq, rk, rv, dg, s0, gn, *, c, v_col, g_col, spb, cps=1, layer=None):
    t = rq.shape[0]
    st_shape = s0.shape if layer is None else s0.shape[1:]
    B = st_shape[0]
    nch = t // (B * c * cps)
    assert spb == 1 or nch == 1
    W = N_HEADS_D * QK_DIM_D
    R = spb * cps * c
    st = pl.BlockSpec((spb, N_HEADS_D, QK_DIM_D, V_DIM_D), lambda b, i: (b, 0, 0, 0))
    st_in = st if layer is None else pl.BlockSpec((None, spb, N_HEADS_D, QK_DIM_D, V_DIM_D),
                                                  lambda b, i: (layer, b, 0, 0, 0))
    return pl.pallas_call(
        functools.partial(_ret_kernel, c=c, spb=spb, cps=cps),
        grid=(B // spb, nch),
        in_specs=[pl.BlockSpec((R, W), lambda b, i: (b * nch + i, 0)),
                  pl.BlockSpec((R, W), lambda b, i: (b * nch + i, 0)),
                  pl.BlockSpec((R, BRANCH_W), lambda b, i: (b * nch + i, v_col)),
                  pl.BlockSpec((R, BRANCH_W), lambda b, i: (b * nch + i, g_col)),
                  st_in, pl.BlockSpec((1, BRANCH_W), lambda b, i: (0, 0))],
        out_specs=[pl.BlockSpec((R, BRANCH_W), lambda b, i: (b * nch + i, 0)), st],
        out_shape=[jax.ShapeDtypeStruct((t, BRANCH_W), BF16), jax.ShapeDtypeStruct(st_shape, F32)],
        scratch_shapes=[pltpu.VMEM((spb, N_HEADS_D, QK_DIM_D, V_DIM_D), F32)],
        compiler_params=_cparams("parallel", "arbitrary"),
    )(rq, rk, rv, dg, s0, gn)


def _sortable(x):
    b = lax.bitcast_convert_type(x, I32)
    return b ^ ((b >> 31) & 0x7FFFFFFF)


def _count(m):
    return jnp.sum(jnp.where(m, 1.0, 0.0), axis=1, keepdims=True)


def _kth_largest_key(keys, kf, bits):
    m = keys.shape[0]
    t0 = jnp.where(_count(keys >= 0) >= kf, 0, INT_MIN).astype(I32)

    def step(nb, shift, t):
        digit = jnp.zeros((m, 1), I32)
        for v in range(1, 2 ** nb):
            cand = t | jnp.left_shift(jnp.int32(v), shift)
            digit = digit + jnp.where(_count(keys >= cand) >= kf, 1, 0)
        return t | jnp.left_shift(digit, shift)

    nfull, rem = divmod(31, bits)
    t = lax.fori_loop(0, nfull, lambda i, t: step(bits, 31 - bits * (i + 1), t), t0)
    return step(rem, 0, t) if rem else t


def _topk_mask(keys, valid, idx, k, bits):
    m, n = keys.shape
    kf = float(k)
    thr = _kth_largest_key(keys, kf, bits)
    gt = keys > thr
    eq = (keys == thr) & valid
    need = kf - _count(gt)
    excess = jnp.max(_count(eq) - need) > 0.0
    nbits = max(1, (n - 1).bit_length())

    def first_ties():
        def idx_step(i, j):
            cand = j | jnp.left_shift(jnp.int32(1), nbits - 1 - i)
            return jnp.where(_count(eq & (idx < cand)) <= need - 1.0, cand, j)

        return lax.fori_loop(0, nbits, idx_step, jnp.zeros((m, 1), I32))

    jthr = lax.cond(excess, first_ties, lambda: jnp.full((m, 1), 2 ** nbits, I32))
    return (gt & valid) | (eq & (idx <= jthr))


Q_DIGITS = (7, 8, 8)
Q_MAX = float(2 ** sum(Q_DIGITS) - 1)


def _count_ge(b, cand):
    m, n = b.shape
    ct = jnp.broadcast_to(cand, (m, LANES)).astype(b.dtype)
    one, zero = jnp.ones((), b.dtype), jnp.zeros((), b.dtype)
    acc = jnp.where(b[:, 0:LANES] >= ct, one, zero)
    for c in range(1, n // LANES):
        acc = acc + jnp.where(b[:, c * LANES:(c + 1) * LANES] >= ct, one, zero)
    return jnp.sum(acc.astype(F32), axis=1, keepdims=True)


def _digit_search(b, kf, nbits):
    def step(i, t):
        cand = t | jnp.left_shift(jnp.int32(1), nbits - 1 - i)
        return jnp.where(_count_ge(b, cand.astype(F32)) >= kf, cand, t)

    return lax.fori_loop(0, nbits, step, jnp.zeros((b.shape[0], 1), I32)).astype(F32)


def _topk_mask_fast(score, valid, idx, k, digit_dtype):
    kf = float(k)
    lo = jnp.min(jnp.where(valid, score, jnp.inf), axis=1, keepdims=True)
    hi = jnp.max(jnp.where(valid, score, -jnp.inf), axis=1, keepdims=True)
    scale = jnp.where(hi > lo, Q_MAX / (hi - lo), 0.0)
    q = jnp.where(valid, jnp.minimum(jnp.floor((score - lo) * scale), Q_MAX), -1.0)
    w1, w0 = float(2 ** Q_DIGITS[2]), float(2 ** (Q_DIGITS[1] + Q_DIGITS[2]))
    d2 = jnp.floor(q * (1.0 / w0))
    r = q - d2 * w0
    d1 = jnp.floor(r * (1.0 / w1))
    d0 = r - d1 * w1
    b2 = d2.astype(digit_dtype)
    t2 = _digit_search(b2, kf, Q_DIGITS[0])
    k1 = kf - _count_ge(b2, t2 + 1.0)
    in2 = d2 == t2
    b1 = jnp.where(in2, d1, -1.0).astype(digit_dtype)
    t1 = _digit_search(b1, k1, Q_DIGITS[1])
    k0 = k1 - _count_ge(b1, t1 + 1.0)
    b0 = jnp.where(in2 & (d1 == t1), d0, -1.0).astype(digit_dtype)
    t0 = _digit_search(b0, k0, Q_DIGITS[2])
    thr = t2 * w0 + t1 * w1 + t0
    gt = q > thr
    eq = q == thr
    ambiguous = jnp.max(_count(eq) - (kf - _count(gt))) > 0.0
    fast = jnp.where(gt | eq, 1, 0).astype(I32)

    def exact():
        keys = _sortable(jnp.where(valid, score, -jnp.inf))
        return jnp.where(_topk_mask(keys, valid, idx, k, 1), 1, 0).astype(I32)

    return lax.cond(ambiguous, exact, lambda: fast) > 0


def _masked_attention(s, mask, v):
    s = jnp.where(mask, s, -jnp.inf)
    p = jnp.exp2(s - jnp.max(s, axis=1, keepdims=True))
    return _dot(p.astype(BF16), v) / jnp.sum(p, axis=1, keepdims=True)


def _dsa_prompt_kernel(q_ref, kb_ref, vb_ref, iq4_ref, ik4_ref, iw_ref, *rest, j0, topk):
    o_ref = rest[-1]
    nq = q_ref.shape[0]
    nk = kb_ref.shape[0]
    t0 = (j0 + pl.program_id(1)) * nq
    ik4 = ik4_ref[...]
    score = jnp.zeros((nq, nk), F32)
    for h in range(N_IDX_HEADS):
        s = _dot_nt(iq4_ref[:, h * IDX4:(h + 1) * IDX4], ik4)
        score = score + jnp.maximum(s, 0.0) * iw_ref[:, IDX_DIM + h:IDX_DIM + h + 1]
    score = score * IDX_SCALE
    kpos = lax.broadcasted_iota(I32, (nq, nk), 1)
    valid = kpos <= t0 + lax.broadcasted_iota(I32, (nq, nk), 0)
    if nk > topk and j0 * nq + 1 >= topk:
        mask = _topk_mask_fast(score, valid, kpos, topk, BF16)
    elif nk > topk:
        mask = _topk_mask(_sortable(jnp.where(valid, score, -jnp.inf)), valid, kpos, topk, 1)
    else:
        mask = valid
    for g in range(N_KV_B):
        gs = slice(g * HEAD_DIM_B, (g + 1) * HEAD_DIM_B)
        kg = kb_ref[:, gs]
        vg = vb_ref[:, gs]
        for r in range(KV_REP):
            hs = slice((g * KV_REP + r) * HEAD_DIM_B, (g * KV_REP + r + 1) * HEAD_DIM_B)
            o_ref[:, hs] = _masked_attention(_dot_nt(q_ref[:, hs], kg), mask, vg).astype(o_ref.dtype)


def _dsa_prompt(q, kb, vb, iq4, ik4, ikw, *, n_seq, n_cls, qb):
    t = q.shape[0]
    s_len = t // n_seq
    nb = s_len // qb
    nbc = nb // n_cls
    topk = min(TOPK_MAX, s_len // 4)
    q, kb, vb, iq4, ik4, ikw = (a.reshape(n_seq, s_len, a.shape[-1]) for a in (q, kb, vb, iq4, ik4, ikw))
    out = None
    for c in range(n_cls):
        nk = (c + 1) * nbc * qb
        qrow = lambda w, c=c: pl.BlockSpec((None, qb, w), lambda b, j: (b, c * nbc + j, 0))
        seq = lambda w, nk=nk: pl.BlockSpec((None, nk, w), lambda b, j: (b, 0, 0))
        prev = [] if out is None else [out]
        out = pl.pallas_call(
            functools.partial(_dsa_prompt_kernel, j0=c * nbc, topk=topk),
            grid=(n_seq, nbc),
            in_specs=[qrow(BRANCH_W), seq(N_KV_B * HEAD_DIM_B), seq(N_KV_B * HEAD_DIM_B), qrow(N_IDX_HEADS * IDX4),
                      seq(IDX4), qrow(LANES)] + [pl.BlockSpec(memory_space=pl.ANY)] * len(prev),
            out_specs=qrow(BRANCH_W),
            out_shape=jax.ShapeDtypeStruct((n_seq, s_len, BRANCH_W), BF16),
            input_output_aliases={6: 0} if prev else {},
            compiler_params=_cparams("parallel", "arbitrary"),
        )(q, kb, vb, iq4, ik4, ikw, *prev)
    return out.reshape(t, BRANCH_W)


def _dsa_sample_kernel(pt_ref, qg_ref, iq4_ref, w_ref, kn_ref, vn_ref, ikn_ref, *rest, n_pages, t_new, spb):
    del pt_ref
    n_in = 3 * n_pages * spb
    page_refs, (o_ref, sc_s, k_s, v_s) = rest[:n_in], rest[n_in:]
    P = PAGE_SIZE
    pad = P - t_new
    nk = (n_pages + 1) * P
    past = n_pages * P

    for s in range(spb):
        idx_refs = page_refs[3 * n_pages * s:3 * n_pages * s + n_pages]
        k_refs = page_refs[3 * n_pages * s + n_pages:3 * n_pages * s + 2 * n_pages]
        v_refs = page_refs[3 * n_pages * s + 2 * n_pages:3 * n_pages * (s + 1)]
        iq4 = iq4_ref[s]
        w = w_ref[s]

        def page_scores(ik, page, keys_on_lanes):
            hi, lo = _split_bf16(ik)
            if keys_on_lanes:
                sc = _dot(iq4, jnp.concatenate([hi, lo, hi, lo], axis=0))
            else:
                sc = _dot_nt(iq4, jnp.concatenate([hi, lo, hi, lo], axis=1))
            r = jnp.maximum(sc, 0.0) * w
            acc = r[0:t_new]
            for h in range(1, N_IDX_HEADS):
                acc = acc + r[h * t_new:(h + 1) * t_new]
            sc_s[s * t_new:(s + 1) * t_new, page * P:(page + 1) * P] = acc * IDX_SCALE

        for p in range(n_pages):
            page_scores(idx_refs[p][0, 0], p, True)
            for g in range(N_KV_B):
                gs = slice(g * HEAD_DIM_B, (g + 1) * HEAD_DIM_B)
                k_s[s, p * P:(p + 1) * P, gs] = k_refs[p][0, 0, pl.ds(g, P, stride=N_KV_B), :].astype(BF16)
                v_s[s, p * P:(p + 1) * P, gs] = v_refs[p][0, 0, pl.ds(g, P, stride=N_KV_B), :].astype(BF16)
        page_scores(jnp.concatenate([ikn_ref[s], jnp.zeros((pad, IDX_DIM), F32)], axis=0), n_pages, False)
        zkv = jnp.zeros((pad, N_KV_B * HEAD_DIM_B), F32)
        k_s[s, past:nk, :] = jnp.concatenate([kn_ref[s], zkv], axis=0).astype(BF16)
        v_s[s, past:nk, :] = jnp.concatenate([vn_ref[s], zkv], axis=0).astype(BF16)

    rows = spb * t_new
    kpos = lax.broadcasted_iota(I32, (rows, nk), 1)
    valid = kpos <= past + (lax.broadcasted_iota(I32, (rows, nk), 0) & (t_new - 1))
    keys = _sortable(jnp.where(valid, sc_s[...], -jnp.inf))
    mask = _topk_mask(keys, valid, kpos, min(TOPK_MAX, (past + t_new) // 4), 3)
    mask_i = jnp.where(mask, 1, 0).astype(I32)
    for s in range(spb):
        m_s = jnp.concatenate([mask_i[s * t_new:(s + 1) * t_new]] * KV_REP, axis=0) > 0
        for g in range(N_KV_B):
            gs = slice(g * HEAD_DIM_B, (g + 1) * HEAD_DIM_B)
            o_ref[s, g] = _masked_attention(_dot_nt(qg_ref[s, g], k_s[s, :, gs]), m_s, v_s[s, :, gs]).astype(o_ref.dtype)


def _dsa_sample(qg, iq4r, wr, k_new, v_new, ik_new, cache_k, cache_v, cache_idx_k, page_table, layer, spb):
    B, _, rows, hd = qg.shape
    t_new = k_new.shape[1]
    assert t_new & (t_new - 1) == 0 and B % spb == 0
    n_pages = page_table.shape[1]
    kvw = N_KV_B * HEAD_DIM_B
    per_seq = lambda shape: pl.BlockSpec((spb,) + shape, lambda b, pt: (b,) + (0,) * len(shape))

    def page_spec(rows_, width, s, p):
        return pl.BlockSpec((1, 1, rows_, width), lambda b, pt: (layer, pt[b * spb + s, p], 0, 0))

    in_specs = [per_seq(qg.shape[1:]), per_seq(iq4r.shape[1:]), per_seq(wr.shape[1:]), per_seq((t_new, kvw)),
                per_seq((t_new, kvw)), per_seq((t_new, IDX_DIM))]
    pages = []
    for s in range(spb):
        in_specs += [page_spec(IDX_DIM, PAGE_SIZE, s, p) for p in range(n_pages)]
        in_specs += [page_spec(PAGE_SIZE * N_KV_B, HEAD_DIM_B, s, p) for p in range(n_pages)]
        in_specs += [page_spec(PAGE_SIZE * N_KV_B, HEAD_DIM_B, s, p) for p in range(n_pages)]
        pages += [cache_idx_k] * n_pages + [cache_k] * n_pages + [cache_v] * n_pages
    nk = (n_pages + 1) * PAGE_SIZE
    grid_spec = pltpu.PrefetchScalarGridSpec(
        num_scalar_prefetch=1, grid=(B // spb,), in_specs=in_specs,
        out_specs=pl.BlockSpec((spb,) + qg.shape[1:], lambda b, pt: (b, 0, 0, 0)),
        scratch_shapes=[pltpu.VMEM((spb * t_new, nk), F32), pltpu.VMEM((spb, nk, kvw), BF16),
                        pltpu.VMEM((spb, nk, kvw), BF16)])
    return pl.pallas_call(
        functools.partial(_dsa_sample_kernel, n_pages=n_pages, t_new=t_new, spb=spb),
        grid_spec=grid_spec,
        out_shape=jax.ShapeDtypeStruct(qg.shape, BF16),
        compiler_params=_cparams("arbitrary"),
    )(page_table, qg, iq4r, wr, k_new, v_new, ik_new, *pages)


def _merge_kernel(xn_ref, a_ref, b_ref, c_ref, d_ref, wg0, wg1, wg2, wg3, bg0, bg1, bg2, bg3, wb_ref, o_ref):
    xn = xn_ref[...]
    acc = None
    for n, (br, wg, bg) in enumerate(zip((a_ref, b_ref, c_ref, d_ref), (wg0, wg1, wg2, wg3), (bg0, bg1, bg2, bg3))):
        term = jax.nn.sigmoid(_dot(xn, wg[...]) + bg[...]) * _dot(br[...], wb_ref[n])
        acc = term if acc is None else acc + term
    o_ref[...] = acc.astype(o_ref.dtype)


def _merge(xn, branches, w_gate, b_gate, w_branch, l, tm, tn):
    t, d = xn.shape
    nj = d // tn
    row = lambda w: pl.BlockSpec((tm, w), lambda i, j: (i, 0))
    wg = [pl.BlockSpec((None, d, tn), functools.partial(lambda i, j, n: (l, 0, n * nj + j), n=n))
          for n in range(N_BRANCH)]
    bg = [pl.BlockSpec((1, tn), functools.partial(lambda i, j, n: (0, n * nj + j), n=n)) for n in range(N_BRANCH)]
    return pl.pallas_call(
        _merge_kernel,
        grid=(t // tm, nj),
        in_specs=[row(d)] + [row(BRANCH_W)] * N_BRANCH + wg + bg
                 + [pl.BlockSpec((None, N_BRANCH, BRANCH_W, tn), lambda i, j: (l, 0, 0, j))],
        out_specs=pl.BlockSpec((tm, tn), lambda i, j: (i, j)),
        out_shape=jax.ShapeDtypeStruct((t, d), BF16),
        compiler_params=_cparams("parallel", "arbitrary"),
    )(xn, *branches, *([w_gate] * N_BRANCH), *([b_gate] * N_BRANCH), w_branch)


def _outproj_kernel(m_ref, x_ref, w_ref, g_ref, o_ref):
    o_ref[...] = x_ref[...] + _rms(_dot(m_ref[...], w_ref[...]), g_ref[...])


def _outproj(merged, x, w_o, g, l, tm):
    t, d = x.shape
    return pl.pallas_call(
        _outproj_kernel,
        grid=(t // tm,),
        in_specs=[pl.BlockSpec((tm, d), lambda i: (i, 0)), pl.BlockSpec((tm, d), lambda i: (i, 0)),
                  pl.BlockSpec((None, d, d), lambda i: (l, 0, 0)), pl.BlockSpec((1, d), lambda i: (0, 0))],
        out_specs=pl.BlockSpec((tm, d), lambda i: (i, 0)),
        out_shape=jax.ShapeDtypeStruct((t, d), F32),
        compiler_params=_cparams("parallel"),
    )(merged, x, w_o, g)


def _ffn_kernel(x_ref, gpre_ref, wug_ref, wuv_ref, wcg_ref, wcv_ref, bcg_ref, bcv_ref, wd_ref, gpost_ref,
                stg_ref, stv_ref, o_ref, ng_ref, nv_ref, h_s, acc_s, cg_s, cv_s, ext_s, *, S, tiles_per_group):
    i = pl.program_id(0)
    j = pl.program_id(1)
    tm = x_ref.shape[0]
    CR = cg_s.shape[1]

    @pl.when(j == 0)
    def _():
        h_s[...] = _rms(x_ref[...], gpre_ref[...]).astype(BF16)
        acc_s[...] = jnp.zeros_like(acc_s)

    @pl.when(i % tiles_per_group == 0)
    def _():
        cg_s[j] = stg_ref[0]
        cv_s[j] = stv_ref[0]

    h = h_s[...]

    def conv(w_ref, wc_ref, bc_ref, c_s, new_ref):
        u = _dot(h, w_ref[...])
        ext_s[0:CR, :] = c_s[j]
        ext_s[CR:CR + tm, :] = u
        y = bc_ref[...] + u * wc_ref[2:3, :]
        y = y + ext_s[CR - S:CR - S + tm, :] * wc_ref[1:2, :]
        y = y + ext_s[CR - 2 * S:CR - 2 * S + tm, :] * wc_ref[0:1, :]
        tail = ext_s[tm:tm + CR, :]
        c_s[j] = tail
        new_ref[0] = tail
        return y

    gate = conv(wug_ref, wcg_ref, bcg_ref, cg_s, ng_ref)
    val = conv(wuv_ref, wcv_ref, bcv_ref, cv_s, nv_ref)
    acc_s[...] += _dot((_gelu_tanh(gate) * val).astype(BF16), wd_ref[...])

    @pl.when(j == pl.num_programs(1) - 1)
    def _():
        o_ref[...] = x_ref[...] + _rms(acc_s[...], gpost_ref[...])


def _ffn(x, lp, state, *, S, tm, tn):
    t, d = x.shape
    G, CR, _ = state.shape
    nj = D_FF // tn
    ni = t // tm
    tiles_per_group = ni // G
    grp = lambda i, j: (i // tiles_per_group, 0, j)
    tile = lambda i, j: (i, 0, j)
    grp_v = lambda i, j: (i // tiles_per_group, 0, nj + j)
    col = lambda r: pl.BlockSpec((r, tn), lambda i, j: (0, j))
    col_v = lambda r: pl.BlockSpec((r, tn), lambda i, j: (0, nj + j))
    vec = pl.BlockSpec((1, d), lambda i, j: (0, 0))
    l = lp["layer"]
    xo, ng, nv = pl.pallas_call(
        functools.partial(_ffn_kernel, S=S, tiles_per_group=tiles_per_group),
        grid=(ni, nj),
        in_specs=[pl.BlockSpec((tm, d), lambda i, j: (i, 0)), vec,
                  pl.BlockSpec((None, d, tn), lambda i, j: (l, 0, j)),
                  pl.BlockSpec((None, d, tn), lambda i, j: (l, 0, nj + j)),
                  col(FFN_CONV_W), col_v(FFN_CONV_W), col(1), col_v(1),
                  pl.BlockSpec((None, tn, d), lambda i, j: (l, j, 0)), vec,
                  pl.BlockSpec((1, CR, tn), grp), pl.BlockSpec((1, CR, tn), grp_v)],
        out_specs=[pl.BlockSpec((tm, d), lambda i, j: (i, 0)),
                   pl.BlockSpec((1, CR, tn), tile), pl.BlockSpec((1, CR, tn), tile)],
        out_shape=[jax.ShapeDtypeStruct((t, d), F32), jax.ShapeDtypeStruct((ni, CR, D_FF), F32),
                   jax.ShapeDtypeStruct((ni, CR, D_FF), F32)],
        scratch_shapes=[pltpu.VMEM((tm, d), BF16), pltpu.VMEM((tm, d), F32), pltpu.VMEM((nj, CR, tn), F32),
                        pltpu.VMEM((nj, CR, tn), F32), pltpu.VMEM((CR + tm, tn), F32)],
        compiler_params=_cparams("arbitrary", "arbitrary"),
    )(x, lp["g_ffn_pre"], lp["w_up"], lp["w_up"], lp["w_ffn_conv"], lp["w_ffn_conv"], lp["b_ffn_conv"],
      lp["b_ffn_conv"], lp["w_down"], lp["g_ffn_post"], state, state)
    last = slice(tiles_per_group - 1, None, tiles_per_group)
    return xo, ng[last], nv[last]


def _layer_front(x, lp, tabs, *, tm, kv_final=None):
    l = lp["layer"]
    u_main, xn = _inproj(x, lp["g_mix_pre"], lp["w_in_main"], l, tm, 1024)
    u_idx = _inproj_split(x, lp["g_mix_pre"], lp["w_in_idx3"], l, 512, N_IDX)
    return u_main, xn, _rope_prep(u_main, u_idx, tabs, 256, kv_final)


def _layer_back(x, xn, branches, lp, ffn_state, *, S, tm, tn_ffn):
    l = lp["layer"]
    merged = _merge(xn, branches, lp["w_gate"], lp["b_gate"], lp["w_branch"], l, tm, 512)
    x1 = _outproj(merged, x, lp["w_o"], lp["g_mix_post"], l, tm)
    return _ffn(x1, lp, ffn_state, S=S, tm=tm, tn=tn_ffn)


def _prep_big_weights(p):
    w_in = jnp.swapaxes(p["w_in"], 1, 2)
    o_iq = 3072
    o_cx = o_iq + N_IDX_HEADS * IDX_DIM + IDX_DIM + N_IDX_HEADS
    w_idx = jnp.pad(w_in[:, o_iq:o_cx], ((0, 0), (0, N_IDX - (o_cx - o_iq)), (0, 0)))
    hi = w_idx.astype(BF16)
    lo = (w_idx - hi.astype(F32)).astype(BF16)
    return dict(
        w_in_main=jnp.concatenate([w_in[:, :o_iq], w_in[:, o_cx:]], axis=1).astype(BF16),
        w_in_idx3=jnp.concatenate([hi, lo, hi], axis=2),
        w_branch=p["w_branch"].astype(BF16), w_gate=p["w_gate"].astype(BF16), w_o=p["w_o"].astype(BF16),
        w_up=p["w_up"].astype(BF16), w_down=p["w_down"].astype(BF16))


def _prep_layer_params(p, big, l):
    row = lambda a: a[l][None, :]
    return dict(
        big, layer=l,
        g_mix_pre=row(p["g_mix_pre"]),
        w_pool=p["w_pool"][l].astype(BF16), pool_scale=row(p["pool_scale"]),
        w_conv_c=p["w_conv_c"][l], b_conv_c=row(p["b_conv_c"]),
        w_rg_a=p["w_rg_a"][l].astype(BF16), b_rg_a=row(p["b_rg_a"]),
        w_rg_x=p["w_rg_x"][l].astype(BF16), b_rg_x=row(p["b_rg_x"]),
        lru_lambda=row(p["lru_lambda"]), ret_gn=row(p["ret_gn"]), b_gate=row(p["b_gate"]),
        g_mix_post=row(p["g_mix_post"]), g_ffn_pre=row(p["g_ffn_pre"]),
        w_ffn_conv=p["w_ffn_conv"][l], b_ffn_conv=row(p["b_ffn_conv"]), g_ffn_post=row(p["g_ffn_post"]))


def _to_time_major(a, nb, nt):
    return jnp.swapaxes(a, 0, 1).reshape((nt * nb,) + a.shape[2:])


def _to_seq_major(a, nb, nt):
    return jnp.swapaxes(a.reshape((nt, nb) + a.shape[1:]), 0, 1)


def kernel(x_prompt, x_sample, cache_k, cache_v, cache_idx_k, state_pool, state_conv, state_rglru, state_ret, state_ffn_conv, page_table, g_mix_pre, w_in, w_pool, pool_scale, w_conv_c, b_conv_c, w_rg_a, b_rg_a, w_rg_x, b_rg_x, lru_lambda, ret_gn, w_branch, w_gate, b_gate, w_o, g_mix_post, g_ffn_pre, w_up, w_ffn_conv, b_ffn_conv, w_down, g_ffn_post):
    params = dict(g_mix_pre=g_mix_pre, w_in=w_in, w_pool=w_pool, pool_scale=pool_scale, w_conv_c=w_conv_c,
                  b_conv_c=b_conv_c, w_rg_a=w_rg_a, b_rg_a=b_rg_a, w_rg_x=w_rg_x, b_rg_x=b_rg_x,
                  lru_lambda=lru_lambda, ret_gn=ret_gn, w_branch=w_branch, w_gate=w_gate, b_gate=b_gate, w_o=w_o,
                  g_mix_post=g_mix_post, g_ffn_pre=g_ffn_pre, w_up=w_up, w_ffn_conv=w_ffn_conv,
                  b_ffn_conv=b_ffn_conv, w_down=w_down, g_ffn_post=g_ffn_post)
    depth = w_in.shape[0]
    bp, seq, d = x_prompt.shape
    bs, tdec, _ = x_sample.shape
    past = page_table.shape[1] * PAGE_SIZE
    n_pool = cache_k.shape[1]
    ck = cache_k.reshape(depth, n_pool, PAGE_SIZE * N_KV_B, HEAD_DIM_B)
    cv = cache_v.reshape(depth, n_pool, PAGE_SIZE * N_KV_B, HEAD_DIM_B)
    cik_t = jnp.swapaxes(cache_idx_k, 2, 3)
    big = _prep_big_weights(params)

    tabs_p = _rope_tables(jnp.arange(seq, dtype=F32))
    tabs_s = _rope_tables(jnp.repeat(past + jnp.arange(tdec, dtype=F32), bs))
    xp = x_prompt.reshape(bp * seq, d)
    xs = _to_time_major(x_sample, bs, tdec)
    cr_p = max(SUBLANES, (FFN_CONV_W - 1))
    outs_p, outs_s = [], []
    kv_p = None
    for l in range(depth):
        lp = _prep_layer_params(params, big, l)
        u, xn, (q, _, kb, vb, iq4, ik4, ikw, rq, rk, *kv_p) = _layer_front(xp, lp, tabs_p, tm=1024,
                                                                         kv_final=(l, depth, kv_p))
        o_a = _pool(u, jnp.zeros((bp, POOL_BUF, BRANCH_W), F32), lp["w_pool"], lp["pool_scale"], S=1, Tc=256, start=0)
        o_b = _dsa_prompt(q, kb, vb, iq4, ik4, ikw, n_seq=bp, n_cls=8, qb=256)
        o_c, h_p = _rglru(u, jnp.zeros((bp, CONV_W - 1, LRU_W), F32), jnp.zeros((bp, 1, LRU_W), F32), lp, S=1, Tc=256)
        o_d, s_p = _retention(rq, rk, u, u, jnp.zeros((bp, N_HEADS_D, QK_DIM_D, V_DIM_D), F32), lp["ret_gn"],
                              c=RET_CHUNK if seq % RET_CHUNK == 0 else seq, v_col=6, g_col=7, spb=1,
                              cps=2 if seq % (2 * RET_CHUNK) == 0 else 1)
        xp, ng, nv = _layer_back(xp, xn, (o_a, o_b, o_c, o_d), lp, jnp.zeros((bp, cr_p, 2 * D_FF), F32),
                                 S=1, tm=512, tn_ffn=512)
        u3 = u.reshape(bp, seq, N_MAIN)
        outs_p.append((
            None, None,
            ikw.reshape(bp, seq, LANES)[:, :, :IDX_DIM], u3[:, seq - POOL_BUF:, 0:BRANCH_W],
            u3[:, seq - (CONV_W - 1):, 3072:3072 + LRU_W], h_p[:, 0], s_p,
            jnp.concatenate([ng, nv], axis=-1)[:, cr_p - (FFN_CONV_W - 1):]))
        u, xn, (q, k, kb, vb, iq4, ik4, ikw, rq, rk) = _layer_front(xs, lp, tabs_s, tm=1024)
        o_a = _pool(u, _to_time_major(state_pool[l], bs, POOL_BUF)[None], lp["w_pool"], lp["pool_scale"],
                    S=bs, Tc=tdec, start=past)
        o_c, h_s = _rglru(u, _to_time_major(state_conv[l], bs, CONV_W - 1)[None], state_rglru[l][None], lp,
                          S=bs, Tc=tdec)
        sm = lambda a: _to_seq_major(a, bs, tdec)
        u_sm = sm(u)
        o_d, s_s = _retention(sm(rq).reshape(bs * tdec, -1), sm(rk).reshape(bs * tdec, -1),
                              u_sm[:, :, 6144:7168].reshape(bs * tdec, -1), u_sm[:, :, 7168:8192].reshape(bs * tdec, -1),
                              state_ret, lp["ret_gn"], c=tdec, v_col=0, g_col=0, spb=8, layer=l)
        o_d = _to_time_major(o_d.reshape(bs, tdec, BRANCH_W), bs, tdec)
        qg = sm(q).reshape(bs, tdec, N_KV_B, KV_REP, HEAD_DIM_B).transpose(0, 2, 3, 1, 4)
        qg = qg.reshape(bs, N_KV_B, KV_REP * tdec, HEAD_DIM_B)
        iq4r = sm(iq4).reshape(bs, tdec, N_IDX_HEADS, IDX4).transpose(0, 2, 1, 3).reshape(bs, N_IDX_HEADS * tdec, IDX4)
        ikw_sm = sm(ikw)
        wr = ikw_sm[:, :, IDX_DIM:IDX_DIM + N_IDX_HEADS].transpose(0, 2, 1).reshape(bs, N_IDX_HEADS * tdec, 1)
        k_sm = sm(k)
        v_sm = u_sm[:, :, 2560:3072]
        o_b = _dsa_sample(qg, iq4r, wr, k_sm, v_sm, ikw_sm[:, :, :IDX_DIM], ck, cv, cik_t, page_table, l, 2)
        o_b = o_b.reshape(bs, N_KV_B, KV_REP, tdec, HEAD_DIM_B).transpose(3, 0, 1, 2, 4).reshape(tdec * bs, BRANCH_W)
        ffn_state = _to_time_major(state_ffn_conv[l], bs, FFN_CONV_W - 1)[None]
        xs, ng, nv = _layer_back(xs, xn, (o_a, o_b, o_c, o_d), lp, ffn_state, S=bs, tm=512, tn_ffn=256)
        ffn_new = _to_seq_major(jnp.concatenate([ng, nv], axis=-1)[0], bs, FFN_CONV_W - 1)
        a_in = u_sm[:, :, 0:BRANCH_W]
        c_x = u_sm[:, :, 3072:3072 + LRU_W]
        outs_s.append((
            k_sm.reshape(bs, tdec, N_KV_B, HEAD_DIM_B), v_sm.reshape(bs, tdec, N_KV_B, HEAD_DIM_B),
            ikw_sm[:, :, :IDX_DIM],
            jnp.concatenate([state_pool[l], a_in], axis=1)[:, tdec:],
            jnp.concatenate([state_conv[l], c_x], axis=1)[:, tdec:],
            h_s[0], s_s, ffn_new))

    stk = lambda outs, i: jnp.stack([o[i] for o in outs], axis=0)
    res = [xp.reshape(bp, seq, d), _to_seq_major(xs, bs, tdec)]
    for i in range(8):
        p_i = kv_p[i].reshape(depth, bp, seq, N_KV_B, HEAD_DIM_B) if i < 2 else stk(outs_p, i)
        res += [p_i, stk(outs_s, i)]
    return tuple(res)
```

```python
import functools
import math

import jax
import jax.numpy as jnp
from jax import lax
from jax.experimental import pallas as pl
from jax.experimental.pallas import tpu as pltpu

F32 = jnp.float32
BF16 = jnp.bfloat16
I32 = jnp.int32

SUBLANES = 8
LANES = 128
VMEM_LIMIT_BYTES = 56 * 1024 * 1024

TM_INPROJ, TN_INPROJ = 1024, 1024
TM_IDX = 512
TM_DENSE = 512
TN_MERGE = 512
TN_FFN = {1: 512, 128: 256}
ROPE_ROWS = 256
SEQ_CHUNK = 256
DSA_QBLOCK, DSA_CLASSES = 256, 8
DSA_SAMPLE_SPB = 2
RET_SAMPLE_SPB = 8

D_MODEL = 2048
BRANCH_W = D_MODEL // 2
N_BRANCH = 4
POOL_WINDOWS = (2, 4, 8, 16)
POOL_GROUP = BRANCH_W // len(POOL_WINDOWS)
POOL_BUF = max(POOL_WINDOWS) - 1
N_HEADS_B = 8
HEAD_DIM_B = BRANCH_W // N_HEADS_B
N_KV_B = 4
KV_REP = N_HEADS_B // N_KV_B
N_IDX_HEADS = 16
IDX_DIM = 64
TOPK_MAX = 256
ROPE_THETA = 10000.0
ATT_SCALE = HEAD_DIM_B ** -0.5
Q_SCALE = ATT_SCALE * math.log2(math.e)
IDX_SCALE = (IDX_DIM * N_IDX_HEADS) ** -0.5
LRU_W = BRANCH_W
LRU_BLOCKS = 4
LRU_BLOCK = LRU_W // LRU_BLOCKS
LRU_C = 8.0
CONV_W = 4
N_HEADS_D = 8
QK_DIM_D = BRANCH_W // (2 * N_HEADS_D)
V_DIM_D = BRANCH_W // N_HEADS_D
RET_CHUNK = 128
D_FF = 11 * D_MODEL // 4
FFN_CONV_W = 3
EPS = 1e-6
PAGE_SIZE = 128

N_MAIN = 8192
N_IDX = 1152
IDX4 = 4 * IDX_DIM

LOG_G = tuple(math.log1p(-(2.0 ** (-5.0 - h))) for h in range(N_HEADS_D))
INT_MIN = -2 ** 31


def _round_up(n, m):
    return (n + m - 1) // m * m


def _cparams(*sem):
    return pltpu.CompilerParams(dimension_semantics=sem, vmem_limit_bytes=VMEM_LIMIT_BYTES)


def _dot(a, b):
    return jnp.dot(a, b, preferred_element_type=F32)


def _dot_nt(a, b):
    return lax.dot_general(a, b, (((1,), (1,)), ((), ())), preferred_element_type=F32)


def _dot_tn(a, b):
    return lax.dot_general(a, b, (((0,), (0,)), ((), ())), preferred_element_type=F32)


def _rms(x, g):
    return x * lax.rsqrt(jnp.mean(x * x, axis=-1, keepdims=True) + EPS) * g


def _gelu_tanh(x):
    return x * (0.5 * (1.0 + jnp.tanh(0.7978845608028654 * (x + 0.044715 * (x * x * x)))))


def _split_bf16(x):
    hi = x.astype(BF16)
    lo = (x - hi.astype(F32)).astype(BF16)
    return hi, lo


def _inproj_kernel(x_ref, g_ref, w_ref, u_ref, xn_ref):
    @pl.when(pl.program_id(1) == 0)
    def _():
        xn_ref[...] = _rms(x_ref[...], g_ref[...]).astype(BF16)

    u_ref[...] = _dot_nt(xn_ref[...], w_ref[...])


def _inproj(x, g, w, l, tm, tn):
    t, d = x.shape
    n = w.shape[1]
    return pl.pallas_call(
        _inproj_kernel,
        grid=(t // tm, n // tn),
        in_specs=[pl.BlockSpec((tm, d), lambda i, j: (i, 0)),
                  pl.BlockSpec((1, d), lambda i, j: (0, 0)),
                  pl.BlockSpec((None, tn, d), lambda i, j: (l, j, 0))],
        out_specs=[pl.BlockSpec((tm, tn), lambda i, j: (i, j)),
                   pl.BlockSpec((tm, d), lambda i, j: (i, 0))],
        out_shape=[jax.ShapeDtypeStruct((t, n), F32), jax.ShapeDtypeStruct((t, d), BF16)],
        compiler_params=_cparams("parallel", "arbitrary"),
    )(x, g, w)


def _inproj_split_kernel(x_ref, g_ref, w3_ref, u_ref, x3_s):
    d = x_ref.shape[1]

    @pl.when(pl.program_id(1) == 0)
    def _():
        hi, lo = _split_bf16(_rms(x_ref[...], g_ref[...]))
        x3_s[:, 0:d] = hi
        x3_s[:, d:2 * d] = hi
        x3_s[:, 2 * d:3 * d] = lo

    u_ref[...] = _dot_nt(x3_s[...], w3_ref[...])


def _inproj_split(x, g, w3, l, tm, tn):
    t, d = x.shape
    n = w3.shape[1]
    return pl.pallas_call(
        _inproj_split_kernel,
        grid=(t // tm, n // tn),
        in_specs=[pl.BlockSpec((tm, d), lambda i, j: (i, 0)),
                  pl.BlockSpec((1, d), lambda i, j: (0, 0)),
                  pl.BlockSpec((None, tn, 3 * d), lambda i, j: (l, j, 0))],
        out_specs=pl.BlockSpec((tm, tn), lambda i, j: (i, j)),
        out_shape=jax.ShapeDtypeStruct((t, n), F32),
        scratch_shapes=[pltpu.VMEM((tm, 3 * d), BF16)],
        compiler_params=_cparams("parallel", "arbitrary"),
    )(x, g, w3)


def _rope128(xs, cos, sin):
    return xs * cos + pltpu.roll(xs, HEAD_DIM_B // 2, axis=1) * sin


def _rope64(xs, cos, sin, first_half):
    rot = jnp.where(first_half, pltpu.roll(xs, LANES - IDX_DIM // 2, axis=1), pltpu.roll(xs, IDX_DIM // 2, axis=1))
    return xs * cos + rot * sin


def _rope_kernel(bq_ref, bk_ref, bv_ref, dq_ref, dk_ref, ui_ref, c128_ref, s128_ref, c64_ref, s64_ref,
                 *rest, n_alias):
    q_ref, k_ref, kb_ref, vb_ref, iq4_ref, ik4_ref, ikw_ref, rq_ref, rk_ref = rest[n_alias:n_alias + 9]
    final = rest[n_alias + 9:]
    rows = bq_ref.shape[0]
    c128, s128, c64, s64 = c128_ref[...], s128_ref[...], c64_ref[...], s64_ref[...]
    lane = lax.broadcasted_iota(I32, (rows, LANES), 1)
    first_half = (lane & (IDX_DIM // 2)) == 0
    low = lane < IDX_DIM
    for c in range(N_HEADS_B):
        sl = slice(c * LANES, (c + 1) * LANES)
        q_ref[:, sl] = (_rope128(bq_ref[:, sl], c128, s128) * Q_SCALE).astype(BF16)
    for c in range(N_KV_B):
        sl = slice(c * LANES, (c + 1) * LANES)
        kr = _rope128(bk_ref[:, sl], c128, s128)
        k_ref[:, sl] = kr
        kb_ref[:, sl] = kr.astype(BF16)
        if final:
            final[0][pl.ds(c, rows, stride=N_KV_B), :] = kr
            final[1][pl.ds(c, rows, stride=N_KV_B), :] = bv_ref[:, sl]
    vb_ref[...] = bv_ref[...].astype(BF16)
    for c in range(N_HEADS_D * QK_DIM_D // LANES):
        sl = slice(c * LANES, (c + 1) * LANES)
        rq_ref[:, sl] = _rope64(dq_ref[:, sl], c64, s64, first_half)
        rk_ref[:, sl] = _rope64(dk_ref[:, sl], c64, s64, first_half) * (QK_DIM_D ** -0.5)
    for c in range(N_IDX_HEADS // 2):
        y = _rope64(ui_ref[:, c * LANES:(c + 1) * LANES], c64, s64, first_half)
        yr = pltpu.roll(y, IDX_DIM, axis=1)
        for hh, dup in enumerate((jnp.where(low, y, yr), jnp.where(low, yr, y))):
            hi, lo = _split_bf16(dup)
            base = (2 * c + hh) * IDX4
            iq4_ref[:, base:base + LANES] = hi
            iq4_ref[:, base + LANES:base + 2 * LANES] = lo
    raw = ui_ref[:, N_IDX_HEADS * IDX_DIM:N_IDX_HEADS * IDX_DIM + LANES]
    y = _rope64(raw, c64, s64, first_half)
    ikw_ref[...] = jnp.where(low, y, raw)
    dup = jnp.where(low, y, pltpu.roll(y, IDX_DIM, axis=1))
    hi = dup.astype(BF16).astype(F32)
    hilo = jnp.where(low, hi, dup - hi).astype(BF16)
    ik4_ref[:, 0:LANES] = hilo
    ik4_ref[:, LANES:2 * LANES] = hilo


def _rope_prep(u_main, u_idx, tabs, tr, kv_final=None):
    t = u_main.shape[0]
    nt = tabs[0].shape[0] // tr
    row = lambda w, c: pl.BlockSpec((tr, w), lambda i: (i, c))
    tab = pl.BlockSpec((tr, LANES), lambda i: (i % nt, 0))
    outs = [(BRANCH_W, BF16), (N_KV_B * HEAD_DIM_B, F32), (N_KV_B * HEAD_DIM_B, BF16), (N_KV_B * HEAD_DIM_B, BF16),
            (N_IDX_HEADS * IDX4, BF16), (IDX4, BF16), (LANES, F32), (N_HEADS_D * QK_DIM_D, F32),
            (N_HEADS_D * QK_DIM_D, F32)]
    in_specs = [row(1024, 1), row(512, 4), row(512, 5), row(512, 10), row(512, 11), row(N_IDX, 0), tab, tab, tab, tab]
    args = [u_main, u_main, u_main, u_main, u_main, u_idx, *tabs]
    out_specs = [row(w, 0) for w, _ in outs]
    out_shape = [jax.ShapeDtypeStruct((t, w), dt) for w, dt in outs]
    aliases = {}
    if kv_final is not None:
        layer, depth, prev = kv_final
        out_specs += [pl.BlockSpec((None, tr * N_KV_B, HEAD_DIM_B), lambda i: (layer, i, 0))] * 2
        out_shape += [jax.ShapeDtypeStruct((depth, t * N_KV_B, HEAD_DIM_B), F32)] * 2
        if prev is not None:
            aliases = {len(args): len(outs), len(args) + 1: len(outs) + 1}
            in_specs += [pl.BlockSpec(memory_space=pl.ANY)] * 2
            args += list(prev)
    return pl.pallas_call(
        functools.partial(_rope_kernel, n_alias=len(aliases)),
        grid=(t // tr,),
        in_specs=in_specs,
        out_specs=out_specs,
        out_shape=out_shape,
        input_output_aliases=aliases,
        compiler_params=_cparams("parallel"),
    )(*args)


def _rope_tables(pos):
    def tab(half, reps):
        inv = jnp.exp(-math.log(ROPE_THETA) * jnp.arange(half, dtype=F32) / half)
        ang = pos[:, None] * inv[None, :]
        cos, sin = jnp.cos(ang), jnp.sin(ang)
        return jnp.tile(jnp.concatenate([cos, cos], 1), (1, reps)), jnp.tile(jnp.concatenate([-sin, sin], 1), (1, reps))

    c128, s128 = tab(HEAD_DIM_B // 2, 1)
    c64, s64 = tab(IDX_DIM // 2, 2)
    return c128, s128, c64, s64


def _pool_kernel(a_ref, prev_ref, w_ref, sc_ref, o_ref, ext_s, *, S, Tc, start, nch):
    ch = pl.program_id(1)
    R = Tc * S
    PS = POOL_BUF * S
    OFF = _round_up(PS, SUBLANES)

    @pl.when(ch == 0)
    def _():
        ext_s[OFF - PS:OFF, :] = prev_ref[0]

    x = a_ref[...]
    ext_s[OFF:OFF + R, :] = x
    t_loc = lax.broadcasted_iota(I32, (R, 1), 0) // S if S > 1 else lax.broadcasted_iota(I32, (R, 1), 0)
    pos1 = start + ch * Tc + t_loc + 1
    for gi, w in enumerate(POOL_WINDOWS):
        sl = slice(gi * POOL_GROUP, (gi + 1) * POOL_GROUP)
        xs = x[:, sl]
        acc = xs
        for j in range(1, w):
            acc = acc + ext_s[OFF - j * S:OFF - j * S + R, sl]
        cnt = jnp.minimum(w, pos1).astype(F32)
        mixed = acc / cnt - xs
        y = _dot(mixed.astype(BF16), w_ref[gi])
        o_ref[:, sl] = (y * sc_ref[:, sl]).astype(o_ref.dtype)
    if nch > 1:
        ext_s[OFF - PS:OFF, :] = ext_s[OFF + R - PS:OFF + R, :]


def _pool(u_main, prev, w_pool, scale, *, S, Tc, start):
    t = u_main.shape[0]
    R = Tc * S
    G = prev.shape[0]
    nch = t // (G * R)
    PS = POOL_BUF * S
    C = BRANCH_W
    return pl.pallas_call(
        functools.partial(_pool_kernel, S=S, Tc=Tc, start=start, nch=nch),
        grid=(G, nch),
        in_specs=[pl.BlockSpec((R, C), lambda g, c: (g * nch + c, 0)),
                  pl.BlockSpec((1, PS, C), lambda g, c: (g, 0, 0)),
                  pl.BlockSpec((len(POOL_WINDOWS), POOL_GROUP, POOL_GROUP), lambda g, c: (0, 0, 0)),
                  pl.BlockSpec((1, C), lambda g, c: (0, 0))],
        out_specs=pl.BlockSpec((R, C), lambda g, c: (g * nch + c, 0)),
        out_shape=jax.ShapeDtypeStruct((t, C), BF16),
        scratch_shapes=[pltpu.VMEM((_round_up(PS, SUBLANES) + R, C), F32)],
        compiler_params=_cparams("parallel", "arbitrary"),
    )(u_main, prev, w_pool, scale)


def _rglru_kernel(cx_ref, cg_ref, prev_ref, h0_ref, wc_ref, bc_ref, wa_ref, ba_ref, wx_ref, bx_ref, lam_ref,
                  o_ref, hl_ref, ext_s, a_s, b_s, h_s, *, S, Tc, nch):
    ch = pl.program_id(1)
    R = Tc * S
    PS = (CONV_W - 1) * S
    OFF = _round_up(PS, SUBLANES)

    @pl.when(ch == 0)
    def _():
        ext_s[OFF - PS:OFF, :] = prev_ref[0]
        h_s[...] = h0_ref[0]

    x = cx_ref[...]
    ext_s[OFF:OFF + R, :] = x
    xc = bc_ref[...] + x * wc_ref[CONV_W - 1:CONV_W, :]
    for j in range(CONV_W - 1):
        k = CONV_W - 1 - j
        xc = xc + ext_s[OFF - k * S:OFF - k * S + R, :] * wc_ref[j:j + 1, :]
    xb = xc.astype(BF16)
    for n in range(LRU_BLOCKS):
        sl = slice(n * LRU_BLOCK, (n + 1) * LRU_BLOCK)
        r = jax.nn.sigmoid(_dot(xb[:, sl], wa_ref[n]) + ba_ref[:, sl])
        i = jax.nn.sigmoid(_dot(xb[:, sl], wx_ref[n]) + bx_ref[:, sl])
        lam = lam_ref[:, sl]
        softplus_neg = jnp.maximum(-lam, 0.0) + jnp.log1p(jnp.exp(-jnp.abs(lam)))
        log_a = (-LRU_C) * r * softplus_neg
        a_s[:, sl] = jnp.exp(log_a)
        th = jnp.tanh(log_a)
        b_s[:, sl] = jnp.sqrt(-2.0 * th / (1.0 - th)) * (i * xc[:, sl])

    if S % SUBLANES == 0:
        def step(t, h):
            off = pl.multiple_of(t * S, S)
            h = a_s[pl.ds(off, S), :] * h + b_s[pl.ds(off, S), :]
            b_s[pl.ds(off, S), :] = h
            return h

        h = lax.fori_loop(0, Tc, step, h_s[...])
    else:
        assert S == 1 and Tc % SUBLANES == 0
        row =lax.broadcasted_iota(I32, (SUBLANES, a_s.shape[1]), 0)

        def tile_step(i, h):
            off = pl.multiple_of(i * SUBLANES, SUBLANES)
            A = a_s[pl.ds(off, SUBLANES), :]
            B = b_s[pl.ds(off, SUBLANES), :]
            for sh in (1, 2, 4):
                keep = row >= sh
                B = A * jnp.where(keep, pltpu.roll(B, sh, axis=0), 0.0) + B
                A = A * jnp.where(keep, pltpu.roll(A, sh, axis=0), 1.0)
            hs = A * h + B
            b_s[pl.ds(off, SUBLANES), :] = hs
            return hs[SUBLANES - 1:SUBLANES, :]

        h = lax.fori_loop(0, Tc // SUBLANES, tile_step, h_s[...])
    h_s[...] = h
    hl_ref[0] = h
    o_ref[...] = (_gelu_tanh(cg_ref[...]) * b_s[...]).astype(o_ref.dtype)
    if nch > 1:
        ext_s[OFF - PS:OFF, :] = ext_s[OFF + R - PS:OFF + R, :]


def _rglru(u_main, prev, h0, lp, *, S, Tc):
    t = u_main.shape[0]
    R = Tc * S
    G = prev.shape[0]
    nch = t // (G * R)
    PS = (CONV_W - 1) * S
    C = LRU_W
    vec = pl.BlockSpec((1, C), lambda g, c: (0, 0))
    blk = pl.BlockSpec((LRU_BLOCKS, LRU_BLOCK, LRU_BLOCK), lambda g, c: (0, 0, 0))
    return pl.pallas_call(
        functools.partial(_rglru_kernel, S=S, Tc=Tc, nch=nch),
        grid=(G, nch),
        in_specs=[pl.BlockSpec((R, C), lambda g, c: (g * nch + c, 3)),
                  pl.BlockSpec((R, C), lambda g, c: (g * nch + c, 4)),
                  pl.BlockSpec((1, PS, C), lambda g, c: (g, 0, 0)),
                  pl.BlockSpec((1, S, C), lambda g, c: (g, 0, 0)),
                  pl.BlockSpec((CONV_W, C), lambda g, c: (0, 0)), vec, blk, vec, blk, vec, vec],
        out_specs=[pl.BlockSpec((R, C), lambda g, c: (g * nch + c, 0)),
                   pl.BlockSpec((1, S, C), lambda g, c: (g, 0, 0))],
        out_shape=[jax.ShapeDtypeStruct((t, C), BF16), jax.ShapeDtypeStruct((G, S, C), F32)],
        scratch_shapes=[pltpu.VMEM((_round_up(PS, SUBLANES) + R, C), F32), pltpu.VMEM((R, C), F32),
                        pltpu.VMEM((R, C), F32), pltpu.VMEM((S, C), F32)],
        compiler_params=_cparams("parallel", "arbitrary"),
    )(u_main, u_main, prev, h0, lp["w_conv_c"], lp["b_conv_c"], lp["w_rg_a"], lp["b_rg_a"], lp["w_rg_x"],
      lp["b_rg_x"], lp["lru_lambda"])


def _ret_kernel(rq_ref, rk_ref, rv_ref, dg_ref, s0_ref, gn_ref, o_ref, sn_ref, s_s, *, c, spb, cps):
    @pl.when(pl.program_id(1) == 0)
    def _():
        s_s[...] = s0_ref[...]

    ii = lax.broadcasted_iota(I32, (c, c), 0)
    jj = lax.broadcasted_iota(I32, (c, c), 1)
    dif = (ii - jj).astype(F32)
    tpos = lax.broadcasted_iota(I32, (c, 1), 0).astype(F32)
    mm = BF16 if c % 16 == 0 else F32
    for h in range(N_HEADS_D):
        lg = LOG_G[h]
        qs = slice(h * QK_DIM_D, (h + 1) * QK_DIM_D)
        vs = slice(h * V_DIM_D, (h + 1) * V_DIM_D)
        intra = jnp.where(dif >= 0.0, jnp.exp(lg * jnp.maximum(dif, 0.0)), 0.0)
        q_dec = jnp.exp(lg * (tpos + 1.0))
        k_dec = jnp.exp(lg * (c - 1.0 - tpos))
        for sq in range(spb):
            s = s_s[sq, h]
            for sub in range(cps):
                rows = slice((sq * cps + sub) * c, (sq * cps + sub + 1) * c)
                q = rq_ref[rows, qs].astype(mm)
                k = rk_ref[rows, qs]
                v = rv_ref[rows, vs].astype(mm)
                att = _dot_nt(q, k.astype(mm)) * intra
                o = _dot(att.astype(mm), v) + _dot(q, s.astype(mm)) * q_dec
                s = s * math.exp(lg * c) + _dot_tn((k * k_dec).astype(mm), v)
                mu = jnp.mean(o, axis=-1, keepdims=True)
                var = jnp.mean(jnp.square(o - mu), axis=-1, keepdims=True)
                y = (o - mu) * lax.rsqrt(var + EPS) * gn_ref[:, vs]
                dg = dg_ref[rows, vs]
                o_ref[rows, vs] = (dg * jax.nn.sigmoid(dg) * y).astype(o_ref.dtype)
            s_s[sq, h] = s
    sn_ref[...] = s_s[...]


def _retention(rq, rk, rv, dg, s0, gn, *, c, v_col, g_col, spb, cps=1, layer=None):
    t = rq.shape[0]
    st_shape = s0.shape if layer is None else s0.shape[1:]
    B = st_shape[0]
    nch = t // (B * c * cps)
    assert spb == 1 or nch == 1
    W = N_HEADS_D * QK_DIM_D
    R = spb * cps * c
    st = pl.BlockSpec((spb, N_HEADS_D, QK_DIM_D, V_DIM_D), lambda b, i: (b, 0, 0, 0))
    st_in = st if layer is None else pl.BlockSpec((None, spb, N_HEADS_D, QK_DIM_D, V_DIM_D),
                                                  lambda b, i: (layer, b, 0, 0, 0))
    return pl.pallas_call(
        functools.partial(_ret_kernel, c=c, spb=spb, cps=cps),
        grid=(B // spb, nch),
        in_specs=[pl.BlockSpec((R, W), lambda b, i: (b * nch + i, 0)),
                  pl.BlockSpec((R, W), lambda b, i: (b * nch + i, 0)),
                  pl.BlockSpec((R, BRANCH_W), lambda b, i: (b * nch + i, v_col)),
                  pl.BlockSpec((R, BRANCH_W), lambda b, i: (b * nch + i, g_col)),
                  st_in, pl.BlockSpec((1, BRANCH_W), lambda b, i: (0, 0))],
        out_specs=[pl.BlockSpec((R, BRANCH_W), lambda b, i: (b * nch + i, 0)), st],
        out_shape=[jax.ShapeDtypeStruct((t, BRANCH_W), BF16), jax.ShapeDtypeStruct(st_shape, F32)],
        scratch_shapes=[pltpu.VMEM((spb, N_HEADS_D, QK_DIM_D, V_DIM_D), F32)],
        compiler_params=_cparams("parallel", "arbitrary"),
    )(rq, rk, rv, dg, s0, gn)


def _sortable(x):
    b = lax.bitcast_convert_type(x, I32)
    return b ^ ((b >> 31) & 0x7FFFFFFF)


def _count(m):
    return jnp.sum(jnp.where(m, 1.0, 0.0), axis=1, keepdims=True)


def _kth_largest_key(keys, kf, bits):
    m = keys.shape[0]
    t0 = jnp.where(_count(keys >= 0) >= kf, 0, INT_MIN).astype(I32)

    def step(nb, shift, t):
        digit = jnp.zeros((m, 1), I32)
        for v in range(1, 2 ** nb):
            cand = t | jnp.left_shift(jnp.int32(v), shift)
            digit = digit + jnp.where(_count(keys >= cand) >= kf, 1, 0)
        return t | jnp.left_shift(digit, shift)

    nfull, rem = divmod(31, bits)
    t = lax.fori_loop(0, nfull, lambda i, t: step(bits, 31 - bits * (i + 1), t), t0)
    return step(rem, 0, t) if rem else t


def _topk_mask(keys, valid, idx, k, bits):
    m, n = keys.shape
    kf = float(k)
    thr = _kth_largest_key(keys, kf, bits)
    gt = keys > thr
    eq = (keys == thr) & valid
    need = kf - _count(gt)
    excess = jnp.max(_count(eq) - need) > 0.0
    nbits = max(1, (n - 1).bit_length())

    def first_ties():
        def idx_step(i, j):
            cand = j | jnp.left_shift(jnp.int32(1), nbits - 1 - i)
            return jnp.where(_count(eq & (idx < cand)) <= need - 1.0, cand, j)

        return lax.fori_loop(0, nbits, idx_step, jnp.zeros((m, 1), I32))

    jthr = lax.cond(excess, first_ties, lambda: jnp.full((m, 1), 2 ** nbits, I32))
    return (gt & valid) | (eq & (idx <= jthr))


Q_DIGITS = (7, 8, 8)
Q_MAX = float(2 ** sum(Q_DIGITS) - 1)


def _count_ge(b, cand):
    m, n = b.shape
    ct = jnp.broadcast_to(cand, (m, LANES)).astype(b.dtype)
    one, zero = jnp.ones((), b.dtype), jnp.zeros((), b.dtype)
    acc = jnp.where(b[:, 0:LANES] >= ct, one, zero)
    for c in range(1, n // LANES):
        acc = acc + jnp.where(b[:, c * LANES:(c + 1) * LANES] >= ct, one, zero)
    return jnp.sum(acc.astype(F32), axis=1, keepdims=True)


def _digit_search(b, kf, nbits):
    def step(i, t):
        cand = t | jnp.left_shift(jnp.int32(1), nbits - 1 - i)
        return jnp.where(_count_ge(b, cand.astype(F32)) >= kf, cand, t)

    return lax.fori_loop(0, nbits, step, jnp.zeros((b.shape[0], 1), I32)).astype(F32)


def _topk_mask_fast(score, valid, idx, k, digit_dtype):
    kf = float(k)
    lo = jnp.min(jnp.where(valid, score, jnp.inf), axis=1, keepdims=True)
    hi = jnp.max(jnp.where(valid, score, -jnp.inf), axis=1, keepdims=True)
    scale = jnp.where(hi > lo, Q_MAX / (hi - lo), 0.0)
    q = jnp.where(valid, jnp.minimum(jnp.floor((score - lo) * scale), Q_MAX), -1.0)
    w1, w0 = float(2 ** Q_DIGITS[2]), float(2 ** (Q_DIGITS[1] + Q_DIGITS[2]))
    d2 = jnp.floor(q * (1.0 / w0))
    r = q - d2 * w0
    d1 = jnp.floor(r * (1.0 / w1))
    d0 = r - d1 * w1
    b2 = d2.astype(digit_dtype)
    t2 = _digit_search(b2, kf, Q_DIGITS[0])
    k1 = kf - _count_ge(b2, t2 + 1.0)
    in2 = d2 == t2
    b1 = jnp.where(in2, d1, -1.0).astype(digit_dtype)
    t1 = _digit_search(b1, k1, Q_DIGITS[1])
    k0 = k1 - _count_ge(b1, t1 + 1.0)
    b0 = jnp.where(in2 & (d1 == t1), d0, -1.0).astype(digit_dtype)
    t0 = _digit_search(b0, k0, Q_DIGITS[2])
    thr = t2 * w0 + t1 * w1 + t0
    gt = q > thr
    eq = q == thr
    ambiguous = jnp.max(_count(eq) - (kf - _count(gt))) > 0.0
    fast = jnp.where(gt | eq, 1, 0).astype(I32)

    def exact():
        keys = _sortable(jnp.where(valid, score, -jnp.inf))
        return jnp.where(_topk_mask(keys, valid, idx, k, 1), 1, 0).astype(I32)

    return lax.cond(ambiguous, exact, lambda: fast) > 0


def _masked_attention(s, mask, v):
    s = jnp.where(mask, s, -jnp.inf)
    p = jnp.exp2(s - jnp.max(s, axis=1, keepdims=True))
    return _dot(p.astype(BF16), v) / jnp.sum(p, axis=1, keepdims=True)


def _dsa_prompt_kernel(q_ref, kb_ref, vb_ref, iq4_ref, ik4_ref, iw_ref, *rest, j0, topk):
    o_ref = rest[-1]
    nq = q_ref.shape[0]
    nk = kb_ref.shape[0]
    t0 = (j0 + pl.program_id(1)) * nq
    ik4 = ik4_ref[...]
    score = jnp.zeros((nq, nk), F32)
    for h in range(N_IDX_HEADS):
        s = _dot_nt(iq4_ref[:, h * IDX4:(h + 1) * IDX4], ik4)
        score = score + jnp.maximum(s, 0.0) * iw_ref[:, IDX_DIM + h:IDX_DIM + h + 1]
    score = score * IDX_SCALE
    kpos = lax.broadcasted_iota(I32, (nq, nk), 1)
    valid = kpos <= t0 + lax.broadcasted_iota(I32, (nq, nk), 0)
    if nk > topk and j0 * nq + 1 >= topk:
        mask = _topk_mask_fast(score, valid, kpos, topk, BF16)
    elif nk > topk:
        mask = _topk_mask(_sortable(jnp.where(valid, score, -jnp.inf)), valid, kpos, topk, 1)
    else:
        mask = valid
    for g in range(N_KV_B):
        gs = slice(g * HEAD_DIM_B, (g + 1) * HEAD_DIM_B)
        kg = kb_ref[:, gs]
        vg = vb_ref[:, gs]
        for r in range(KV_REP):
            hs = slice((g * KV_REP + r) * HEAD_DIM_B, (g * KV_REP + r + 1) * HEAD_DIM_B)
            o_ref[:, hs] = _masked_attention(_dot_nt(q_ref[:, hs], kg), mask, vg).astype(o_ref.dtype)


def _dsa_prompt(q, kb, vb, iq4, ik4, ikw, *, n_seq, n_cls, qb):
    t = q.shape[0]
    s_len = t // n_seq
    nb = s_len // qb
    nbc = nb // n_cls
    topk = min(TOPK_MAX, s_len // 4)
    q, kb, vb, iq4, ik4, ikw = (a.reshape(n_seq, s_len, a.shape[-1]) for a in (q, kb, vb, iq4, ik4, ikw))
    out = None
    for c in range(n_cls):
        nk = (c + 1) * nbc * qb
        qrow = lambda w, c=c: pl.BlockSpec((None, qb, w), lambda b, j: (b, c * nbc + j, 0))
        seq = lambda w, nk=nk: pl.BlockSpec((None, nk, w), lambda b, j: (b, 0, 0))
        prev = [] if out is None else [out]
        out = pl.pallas_call(
            functools.partial(_dsa_prompt_kernel, j0=c * nbc, topk=topk),
            grid=(n_seq, nbc),
            in_specs=[qrow(BRANCH_W), seq(N_KV_B * HEAD_DIM_B), seq(N_KV_B * HEAD_DIM_B), qrow(N_IDX_HEADS * IDX4),
                      seq(IDX4), qrow(LANES)] + [pl.BlockSpec(memory_space=pl.ANY)] * len(prev),
            out_specs=qrow(BRANCH_W),
            out_shape=jax.ShapeDtypeStruct((n_seq, s_len, BRANCH_W), BF16),
            input_output_aliases={6: 0} if prev else {},
            compiler_params=_cparams("parallel", "arbitrary"),
        )(q, kb, vb, iq4, ik4, ikw, *prev)
    return out.reshape(t, BRANCH_W)


def _dsa_sample_kernel(pt_ref, qg_ref, iq4_ref, w_ref, kn_ref, vn_ref, ikn_ref, *rest, n_pages, t_new, spb):
    del pt_ref
    n_in = 3 * n_pages * spb
    page_refs, (o_ref, sc_s, k_s, v_s) = rest[:n_in], rest[n_in:]
    P = PAGE_SIZE
    pad = P - t_new
    nk = (n_pages + 1) * P
    past = n_pages * P

    for s in range(spb):
        idx_refs = page_refs[3 * n_pages * s:3 * n_pages * s + n_pages]
        k_refs = page_refs[3 * n_pages * s + n_pages:3 * n_pages * s + 2 * n_pages]
        v_refs = page_refs[3 * n_pages * s + 2 * n_pages:3 * n_pages * (s + 1)]
        iq4 = iq4_ref[s]
        w = w_ref[s]

        def page_scores(ik, page, keys_on_lanes):
            hi, lo = _split_bf16(ik)
            if keys_on_lanes:
                sc = _dot(iq4, jnp.concatenate([hi, lo, hi, lo], axis=0))
            else:
                sc = _dot_nt(iq4, jnp.concatenate([hi, lo, hi, lo], axis=1))
            r = jnp.maximum(sc, 0.0) * w
            acc = r[0:t_new]
            for h in range(1, N_IDX_HEADS):
                acc = acc + r[h * t_new:(h + 1) * t_new]
            sc_s[s * t_new:(s + 1) * t_new, page * P:(page + 1) * P] = acc * IDX_SCALE

        for p in range(n_pages):
            page_scores(idx_refs[p][0, 0], p, True)
            for g in range(N_KV_B):
                gs = slice(g * HEAD_DIM_B, (g + 1) * HEAD_DIM_B)
                k_s[s, p * P:(p + 1) * P, gs] = k_refs[p][0, 0, pl.ds(g, P, stride=N_KV_B), :].astype(BF16)
                v_s[s, p * P:(p + 1) * P, gs] = v_refs[p][0, 0, pl.ds(g, P, stride=N_KV_B), :].astype(BF16)
        page_scores(jnp.concatenate([ikn_ref[s], jnp.zeros((pad, IDX_DIM), F32)], axis=0), n_pages, False)
        zkv = jnp.zeros((pad, N_KV_B * HEAD_DIM_B), F32)
        k_s[s, past:nk, :] = jnp.concatenate([kn_ref[s], zkv], axis=0).astype(BF16)
        v_s[s, past:nk, :] = jnp.concatenate([vn_ref[s], zkv], axis=0).astype(BF16)

    rows = spb * t_new
    kpos = lax.broadcasted_iota(I32, (rows, nk), 1)
    valid = kpos <= past + (lax.broadcasted_iota(I32, (rows, nk), 0) & (t_new - 1))
    keys = _sortable(jnp.where(valid, sc_s[...], -jnp.inf))
    mask = _topk_mask(keys, valid, kpos, min(TOPK_MAX, (past + t_new) // 4), 3)
    mask_i = jnp.where(mask, 1, 0).astype(I32)
    for s in range(spb):
        m_s = jnp.concatenate([mask_i[s * t_new:(s + 1) * t_new]] * KV_REP, axis=0) > 0
        for g in range(N_KV_B):
            gs = slice(g * HEAD_DIM_B, (g + 1) * HEAD_DIM_B)
            o_ref[s, g] = _masked_attention(_dot_nt(qg_ref[s, g], k_s[s, :, gs]), m_s, v_s[s, :, gs]).astype(o_ref.dtype)


def _dsa_sample(qg, iq4r, wr, k_new, v_new, ik_new, cache_k, cache_v, cache_idx_k, page_table, layer, spb):
    B, _, rows, hd = qg.shape
    t_new = k_new.shape[1]
    assert t_new & (t_new - 1) == 0 and B % spb == 0
    n_pages = page_table.shape[1]
    kvw = N_KV_B * HEAD_DIM_B
    per_seq = lambda shape: pl.BlockSpec((spb,) + shape, lambda b, pt: (b,) + (0,) * len(shape))

    def page_spec(rows_, width, s, p):
        return pl.BlockSpec((1, 1, rows_, width), lambda b, pt: (layer, pt[b * spb + s, p], 0, 0))

    in_specs = [per_seq(qg.shape[1:]), per_seq(iq4r.shape[1:]), per_seq(wr.shape[1:]), per_seq((t_new, kvw)),
                per_seq((t_new, kvw)), per_seq((t_new, IDX_DIM))]
    pages = []
    for s in range(spb):
        in_specs += [page_spec(IDX_DIM, PAGE_SIZE, s, p) for p in range(n_pages)]
        in_specs += [page_spec(PAGE_SIZE * N_KV_B, HEAD_DIM_B, s, p) for p in range(n_pages)]
        in_specs += [page_spec(PAGE_SIZE * N_KV_B, HEAD_DIM_B, s, p) for p in range(n_pages)]
        pages += [cache_idx_k] * n_pages + [cache_k] * n_pages + [cache_v] * n_pages
    nk = (n_pages + 1) * PAGE_SIZE
    grid_spec = pltpu.PrefetchScalarGridSpec(
        num_scalar_prefetch=1, grid=(B // spb,), in_specs=in_specs,
        out_specs=pl.BlockSpec((spb,) + qg.shape[1:], lambda b, pt: (b, 0, 0, 0)),
        scratch_shapes=[pltpu.VMEM((spb * t_new, nk), F32), pltpu.VMEM((spb, nk, kvw), BF16),
                        pltpu.VMEM((spb, nk, kvw), BF16)])
    return pl.pallas_call(
        functools.partial(_dsa_sample_kernel, n_pages=n_pages, t_new=t_new, spb=spb),
        grid_spec=grid_spec,
        out_shape=jax.ShapeDtypeStruct(qg.shape, BF16),
        compiler_params=_cparams("arbitrary"),
    )(page_table, qg, iq4r, wr, k_new, v_new, ik_new, *pages)


def _merge_kernel(xn_ref, a_ref, b_ref, c_ref, d_ref, wg0, wg1, wg2, wg3, bg0, bg1, bg2, bg3, wb_ref, o_ref):
    xn = xn_ref[...]
    acc = None
    for n, (br, wg, bg) in enumerate(zip((a_ref, b_ref, c_ref, d_ref), (wg0, wg1, wg2, wg3), (bg0, bg1, bg2, bg3))):
        term = jax.nn.sigmoid(_dot(xn, wg[...]) + bg[...]) * _dot(br[...], wb_ref[n])
        acc = term if acc is None else acc + term
    o_ref[...] = acc.astype(o_ref.dtype)


def _merge(xn, branches, w_gate, b_gate, w_branch, l, tm, tn):
    t, d = xn.shape
    nj = d // tn
    row = lambda w: pl.BlockSpec((tm, w), lambda i, j: (i, 0))
    wg = [pl.BlockSpec((None, d, tn), functools.partial(lambda i, j, n: (l, 0, n * nj + j), n=n))
          for n in range(N_BRANCH)]
    bg = [pl.BlockSpec((1, tn), functools.partial(lambda i, j, n: (0, n * nj + j), n=n)) for n in range(N_BRANCH)]
    return pl.pallas_call(
        _merge_kernel,
        grid=(t // tm, nj),
        in_specs=[row(d)] + [row(BRANCH_W)] * N_BRANCH + wg + bg
                 + [pl.BlockSpec((None, N_BRANCH, BRANCH_W, tn), lambda i, j: (l, 0, 0, j))],
        out_specs=pl.BlockSpec((tm, tn), lambda i, j: (i, j)),
        out_shape=jax.ShapeDtypeStruct((t, d), BF16),
        compiler_params=_cparams("parallel", "arbitrary"),
    )(xn, *branches, *([w_gate] * N_BRANCH), *([b_gate] * N_BRANCH), w_branch)


def _outproj_kernel(m_ref, x_ref, w_ref, g_ref, o_ref):
    o_ref[...] = x_ref[...] + _rms(_dot(m_ref[...], w_ref[...]), g_ref[...])


def _outproj(merged, x, w_o, g, l, tm):
    t, d = x.shape
    return pl.pallas_call(
        _outproj_kernel,
        grid=(t // tm,),
        in_specs=[pl.BlockSpec((tm, d), lambda i: (i, 0)), pl.BlockSpec((tm, d), lambda i: (i, 0)),
                  pl.BlockSpec((None, d, d), lambda i: (l, 0, 0)), pl.BlockSpec((1, d), lambda i: (0, 0))],
        out_specs=pl.BlockSpec((tm, d), lambda i: (i, 0)),
        out_shape=jax.ShapeDtypeStruct((t, d), F32),
        compiler_params=_cparams("parallel"),
    )(merged, x, w_o, g)


def _ffn_kernel(x_ref, gpre_ref, wug_ref, wuv_ref, wcg_ref, wcv_ref, bcg_ref, bcv_ref, wd_ref, gpost_ref,
                stg_ref, stv_ref, o_ref, ng_ref, nv_ref, h_s, acc_s, cg_s, cv_s, ext_s, *, S, tiles_per_group):
    i = pl.program_id(0)
    j = pl.program_id(1)
    tm = x_ref.shape[0]
    CR = cg_s.shape[1]

    @pl.when(j == 0)
    def _():
        h_s[...] = _rms(x_ref[...], gpre_ref[...]).astype(BF16)
        acc_s[...] = jnp.zeros_like(acc_s)

    @pl.when(i % tiles_per_group == 0)
    def _():
        cg_s[j] = stg_ref[0]
        cv_s[j] = stv_ref[0]

    h = h_s[...]

    def conv(w_ref, wc_ref, bc_ref, c_s, new_ref):
        u = _dot(h, w_ref[...])
        ext_s[0:CR, :] = c_s[j]
        ext_s[CR:CR + tm, :] = u
        y = bc_ref[...] + u * wc_ref[2:3, :]
        y = y + ext_s[CR - S:CR - S + tm, :] * wc_ref[1:2, :]
        y = y + ext_s[CR - 2 * S:CR - 2 * S + tm, :] * wc_ref[0:1, :]
        tail = ext_s[tm:tm + CR, :]
        c_s[j] = tail
        new_ref[0] = tail
        return y

    gate = conv(wug_ref, wcg_ref, bcg_ref, cg_s, ng_ref)
    val = conv(wuv_ref, wcv_ref, bcv_ref, cv_s, nv_ref)
    acc_s[...] += _dot((_gelu_tanh(gate) * val).astype(BF16), wd_ref[...])

    @pl.when(j == pl.num_programs(1) - 1)
    def _():
        o_ref[...] = x_ref[...] + _rms(acc_s[...], gpost_ref[...])


def _ffn(x, lp, state, *, S, tm, tn):
    t, d = x.shape
    G, CR, _ = state.shape
    nj = D_FF // tn
    ni = t // tm
    tiles_per_group = ni // G
    grp = lambda i, j: (i // tiles_per_group, 0, j)
    tile = lambda i, j: (i, 0, j)
    grp_v = lambda i, j: (i // tiles_per_group, 0, nj + j)
    col = lambda r: pl.BlockSpec((r, tn), lambda i, j: (0, j))
    col_v = lambda r: pl.BlockSpec((r, tn), lambda i, j: (0, nj + j))
    vec = pl.BlockSpec((1, d), lambda i, j: (0, 0))
    l = lp["layer"]
    xo, ng, nv = pl.pallas_call(
        functools.partial(_ffn_kernel, S=S, tiles_per_group=tiles_per_group),
        grid=(ni, nj),
        in_specs=[pl.BlockSpec((tm, d), lambda i, j: (i, 0)), vec,
                  pl.BlockSpec((None, d, tn), lambda i, j: (l, 0, j)),
                  pl.BlockSpec((None, d, tn), lambda i, j: (l, 0, nj + j)),
                  col(FFN_CONV_W), col_v(FFN_CONV_W), col(1), col_v(1),
                  pl.BlockSpec((None, tn, d), lambda i, j: (l, j, 0)), vec,
                  pl.BlockSpec((1, CR, tn), grp), pl.BlockSpec((1, CR, tn), grp_v)],
        out_specs=[pl.BlockSpec((tm, d), lambda i, j: (i, 0)),
                   pl.BlockSpec((1, CR, tn), tile), pl.BlockSpec((1, CR, tn), tile)],
        out_shape=[jax.ShapeDtypeStruct((t, d), F32), jax.ShapeDtypeStruct((ni, CR, D_FF), F32),
                   jax.ShapeDtypeStruct((ni, CR, D_FF), F32)],
        scratch_shapes=[pltpu.VMEM((tm, d), BF16), pltpu.VMEM((tm, d), F32), pltpu.VMEM((nj, CR, tn), F32),
                        pltpu.VMEM((nj, CR, tn), F32), pltpu.VMEM((CR + tm, tn), F32)],
        compiler_params=_cparams("arbitrary", "arbitrary"),
    )(x, lp["g_ffn_pre"], lp["w_up"], lp["w_up"], lp["w_ffn_conv"], lp["w_ffn_conv"], lp["b_ffn_conv"],
      lp["b_ffn_conv"], lp["w_down"], lp["g_ffn_post"], state, state)
    last = slice(tiles_per_group - 1, None, tiles_per_group)
    return xo, ng[last], nv[last]


def _layer_front(x, lp, tabs, kv_final=None):
    l = lp["layer"]
    u_main, xn = _inproj(x, lp["g_mix_pre"], lp["w_in_main"], l, TM_INPROJ, TN_INPROJ)
    u_idx = _inproj_split(x, lp["g_mix_pre"], lp["w_in_idx3"], l, TM_IDX, N_IDX)
    return u_main, xn, _rope_prep(u_main, u_idx, tabs, ROPE_ROWS, kv_final)


def _layer_back(x, xn, branches, lp, ffn_state, *, S):
    l = lp["layer"]
    merged = _merge(xn, branches, lp["w_gate"], lp["b_gate"], lp["w_branch"], l, TM_DENSE, TN_MERGE)
    x1 = _outproj(merged, x, lp["w_o"], lp["g_mix_post"], l, TM_DENSE)
    return _ffn(x1, lp, ffn_state, S=S, tm=TM_DENSE, tn=TN_FFN[S])


def _prep_big_weights(p):
    w_in = jnp.swapaxes(p["w_in"], 1, 2)
    o_iq = 3072
    o_cx = o_iq + N_IDX_HEADS * IDX_DIM + IDX_DIM + N_IDX_HEADS
    w_idx = jnp.pad(w_in[:, o_iq:o_cx], ((0, 0), (0, N_IDX - (o_cx - o_iq)), (0, 0)))
    hi = w_idx.astype(BF16)
    lo = (w_idx - hi.astype(F32)).astype(BF16)
    return dict(
        w_in_main=jnp.concatenate([w_in[:, :o_iq], w_in[:, o_cx:]], axis=1).astype(BF16),
        w_in_idx3=jnp.concatenate([hi, lo, hi], axis=2),
        w_branch=p["w_branch"].astype(BF16), w_gate=p["w_gate"].astype(BF16), w_o=p["w_o"].astype(BF16),
        w_up=p["w_up"].astype(BF16), w_down=p["w_down"].astype(BF16))


def _prep_layer_params(p, big, l):
    row = lambda a: a[l][None, :]
    return dict(
        big, layer=l,
        g_mix_pre=row(p["g_mix_pre"]),
        w_pool=p["w_pool"][l].astype(BF16), pool_scale=row(p["pool_scale"]),
        w_conv_c=p["w_conv_c"][l], b_conv_c=row(p["b_conv_c"]),
        w_rg_a=p["w_rg_a"][l].astype(BF16), b_rg_a=row(p["b_rg_a"]),
        w_rg_x=p["w_rg_x"][l].astype(BF16), b_rg_x=row(p["b_rg_x"]),
        lru_lambda=row(p["lru_lambda"]), ret_gn=row(p["ret_gn"]), b_gate=row(p["b_gate"]),
        g_mix_post=row(p["g_mix_post"]), g_ffn_pre=row(p["g_ffn_pre"]),
        w_ffn_conv=p["w_ffn_conv"][l], b_ffn_conv=row(p["b_ffn_conv"]), g_ffn_post=row(p["g_ffn_post"]))


def _to_time_major(a, nb, nt):
    return jnp.swapaxes(a, 0, 1).reshape((nt * nb,) + a.shape[2:])


def _to_seq_major(a, nb, nt):
    return jnp.swapaxes(a.reshape((nt, nb) + a.shape[1:]), 0, 1)


def kernel(x_prompt, x_sample, cache_k, cache_v, cache_idx_k, state_pool, state_conv, state_rglru, state_ret, state_ffn_conv, page_table, g_mix_pre, w_in, w_pool, pool_scale, w_conv_c, b_conv_c, w_rg_a, b_rg_a, w_rg_x, b_rg_x, lru_lambda, ret_gn, w_branch, w_gate, b_gate, w_o, g_mix_post, g_ffn_pre, w_up, w_ffn_conv, b_ffn_conv, w_down, g_ffn_post):
    params = dict(g_mix_pre=g_mix_pre, w_in=w_in, w_pool=w_pool, pool_scale=pool_scale, w_conv_c=w_conv_c,
                  b_conv_c=b_conv_c, w_rg_a=w_rg_a, b_rg_a=b_rg_a, w_rg_x=w_rg_x, b_rg_x=b_rg_x,
                  lru_lambda=lru_lambda, ret_gn=ret_gn, w_branch=w_branch, w_gate=w_gate, b_gate=b_gate, w_o=w_o,
                  g_mix_post=g_mix_post, g_ffn_pre=g_ffn_pre, w_up=w_up, w_ffn_conv=w_ffn_conv,
                  b_ffn_conv=b_ffn_conv, w_down=w_down, g_ffn_post=g_ffn_post)
    depth = w_in.shape[0]
    bp, seq, d = x_prompt.shape
    bs, tdec, _ = x_sample.shape
    past = page_table.shape[1] * PAGE_SIZE
    n_pool = cache_k.shape[1]
    ck = cache_k.reshape(depth, n_pool, PAGE_SIZE * N_KV_B, HEAD_DIM_B)
    cv = cache_v.reshape(depth, n_pool, PAGE_SIZE * N_KV_B, HEAD_DIM_B)
    cik_t = jnp.swapaxes(cache_idx_k, 2, 3)
    big = _prep_big_weights(params)

    tabs_p = _rope_tables(jnp.arange(seq, dtype=F32))
    tabs_s = _rope_tables(jnp.repeat(past + jnp.arange(tdec, dtype=F32), bs))
    xp = x_prompt.reshape(bp * seq, d)
    xs = _to_time_major(x_sample, bs, tdec)
    cr_p = max(SUBLANES, (FFN_CONV_W - 1))
    outs_p, outs_s = [], []
    kv_p = None
    for l in range(depth):
        lp = _prep_layer_params(params, big, l)
        u, xn, (q, _, kb, vb, iq4, ik4, ikw, rq, rk, *kv_p) = _layer_front(xp, lp, tabs_p, (l, depth, kv_p))
        o_a = _pool(u, jnp.zeros((bp, POOL_BUF, BRANCH_W), F32), lp["w_pool"], lp["pool_scale"], S=1, Tc=SEQ_CHUNK,
                    start=0)
        o_b = _dsa_prompt(q, kb, vb, iq4, ik4, ikw, n_seq=bp, n_cls=DSA_CLASSES, qb=DSA_QBLOCK)
        o_c, h_p = _rglru(u, jnp.zeros((bp, CONV_W - 1, LRU_W), F32), jnp.zeros((bp, 1, LRU_W), F32), lp, S=1,
                          Tc=SEQ_CHUNK)
        o_d, s_p = _retention(rq, rk, u, u, jnp.zeros((bp, N_HEADS_D, QK_DIM_D, V_DIM_D), F32), lp["ret_gn"],
                              c=RET_CHUNK if seq % RET_CHUNK == 0 else seq, v_col=6, g_col=7, spb=1,
                              cps=2 if seq % (2 * RET_CHUNK) == 0 else 1)
        xp, ng, nv = _layer_back(xp, xn, (o_a, o_b, o_c, o_d), lp, jnp.zeros((bp, cr_p, 2 * D_FF), F32), S=1)
        u3 = u.reshape(bp, seq, N_MAIN)
        outs_p.append((
            None, None,
            ikw.reshape(bp, seq, LANES)[:, :, :IDX_DIM], u3[:, seq - POOL_BUF:, 0:BRANCH_W],
            u3[:, seq - (CONV_W - 1):, 3072:3072 + LRU_W], h_p[:, 0], s_p,
            jnp.concatenate([ng, nv], axis=-1)[:, cr_p - (FFN_CONV_W - 1):]))
        u, xn, (q, k, kb, vb, iq4, ik4, ikw, rq, rk) = _layer_front(xs, lp, tabs_s)
        o_a = _pool(u, _to_time_major(state_pool[l], bs, POOL_BUF)[None], lp["w_pool"], lp["pool_scale"],
                    S=bs, Tc=tdec, start=past)
        o_c, h_s = _rglru(u, _to_time_major(state_conv[l], bs, CONV_W - 1)[None], state_rglru[l][None], lp,
                          S=bs, Tc=tdec)
        sm = lambda a: _to_seq_major(a, bs, tdec)
        u_sm = sm(u)
        o_d, s_s = _retention(sm(rq).reshape(bs * tdec, -1), sm(rk).reshape(bs * tdec, -1),
                              u_sm[:, :, 6144:7168].reshape(bs * tdec, -1), u_sm[:, :, 7168:8192].reshape(bs * tdec, -1),
                              state_ret, lp["ret_gn"], c=tdec, v_col=0, g_col=0, spb=RET_SAMPLE_SPB, layer=l)
        o_d = _to_time_major(o_d.reshape(bs, tdec, BRANCH_W), bs, tdec)
        qg = sm(q).reshape(bs, tdec, N_KV_B, KV_REP, HEAD_DIM_B).transpose(0, 2, 3, 1, 4)
        qg = qg.reshape(bs, N_KV_B, KV_REP * tdec, HEAD_DIM_B)
        iq4r = sm(iq4).reshape(bs, tdec, N_IDX_HEADS, IDX4).transpose(0, 2, 1, 3).reshape(bs, N_IDX_HEADS * tdec, IDX4)
        ikw_sm = sm(ikw)
        wr = ikw_sm[:, :, IDX_DIM:IDX_DIM + N_IDX_HEADS].transpose(0, 2, 1).reshape(bs, N_IDX_HEADS * tdec, 1)
        k_sm = sm(k)
        v_sm = u_sm[:, :, 2560:3072]
        o_b = _dsa_sample(qg, iq4r, wr, k_sm, v_sm, ikw_sm[:, :, :IDX_DIM], ck, cv, cik_t, page_table, l,
                          DSA_SAMPLE_SPB)
        o_b = o_b.reshape(bs, N_KV_B, KV_REP, tdec, HEAD_DIM_B).transpose(3, 0, 1, 2, 4).reshape(tdec * bs, BRANCH_W)
        ffn_state = _to_time_major(state_ffn_conv[l], bs, FFN_CONV_W - 1)[None]
        xs, ng, nv = _layer_back(xs, xn, (o_a, o_b, o_c, o_d), lp, ffn_state, S=bs)
        ffn_new = _to_seq_major(jnp.concatenate([ng, nv], axis=-1)[0], bs, FFN_CONV_W - 1)
        a_in = u_sm[:, :, 0:BRANCH_W]
        c_x = u_sm[:, :, 3072:3072 + LRU_W]
        outs_s.append((
            k_sm.reshape(bs, tdec, N_KV_B, HEAD_DIM_B), v_sm.reshape(bs, tdec, N_KV_B, HEAD_DIM_B),
            ikw_sm[:, :, :IDX_DIM],
            jnp.concatenate([state_pool[l], a_in], axis=1)[:, tdec:],
            jnp.concatenate([state_conv[l], c_x], axis=1)[:, tdec:],
            h_s[0], s_s, ffn_new))

    stk = lambda outs, i: jnp.stack([o[i] for o in outs], axis=0)
    res = [xp.reshape(bp, seq, d), _to_seq_major(xs, bs, tdec)]
    for i in range(8):
        p_i = kv_p[i].reshape(depth, bp, seq, N_KV_B, HEAD_DIM_B) if i < 2 else stk(outs_p, i)
        res += [p_i, stk(outs_s, i)]
    return tuple(res)
```

```python
import functools
import math

import jax
import jax.numpy as jnp
from jax import lax
from jax.experimental import pallas as pl
from jax.experimental.pallas import tpu as pltpu

F32 = jnp.float32
BF16 = jnp.bfloat16
I32 = jnp.int32

SUBLANES = 8
LANES = 128
VMEM_LIMIT_BYTES = 56 * 1024 * 1024

TM_INPROJ, TN_INPROJ = 1024, 1024
TM_IDX = 512
TM_DENSE = 512
TN_MERGE = 512
TN_FFN = {1: 512, 128: 256}
ROPE_ROWS = 256
SEQ_CHUNK = 512
DSA_QBLOCK, DSA_CLASSES = 256, 8
DSA_SAMPLE_SPB = 2
RET_SAMPLE_SPB = 8

D_MODEL = 2048
BRANCH_W = D_MODEL // 2
N_BRANCH = 4
POOL_WINDOWS = (2, 4, 8, 16)
POOL_GROUP = BRANCH_W // len(POOL_WINDOWS)
POOL_BUF = max(POOL_WINDOWS) - 1
N_HEADS_B = 8
HEAD_DIM_B = BRANCH_W // N_HEADS_B
N_KV_B = 4
KV_REP = N_HEADS_B // N_KV_B
N_IDX_HEADS = 16
IDX_DIM = 64
TOPK_MAX = 256
ROPE_THETA = 10000.0
ATT_SCALE = HEAD_DIM_B ** -0.5
Q_SCALE = ATT_SCALE * math.log2(math.e)
IDX_SCALE = (IDX_DIM * N_IDX_HEADS) ** -0.5
LRU_W = BRANCH_W
LRU_BLOCKS = 4
LRU_BLOCK = LRU_W // LRU_BLOCKS
LRU_C = 8.0
CONV_W = 4
N_HEADS_D = 8
QK_DIM_D = BRANCH_W // (2 * N_HEADS_D)
V_DIM_D = BRANCH_W // N_HEADS_D
RET_CHUNK = 128
D_FF = 11 * D_MODEL // 4
FFN_CONV_W = 3
EPS = 1e-6
PAGE_SIZE = 128

N_MAIN = 8192
N_IDX = 1152
IDX4 = 4 * IDX_DIM

LOG_G = tuple(math.log1p(-(2.0 ** (-5.0 - h))) for h in range(N_HEADS_D))
INT_MIN = -2 ** 31


def _round_up(n, m):
    return (n + m - 1) // m * m


def _cparams(*sem):
    return pltpu.CompilerParams(dimension_semantics=sem, vmem_limit_bytes=VMEM_LIMIT_BYTES)


def _dot(a, b):
    return jnp.dot(a, b, preferred_element_type=F32)


def _dot_nt(a, b):
    return lax.dot_general(a, b, (((1,), (1,)), ((), ())), preferred_element_type=F32)


def _dot_tn(a, b):
    return lax.dot_general(a, b, (((0,), (0,)), ((), ())), preferred_element_type=F32)


def _rms(x, g):
    return x * lax.rsqrt(jnp.mean(x * x, axis=-1, keepdims=True) + EPS) * g


def _gelu_tanh(x):
    return x * (0.5 * (1.0 + jnp.tanh(0.7978845608028654 * (x + 0.044715 * (x * x * x)))))


def _split_bf16(x):
    hi = x.astype(BF16)
    lo = (x - hi.astype(F32)).astype(BF16)
    return hi, lo


def _inproj_kernel(x_ref, g_ref, w_ref, u_ref, xn_ref):
    @pl.when(pl.program_id(1) == 0)
    def _():
        xn_ref[...] = _rms(x_ref[...], g_ref[...]).astype(BF16)

    u_ref[...] = _dot_nt(xn_ref[...], w_ref[...])


def _inproj(x, g, w, l, tm, tn):
    t, d = x.shape
    n = w.shape[1]
    return pl.pallas_call(
        _inproj_kernel,
        grid=(t // tm, n // tn),
        in_specs=[pl.BlockSpec((tm, d), lambda i, j: (i, 0)),
                  pl.BlockSpec((1, d), lambda i, j: (0, 0)),
                  pl.BlockSpec((None, tn, d), lambda i, j: (l, j, 0))],
        out_specs=[pl.BlockSpec((tm, tn), lambda i, j: (i, j)),
                   pl.BlockSpec((tm, d), lambda i, j: (i, 0))],
        out_shape=[jax.ShapeDtypeStruct((t, n), F32), jax.ShapeDtypeStruct((t, d), BF16)],
        compiler_params=_cparams("parallel", "arbitrary"),
    )(x, g, w)


def _inproj_split_kernel(x_ref, g_ref, w3_ref, u_ref, x3_s):
    d = x_ref.shape[1]

    @pl.when(pl.program_id(1) == 0)
    def _():
        hi, lo = _split_bf16(_rms(x_ref[...], g_ref[...]))
        x3_s[:, 0:d] = hi
        x3_s[:, d:2 * d] = hi
        x3_s[:, 2 * d:3 * d] = lo

    u_ref[...] = _dot_nt(x3_s[...], w3_ref[...])


def _inproj_split(x, g, w3, l, tm, tn):
    t, d = x.shape
    n = w3.shape[1]
    return pl.pallas_call(
        _inproj_split_kernel,
        grid=(t // tm, n // tn),
        in_specs=[pl.BlockSpec((tm, d), lambda i, j: (i, 0)),
                  pl.BlockSpec((1, d), lambda i, j: (0, 0)),
                  pl.BlockSpec((None, tn, 3 * d), lambda i, j: (l, j, 0))],
        out_specs=pl.BlockSpec((tm, tn), lambda i, j: (i, j)),
        out_shape=jax.ShapeDtypeStruct((t, n), F32),
        scratch_shapes=[pltpu.VMEM((tm, 3 * d), BF16)],
        compiler_params=_cparams("parallel", "arbitrary"),
    )(x, g, w3)


def _rope128(xs, cos, sin):
    return xs * cos + pltpu.roll(xs, HEAD_DIM_B // 2, axis=1) * sin


def _rope64(xs, cos, sin, first_half):
    rot = jnp.where(first_half, pltpu.roll(xs, LANES - IDX_DIM // 2, axis=1), pltpu.roll(xs, IDX_DIM // 2, axis=1))
    return xs * cos + rot * sin


def _rope_kernel(bq_ref, bk_ref, bv_ref, dq_ref, dk_ref, ui_ref, c128_ref, s128_ref, c64_ref, s64_ref,
                 *rest, n_alias):
    q_ref, k_ref, kb_ref, vb_ref, iq4_ref, ik4_ref, ikw_ref, rq_ref, rk_ref = rest[n_alias:n_alias + 9]
    final = rest[n_alias + 9:]
    rows = bq_ref.shape[0]
    c128, s128, c64, s64 = c128_ref[...], s128_ref[...], c64_ref[...], s64_ref[...]
    lane = lax.broadcasted_iota(I32, (rows, LANES), 1)
    first_half = (lane & (IDX_DIM // 2)) == 0
    low = lane < IDX_DIM
    for c in range(N_HEADS_B):
        sl = slice(c * LANES, (c + 1) * LANES)
        q_ref[:, sl] = (_rope128(bq_ref[:, sl], c128, s128) * Q_SCALE).astype(BF16)
    for c in range(N_KV_B):
        sl = slice(c * LANES, (c + 1) * LANES)
        kr = _rope128(bk_ref[:, sl], c128, s128)
        k_ref[:, sl] = kr
        kb_ref[:, sl] = kr.astype(BF16)
        if final:
            final[0][pl.ds(c, rows, stride=N_KV_B), :] = kr
            final[1][pl.ds(c, rows, stride=N_KV_B), :] = bv_ref[:, sl]
    vb_ref[...] = bv_ref[...].astype(BF16)
    for c in range(N_HEADS_D * QK_DIM_D // LANES):
        sl = slice(c * LANES, (c + 1) * LANES)
        rq_ref[:, sl] = _rope64(dq_ref[:, sl], c64, s64, first_half)
        rk_ref[:, sl] = _rope64(dk_ref[:, sl], c64, s64, first_half) * (QK_DIM_D ** -0.5)
    for c in range(N_IDX_HEADS // 2):
        y = _rope64(ui_ref[:, c * LANES:(c + 1) * LANES], c64, s64, first_half)
        yr = pltpu.roll(y, IDX_DIM, axis=1)
        for hh, dup in enumerate((jnp.where(low, y, yr), jnp.where(low, yr, y))):
            hi, lo = _split_bf16(dup)
            base = (2 * c + hh) * IDX4
            iq4_ref[:, base:base + LANES] = hi
            iq4_ref[:, base + LANES:base + 2 * LANES] = lo
    raw = ui_ref[:, N_IDX_HEADS * IDX_DIM:N_IDX_HEADS * IDX_DIM + LANES]
    y = _rope64(raw, c64, s64, first_half)
    ikw_ref[...] = jnp.where(low, y, raw)
    dup = jnp.where(low, y, pltpu.roll(y, IDX_DIM, axis=1))
    hi = dup.astype(BF16).astype(F32)
    hilo = jnp.where(low, hi, dup - hi).astype(BF16)
    ik4_ref[:, 0:LANES] = hilo
    ik4_ref[:, LANES:2 * LANES] = hilo


def _rope_prep(u_main, u_idx, tabs, tr, kv_final=None):
    t = u_main.shape[0]
    nt = tabs[0].shape[0] // tr
    row = lambda w, c: pl.BlockSpec((tr, w), lambda i: (i, c))
    tab = pl.BlockSpec((tr, LANES), lambda i: (i % nt, 0))
    outs = [(BRANCH_W, BF16), (N_KV_B * HEAD_DIM_B, F32), (N_KV_B * HEAD_DIM_B, BF16), (N_KV_B * HEAD_DIM_B, BF16),
            (N_IDX_HEADS * IDX4, BF16), (IDX4, BF16), (LANES, F32), (N_HEADS_D * QK_DIM_D, F32),
            (N_HEADS_D * QK_DIM_D, F32)]
    in_specs = [row(1024, 1), row(512, 4), row(512, 5), row(512, 10), row(512, 11), row(N_IDX, 0), tab, tab, tab, tab]
    args = [u_main, u_main, u_main, u_main, u_main, u_idx, *tabs]
    out_specs = [row(w, 0) for w, _ in outs]
    out_shape = [jax.ShapeDtypeStruct((t, w), dt) for w, dt in outs]
    aliases = {}
    if kv_final is not None:
        layer, depth, prev = kv_final
        out_specs += [pl.BlockSpec((None, tr * N_KV_B, HEAD_DIM_B), lambda i: (layer, i, 0))] * 2
        out_shape += [jax.ShapeDtypeStruct((depth, t * N_KV_B, HEAD_DIM_B), F32)] * 2
        if prev is not None:
            aliases = {len(args): len(outs), len(args) + 1: len(outs) + 1}
            in_specs += [pl.BlockSpec(memory_space=pl.ANY)] * 2
            args += list(prev)
    return pl.pallas_call(
        functools.partial(_rope_kernel, n_alias=len(aliases)),
        grid=(t // tr,),
        in_specs=in_specs,
        out_specs=out_specs,
        out_shape=out_shape,
        input_output_aliases=aliases,
        compiler_params=_cparams("parallel"),
    )(*args)


def _rope_tables(pos):
    def tab(half, reps):
        inv = jnp.exp(-math.log(ROPE_THETA) * jnp.arange(half, dtype=F32) / half)
        ang = pos[:, None] * inv[None, :]
        cos, sin = jnp.cos(ang), jnp.sin(ang)
        return jnp.tile(jnp.concatenate([cos, cos], 1), (1, reps)), jnp.tile(jnp.concatenate([-sin, sin], 1), (1, reps))

    c128, s128 = tab(HEAD_DIM_B // 2, 1)
    c64, s64 = tab(IDX_DIM // 2, 2)
    return c128, s128, c64, s64


def _pool_kernel(a_ref, prev_ref, w_ref, sc_ref, o_ref, ext_s, *, S, Tc, start, nch):
    ch = pl.program_id(1)
    R = Tc * S
    PS = POOL_BUF * S
    OFF = _round_up(PS, SUBLANES)

    @pl.when(ch == 0)
    def _():
        ext_s[OFF - PS:OFF, :] = prev_ref[0]

    x = a_ref[...]
    ext_s[OFF:OFF + R, :] = x
    t_loc = lax.broadcasted_iota(I32, (R, 1), 0) // S if S > 1 else lax.broadcasted_iota(I32, (R, 1), 0)
    pos1 = start + ch * Tc + t_loc + 1
    for gi, w in enumerate(POOL_WINDOWS):
        sl = slice(gi * POOL_GROUP, (gi + 1) * POOL_GROUP)
        xs = x[:, sl]
        acc = xs
        for j in range(1, w):
            acc = acc + ext_s[OFF - j * S:OFF - j * S + R, sl]
        cnt = jnp.minimum(w, pos1).astype(F32)
        mixed = acc / cnt - xs
        y = _dot(mixed.astype(BF16), w_ref[gi])
        o_ref[:, sl] = (y * sc_ref[:, sl]).astype(o_ref.dtype)
    if nch > 1:
        ext_s[OFF - PS:OFF, :] = ext_s[OFF + R - PS:OFF + R, :]


def _pool(u_main, prev, w_pool, scale, *, S, Tc, start):
    t = u_main.shape[0]
    R = Tc * S
    G = prev.shape[0]
    nch = t // (G * R)
    PS = POOL_BUF * S
    C = BRANCH_W
    return pl.pallas_call(
        functools.partial(_pool_kernel, S=S, Tc=Tc, start=start, nch=nch),
        grid=(G, nch),
        in_specs=[pl.BlockSpec((R, C), lambda g, c: (g * nch + c, 0)),
                  pl.BlockSpec((1, PS, C), lambda g, c: (g, 0, 0)),
                  pl.BlockSpec((len(POOL_WINDOWS), POOL_GROUP, POOL_GROUP), lambda g, c: (0, 0, 0)),
                  pl.BlockSpec((1, C), lambda g, c: (0, 0))],
        out_specs=pl.BlockSpec((R, C), lambda g, c: (g * nch + c, 0)),
        out_shape=jax.ShapeDtypeStruct((t, C), BF16),
        scratch_shapes=[pltpu.VMEM((_round_up(PS, SUBLANES) + R, C), F32)],
        compiler_params=_cparams("parallel", "arbitrary"),
    )(u_main, prev, w_pool, scale)


def _rglru_kernel(cx_ref, cg_ref, prev_ref, h0_ref, wc_ref, bc_ref, wa_ref, ba_ref, wx_ref, bx_ref, lam_ref,
                  o_ref, hl_ref, ext_s, a_s, b_s, h_s, *, S, Tc, nch):
    ch = pl.program_id(1)
    R = Tc * S
    PS = (CONV_W - 1) * S
    OFF = _round_up(PS, SUBLANES)

    @pl.when(ch == 0)
    def _():
        ext_s[OFF - PS:OFF, :] = prev_ref[0]
        h_s[...] = h0_ref[0]

    x = cx_ref[...]
    ext_s[OFF:OFF + R, :] = x
    xc = bc_ref[...] + x * wc_ref[CONV_W - 1:CONV_W, :]
    for j in range(CONV_W - 1):
        k = CONV_W - 1 - j
        xc = xc + ext_s[OFF - k * S:OFF - k * S + R, :] * wc_ref[j:j + 1, :]
    xb = xc.astype(BF16)
    for n in range(LRU_BLOCKS):
        sl = slice(n * LRU_BLOCK, (n + 1) * LRU_BLOCK)
        r = jax.nn.sigmoid(_dot(xb[:, sl], wa_ref[n]) + ba_ref[:, sl])
        i = jax.nn.sigmoid(_dot(xb[:, sl], wx_ref[n]) + bx_ref[:, sl])
        lam = lam_ref[:, sl]
        softplus_neg = jnp.maximum(-lam, 0.0) + jnp.log1p(jnp.exp(-jnp.abs(lam)))
        log_a = (-LRU_C) * r * softplus_neg
        a_s[:, sl] = jnp.exp(log_a)
        th = jnp.tanh(log_a)
        b_s[:, sl] = jnp.sqrt(-2.0 * th / (1.0 - th)) * (i * xc[:, sl])

    if S % SUBLANES == 0:
        def step(t, h):
            off = pl.multiple_of(t * S, S)
            h = a_s[pl.ds(off, S), :] * h + b_s[pl.ds(off, S), :]
            b_s[pl.ds(off, S), :] = h
            return h

        h = lax.fori_loop(0, Tc, step, h_s[...])
    else:
        assert S == 1 and Tc % SUBLANES == 0
        row =lax.broadcasted_iota(I32, (SUBLANES, a_s.shape[1]), 0)

        def tile_step(i, h):
            off = pl.multiple_of(i * SUBLANES, SUBLANES)
            A = a_s[pl.ds(off, SUBLANES), :]
            B = b_s[pl.ds(off, SUBLANES), :]
            for sh in (1, 2, 4):
                keep = row >= sh
                B = A * jnp.where(keep, pltpu.roll(B, sh, axis=0), 0.0) + B
                A = A * jnp.where(keep, pltpu.roll(A, sh, axis=0), 1.0)
            hs = A * h + B
            b_s[pl.ds(off, SUBLANES), :] = hs
            return hs[SUBLANES - 1:SUBLANES, :]

        h = lax.fori_loop(0, Tc // SUBLANES, tile_step, h_s[...])
    h_s[...] = h
    hl_ref[0] = h
    o_ref[...] = (_gelu_tanh(cg_ref[...]) * b_s[...]).astype(o_ref.dtype)
    if nch > 1:
        ext_s[OFF - PS:OFF, :] = ext_s[OFF + R - PS:OFF + R, :]


def _rglru(u_main, prev, h0, lp, *, S, Tc):
    t = u_main.shape[0]
    R = Tc * S
    G = prev.shape[0]
    nch = t // (G * R)
    PS = (CONV_W - 1) * S
    C = LRU_W
    vec = pl.BlockSpec((1, C), lambda g, c: (0, 0))
    blk = pl.BlockSpec((LRU_BLOCKS, LRU_BLOCK, LRU_BLOCK), lambda g, c: (0, 0, 0))
    return pl.pallas_call(
        functools.partial(_rglru_kernel, S=S, Tc=Tc, nch=nch),
        grid=(G, nch),
        in_specs=[pl.BlockSpec((R, C), lambda g, c: (g * nch + c, 3)),
                  pl.BlockSpec((R, C), lambda g, c: (g * nch + c, 4)),
                  pl.BlockSpec((1, PS, C), lambda g, c: (g, 0, 0)),
                  pl.BlockSpec((1, S, C), lambda g, c: (g, 0, 0)),
                  pl.BlockSpec((CONV_W, C), lambda g, c: (0, 0)), vec, blk, vec, blk, vec, vec],
        out_specs=[pl.BlockSpec((R, C), lambda g, c: (g * nch + c, 0)),
                   pl.BlockSpec((1, S, C), lambda g, c: (g, 0, 0))],
        out_shape=[jax.ShapeDtypeStruct((t, C), BF16), jax.ShapeDtypeStruct((G, S, C), F32)],
        scratch_shapes=[pltpu.VMEM((_round_up(PS, SUBLANES) + R, C), F32), pltpu.VMEM((R, C), F32),
                        pltpu.VMEM((R, C), F32), pltpu.VMEM((S, C), F32)],
        compiler_params=_cparams("parallel", "arbitrary"),
    )(u_main, u_main, prev, h0, lp["w_conv_c"], lp["b_conv_c"], lp["w_rg_a"], lp["b_rg_a"], lp["w_rg_x"],
      lp["b_rg_x"], lp["lru_lambda"])


def _ret_kernel(rq_ref, rk_ref, rv_ref, dg_ref, s0_ref, gn_ref, o_ref, sn_ref, s_s, *, c, spb, cps):
    @pl.when(pl.program_id(1) == 0)
    def _():
        s_s[...] = s0_ref[...]

    ii = lax.broadcasted_iota(I32, (c, c), 0)
    jj = lax.broadcasted_iota(I32, (c, c), 1)
    dif = (ii - jj).astype(F32)
    tpos = lax.broadcasted_iota(I32, (c, 1), 0).astype(F32)
    mm = BF16 if c % 16 == 0 else F32
    for h in range(N_HEADS_D):
        lg = LOG_G[h]
        qs = slice(h * QK_DIM_D, (h + 1) * QK_DIM_D)
        vs = slice(h * V_DIM_D, (h + 1) * V_DIM_D)
        intra = jnp.where(dif >= 0.0, jnp.exp(lg * jnp.maximum(dif, 0.0)), 0.0)
        q_dec = jnp.exp(lg * (tpos + 1.0))
        k_dec = jnp.exp(lg * (c - 1.0 - tpos))
        for sq in range(spb):
            s = s_s[sq, h]
            for sub in range(cps):
                rows = slice((sq * cps + sub) * c, (sq * cps + sub + 1) * c)
                q = rq_ref[rows, qs].astype(mm)
                k = rk_ref[rows, qs]
                v = rv_ref[rows, vs].astype(mm)
                att = _dot_nt(q, k.astype(mm)) * intra
                o = _dot(att.astype(mm), v) + _dot(q, s.astype(mm)) * q_dec
                s = s * math.exp(lg * c) + _dot_tn((k * k_dec).astype(mm), v)
                mu = jnp.mean(o, axis=-1, keepdims=True)
                var = jnp.mean(jnp.square(o - mu), axis=-1, keepdims=True)
                y = (o - mu) * lax.rsqrt(var + EPS) * gn_ref[:, vs]
                dg = dg_ref[rows, vs]
                o_ref[rows, vs] = (dg * jax.nn.sigmoid(dg) * y).astype(o_ref.dtype)
            s_s[sq, h] = s
    sn_ref[...] = s_s[...]


def _retention(rq, rk, rv, dg, s0, gn, *, c, v_col, g_col, spb, cps=1, layer=None):
    t = rq.shape[0]
    st_shape = s0.shape if layer is None else s0.shape[1:]
    B = st_shape[0]
    nch = t // (B * c * cps)
    assert spb == 1 or nch == 1
    W = N_HEADS_D * QK_DIM_D
    R = spb * cps * c
    st = pl.BlockSpec((spb, N_HEADS_D, QK_DIM_D, V_DIM_D), lambda b, i: (b, 0, 0, 0))
    st_in = st if layer is None else pl.BlockSpec((None, spb, N_HEADS_D, QK_DIM_D, V_DIM_D),
                                                  lambda b, i: (layer, b, 0, 0, 0))
    return pl.pallas_call(
        functools.partial(_ret_kernel, c=c, spb=spb, cps=cps),
        grid=(B // spb, nch),
        in_specs=[pl.BlockSpec((R, W), lambda b, i: (b * nch + i, 0)),
                  pl.BlockSpec((R, W), lambda b, i: (b * nch + i, 0)),
                  pl.BlockSpec((R, BRANCH_W), lambda b, i: (b * nch + i, v_col)),
                  pl.BlockSpec((R, BRANCH_W), lambda b, i: (b * nch + i, g_col)),
                  st_in, pl.BlockSpec((1, BRANCH_W), lambda b, i: (0, 0))],
        out_specs=[pl.BlockSpec((R, BRANCH_W), lambda b, i: (b * nch + i, 0)), st],
        out_shape=[jax.ShapeDtypeStruct((t, BRANCH_W), BF16), jax.ShapeDtypeStruct(st_shape, F32)],
        scratch_shapes=[pltpu.VMEM((spb, N_HEADS_D, QK_DIM_D, V_DIM_D), F32)],
        compiler_params=_cparams("parallel", "arbitrary"),
    )(rq, rk, rv, dg, s0, gn)


def _sortable(x):
    b = lax.bitcast_convert_type(x, I32)
    return b ^ ((b >> 31) & 0x7FFFFFFF)


def _count(m):
    return jnp.sum(jnp.where(m, 1.0, 0.0), axis=1, keepdims=True)


def _kth_largest_key(keys, kf, bits):
    m = keys.shape[0]
    t0 = jnp.where(_count(keys >= 0) >= kf, 0, INT_MIN).astype(I32)

    def step(nb, shift, t):
        digit = jnp.zeros((m, 1), I32)
        for v in range(1, 2 ** nb):
            cand = t | jnp.left_shift(jnp.int32(v), shift)
            digit = digit + jnp.where(_count(keys >= cand) >= kf, 1, 0)
        return t | jnp.left_shift(digit, shift)

    nfull, rem = divmod(31, bits)
    t = lax.fori_loop(0, nfull, lambda i, t: step(bits, 31 - bits * (i + 1), t), t0)
    return step(rem, 0, t) if rem else t


def _topk_mask(keys, valid, idx, k, bits):
    m, n = keys.shape
    kf = float(k)
    thr = _kth_largest_key(keys, kf, bits)
    gt = keys > thr
    eq = (keys == thr) & valid
    need = kf - _count(gt)
    excess = jnp.max(_count(eq) - need) > 0.0
    nbits = max(1, (n - 1).bit_length())

    def first_ties():
        def idx_step(i, j):
            cand = j | jnp.left_shift(jnp.int32(1), nbits - 1 - i)
            return jnp.where(_count(eq & (idx < cand)) <= need - 1.0, cand, j)

        return lax.fori_loop(0, nbits, idx_step, jnp.zeros((m, 1), I32))

    jthr = lax.cond(excess, first_ties, lambda: jnp.full((m, 1), 2 ** nbits, I32))
    return (gt & valid) | (eq & (idx <= jthr))


Q_DIGITS = (7, 8, 8)
Q_MAX = float(2 ** sum(Q_DIGITS) - 1)


def _count_ge(b, cand):
    m, n = b.shape
    ct = jnp.broadcast_to(cand, (m, LANES)).astype(b.dtype)
    one, zero = jnp.ones((), b.dtype), jnp.zeros((), b.dtype)
    acc = jnp.where(b[:, 0:LANES] >= ct, one, zero)
    for c in range(1, n // LANES):
        acc = acc + jnp.where(b[:, c * LANES:(c + 1) * LANES] >= ct, one, zero)
    return jnp.sum(acc.astype(F32), axis=1, keepdims=True)


def _digit_search(b, kf, nbits):
    def step(i, t):
        cand = t | jnp.left_shift(jnp.int32(1), nbits - 1 - i)
        return jnp.where(_count_ge(b, cand.astype(F32)) >= kf, cand, t)

    return lax.fori_loop(0, nbits, step, jnp.zeros((b.shape[0], 1), I32)).astype(F32)


def _topk_mask_fast(score, valid, idx, k, digit_dtype):
    kf = float(k)
    lo = jnp.min(jnp.where(valid, score, jnp.inf), axis=1, keepdims=True)
    hi = jnp.max(jnp.where(valid, score, -jnp.inf), axis=1, keepdims=True)
    scale = jnp.where(hi > lo, Q_MAX / (hi - lo), 0.0)
    q = jnp.where(valid, jnp.minimum(jnp.floor((score - lo) * scale), Q_MAX), -1.0)
    w1, w0 = float(2 ** Q_DIGITS[2]), float(2 ** (Q_DIGITS[1] + Q_DIGITS[2]))
    d2 = jnp.floor(q * (1.0 / w0))
    r = q - d2 * w0
    d1 = jnp.floor(r * (1.0 / w1))
    d0 = r - d1 * w1
    b2 = d2.astype(digit_dtype)
    t2 = _digit_search(b2, kf, Q_DIGITS[0])
    k1 = kf - _count_ge(b2, t2 + 1.0)
    in2 = d2 == t2
    b1 = jnp.where(in2, d1, -1.0).astype(digit_dtype)
    t1 = _digit_search(b1, k1, Q_DIGITS[1])
    k0 = k1 - _count_ge(b1, t1 + 1.0)
    b0 = jnp.where(in2 & (d1 == t1), d0, -1.0).astype(digit_dtype)
    t0 = _digit_search(b0, k0, Q_DIGITS[2])
    thr = t2 * w0 + t1 * w1 + t0
    ambiguous = jnp.max(_count_ge(b0, t0) - k0) > 0.0
    fast = jnp.where(q >= thr, 1, 0).astype(I32)

    def exact():
        keys = _sortable(jnp.where(valid, score, -jnp.inf))
        return jnp.where(_topk_mask(keys, valid, idx, k, 1), 1, 0).astype(I32)

    return lax.cond(ambiguous, exact, lambda: fast) > 0


def _masked_attention(s, mask, v):
    s = jnp.where(mask, s, -jnp.inf)
    p = jnp.exp2(s - jnp.max(s, axis=1, keepdims=True))
    return _dot(p.astype(BF16), v) / jnp.sum(p, axis=1, keepdims=True)


def _dsa_prompt_kernel(q_ref, kb_ref, vb_ref, iq4_ref, ik4_ref, iw_ref, *rest, j0, topk):
    o_ref = rest[-1]
    nq = q_ref.shape[0]
    nk = kb_ref.shape[0]
    t0 = (j0 + pl.program_id(1)) * nq
    ik4 = ik4_ref[...]
    score = jnp.zeros((nq, nk), F32)
    for h in range(N_IDX_HEADS):
        s = _dot_nt(iq4_ref[:, h * IDX4:(h + 1) * IDX4], ik4)
        score = score + jnp.maximum(s, 0.0) * iw_ref[:, IDX_DIM + h:IDX_DIM + h + 1]
    score = score * IDX_SCALE
    kpos = lax.broadcasted_iota(I32, (nq, nk), 1)
    valid = kpos <= t0 + lax.broadcasted_iota(I32, (nq, nk), 0)
    if nk > topk and j0 * nq + 1 >= topk:
        mask = _topk_mask_fast(score, valid, kpos, topk, BF16)
    elif nk > topk:
        mask = _topk_mask(_sortable(jnp.where(valid, score, -jnp.inf)), valid, kpos, topk, 1)
    else:
        mask = valid
    for g in range(N_KV_B):
        gs = slice(g * HEAD_DIM_B, (g + 1) * HEAD_DIM_B)
        kg = kb_ref[:, gs]
        vg = vb_ref[:, gs]
        for r in range(KV_REP):
            hs = slice((g * KV_REP + r) * HEAD_DIM_B, (g * KV_REP + r + 1) * HEAD_DIM_B)
            o_ref[:, hs] = _masked_attention(_dot_nt(q_ref[:, hs], kg), mask, vg).astype(o_ref.dtype)


def _dsa_prompt(q, kb, vb, iq4, ik4, ikw, *, n_seq, n_cls, qb):
    t = q.shape[0]
    s_len = t // n_seq
    nb = s_len // qb
    nbc = nb // n_cls
    topk = min(TOPK_MAX, s_len // 4)
    q, kb, vb, iq4, ik4, ikw = (a.reshape(n_seq, s_len, a.shape[-1]) for a in (q, kb, vb, iq4, ik4, ikw))
    out = None
    for c in range(n_cls):
        nk = (c + 1) * nbc * qb
        qrow = lambda w, c=c: pl.BlockSpec((None, qb, w), lambda b, j: (b, c * nbc + j, 0))
        seq = lambda w, nk=nk: pl.BlockSpec((None, nk, w), lambda b, j: (b, 0, 0))
        prev = [] if out is None else [out]
        out = pl.pallas_call(
            functools.partial(_dsa_prompt_kernel, j0=c * nbc, topk=topk),
            grid=(n_seq, nbc),
            in_specs=[qrow(BRANCH_W), seq(N_KV_B * HEAD_DIM_B), seq(N_KV_B * HEAD_DIM_B), qrow(N_IDX_HEADS * IDX4),
                      seq(IDX4), qrow(LANES)] + [pl.BlockSpec(memory_space=pl.ANY)] * len(prev),
            out_specs=qrow(BRANCH_W),
            out_shape=jax.ShapeDtypeStruct((n_seq, s_len, BRANCH_W), BF16),
            input_output_aliases={6: 0} if prev else {},
            compiler_params=_cparams("parallel", "arbitrary"),
        )(q, kb, vb, iq4, ik4, ikw, *prev)
    return out.reshape(t, BRANCH_W)


def _dsa_sample_kernel(pt_ref, qg_ref, iq4_ref, w_ref, kn_ref, vn_ref, ikn_ref, *rest, n_pages, t_new, spb):
    del pt_ref
    n_in = 3 * n_pages * spb
    page_refs, (o_ref, sc_s, k_s, v_s) = rest[:n_in], rest[n_in:]
    P = PAGE_SIZE
    pad = P - t_new
    nk = (n_pages + 1) * P
    past = n_pages * P

    for s in range(spb):
        idx_refs = page_refs[3 * n_pages * s:3 * n_pages * s + n_pages]
        k_refs = page_refs[3 * n_pages * s + n_pages:3 * n_pages * s + 2 * n_pages]
        v_refs = page_refs[3 * n_pages * s + 2 * n_pages:3 * n_pages * (s + 1)]
        iq4 = iq4_ref[s]
        w = w_ref[s]

        def page_scores(ik, page, keys_on_lanes):
            hi, lo = _split_bf16(ik)
            if keys_on_lanes:
                sc = _dot(iq4, jnp.concatenate([hi, lo, hi, lo], axis=0))
            else:
                sc = _dot_nt(iq4, jnp.concatenate([hi, lo, hi, lo], axis=1))
            r = jnp.maximum(sc, 0.0) * w
            acc = r[0:t_new]
            for h in range(1, N_IDX_HEADS):
                acc = acc + r[h * t_new:(h + 1) * t_new]
            sc_s[s * t_new:(s + 1) * t_new, page * P:(page + 1) * P] = acc * IDX_SCALE

        for p in range(n_pages):
            page_scores(idx_refs[p][0, 0], p, True)
            for g in range(N_KV_B):
                gs = slice(g * HEAD_DIM_B, (g + 1) * HEAD_DIM_B)
                k_s[s, p * P:(p + 1) * P, gs] = k_refs[p][0, 0, pl.ds(g, P, stride=N_KV_B), :].astype(BF16)
                v_s[s, p * P:(p + 1) * P, gs] = v_refs[p][0, 0, pl.ds(g, P, stride=N_KV_B), :].astype(BF16)
        page_scores(jnp.concatenate([ikn_ref[s], jnp.zeros((pad, IDX_DIM), F32)], axis=0), n_pages, False)
        zkv = jnp.zeros((pad, N_KV_B * HEAD_DIM_B), F32)
        k_s[s, past:nk, :] = jnp.concatenate([kn_ref[s], zkv], axis=0).astype(BF16)
        v_s[s, past:nk, :] = jnp.concatenate([vn_ref[s], zkv], axis=0).astype(BF16)

    rows = spb * t_new
    kpos = lax.broadcasted_iota(I32, (rows, nk), 1)
    valid = kpos <= past + (lax.broadcasted_iota(I32, (rows, nk), 0) & (t_new - 1))
    keys = _sortable(jnp.where(valid, sc_s[...], -jnp.inf))
    mask = _topk_mask(keys, valid, kpos, min(TOPK_MAX, (past + t_new) // 4), 3)
    mask_i = jnp.where(mask, 1, 0).astype(I32)
    for s in range(spb):
        m_s = jnp.concatenate([mask_i[s * t_new:(s + 1) * t_new]] * KV_REP, axis=0) > 0
        for g in range(N_KV_B):
            gs = slice(g * HEAD_DIM_B, (g + 1) * HEAD_DIM_B)
            o_ref[s, g] = _masked_attention(_dot_nt(qg_ref[s, g], k_s[s, :, gs]), m_s, v_s[s, :, gs]).astype(o_ref.dtype)


def _dsa_sample(qg, iq4r, wr, k_new, v_new, ik_new, cache_k, cache_v, cache_idx_k, page_table, layer, spb):
    B, _, rows, hd = qg.shape
    t_new = k_new.shape[1]
    assert t_new & (t_new - 1) == 0 and B % spb == 0
    n_pages = page_table.shape[1]
    kvw = N_KV_B * HEAD_DIM_B
    per_seq = lambda shape: pl.BlockSpec((spb,) + shape, lambda b, pt: (b,) + (0,) * len(shape))

    def page_spec(rows_, width, s, p):
        return pl.BlockSpec((1, 1, rows_, width), lambda b, pt: (layer, pt[b * spb + s, p], 0, 0))

    in_specs = [per_seq(qg.shape[1:]), per_seq(iq4r.shape[1:]), per_seq(wr.shape[1:]), per_seq((t_new, kvw)),
                per_seq((t_new, kvw)), per_seq((t_new, IDX_DIM))]
    pages = []
    for s in range(spb):
        in_specs += [page_spec(IDX_DIM, PAGE_SIZE, s, p) for p in range(n_pages)]
        in_specs += [page_spec(PAGE_SIZE * N_KV_B, HEAD_DIM_B, s, p) for p in range(n_pages)]
        in_specs += [page_spec(PAGE_SIZE * N_KV_B, HEAD_DIM_B, s, p) for p in range(n_pages)]
        pages += [cache_idx_k] * n_pages + [cache_k] * n_pages + [cache_v] * n_pages
    nk = (n_pages + 1) * PAGE_SIZE
    grid_spec = pltpu.PrefetchScalarGridSpec(
        num_scalar_prefetch=1, grid=(B // spb,), in_specs=in_specs,
        out_specs=pl.BlockSpec((spb,) + qg.shape[1:], lambda b, pt: (b, 0, 0, 0)),
        scratch_shapes=[pltpu.VMEM((spb * t_new, nk), F32), pltpu.VMEM((spb, nk, kvw), BF16),
                        pltpu.VMEM((spb, nk, kvw), BF16)])
    return pl.pallas_call(
        functools.partial(_dsa_sample_kernel, n_pages=n_pages, t_new=t_new, spb=spb),
        grid_spec=grid_spec,
        out_shape=jax.ShapeDtypeStruct(qg.shape, BF16),
        compiler_params=_cparams("arbitrary"),
    )(page_table, qg, iq4r, wr, k_new, v_new, ik_new, *pages)


def _merge_kernel(xn_ref, a_ref, b_ref, c_ref, d_ref, wg0, wg1, wg2, wg3, bg0, bg1, bg2, bg3, wb_ref, o_ref):
    xn = xn_ref[...]
    acc = None
    for n, (br, wg, bg) in enumerate(zip((a_ref, b_ref, c_ref, d_ref), (wg0, wg1, wg2, wg3), (bg0, bg1, bg2, bg3))):
        term = jax.nn.sigmoid(_dot(xn, wg[...]) + bg[...]) * _dot(br[...], wb_ref[n])
        acc = term if acc is None else acc + term
    o_ref[...] = acc.astype(o_ref.dtype)


def _merge(xn, branches, w_gate, b_gate, w_branch, l, tm, tn):
    t, d = xn.shape
    nj = d // tn
    row = lambda w: pl.BlockSpec((tm, w), lambda i, j: (i, 0))
    wg = [pl.BlockSpec((None, d, tn), functools.partial(lambda i, j, n: (l, 0, n * nj + j), n=n))
          for n in range(N_BRANCH)]
    bg = [pl.BlockSpec((1, tn), functools.partial(lambda i, j, n: (0, n * nj + j), n=n)) for n in range(N_BRANCH)]
    return pl.pallas_call(
        _merge_kernel,
        grid=(t // tm, nj),
        in_specs=[row(d)] + [row(BRANCH_W)] * N_BRANCH + wg + bg
                 + [pl.BlockSpec((None, N_BRANCH, BRANCH_W, tn), lambda i, j: (l, 0, 0, j))],
        out_specs=pl.BlockSpec((tm, tn), lambda i, j: (i, j)),
        out_shape=jax.ShapeDtypeStruct((t, d), BF16),
        compiler_params=_cparams("parallel", "arbitrary"),
    )(xn, *branches, *([w_gate] * N_BRANCH), *([b_gate] * N_BRANCH), w_branch)


def _outproj_kernel(m_ref, x_ref, w_ref, g_ref, o_ref):
    o_ref[...] = x_ref[...] + _rms(_dot(m_ref[...], w_ref[...]), g_ref[...])


def _outproj(merged, x, w_o, g, l, tm):
    t, d = x.shape
    return pl.pallas_call(
        _outproj_kernel,
        grid=(t // tm,),
        in_specs=[pl.BlockSpec((tm, d), lambda i: (i, 0)), pl.BlockSpec((tm, d), lambda i: (i, 0)),
                  pl.BlockSpec((None, d, d), lambda i: (l, 0, 0)), pl.BlockSpec((1, d), lambda i: (0, 0))],
        out_specs=pl.BlockSpec((tm, d), lambda i: (i, 0)),
        out_shape=jax.ShapeDtypeStruct((t, d), F32),
        compiler_params=_cparams("parallel"),
    )(merged, x, w_o, g)


def _ffn_kernel(x_ref, gpre_ref, wug_ref, wuv_ref, wcg_ref, wcv_ref, bcg_ref, bcv_ref, wd_ref, gpost_ref,
                stg_ref, stv_ref, o_ref, ng_ref, nv_ref, h_s, acc_s, cg_s, cv_s, ext_s, *, S, tiles_per_group):
    i = pl.program_id(0)
    j = pl.program_id(1)
    tm = x_ref.shape[0]
    CR = cg_s.shape[1]

    @pl.when(j == 0)
    def _():
        h_s[...] = _rms(x_ref[...], gpre_ref[...]).astype(BF16)
        acc_s[...] = jnp.zeros_like(acc_s)

    @pl.when(i % tiles_per_group == 0)
    def _():
        cg_s[j] = stg_ref[0]
        cv_s[j] = stv_ref[0]

    h = h_s[...]

    def conv(w_ref, wc_ref, bc_ref, c_s, new_ref):
        u = _dot(h, w_ref[...])
        ext_s[0:CR, :] = c_s[j]
        ext_s[CR:CR + tm, :] = u
        y = bc_ref[...] + u * wc_ref[2:3, :]
        y = y + ext_s[CR - S:CR - S + tm, :] * wc_ref[1:2, :]
        y = y + ext_s[CR - 2 * S:CR - 2 * S + tm, :] * wc_ref[0:1, :]
        tail = ext_s[tm:tm + CR, :]
        c_s[j] = tail
        new_ref[0] = tail
        return y

    gate = conv(wug_ref, wcg_ref, bcg_ref, cg_s, ng_ref)
    val = conv(wuv_ref, wcv_ref, bcv_ref, cv_s, nv_ref)
    acc_s[...] += _dot((_gelu_tanh(gate) * val).astype(BF16), wd_ref[...])

    @pl.when(j == pl.num_programs(1) - 1)
    def _():
        o_ref[...] = x_ref[...] + _rms(acc_s[...], gpost_ref[...])


def _ffn(x, lp, state, *, S, tm, tn):
    t, d = x.shape
    G, CR, _ = state.shape
    nj = D_FF // tn
    ni = t // tm
    tiles_per_group = ni // G
    grp = lambda i, j: (i // tiles_per_group, 0, j)
    tile = lambda i, j: (i, 0, j)
    grp_v = lambda i, j: (i // tiles_per_group, 0, nj + j)
    col = lambda r: pl.BlockSpec((r, tn), lambda i, j: (0, j))
    col_v = lambda r: pl.BlockSpec((r, tn), lambda i, j: (0, nj + j))
    vec = pl.BlockSpec((1, d), lambda i, j: (0, 0))
    l = lp["layer"]
    xo, ng, nv = pl.pallas_call(
        functools.partial(_ffn_kernel, S=S, tiles_per_group=tiles_per_group),
        grid=(ni, nj),
        in_specs=[pl.BlockSpec((tm, d), lambda i, j: (i, 0)), vec,
                  pl.BlockSpec((None, d, tn), lambda i, j: (l, 0, j)),
                  pl.BlockSpec((None, d, tn), lambda i, j: (l, 0, nj + j)),
                  col(FFN_CONV_W), col_v(FFN_CONV_W), col(1), col_v(1),
                  pl.BlockSpec((None, tn, d), lambda i, j: (l, j, 0)), vec,
                  pl.BlockSpec((1, CR, tn), grp), pl.BlockSpec((1, CR, tn), grp_v)],
        out_specs=[pl.BlockSpec((tm, d), lambda i, j: (i, 0)),
                   pl.BlockSpec((1, CR, tn), tile), pl.BlockSpec((1, CR, tn), tile)],
        out_shape=[jax.ShapeDtypeStruct((t, d), F32), jax.ShapeDtypeStruct((ni, CR, D_FF), F32),
                   jax.ShapeDtypeStruct((ni, CR, D_FF), F32)],
        scratch_shapes=[pltpu.VMEM((tm, d), BF16), pltpu.VMEM((tm, d), F32), pltpu.VMEM((nj, CR, tn), F32),
                        pltpu.VMEM((nj, CR, tn), F32), pltpu.VMEM((CR + tm, tn), F32)],
        compiler_params=_cparams("arbitrary", "arbitrary"),
    )(x, lp["g_ffn_pre"], lp["w_up"], lp["w_up"], lp["w_ffn_conv"], lp["w_ffn_conv"], lp["b_ffn_conv"],
      lp["b_ffn_conv"], lp["w_down"], lp["g_ffn_post"], state, state)
    last = slice(tiles_per_group - 1, None, tiles_per_group)
    return xo, ng[last], nv[last]


def _layer_front(x, lp, tabs, kv_final=None):
    l = lp["layer"]
    u_main, xn = _inproj(x, lp["g_mix_pre"], lp["w_in_main"], l, TM_INPROJ, TN_INPROJ)
    u_idx = _inproj_split(x, lp["g_mix_pre"], lp["w_in_idx3"], l, TM_IDX, N_IDX)
    return u_main, xn, _rope_prep(u_main, u_idx, tabs, ROPE_ROWS, kv_final)


def _layer_back(x, xn, branches, lp, ffn_state, *, S):
    l = lp["layer"]
    merged = _merge(xn, branches, lp["w_gate"], lp["b_gate"], lp["w_branch"], l, TM_DENSE, TN_MERGE)
    x1 = _outproj(merged, x, lp["w_o"], lp["g_mix_post"], l, TM_DENSE)
    return _ffn(x1, lp, ffn_state, S=S, tm=TM_DENSE, tn=TN_FFN[S])


def _prep_big_weights(p):
    w_in = jnp.swapaxes(p["w_in"], 1, 2)
    o_iq = 3072
    o_cx = o_iq + N_IDX_HEADS * IDX_DIM + IDX_DIM + N_IDX_HEADS
    w_idx = jnp.pad(w_in[:, o_iq:o_cx], ((0, 0), (0, N_IDX - (o_cx - o_iq)), (0, 0)))
    hi = w_idx.astype(BF16)
    lo = (w_idx - hi.astype(F32)).astype(BF16)
    return dict(
        w_in_main=jnp.concatenate([w_in[:, :o_iq], w_in[:, o_cx:]], axis=1).astype(BF16),
        w_in_idx3=jnp.concatenate([hi, lo, hi], axis=2),
        w_branch=p["w_branch"].astype(BF16), w_gate=p["w_gate"].astype(BF16), w_o=p["w_o"].astype(BF16),
        w_up=p["w_up"].astype(BF16), w_down=p["w_down"].astype(BF16))


def _prep_layer_params(p, big, l):
    row = lambda a: a[l][None, :]
    return dict(
        big, layer=l,
        g_mix_pre=row(p["g_mix_pre"]),
        w_pool=p["w_pool"][l].astype(BF16), pool_scale=row(p["pool_scale"]),
        w_conv_c=p["w_conv_c"][l], b_conv_c=row(p["b_conv_c"]),
        w_rg_a=p["w_rg_a"][l].astype(BF16), b_rg_a=row(p["b_rg_a"]),
        w_rg_x=p["w_rg_x"][l].astype(BF16), b_rg_x=row(p["b_rg_x"]),
        lru_lambda=row(p["lru_lambda"]), ret_gn=row(p["ret_gn"]), b_gate=row(p["b_gate"]),
        g_mix_post=row(p["g_mix_post"]), g_ffn_pre=row(p["g_ffn_pre"]),
        w_ffn_conv=p["w_ffn_conv"][l], b_ffn_conv=row(p["b_ffn_conv"]), g_ffn_post=row(p["g_ffn_post"]))


def _to_time_major(a, nb, nt):
    return jnp.swapaxes(a, 0, 1).reshape((nt * nb,) + a.shape[2:])


def _to_seq_major(a, nb, nt):
    return jnp.swapaxes(a.reshape((nt, nb) + a.shape[1:]), 0, 1)


def kernel(x_prompt, x_sample, cache_k, cache_v, cache_idx_k, state_pool, state_conv, state_rglru, state_ret, state_ffn_conv, page_table, g_mix_pre, w_in, w_pool, pool_scale, w_conv_c, b_conv_c, w_rg_a, b_rg_a, w_rg_x, b_rg_x, lru_lambda, ret_gn, w_branch, w_gate, b_gate, w_o, g_mix_post, g_ffn_pre, w_up, w_ffn_conv, b_ffn_conv, w_down, g_ffn_post):
    params = dict(g_mix_pre=g_mix_pre, w_in=w_in, w_pool=w_pool, pool_scale=pool_scale, w_conv_c=w_conv_c,
                  b_conv_c=b_conv_c, w_rg_a=w_rg_a, b_rg_a=b_rg_a, w_rg_x=w_rg_x, b_rg_x=b_rg_x,
                  lru_lambda=lru_lambda, ret_gn=ret_gn, w_branch=w_branch, w_gate=w_gate, b_gate=b_gate, w_o=w_o,
                  g_mix_post=g_mix_post, g_ffn_pre=g_ffn_pre, w_up=w_up, w_ffn_conv=w_ffn_conv,
                  b_ffn_conv=b_ffn_conv, w_down=w_down, g_ffn_post=g_ffn_post)
    depth = w_in.shape[0]
    bp, seq, d = x_prompt.shape
    bs, tdec, _ = x_sample.shape
    past = page_table.shape[1] * PAGE_SIZE
    n_pool = cache_k.shape[1]
    ck = cache_k.reshape(depth, n_pool, PAGE_SIZE * N_KV_B, HEAD_DIM_B)
    cv = cache_v.reshape(depth, n_pool, PAGE_SIZE * N_KV_B, HEAD_DIM_B)
    cik_t = jnp.swapaxes(cache_idx_k, 2, 3)
    big = _prep_big_weights(params)

    tabs_p = _rope_tables(jnp.arange(seq, dtype=F32))
    tabs_s = _rope_tables(jnp.repeat(past + jnp.arange(tdec, dtype=F32), bs))
    xp = x_prompt.reshape(bp * seq, d)
    xs = _to_time_major(x_sample, bs, tdec)
    cr_p = max(SUBLANES, (FFN_CONV_W - 1))
    outs_p, outs_s = [], []
    kv_p = None
    for l in range(depth):
        lp = _prep_layer_params(params, big, l)
        u, xn, (q, _, kb, vb, iq4, ik4, ikw, rq, rk, *kv_p) = _layer_front(xp, lp, tabs_p, (l, depth, kv_p))
        o_a = _pool(u, jnp.zeros((bp, POOL_BUF, BRANCH_W), F32), lp["w_pool"], lp["pool_scale"], S=1, Tc=SEQ_CHUNK,
                    start=0)
        o_b = _dsa_prompt(q, kb, vb, iq4, ik4, ikw, n_seq=bp, n_cls=DSA_CLASSES, qb=DSA_QBLOCK)
        o_c, h_p = _rglru(u, jnp.zeros((bp, CONV_W - 1, LRU_W), F32), jnp.zeros((bp, 1, LRU_W), F32), lp, S=1,
                          Tc=SEQ_CHUNK)
        o_d, s_p = _retention(rq, rk, u, u, jnp.zeros((bp, N_HEADS_D, QK_DIM_D, V_DIM_D), F32), lp["ret_gn"],
                              c=RET_CHUNK if seq % RET_CHUNK == 0 else seq, v_col=6, g_col=7, spb=1,
                              cps=2 if seq % (2 * RET_CHUNK) == 0 else 1)
        xp, ng, nv = _layer_back(xp, xn, (o_a, o_b, o_c, o_d), lp, jnp.zeros((bp, cr_p, 2 * D_FF), F32), S=1)
        u3 = u.reshape(bp, seq, N_MAIN)
        outs_p.append((
            None, None,
            ikw.reshape(bp, seq, LANES)[:, :, :IDX_DIM], u3[:, seq - POOL_BUF:, 0:BRANCH_W],
            u3[:, seq - (CONV_W - 1):, 3072:3072 + LRU_W], h_p[:, 0], s_p,
            jnp.concatenate([ng, nv], axis=-1)[:, cr_p - (FFN_CONV_W - 1):]))
        u, xn, (q, k, kb, vb, iq4, ik4, ikw, rq, rk) = _layer_front(xs, lp, tabs_s)
        o_a = _pool(u, _to_time_major(state_pool[l], bs, POOL_BUF)[None], lp["w_pool"], lp["pool_scale"],
                    S=bs, Tc=tdec, start=past)
        o_c, h_s = _rglru(u, _to_time_major(state_conv[l], bs, CONV_W - 1)[None], state_rglru[l][None], lp,
                          S=bs, Tc=tdec)
        sm = lambda a: _to_seq_major(a, bs, tdec)
        u_sm = sm(u)
        o_d, s_s = _retention(sm(rq).reshape(bs * tdec, -1), sm(rk).reshape(bs * tdec, -1),
                              u_sm[:, :, 6144:7168].reshape(bs * tdec, -1), u_sm[:, :, 7168:8192].reshape(bs * tdec, -1),
                              state_ret, lp["ret_gn"], c=tdec, v_col=0, g_col=0, spb=RET_SAMPLE_SPB, layer=l)
        o_d = _to_time_major(o_d.reshape(bs, tdec, BRANCH_W), bs, tdec)
        qg = sm(q).reshape(bs, tdec, N_KV_B, KV_REP, HEAD_DIM_B).transpose(0, 2, 3, 1, 4)
        qg = qg.reshape(bs, N_KV_B, KV_REP * tdec, HEAD_DIM_B)
        iq4r = sm(iq4).reshape(bs, tdec, N_IDX_HEADS, IDX4).transpose(0, 2, 1, 3).reshape(bs, N_IDX_HEADS * tdec, IDX4)
        ikw_sm = sm(ikw)
        wr = ikw_sm[:, :, IDX_DIM:IDX_DIM + N_IDX_HEADS].transpose(0, 2, 1).reshape(bs, N_IDX_HEADS * tdec, 1)
        k_sm = sm(k)
        v_sm = u_sm[:, :, 2560:3072]
        o_b = _dsa_sample(qg, iq4r, wr, k_sm, v_sm, ikw_sm[:, :, :IDX_DIM], ck, cv, cik_t, page_table, l,
                          DSA_SAMPLE_SPB)
        o_b = o_b.reshape(bs, N_KV_B, KV_REP, tdec, HEAD_DIM_B).transpose(3, 0, 1, 2, 4).reshape(tdec * bs, BRANCH_W)
        ffn_state = _to_time_major(state_ffn_conv[l], bs, FFN_CONV_W - 1)[None]
        xs, ng, nv = _layer_back(xs, xn, (o_a, o_b, o_c, o_d), lp, ffn_state, S=bs)
        ffn_new = _to_seq_major(jnp.concatenate([ng, nv], axis=-1)[0], bs, FFN_CONV_W - 1)
        a_in = u_sm[:, :, 0:BRANCH_W]
        c_x = u_sm[:, :, 3072:3072 + LRU_W]
        outs_s.append((
            k_sm.reshape(bs, tdec, N_KV_B, HEAD_DIM_B), v_sm.reshape(bs, tdec, N_KV_B, HEAD_DIM_B),
            ikw_sm[:, :, :IDX_DIM],
            jnp.concatenate([state_pool[l], a_in], axis=1)[:, tdec:],
            jnp.concatenate([state_conv[l], c_x], axis=1)[:, tdec:],
            h_s[0], s_s, ffn_new))

    stk = lambda outs, i: jnp.stack([o[i] for o in outs], axis=0)
    res = [xp.reshape(bp, seq, d), _to_seq_major(xs, bs, tdec)]
    for i in range(8):
        p_i = kv_p[i].reshape(depth, bp, seq, N_KV_B, HEAD_DIM_B) if i < 2 else stk(outs_p, i)
        res += [p_i, stk(outs_s, i)]
    return tuple(res)
```

```python
import functools
import math

import jax
import jax.numpy as jnp
from jax import lax
from jax.experimental import pallas as pl
from jax.experimental.pallas import tpu as pltpu

F32 = jnp.float32
BF16 = jnp.bfloat16
I32 = jnp.int32

SUBLANES = 8
LANES = 128
VMEM_LIMIT_BYTES = 56 * 1024 * 1024

TM_INPROJ, TN_INPROJ = 1024, 1024
TM_IDX = 512
TM_DENSE = 512
TN_MERGE = 512
TN_FFN = {1: 512, 128: 256}
ROPE_ROWS = 256
SEQ_CHUNK = 512
DSA_QBLOCK, DSA_CLASSES = 256, 8
DSA_SAMPLE_SPB = 2
DSA_SELECT_SPB = 8
RET_SAMPLE_SPB = 8

D_MODEL = 2048
BRANCH_W = D_MODEL // 2
N_BRANCH = 4
POOL_WINDOWS = (2, 4, 8, 16)
POOL_GROUP = BRANCH_W // len(POOL_WINDOWS)
POOL_BUF = max(POOL_WINDOWS) - 1
N_HEADS_B = 8
HEAD_DIM_B = BRANCH_W // N_HEADS_B
N_KV_B = 4
KV_REP = N_HEADS_B // N_KV_B
N_IDX_HEADS = 16
IDX_DIM = 64
TOPK_MAX = 256
ROPE_THETA = 10000.0
ATT_SCALE = HEAD_DIM_B ** -0.5
Q_SCALE = ATT_SCALE * math.log2(math.e)
IDX_SCALE = (IDX_DIM * N_IDX_HEADS) ** -0.5
LRU_W = BRANCH_W
LRU_BLOCKS = 4
LRU_BLOCK = LRU_W // LRU_BLOCKS
LRU_C = 8.0
CONV_W = 4
N_HEADS_D = 8
QK_DIM_D = BRANCH_W // (2 * N_HEADS_D)
V_DIM_D = BRANCH_W // N_HEADS_D
RET_CHUNK = 128
D_FF = 11 * D_MODEL // 4
FFN_CONV_W = 3
EPS = 1e-6
PAGE_SIZE = 128

N_MAIN = 8192
N_IDX = 1152
IDX4 = 4 * IDX_DIM

LOG_G = tuple(math.log1p(-(2.0 ** (-5.0 - h))) for h in range(N_HEADS_D))
INT_MIN = -2 ** 31


def _round_up(n, m):
    return (n + m - 1) // m * m


def _cparams(*sem):
    return pltpu.CompilerParams(dimension_semantics=sem, vmem_limit_bytes=VMEM_LIMIT_BYTES)


def _dot(a, b):
    return jnp.dot(a, b, preferred_element_type=F32)


def _dot_nt(a, b):
    return lax.dot_general(a, b, (((1,), (1,)), ((), ())), preferred_element_type=F32)


def _dot_tn(a, b):
    return lax.dot_general(a, b, (((0,), (0,)), ((), ())), preferred_element_type=F32)


def _rms(x, g):
    return x * lax.rsqrt(jnp.mean(x * x, axis=-1, keepdims=True) + EPS) * g


def _gelu_tanh(x):
    return x * (0.5 * (1.0 + jnp.tanh(0.7978845608028654 * (x + 0.044715 * (x * x * x)))))


def _split_bf16(x):
    hi = x.astype(BF16)
    lo = (x - hi.astype(F32)).astype(BF16)
    return hi, lo


def _inproj_kernel(x_ref, g_ref, w_ref, u_ref, xn_ref):
    @pl.when(pl.program_id(1) == 0)
    def _():
        xn_ref[...] = _rms(x_ref[...], g_ref[...]).astype(BF16)

    u_ref[...] = _dot_nt(xn_ref[...], w_ref[...])


def _inproj(x, g, w, l, tm, tn):
    t, d = x.shape
    n = w.shape[1]
    return pl.pallas_call(
        _inproj_kernel,
        grid=(t // tm, n // tn),
        in_specs=[pl.BlockSpec((tm, d), lambda i, j: (i, 0)),
                  pl.BlockSpec((1, d), lambda i, j: (0, 0)),
                  pl.BlockSpec((None, tn, d), lambda i, j: (l, j, 0))],
        out_specs=[pl.BlockSpec((tm, tn), lambda i, j: (i, j)),
                   pl.BlockSpec((tm, d), lambda i, j: (i, 0))],
        out_shape=[jax.ShapeDtypeStruct((t, n), F32), jax.ShapeDtypeStruct((t, d), BF16)],
        compiler_params=_cparams("parallel", "arbitrary"),
    )(x, g, w)


def _inproj_split_kernel(x_ref, g_ref, w3_ref, u_ref, x3_s):
    d = x_ref.shape[1]

    @pl.when(pl.program_id(1) == 0)
    def _():
        hi, lo = _split_bf16(_rms(x_ref[...], g_ref[...]))
        x3_s[:, 0:d] = hi
        x3_s[:, d:2 * d] = hi
        x3_s[:, 2 * d:3 * d] = lo

    u_ref[...] = _dot_nt(x3_s[...], w3_ref[...])


def _inproj_split(x, g, w3, l, tm, tn):
    t, d = x.shape
    n = w3.shape[1]
    return pl.pallas_call(
        _inproj_split_kernel,
        grid=(t // tm, n // tn),
        in_specs=[pl.BlockSpec((tm, d), lambda i, j: (i, 0)),
                  pl.BlockSpec((1, d), lambda i, j: (0, 0)),
                  pl.BlockSpec((None, tn, 3 * d), lambda i, j: (l, j, 0))],
        out_specs=pl.BlockSpec((tm, tn), lambda i, j: (i, j)),
        out_shape=jax.ShapeDtypeStruct((t, n), F32),
        scratch_shapes=[pltpu.VMEM((tm, 3 * d), BF16)],
        compiler_params=_cparams("parallel", "arbitrary"),
    )(x, g, w3)


def _rope128(xs, cos, sin):
    return xs * cos + pltpu.roll(xs, HEAD_DIM_B // 2, axis=1) * sin


def _rope64(xs, cos, sin, first_half):
    rot = jnp.where(first_half, pltpu.roll(xs, LANES - IDX_DIM // 2, axis=1), pltpu.roll(xs, IDX_DIM // 2, axis=1))
    return xs * cos + rot * sin


def _rope_kernel(bq_ref, bk_ref, bv_ref, dq_ref, dk_ref, ui_ref, c128_ref, s128_ref, c64_ref, s64_ref,
                 *rest, n_alias):
    q_ref, k_ref, kb_ref, vb_ref, iq4_ref, ik4_ref, ikw_ref, rq_ref, rk_ref = rest[n_alias:n_alias + 9]
    final = rest[n_alias + 9:]
    rows = bq_ref.shape[0]
    c128, s128, c64, s64 = c128_ref[...], s128_ref[...], c64_ref[...], s64_ref[...]
    lane = lax.broadcasted_iota(I32, (rows, LANES), 1)
    first_half = (lane & (IDX_DIM // 2)) == 0
    low = lane < IDX_DIM
    for c in range(N_HEADS_B):
        sl = slice(c * LANES, (c + 1) * LANES)
        q_ref[:, sl] = (_rope128(bq_ref[:, sl], c128, s128) * Q_SCALE).astype(BF16)
    for c in range(N_KV_B):
        sl = slice(c * LANES, (c + 1) * LANES)
        kr = _rope128(bk_ref[:, sl], c128, s128)
        k_ref[:, sl] = kr
        kb_ref[:, sl] = kr.astype(BF16)
        if final:
            final[0][pl.ds(c, rows, stride=N_KV_B), :] = kr
            final[1][pl.ds(c, rows, stride=N_KV_B), :] = bv_ref[:, sl]
    vb_ref[...] = bv_ref[...].astype(BF16)
    for c in range(N_HEADS_D * QK_DIM_D // LANES):
        sl = slice(c * LANES, (c + 1) * LANES)
        rq_ref[:, sl] = _rope64(dq_ref[:, sl], c64, s64, first_half)
        rk_ref[:, sl] = _rope64(dk_ref[:, sl], c64, s64, first_half) * (QK_DIM_D ** -0.5)
    for c in range(N_IDX_HEADS // 2):
        y = _rope64(ui_ref[:, c * LANES:(c + 1) * LANES], c64, s64, first_half)
        yr = pltpu.roll(y, IDX_DIM, axis=1)
        for hh, dup in enumerate((jnp.where(low, y, yr), jnp.where(low, yr, y))):
            hi, lo = _split_bf16(dup)
            base = (2 * c + hh) * IDX4
            iq4_ref[:, base:base + LANES] = hi
            iq4_ref[:, base + LANES:base + 2 * LANES] = lo
    raw = ui_ref[:, N_IDX_HEADS * IDX_DIM:N_IDX_HEADS * IDX_DIM + LANES]
    y = _rope64(raw, c64, s64, first_half)
    ikw_ref[...] = jnp.where(low, y, raw)
    dup = jnp.where(low, y, pltpu.roll(y, IDX_DIM, axis=1))
    hi = dup.astype(BF16).astype(F32)
    hilo = jnp.where(low, hi, dup - hi).astype(BF16)
    ik4_ref[:, 0:LANES] = hilo
    ik4_ref[:, LANES:2 * LANES] = hilo


def _rope_prep(u_main, u_idx, tabs, tr, kv_final=None):
    t = u_main.shape[0]
    nt = tabs[0].shape[0] // tr
    row = lambda w, c: pl.BlockSpec((tr, w), lambda i: (i, c))
    tab = pl.BlockSpec((tr, LANES), lambda i: (i % nt, 0))
    outs = [(BRANCH_W, BF16), (N_KV_B * HEAD_DIM_B, F32), (N_KV_B * HEAD_DIM_B, BF16), (N_KV_B * HEAD_DIM_B, BF16),
            (N_IDX_HEADS * IDX4, BF16), (IDX4, BF16), (LANES, F32), (N_HEADS_D * QK_DIM_D, F32),
            (N_HEADS_D * QK_DIM_D, F32)]
    in_specs = [row(1024, 1), row(512, 4), row(512, 5), row(512, 10), row(512, 11), row(N_IDX, 0), tab, tab, tab, tab]
    args = [u_main, u_main, u_main, u_main, u_main, u_idx, *tabs]
    out_specs = [row(w, 0) for w, _ in outs]
    out_shape = [jax.ShapeDtypeStruct((t, w), dt) for w, dt in outs]
    aliases = {}
    if kv_final is not None:
        layer, depth, prev = kv_final
        out_specs += [pl.BlockSpec((None, tr * N_KV_B, HEAD_DIM_B), lambda i: (layer, i, 0))] * 2
        out_shape += [jax.ShapeDtypeStruct((depth, t * N_KV_B, HEAD_DIM_B), F32)] * 2
        if prev is not None:
            aliases = {len(args): len(outs), len(args) + 1: len(outs) + 1}
            in_specs += [pl.BlockSpec(memory_space=pl.ANY)] * 2
            args += list(prev)
    return pl.pallas_call(
        functools.partial(_rope_kernel, n_alias=len(aliases)),
        grid=(t // tr,),
        in_specs=in_specs,
        out_specs=out_specs,
        out_shape=out_shape,
        input_output_aliases=aliases,
        compiler_params=_cparams("parallel"),
    )(*args)


def _rope_tables(pos):
    def tab(half, reps):
        inv = jnp.exp(-math.log(ROPE_THETA) * jnp.arange(half, dtype=F32) / half)
        ang = pos[:, None] * inv[None, :]
        cos, sin = jnp.cos(ang), jnp.sin(ang)
        return jnp.tile(jnp.concatenate([cos, cos], 1), (1, reps)), jnp.tile(jnp.concatenate([-sin, sin], 1), (1, reps))

    c128, s128 = tab(HEAD_DIM_B // 2, 1)
    c64, s64 = tab(IDX_DIM // 2, 2)
    return c128, s128, c64, s64


def _pool_kernel(a_ref, prev_ref, w_ref, sc_ref, o_ref, ext_s, *, S, Tc, start, nch):
    ch = pl.program_id(1)
    R = Tc * S
    PS = POOL_BUF * S
    OFF = _round_up(PS, SUBLANES)

    @pl.when(ch == 0)
    def _():
        ext_s[OFF - PS:OFF, :] = prev_ref[0]

    x = a_ref[...]
    ext_s[OFF:OFF + R, :] = x
    t_loc = lax.broadcasted_iota(I32, (R, 1), 0) // S if S > 1 else lax.broadcasted_iota(I32, (R, 1), 0)
    pos1 = start + ch * Tc + t_loc + 1
    for gi, w in enumerate(POOL_WINDOWS):
        sl = slice(gi * POOL_GROUP, (gi + 1) * POOL_GROUP)
        xs = x[:, sl]
        acc = xs
        for j in range(1, w):
            acc = acc + ext_s[OFF - j * S:OFF - j * S + R, sl]
        cnt = jnp.minimum(w, pos1).astype(F32)
        mixed = acc / cnt - xs
        y = _dot(mixed.astype(BF16), w_ref[gi])
        o_ref[:, sl] = (y * sc_ref[:, sl]).astype(o_ref.dtype)
    if nch > 1:
        ext_s[OFF - PS:OFF, :] = ext_s[OFF + R - PS:OFF + R, :]


def _pool(u_main, prev, w_pool, scale, *, S, Tc, start):
    t = u_main.shape[0]
    R = Tc * S
    G = prev.shape[0]
    nch = t // (G * R)
    PS = POOL_BUF * S
    C = BRANCH_W
    return pl.pallas_call(
        functools.partial(_pool_kernel, S=S, Tc=Tc, start=start, nch=nch),
        grid=(G, nch),
        in_specs=[pl.BlockSpec((R, C), lambda g, c: (g * nch + c, 0)),
                  pl.BlockSpec((1, PS, C), lambda g, c: (g, 0, 0)),
                  pl.BlockSpec((len(POOL_WINDOWS), POOL_GROUP, POOL_GROUP), lambda g, c: (0, 0, 0)),
                  pl.BlockSpec((1, C), lambda g, c: (0, 0))],
        out_specs=pl.BlockSpec((R, C), lambda g, c: (g * nch + c, 0)),
        out_shape=jax.ShapeDtypeStruct((t, C), BF16),
        scratch_shapes=[pltpu.VMEM((_round_up(PS, SUBLANES) + R, C), F32)],
        compiler_params=_cparams("parallel", "arbitrary"),
    )(u_main, prev, w_pool, scale)


def _rglru_kernel(cx_ref, cg_ref, prev_ref, h0_ref, wc_ref, bc_ref, wa_ref, ba_ref, wx_ref, bx_ref, lam_ref,
                  o_ref, hl_ref, ext_s, a_s, b_s, h_s, *, S, Tc, nch):
    ch = pl.program_id(1)
    R = Tc * S
    PS = (CONV_W - 1) * S
    OFF = _round_up(PS, SUBLANES)

    @pl.when(ch == 0)
    def _():
        ext_s[OFF - PS:OFF, :] = prev_ref[0]
        h_s[...] = h0_ref[0]

    x = cx_ref[...]
    ext_s[OFF:OFF + R, :] = x
    xc = bc_ref[...] + x * wc_ref[CONV_W - 1:CONV_W, :]
    for j in range(CONV_W - 1):
        k = CONV_W - 1 - j
        xc = xc + ext_s[OFF - k * S:OFF - k * S + R, :] * wc_ref[j:j + 1, :]
    xb = xc.astype(BF16)
    for n in range(LRU_BLOCKS):
        sl = slice(n * LRU_BLOCK, (n + 1) * LRU_BLOCK)
        r = jax.nn.sigmoid(_dot(xb[:, sl], wa_ref[n]) + ba_ref[:, sl])
        i = jax.nn.sigmoid(_dot(xb[:, sl], wx_ref[n]) + bx_ref[:, sl])
        lam = lam_ref[:, sl]
        softplus_neg = jnp.maximum(-lam, 0.0) + jnp.log1p(jnp.exp(-jnp.abs(lam)))
        log_a = (-LRU_C) * r * softplus_neg
        a_s[:, sl] = jnp.exp(log_a)
        th = jnp.tanh(log_a)
        b_s[:, sl] = jnp.sqrt(-2.0 * th / (1.0 - th)) * (i * xc[:, sl])

    if S % SUBLANES == 0:
        def step(t, h):
            off = pl.multiple_of(t * S, S)
            h = a_s[pl.ds(off, S), :] * h + b_s[pl.ds(off, S), :]
            b_s[pl.ds(off, S), :] = h
            return h

        h = lax.fori_loop(0, Tc, step, h_s[...])
    else:
        assert S == 1 and Tc % SUBLANES == 0
        row =lax.broadcasted_iota(I32, (SUBLANES, a_s.shape[1]), 0)

        def tile_step(i, h):
            off = pl.multiple_of(i * SUBLANES, SUBLANES)
            A = a_s[pl.ds(off, SUBLANES), :]
            B = b_s[pl.ds(off, SUBLANES), :]
            for sh in (1, 2, 4):
                keep = row >= sh
                B = A * jnp.where(keep, pltpu.roll(B, sh, axis=0), 0.0) + B
                A = A * jnp.where(keep, pltpu.roll(A, sh, axis=0), 1.0)
            hs = A * h + B
            b_s[pl.ds(off, SUBLANES), :] = hs
            return hs[SUBLANES - 1:SUBLANES, :]

        h = lax.fori_loop(0, Tc // SUBLANES, tile_step, h_s[...])
    h_s[...] = h
    hl_ref[0] = h
    o_ref[...] = (_gelu_tanh(cg_ref[...]) * b_s[...]).astype(o_ref.dtype)
    if nch > 1:
        ext_s[OFF - PS:OFF, :] = ext_s[OFF + R - PS:OFF + R, :]


def _rglru(u_main, prev, h0, lp, *, S, Tc):
    t = u_main.shape[0]
    R = Tc * S
    G = prev.shape[0]
    nch = t // (G * R)
    PS = (CONV_W - 1) * S
    C = LRU_W
    vec = pl.BlockSpec((1, C), lambda g, c: (0, 0))
    blk = pl.BlockSpec((LRU_BLOCKS, LRU_BLOCK, LRU_BLOCK), lambda g, c: (0, 0, 0))
    return pl.pallas_call(
        functools.partial(_rglru_kernel, S=S, Tc=Tc, nch=nch),
        grid=(G, nch),
        in_specs=[pl.BlockSpec((R, C), lambda g, c: (g * nch + c, 3)),
                  pl.BlockSpec((R, C), lambda g, c: (g * nch + c, 4)),
                  pl.BlockSpec((1, PS, C), lambda g, c: (g, 0, 0)),
                  pl.BlockSpec((1, S, C), lambda g, c: (g, 0, 0)),
                  pl.BlockSpec((CONV_W, C), lambda g, c: (0, 0)), vec, blk, vec, blk, vec, vec],
        out_specs=[pl.BlockSpec((R, C), lambda g, c: (g * nch + c, 0)),
                   pl.BlockSpec((1, S, C), lambda g, c: (g, 0, 0))],
        out_shape=[jax.ShapeDtypeStruct((t, C), BF16), jax.ShapeDtypeStruct((G, S, C), F32)],
        scratch_shapes=[pltpu.VMEM((_round_up(PS, SUBLANES) + R, C), F32), pltpu.VMEM((R, C), F32),
                        pltpu.VMEM((R, C), F32), pltpu.VMEM((S, C), F32)],
        compiler_params=_cparams("parallel", "arbitrary"),
    )(u_main, u_main, prev, h0, lp["w_conv_c"], lp["b_conv_c"], lp["w_rg_a"], lp["b_rg_a"], lp["w_rg_x"],
      lp["b_rg_x"], lp["lru_lambda"])


def _ret_kernel(rq_ref, rk_ref, rv_ref, dg_ref, s0_ref, gn_ref, o_ref, sn_ref, s_s, *, c, spb, cps):
    @pl.when(pl.program_id(1) == 0)
    def _():
        s_s[...] = s0_ref[...]

    ii = lax.broadcasted_iota(I32, (c, c), 0)
    jj = lax.broadcasted_iota(I32, (c, c), 1)
    dif = (ii - jj).astype(F32)
    tpos = lax.broadcasted_iota(I32, (c, 1), 0).astype(F32)
    mm = BF16 if c % 16 == 0 else F32
    for h in range(N_HEADS_D):
        lg = LOG_G[h]
        qs = slice(h * QK_DIM_D, (h + 1) * QK_DIM_D)
        vs = slice(h * V_DIM_D, (h + 1) * V_DIM_D)
        intra = jnp.where(dif >= 0.0, jnp.exp(lg * jnp.maximum(dif, 0.0)), 0.0)
        q_dec = jnp.exp(lg * (tpos + 1.0))
        k_dec = jnp.exp(lg * (c - 1.0 - tpos))
        for sq in range(spb):
            s = s_s[sq, h]
            for sub in range(cps):
                rows = slice((sq * cps + sub) * c, (sq * cps + sub + 1) * c)
                q = rq_ref[rows, qs].astype(mm)
                k = rk_ref[rows, qs]
                v = rv_ref[rows, vs].astype(mm)
                att = _dot_nt(q, k.astype(mm)) * intra
                o = _dot(att.astype(mm), v) + _dot(q, s.astype(mm)) * q_dec
                s = s * math.exp(lg * c) + _dot_tn((k * k_dec).astype(mm), v)
                mu = jnp.mean(o, axis=-1, keepdims=True)
                var = jnp.mean(jnp.square(o - mu), axis=-1, keepdims=True)
                y = (o - mu) * lax.rsqrt(var + EPS) * gn_ref[:, vs]
                dg = dg_ref[rows, vs]
                o_ref[rows, vs] = (dg * jax.nn.sigmoid(dg) * y).astype(o_ref.dtype)
            s_s[sq, h] = s
    sn_ref[...] = s_s[...]


def _retention(rq, rk, rv, dg, s0, gn, *, c, v_col, g_col, spb, cps=1, layer=None):
    t = rq.shape[0]
    st_shape = s0.shape if layer is None else s0.shape[1:]
    B = st_shape[0]
    nch = t // (B * c * cps)
    assert spb == 1 or nch == 1
    W = N_HEADS_D * QK_DIM_D
    R = spb * cps * c
    st = pl.BlockSpec((spb, N_HEADS_D, QK_DIM_D, V_DIM_D), lambda b, i: (b, 0, 0, 0))
    st_in = st if layer is None else pl.BlockSpec((None, spb, N_HEADS_D, QK_DIM_D, V_DIM_D),
                                                  lambda b, i: (layer, b, 0, 0, 0))
    return pl.pallas_call(
        functools.partial(_ret_kernel, c=c, spb=spb, cps=cps),
        grid=(B // spb, nch),
        in_specs=[pl.BlockSpec((R, W), lambda b, i: (b * nch + i, 0)),
                  pl.BlockSpec((R, W), lambda b, i: (b * nch + i, 0)),
                  pl.BlockSpec((R, BRANCH_W), lambda b, i: (b * nch + i, v_col)),
                  pl.BlockSpec((R, BRANCH_W), lambda b, i: (b * nch + i, g_col)),
                  st_in, pl.BlockSpec((1, BRANCH_W), lambda b, i: (0, 0))],
        out_specs=[pl.BlockSpec((R, BRANCH_W), lambda b, i: (b * nch + i, 0)), st],
        out_shape=[jax.ShapeDtypeStruct((t, BRANCH_W), BF16), jax.ShapeDtypeStruct(st_shape, F32)],
        scratch_shapes=[pltpu.VMEM((spb, N_HEADS_D, QK_DIM_D, V_DIM_D), F32)],
        compiler_params=_cparams("parallel", "arbitrary"),
    )(rq, rk, rv, dg, s0, gn)


def _sortable(x):
    b = lax.bitcast_convert_type(x, I32)
    return b ^ ((b >> 31) & 0x7FFFFFFF)


def _count(m):
    return jnp.sum(jnp.where(m, 1.0, 0.0), axis=1, keepdims=True)


def _kth_largest_key(keys, kf, bits):
    m = keys.shape[0]
    t0 = jnp.where(_count(keys >= 0) >= kf, 0, INT_MIN).astype(I32)

    def step(nb, shift, t):
        digit = jnp.zeros((m, 1), I32)
        for v in range(1, 2 ** nb):
            cand = t | jnp.left_shift(jnp.int32(v), shift)
            digit = digit + jnp.where(_count(keys >= cand) >= kf, 1, 0)
        return t | jnp.left_shift(digit, shift)

    nfull, rem = divmod(31, bits)
    t = lax.fori_loop(0, nfull, lambda i, t: step(bits, 31 - bits * (i + 1), t), t0)
    return step(rem, 0, t) if rem else t


def _topk_mask(keys, valid, idx, k, bits):
    m, n = keys.shape
    kf = float(k)
    thr = _kth_largest_key(keys, kf, bits)
    gt = keys > thr
    eq = (keys == thr) & valid
    need = kf - _count(gt)
    excess = jnp.max(_count(eq) - need) > 0.0
    nbits = max(1, (n - 1).bit_length())

    def first_ties():
        def idx_step(i, j):
            cand = j | jnp.left_shift(jnp.int32(1), nbits - 1 - i)
            return jnp.where(_count(eq & (idx < cand)) <= need - 1.0, cand, j)

        return lax.fori_loop(0, nbits, idx_step, jnp.zeros((m, 1), I32))

    jthr = lax.cond(excess, first_ties, lambda: jnp.full((m, 1), 2 ** nbits, I32))
    return (gt & valid) | (eq & (idx <= jthr))


Q_DIGITS = (7, 8, 8)
Q_MAX = float(2 ** sum(Q_DIGITS) - 1)


def _count_ge(b, cand):
    m, n = b.shape
    ct = jnp.broadcast_to(cand, (m, LANES)).astype(b.dtype)
    one, zero = jnp.ones((), b.dtype), jnp.zeros((), b.dtype)
    acc = jnp.where(b[:, 0:LANES] >= ct, one, zero)
    for c in range(1, n // LANES):
        acc = acc + jnp.where(b[:, c * LANES:(c + 1) * LANES] >= ct, one, zero)
    return jnp.sum(acc.astype(F32), axis=1, keepdims=True)


def _digit_search(b, kf, nbits):
    def step(i, t):
        cand = t | jnp.left_shift(jnp.int32(1), nbits - 1 - i)
        return jnp.where(_count_ge(b, cand.astype(F32)) >= kf, cand, t)

    return lax.fori_loop(0, nbits, step, jnp.zeros((b.shape[0], 1), I32)).astype(F32)


def _topk_mask_fast(score, valid, idx, k, digit_dtype):
    kf = float(k)
    lo = jnp.min(jnp.where(valid, score, jnp.inf), axis=1, keepdims=True)
    hi = jnp.max(jnp.where(valid, score, -jnp.inf), axis=1, keepdims=True)
    scale = jnp.where(hi > lo, Q_MAX / (hi - lo), 0.0)
    q = jnp.where(valid, jnp.minimum(jnp.floor((score - lo) * scale), Q_MAX), -1.0)
    w1, w0 = float(2 ** Q_DIGITS[2]), float(2 ** (Q_DIGITS[1] + Q_DIGITS[2]))
    d2 = jnp.floor(q * (1.0 / w0))
    r = q - d2 * w0
    d1 = jnp.floor(r * (1.0 / w1))
    d0 = r - d1 * w1
    b2 = d2.astype(digit_dtype)
    t2 = _digit_search(b2, kf, Q_DIGITS[0])
    k1 = kf - _count_ge(b2, t2 + 1.0)
    in2 = d2 == t2
    b1 = jnp.where(in2, d1, -1.0).astype(digit_dtype)
    t1 = _digit_search(b1, k1, Q_DIGITS[1])
    k0 = k1 - _count_ge(b1, t1 + 1.0)
    b0 = jnp.where(in2 & (d1 == t1), d0, -1.0).astype(digit_dtype)
    t0 = _digit_search(b0, k0, Q_DIGITS[2])
    thr = t2 * w0 + t1 * w1 + t0
    ambiguous = jnp.max(_count_ge(b0, t0) - k0) > 0.0
    fast = jnp.where(q >= thr, 1, 0).astype(I32)

    def exact():
        keys = _sortable(jnp.where(valid, score, -jnp.inf))
        return jnp.where(_topk_mask(keys, valid, idx, k, 1), 1, 0).astype(I32)

    return lax.cond(ambiguous, exact, lambda: fast) > 0


def _masked_attention(s, mask, v):
    s = jnp.where(mask, s, -jnp.inf)
    p = jnp.exp2(s - jnp.max(s, axis=1, keepdims=True))
    return _dot(p.astype(BF16), v) / jnp.sum(p, axis=1, keepdims=True)


def _dsa_prompt_kernel(q_ref, kb_ref, vb_ref, iq4_ref, ik4_ref, iw_ref, *rest, j0, topk):
    o_ref = rest[-1]
    nq = q_ref.shape[0]
    nk = kb_ref.shape[0]
    t0 = (j0 + pl.program_id(1)) * nq
    ik4 = ik4_ref[...]
    score = jnp.zeros((nq, nk), F32)
    for h in range(N_IDX_HEADS):
        s = _dot_nt(iq4_ref[:, h * IDX4:(h + 1) * IDX4], ik4)
        score = score + jnp.maximum(s, 0.0) * iw_ref[:, IDX_DIM + h:IDX_DIM + h + 1]
    score = score * IDX_SCALE
    kpos = lax.broadcasted_iota(I32, (nq, nk), 1)
    valid = kpos <= t0 + lax.broadcasted_iota(I32, (nq, nk), 0)
    if nk > topk and j0 * nq + 1 >= topk:
        mask = _topk_mask_fast(score, valid, kpos, topk, BF16)
    elif nk > topk:
        mask = _topk_mask(_sortable(jnp.where(valid, score, -jnp.inf)), valid, kpos, topk, 1)
    else:
        mask = valid
    for g in range(N_KV_B):
        gs = slice(g * HEAD_DIM_B, (g + 1) * HEAD_DIM_B)
        kg = kb_ref[:, gs]
        vg = vb_ref[:, gs]
        for r in range(KV_REP):
            hs = slice((g * KV_REP + r) * HEAD_DIM_B, (g * KV_REP + r + 1) * HEAD_DIM_B)
            o_ref[:, hs] = _masked_attention(_dot_nt(q_ref[:, hs], kg), mask, vg).astype(o_ref.dtype)


def _dsa_prompt(q, kb, vb, iq4, ik4, ikw, *, n_seq, n_cls, qb):
    t = q.shape[0]
    s_len = t // n_seq
    nb = s_len // qb
    nbc = nb // n_cls
    topk = min(TOPK_MAX, s_len // 4)
    q, kb, vb, iq4, ik4, ikw = (a.reshape(n_seq, s_len, a.shape[-1]) for a in (q, kb, vb, iq4, ik4, ikw))
    out = None
    for c in range(n_cls):
        nk = (c + 1) * nbc * qb
        qrow = lambda w, c=c: pl.BlockSpec((None, qb, w), lambda b, j: (b, c * nbc + j, 0))
        seq = lambda w, nk=nk: pl.BlockSpec((None, nk, w), lambda b, j: (b, 0, 0))
        prev = [] if out is None else [out]
        out = pl.pallas_call(
            functools.partial(_dsa_prompt_kernel, j0=c * nbc, topk=topk),
            grid=(n_seq, nbc),
            in_specs=[qrow(BRANCH_W), seq(N_KV_B * HEAD_DIM_B), seq(N_KV_B * HEAD_DIM_B), qrow(N_IDX_HEADS * IDX4),
                      seq(IDX4), qrow(LANES)] + [pl.BlockSpec(memory_space=pl.ANY)] * len(prev),
            out_specs=qrow(BRANCH_W),
            out_shape=jax.ShapeDtypeStruct((n_seq, s_len, BRANCH_W), BF16),
            input_output_aliases={6: 0} if prev else {},
            compiler_params=_cparams("parallel", "arbitrary"),
        )(q, kb, vb, iq4, ik4, ikw, *prev)
    return out.reshape(t, BRANCH_W)


def _dsa_sample_select_kernel(pt_ref, iq4_ref, w_ref, ikn_ref, *rest, n_pages, t_new, spb):
    del pt_ref
    idx_refs, (m_ref, sc_s) = rest[:n_pages * spb], rest[n_pages * spb:]
    P = PAGE_SIZE
    pad = P - t_new
    nk = (n_pages + 1) * P
    past = n_pages * P
    for s in range(spb):
        iq4 = iq4_ref[s]
        w = w_ref[s]

        def page_scores(ik, page, keys_on_lanes):
            hi, lo = _split_bf16(ik)
            if keys_on_lanes:
                sc = _dot(iq4, jnp.concatenate([hi, lo, hi, lo], axis=0))
            else:
                sc = _dot_nt(iq4, jnp.concatenate([hi, lo, hi, lo], axis=1))
            r = jnp.maximum(sc, 0.0) * w
            acc = r[0:t_new]
            for h in range(1, N_IDX_HEADS):
                acc = acc + r[h * t_new:(h + 1) * t_new]
            sc_s[s * t_new:(s + 1) * t_new, page * P:(page + 1) * P] = acc * IDX_SCALE

        for p in range(n_pages):
            page_scores(idx_refs[s * n_pages + p][0, 0], p, True)
        page_scores(jnp.concatenate([ikn_ref[s], jnp.zeros((pad, IDX_DIM), F32)], axis=0), n_pages, False)

    rows = spb * t_new
    kpos = lax.broadcasted_iota(I32, (rows, nk), 1)
    valid = kpos <= past + (lax.broadcasted_iota(I32, (rows, nk), 0) & (t_new - 1))
    keys = _sortable(jnp.where(valid, sc_s[...], -jnp.inf))
    mask = _topk_mask(keys, valid, kpos, min(TOPK_MAX, (past + t_new) // 4), 2)
    m_ref[...] = jnp.where(mask, 1, 0).astype(I32)


def _dsa_sample_attend_kernel(pt_ref, qg_ref, kn_ref, vn_ref, m_ref, *rest, n_pages, t_new, spb):
    del pt_ref
    n_in = 2 * n_pages * spb
    page_refs, (o_ref, k_s, v_s) = rest[:n_in], rest[n_in:]
    P = PAGE_SIZE
    pad = P - t_new
    nk = (n_pages + 1) * P
    past = n_pages * P
    zkv = jnp.zeros((pad, N_KV_B * HEAD_DIM_B), F32)
    for s in range(spb):
        k_refs = page_refs[2 * n_pages * s:2 * n_pages * s + n_pages]
        v_refs = page_refs[2 * n_pages * s + n_pages:2 * n_pages * (s + 1)]
        for p in range(n_pages):
            for g in range(N_KV_B):
                gs = slice(g * HEAD_DIM_B, (g + 1) * HEAD_DIM_B)
                k_s[s, p * P:(p + 1) * P, gs] = k_refs[p][0, 0, pl.ds(g, P, stride=N_KV_B), :].astype(BF16)
                v_s[s, p * P:(p + 1) * P, gs] = v_refs[p][0, 0, pl.ds(g, P, stride=N_KV_B), :].astype(BF16)
        k_s[s, past:nk, :] = jnp.concatenate([kn_ref[s], zkv], axis=0).astype(BF16)
        v_s[s, past:nk, :] = jnp.concatenate([vn_ref[s], zkv], axis=0).astype(BF16)
        m_s = jnp.concatenate([m_ref[s * t_new:(s + 1) * t_new, :]] * KV_REP, axis=0) > 0
        for g in range(N_KV_B):
            gs = slice(g * HEAD_DIM_B, (g + 1) * HEAD_DIM_B)
            o_ref[s, g] = _masked_attention(_dot_nt(qg_ref[s, g], k_s[s, :, gs]), m_s, v_s[s, :, gs]).astype(o_ref.dtype)


def _dsa_sample(qg, iq4r, wr, k_new, v_new, ik_new, cache_k, cache_v, cache_idx_k, page_table, layer, spb, spb_sel):
    B, _, rows, hd = qg.shape
    t_new = k_new.shape[1]
    assert t_new & (t_new - 1) == 0 and B % spb == 0 and B % spb_sel == 0
    n_pages = page_table.shape[1]
    kvw = N_KV_B * HEAD_DIM_B
    nk = (n_pages + 1) * PAGE_SIZE

    def per_seq(n, shape):
        return pl.BlockSpec((n,) + shape, lambda b, pt: (b,) + (0,) * len(shape))

    def page_spec(n, rows_, width, s, p):
        return pl.BlockSpec((1, 1, rows_, width), lambda b, pt: (layer, pt[b * n + s, p], 0, 0))

    sel_specs = [per_seq(spb_sel, iq4r.shape[1:]), per_seq(spb_sel, wr.shape[1:]), per_seq(spb_sel, (t_new, IDX_DIM))]
    sel_specs += [page_spec(spb_sel, IDX_DIM, PAGE_SIZE, s, p) for s in range(spb_sel) for p in range(n_pages)]
    mask = pl.pallas_call(
        functools.partial(_dsa_sample_select_kernel, n_pages=n_pages, t_new=t_new, spb=spb_sel),
        grid_spec=pltpu.PrefetchScalarGridSpec(
            num_scalar_prefetch=1, grid=(B // spb_sel,), in_specs=sel_specs,
            out_specs=pl.BlockSpec((spb_sel * t_new, nk), lambda b, pt: (b, 0)),
            scratch_shapes=[pltpu.VMEM((spb_sel * t_new, nk), F32)]),
        out_shape=jax.ShapeDtypeStruct((B * t_new, nk), I32),
        compiler_params=_cparams("arbitrary"),
    )(page_table, iq4r, wr, ik_new, *([cache_idx_k] * (n_pages * spb_sel)))

    att_specs = [per_seq(spb, qg.shape[1:]), per_seq(spb, (t_new, kvw)), per_seq(spb, (t_new, kvw)),
                 pl.BlockSpec((spb * t_new, nk), lambda b, pt: (b, 0))]
    pages = []
    for s in range(spb):
        att_specs += [page_spec(spb, PAGE_SIZE * N_KV_B, HEAD_DIM_B, s, p) for p in range(n_pages)] * 2
        pages += [cache_k] * n_pages + [cache_v] * n_pages
    return pl.pallas_call(
        functools.partial(_dsa_sample_attend_kernel, n_pages=n_pages, t_new=t_new, spb=spb),
        grid_spec=pltpu.PrefetchScalarGridSpec(
            num_scalar_prefetch=1, grid=(B // spb,), in_specs=att_specs,
            out_specs=pl.BlockSpec((spb,) + qg.shape[1:], lambda b, pt: (b, 0, 0, 0)),
            scratch_shapes=[pltpu.VMEM((spb, nk, kvw), BF16), pltpu.VMEM((spb, nk, kvw), BF16)]),
        out_shape=jax.ShapeDtypeStruct(qg.shape, BF16),
        compiler_params=_cparams("arbitrary"),
    )(page_table, qg, k_new, v_new, mask, *pages)


def _merge_kernel(xn_ref, a_ref, b_ref, c_ref, d_ref, wg0, wg1, wg2, wg3, bg0, bg1, bg2, bg3, wb_ref, o_ref):
    xn = xn_ref[...]
    acc = None
    for n, (br, wg, bg) in enumerate(zip((a_ref, b_ref, c_ref, d_ref), (wg0, wg1, wg2, wg3), (bg0, bg1, bg2, bg3))):
        term = jax.nn.sigmoid(_dot(xn, wg[...]) + bg[...]) * _dot(br[...], wb_ref[n])
        acc = term if acc is None else acc + term
    o_ref[...] = acc.astype(o_ref.dtype)


def _merge(xn, branches, w_gate, b_gate, w_branch, l, tm, tn):
    t, d = xn.shape
    nj = d // tn
    row = lambda w: pl.BlockSpec((tm, w), lambda i, j: (i, 0))
    wg = [pl.BlockSpec((None, d, tn), functools.partial(lambda i, j, n: (l, 0, n * nj + j), n=n))
          for n in range(N_BRANCH)]
    bg = [pl.BlockSpec((1, tn), functools.partial(lambda i, j, n: (0, n * nj + j), n=n)) for n in range(N_BRANCH)]
    return pl.pallas_call(
        _merge_kernel,
        grid=(t // tm, nj),
        in_specs=[row(d)] + [row(BRANCH_W)] * N_BRANCH + wg + bg
                 + [pl.BlockSpec((None, N_BRANCH, BRANCH_W, tn), lambda i, j: (l, 0, 0, j))],
        out_specs=pl.BlockSpec((tm, tn), lambda i, j: (i, j)),
        out_shape=jax.ShapeDtypeStruct((t, d), BF16),
        compiler_params=_cparams("parallel", "arbitrary"),
    )(xn, *branches, *([w_gate] * N_BRANCH), *([b_gate] * N_BRANCH), w_branch)


def _outproj_kernel(m_ref, x_ref, w_ref, g_ref, o_ref):
    o_ref[...] = x_ref[...] + _rms(_dot(m_ref[...], w_ref[...]), g_ref[...])


def _outproj(merged, x, w_o, g, l, tm):
    t, d = x.shape
    return pl.pallas_call(
        _outproj_kernel,
        grid=(t // tm,),
        in_specs=[pl.BlockSpec((tm, d), lambda i: (i, 0)), pl.BlockSpec((tm, d), lambda i: (i, 0)),
                  pl.BlockSpec((None, d, d), lambda i: (l, 0, 0)), pl.BlockSpec((1, d), lambda i: (0, 0))],
        out_specs=pl.BlockSpec((tm, d), lambda i: (i, 0)),
        out_shape=jax.ShapeDtypeStruct((t, d), F32),
        compiler_params=_cparams("parallel"),
    )(merged, x, w_o, g)


def _ffn_kernel(x_ref, gpre_ref, wug_ref, wuv_ref, wcg_ref, wcv_ref, bcg_ref, bcv_ref, wd_ref, gpost_ref,
                stg_ref, stv_ref, o_ref, ng_ref, nv_ref, h_s, acc_s, cg_s, cv_s, ext_s, *, S, tiles_per_group):
    i = pl.program_id(0)
    j = pl.program_id(1)
    tm = x_ref.shape[0]
    CR = cg_s.shape[1]

    @pl.when(j == 0)
    def _():
        h_s[...] = _rms(x_ref[...], gpre_ref[...]).astype(BF16)
        acc_s[...] = jnp.zeros_like(acc_s)

    @pl.when(i % tiles_per_group == 0)
    def _():
        cg_s[j] = stg_ref[0]
        cv_s[j] = stv_ref[0]

    h = h_s[...]

    def conv(w_ref, wc_ref, bc_ref, c_s, new_ref):
        u = _dot(h, w_ref[...])
        ext_s[0:CR, :] = c_s[j]
        ext_s[CR:CR + tm, :] = u
        y = bc_ref[...] + u * wc_ref[2:3, :]
        y = y + ext_s[CR - S:CR - S + tm, :] * wc_ref[1:2, :]
        y = y + ext_s[CR - 2 * S:CR - 2 * S + tm, :] * wc_ref[0:1, :]
        tail = ext_s[tm:tm + CR, :]
        c_s[j] = tail
        new_ref[0] = tail
        return y

    gate = conv(wug_ref, wcg_ref, bcg_ref, cg_s, ng_ref)
    val = conv(wuv_ref, wcv_ref, bcv_ref, cv_s, nv_ref)
    acc_s[...] += _dot((_gelu_tanh(gate) * val).astype(BF16), wd_ref[...])

    @pl.when(j == pl.num_programs(1) - 1)
    def _():
        o_ref[...] = x_ref[...] + _rms(acc_s[...], gpost_ref[...])


def _ffn(x, lp, state, *, S, tm, tn):
    t, d = x.shape
    G, CR, _ = state.shape
    nj = D_FF // tn
    ni = t // tm
    tiles_per_group = ni // G
    grp = lambda i, j: (i // tiles_per_group, 0, j)
    tile = lambda i, j: (i, 0, j)
    grp_v = lambda i, j: (i // tiles_per_group, 0, nj + j)
    col = lambda r: pl.BlockSpec((r, tn), lambda i, j: (0, j))
    col_v = lambda r: pl.BlockSpec((r, tn), lambda i, j: (0, nj + j))
    vec = pl.BlockSpec((1, d), lambda i, j: (0, 0))
    l = lp["layer"]
    xo, ng, nv = pl.pallas_call(
        functools.partial(_ffn_kernel, S=S, tiles_per_group=tiles_per_group),
        grid=(ni, nj),
        in_specs=[pl.BlockSpec((tm, d), lambda i, j: (i, 0)), vec,
                  pl.BlockSpec((None, d, tn), lambda i, j: (l, 0, j)),
                  pl.BlockSpec((None, d, tn), lambda i, j: (l, 0, nj + j)),
                  col(FFN_CONV_W), col_v(FFN_CONV_W), col(1), col_v(1),
                  pl.BlockSpec((None, tn, d), lambda i, j: (l, j, 0)), vec,
                  pl.BlockSpec((1, CR, tn), grp), pl.BlockSpec((1, CR, tn), grp_v)],
        out_specs=[pl.BlockSpec((tm, d), lambda i, j: (i, 0)),
                   pl.BlockSpec((1, CR, tn), tile), pl.BlockSpec((1, CR, tn), tile)],
        out_shape=[jax.ShapeDtypeStruct((t, d), F32), jax.ShapeDtypeStruct((ni, CR, D_FF), F32),
                   jax.ShapeDtypeStruct((ni, CR, D_FF), F32)],
        scratch_shapes=[pltpu.VMEM((tm, d), BF16), pltpu.VMEM((tm, d), F32), pltpu.VMEM((nj, CR, tn), F32),
                        pltpu.VMEM((nj, CR, tn), F32), pltpu.VMEM((CR + tm, tn), F32)],
        compiler_params=_cparams("arbitrary", "arbitrary"),
    )(x, lp["g_ffn_pre"], lp["w_up"], lp["w_up"], lp["w_ffn_conv"], lp["w_ffn_conv"], lp["b_ffn_conv"],
      lp["b_ffn_conv"], lp["w_down"], lp["g_ffn_post"], state, state)
    last = slice(tiles_per_group - 1, None, tiles_per_group)
    return xo, ng[last], nv[last]


def _layer_front(x, lp, tabs, kv_final=None):
    l = lp["layer"]
    u_main, xn = _inproj(x, lp["g_mix_pre"], lp["w_in_main"], l, TM_INPROJ, TN_INPROJ)
    u_idx = _inproj_split(x, lp["g_mix_pre"], lp["w_in_idx3"], l, TM_IDX, N_IDX)
    return u_main, xn, _rope_prep(u_main, u_idx, tabs, ROPE_ROWS, kv_final)


def _layer_back(x, xn, branches, lp, ffn_state, *, S):
    l = lp["layer"]
    merged = _merge(xn, branches, lp["w_gate"], lp["b_gate"], lp["w_branch"], l, TM_DENSE, TN_MERGE)
    x1 = _outproj(merged, x, lp["w_o"], lp["g_mix_post"], l, TM_DENSE)
    return _ffn(x1, lp, ffn_state, S=S, tm=TM_DENSE, tn=TN_FFN[S])


def _prep_big_weights(p):
    w_in = jnp.swapaxes(p["w_in"], 1, 2)
    o_iq = 3072
    o_cx = o_iq + N_IDX_HEADS * IDX_DIM + IDX_DIM + N_IDX_HEADS
    w_idx = jnp.pad(w_in[:, o_iq:o_cx], ((0, 0), (0, N_IDX - (o_cx - o_iq)), (0, 0)))
    hi = w_idx.astype(BF16)
    lo = (w_idx - hi.astype(F32)).astype(BF16)
    return dict(
        w_in_main=jnp.concatenate([w_in[:, :o_iq], w_in[:, o_cx:]], axis=1).astype(BF16),
        w_in_idx3=jnp.concatenate([hi, lo, hi], axis=2),
        w_branch=p["w_branch"].astype(BF16), w_gate=p["w_gate"].astype(BF16), w_o=p["w_o"].astype(BF16),
        w_up=p["w_up"].astype(BF16), w_down=p["w_down"].astype(BF16))


def _prep_layer_params(p, big, l):
    row = lambda a: a[l][None, :]
    return dict(
        big, layer=l,
        g_mix_pre=row(p["g_mix_pre"]),
        w_pool=p["w_pool"][l].astype(BF16), pool_scale=row(p["pool_scale"]),
        w_conv_c=p["w_conv_c"][l], b_conv_c=row(p["b_conv_c"]),
        w_rg_a=p["w_rg_a"][l].astype(BF16), b_rg_a=row(p["b_rg_a"]),
        w_rg_x=p["w_rg_x"][l].astype(BF16), b_rg_x=row(p["b_rg_x"]),
        lru_lambda=row(p["lru_lambda"]), ret_gn=row(p["ret_gn"]), b_gate=row(p["b_gate"]),
        g_mix_post=row(p["g_mix_post"]), g_ffn_pre=row(p["g_ffn_pre"]),
        w_ffn_conv=p["w_ffn_conv"][l], b_ffn_conv=row(p["b_ffn_conv"]), g_ffn_post=row(p["g_ffn_post"]))


def _to_time_major(a, nb, nt):
    return jnp.swapaxes(a, 0, 1).reshape((nt * nb,) + a.shape[2:])


def _to_seq_major(a, nb, nt):
    return jnp.swapaxes(a.reshape((nt, nb) + a.shape[1:]), 0, 1)


def kernel(x_prompt, x_sample, cache_k, cache_v, cache_idx_k, state_pool, state_conv, state_rglru, state_ret, state_ffn_conv, page_table, g_mix_pre, w_in, w_pool, pool_scale, w_conv_c, b_conv_c, w_rg_a, b_rg_a, w_rg_x, b_rg_x, lru_lambda, ret_gn, w_branch, w_gate, b_gate, w_o, g_mix_post, g_ffn_pre, w_up, w_ffn_conv, b_ffn_conv, w_down, g_ffn_post):
    params = dict(g_mix_pre=g_mix_pre, w_in=w_in, w_pool=w_pool, pool_scale=pool_scale, w_conv_c=w_conv_c,
                  b_conv_c=b_conv_c, w_rg_a=w_rg_a, b_rg_a=b_rg_a, w_rg_x=w_rg_x, b_rg_x=b_rg_x,
                  lru_lambda=lru_lambda, ret_gn=ret_gn, w_branch=w_branch, w_gate=w_gate, b_gate=b_gate, w_o=w_o,
                  g_mix_post=g_mix_post, g_ffn_pre=g_ffn_pre, w_up=w_up, w_ffn_conv=w_ffn_conv,
                  b_ffn_conv=b_ffn_conv, w_down=w_down, g_ffn_post=g_ffn_post)
    depth = w_in.shape[0]
    bp, seq, d = x_prompt.shape
    bs, tdec, _ = x_sample.shape
    past = page_table.shape[1] * PAGE_SIZE
    n_pool = cache_k.shape[1]
    ck = cache_k.reshape(depth, n_pool, PAGE_SIZE * N_KV_B, HEAD_DIM_B)
    cv = cache_v.reshape(depth, n_pool, PAGE_SIZE * N_KV_B, HEAD_DIM_B)
    cik_t = jnp.swapaxes(cache_idx_k, 2, 3)
    big = _prep_big_weights(params)

    tabs_p = _rope_tables(jnp.arange(seq, dtype=F32))
    tabs_s = _rope_tables(jnp.repeat(past + jnp.arange(tdec, dtype=F32), bs))
    xp = x_prompt.reshape(bp * seq, d)
    xs = _to_time_major(x_sample, bs, tdec)
    cr_p = max(SUBLANES, (FFN_CONV_W - 1))
    outs_p, outs_s = [], []
    kv_p = None
    for l in range(depth):
        lp = _prep_layer_params(params, big, l)
        u, xn, (q, _, kb, vb, iq4, ik4, ikw, rq, rk, *kv_p) = _layer_front(xp, lp, tabs_p, (l, depth, kv_p))
        o_a = _pool(u, jnp.zeros((bp, POOL_BUF, BRANCH_W), F32), lp["w_pool"], lp["pool_scale"], S=1, Tc=SEQ_CHUNK,
                    start=0)
        o_b = _dsa_prompt(q, kb, vb, iq4, ik4, ikw, n_seq=bp, n_cls=DSA_CLASSES, qb=DSA_QBLOCK)
        o_c, h_p = _rglru(u, jnp.zeros((bp, CONV_W - 1, LRU_W), F32), jnp.zeros((bp, 1, LRU_W), F32), lp, S=1,
                          Tc=SEQ_CHUNK)
        o_d, s_p = _retention(rq, rk, u, u, jnp.zeros((bp, N_HEADS_D, QK_DIM_D, V_DIM_D), F32), lp["ret_gn"],
                              c=RET_CHUNK if seq % RET_CHUNK == 0 else seq, v_col=6, g_col=7, spb=1,
                              cps=2 if seq % (2 * RET_CHUNK) == 0 else 1)
        xp, ng, nv = _layer_back(xp, xn, (o_a, o_b, o_c, o_d), lp, jnp.zeros((bp, cr_p, 2 * D_FF), F32), S=1)
        u3 = u.reshape(bp, seq, N_MAIN)
        outs_p.append((
            None, None,
            ikw.reshape(bp, seq, LANES)[:, :, :IDX_DIM], u3[:, seq - POOL_BUF:, 0:BRANCH_W],
            u3[:, seq - (CONV_W - 1):, 3072:3072 + LRU_W], h_p[:, 0], s_p,
            jnp.concatenate([ng, nv], axis=-1)[:, cr_p - (FFN_CONV_W - 1):]))
        u, xn, (q, k, kb, vb, iq4, ik4, ikw, rq, rk) = _layer_front(xs, lp, tabs_s)
        o_a = _pool(u, _to_time_major(state_pool[l], bs, POOL_BUF)[None], lp["w_pool"], lp["pool_scale"],
                    S=bs, Tc=tdec, start=past)
        o_c, h_s = _rglru(u, _to_time_major(state_conv[l], bs, CONV_W - 1)[None], state_rglru[l][None], lp,
                          S=bs, Tc=tdec)
        sm = lambda a: _to_seq_major(a, bs, tdec)
        u_sm = sm(u)
        o_d, s_s = _retention(sm(rq).reshape(bs * tdec, -1), sm(rk).reshape(bs * tdec, -1),
                              u_sm[:, :, 6144:7168].reshape(bs * tdec, -1), u_sm[:, :, 7168:8192].reshape(bs * tdec, -1),
                              state_ret, lp["ret_gn"], c=tdec, v_col=0, g_col=0, spb=RET_SAMPLE_SPB, layer=l)
        o_d = _to_time_major(o_d.reshape(bs, tdec, BRANCH_W), bs, tdec)
        qg = sm(q).reshape(bs, tdec, N_KV_B, KV_REP, HEAD_DIM_B).transpose(0, 2, 3, 1, 4)
        qg = qg.reshape(bs, N_KV_B, KV_REP * tdec, HEAD_DIM_B)
        iq4r = sm(iq4).reshape(bs, tdec, N_IDX_HEADS, IDX4).transpose(0, 2, 1, 3).reshape(bs, N_IDX_HEADS * tdec, IDX4)
        ikw_sm = sm(ikw)
        wr = ikw_sm[:, :, IDX_DIM:IDX_DIM + N_IDX_HEADS].transpose(0, 2, 1).reshape(bs, N_IDX_HEADS * tdec, 1)
        k_sm = sm(k)
        v_sm = u_sm[:, :, 2560:3072]
        o_b = _dsa_sample(qg, iq4r, wr, k_sm, v_sm, ikw_sm[:, :, :IDX_DIM], ck, cv, cik_t, page_table, l,
                          DSA_SAMPLE_SPB, DSA_SELECT_SPB)
        o_b = o_b.reshape(bs, N_KV_B, KV_REP, tdec, HEAD_DIM_B).transpose(3, 0, 1, 2, 4).reshape(tdec * bs, BRANCH_W)
        ffn_state = _to_time_major(state_ffn_conv[l], bs, FFN_CONV_W - 1)[None]
        xs, ng, nv = _layer_back(xs, xn, (o_a, o_b, o_c, o_d), lp, ffn_state, S=bs)
        ffn_new = _to_seq_major(jnp.concatenate([ng, nv], axis=-1)[0], bs, FFN_CONV_W - 1)
        a_in = u_sm[:, :, 0:BRANCH_W]
        c_x = u_sm[:, :, 3072:3072 + LRU_W]
        outs_s.append((
            k_sm.reshape(bs, tdec, N_KV_B, HEAD_DIM_B), v_sm.reshape(bs, tdec, N_KV_B, HEAD_DIM_B),
            ikw_sm[:, :, :IDX_DIM],
            jnp.concatenate([state_pool[l], a_in], axis=1)[:, tdec:],
            jnp.concatenate([state_conv[l], c_x], axis=1)[:, tdec:],
            h_s[0], s_s, ffn_new))

    stk = lambda outs, i: jnp.stack([o[i] for o in outs], axis=0)
    res = [xp.reshape(bp, seq, d), _to_seq_major(xs, bs, tdec)]
    for i in range(8):
        p_i = kv_p[i].reshape(depth, bp, seq, N_KV_B, HEAD_DIM_B) if i < 2 else stk(outs_p, i)
        res += [p_i, stk(outs_s, i)]
    return tuple(res)
```

```python
import functools
import math

import jax
import jax.numpy as jnp
from jax import lax
from jax.experimental import pallas as pl
from jax.experimental.pallas import tpu as pltpu

F32 = jnp.float32
BF16 = jnp.bfloat16
I32 = jnp.int32

SUBLANES = 8
LANES = 128
VMEM_LIMIT_BYTES = 56 * 1024 * 1024

TM_INPROJ, TN_INPROJ = 1024, 1024
TM_IDX = 512
TM_DENSE = 512
TN_MERGE = 512
TN_FFN = {1: 512, 128: 256}
ROPE_ROWS = 256
SEQ_CHUNK = 512
DSA_QBLOCK, DSA_CLASSES = 256, 8
DSA_SAMPLE_SPB = 2
DSA_SELECT_SPB = 16
RET_SAMPLE_SPB = 8

D_MODEL = 2048
BRANCH_W = D_MODEL // 2
N_BRANCH = 4
POOL_WINDOWS = (2, 4, 8, 16)
POOL_GROUP = BRANCH_W // len(POOL_WINDOWS)
POOL_BUF = max(POOL_WINDOWS) - 1
N_HEADS_B = 8
HEAD_DIM_B = BRANCH_W // N_HEADS_B
N_KV_B = 4
KV_REP = N_HEADS_B // N_KV_B
N_IDX_HEADS = 16
IDX_DIM = 64
TOPK_MAX = 256
ROPE_THETA = 10000.0
ATT_SCALE = HEAD_DIM_B ** -0.5
Q_SCALE = ATT_SCALE * math.log2(math.e)
IDX_SCALE = (IDX_DIM * N_IDX_HEADS) ** -0.5
LRU_W = BRANCH_W
LRU_BLOCKS = 4
LRU_BLOCK = LRU_W // LRU_BLOCKS
LRU_C = 8.0
CONV_W = 4
N_HEADS_D = 8
QK_DIM_D = BRANCH_W // (2 * N_HEADS_D)
V_DIM_D = BRANCH_W // N_HEADS_D
RET_CHUNK = 128
D_FF = 11 * D_MODEL // 4
FFN_CONV_W = 3
EPS = 1e-6
PAGE_SIZE = 128

N_MAIN = 8192
N_IDX = 1152
IDX4 = 4 * IDX_DIM

LOG_G = tuple(math.log1p(-(2.0 ** (-5.0 - h))) for h in range(N_HEADS_D))
INT_MIN = -2 ** 31


def _round_up(n, m):
    return (n + m - 1) // m * m


def _cparams(*sem):
    return pltpu.CompilerParams(dimension_semantics=sem, vmem_limit_bytes=VMEM_LIMIT_BYTES)


def _dot(a, b):
    return jnp.dot(a, b, preferred_element_type=F32)


def _dot_nt(a, b):
    return lax.dot_general(a, b, (((1,), (1,)), ((), ())), preferred_element_type=F32)


def _dot_tn(a, b):
    return lax.dot_general(a, b, (((0,), (0,)), ((), ())), preferred_element_type=F32)


def _rms(x, g):
    return x * lax.rsqrt(jnp.mean(x * x, axis=-1, keepdims=True) + EPS) * g


def _gelu_tanh(x):
    return x * (0.5 * (1.0 + jnp.tanh(0.7978845608028654 * (x + 0.044715 * (x * x * x)))))


def _split_bf16(x):
    hi = x.astype(BF16)
    lo = (x - hi.astype(F32)).astype(BF16)
    return hi, lo


def _inproj_kernel(x_ref, g_ref, w_ref, u_ref, xn_ref):
    @pl.when(pl.program_id(1) == 0)
    def _():
        xn_ref[...] = _rms(x_ref[...], g_ref[...]).astype(BF16)

    u_ref[...] = _dot_nt(xn_ref[...], w_ref[...])


def _inproj(x, g, w, l, tm, tn):
    t, d = x.shape
    n = w.shape[1]
    return pl.pallas_call(
        _inproj_kernel,
        grid=(t // tm, n // tn),
        in_specs=[pl.BlockSpec((tm, d), lambda i, j: (i, 0)),
                  pl.BlockSpec((1, d), lambda i, j: (0, 0)),
                  pl.BlockSpec((None, tn, d), lambda i, j: (l, j, 0))],
        out_specs=[pl.BlockSpec((tm, tn), lambda i, j: (i, j)),
                   pl.BlockSpec((tm, d), lambda i, j: (i, 0))],
        out_shape=[jax.ShapeDtypeStruct((t, n), F32), jax.ShapeDtypeStruct((t, d), BF16)],
        compiler_params=_cparams("parallel", "arbitrary"),
    )(x, g, w)


def _inproj_split_kernel(x_ref, g_ref, w3_ref, u_ref, x3_s):
    d = x_ref.shape[1]

    @pl.when(pl.program_id(1) == 0)
    def _():
        hi, lo = _split_bf16(_rms(x_ref[...], g_ref[...]))
        x3_s[:, 0:d] = hi
        x3_s[:, d:2 * d] = hi
        x3_s[:, 2 * d:3 * d] = lo

    u_ref[...] = _dot_nt(x3_s[...], w3_ref[...])


def _inproj_split(x, g, w3, l, tm, tn):
    t, d = x.shape
    n = w3.shape[1]
    return pl.pallas_call(
        _inproj_split_kernel,
        grid=(t // tm, n // tn),
        in_specs=[pl.BlockSpec((tm, d), lambda i, j: (i, 0)),
                  pl.BlockSpec((1, d), lambda i, j: (0, 0)),
                  pl.BlockSpec((None, tn, 3 * d), lambda i, j: (l, j, 0))],
        out_specs=pl.BlockSpec((tm, tn), lambda i, j: (i, j)),
        out_shape=jax.ShapeDtypeStruct((t, n), F32),
        scratch_shapes=[pltpu.VMEM((tm, 3 * d), BF16)],
        compiler_params=_cparams("parallel", "arbitrary"),
    )(x, g, w3)


def _rope128(xs, cos, sin):
    return xs * cos + pltpu.roll(xs, HEAD_DIM_B // 2, axis=1) * sin


def _rope64(xs, cos, sin, first_half):
    rot = jnp.where(first_half, pltpu.roll(xs, LANES - IDX_DIM // 2, axis=1), pltpu.roll(xs, IDX_DIM // 2, axis=1))
    return xs * cos + rot * sin


def _rope_kernel(bq_ref, bk_ref, bv_ref, dq_ref, dk_ref, ui_ref, c128_ref, s128_ref, c64_ref, s64_ref,
                 *rest, n_alias):
    q_ref, k_ref, kb_ref, vb_ref, iq4_ref, ik4_ref, ikw_ref, rq_ref, rk_ref = rest[n_alias:n_alias + 9]
    final = rest[n_alias + 9:]
    rows = bq_ref.shape[0]
    c128, s128, c64, s64 = c128_ref[...], s128_ref[...], c64_ref[...], s64_ref[...]
    lane = lax.broadcasted_iota(I32, (rows, LANES), 1)
    first_half = (lane & (IDX_DIM // 2)) == 0
    low = lane < IDX_DIM
    for c in range(N_HEADS_B):
        sl = slice(c * LANES, (c + 1) * LANES)
        q_ref[:, sl] = (_rope128(bq_ref[:, sl], c128, s128) * Q_SCALE).astype(BF16)
    for c in range(N_KV_B):
        sl = slice(c * LANES, (c + 1) * LANES)
        kr = _rope128(bk_ref[:, sl], c128, s128)
        k_ref[:, sl] = kr
        kb_ref[:, sl] = kr.astype(BF16)
        if final:
            final[0][pl.ds(c, rows, stride=N_KV_B), :] = kr
            final[1][pl.ds(c, rows, stride=N_KV_B), :] = bv_ref[:, sl]
    vb_ref[...] = bv_ref[...].astype(BF16)
    for c in range(N_HEADS_D * QK_DIM_D // LANES):
        sl = slice(c * LANES, (c + 1) * LANES)
        rq_ref[:, sl] = _rope64(dq_ref[:, sl], c64, s64, first_half)
        rk_ref[:, sl] = _rope64(dk_ref[:, sl], c64, s64, first_half) * (QK_DIM_D ** -0.5)
    for c in range(N_IDX_HEADS // 2):
        y = _rope64(ui_ref[:, c * LANES:(c + 1) * LANES], c64, s64, first_half)
        yr = pltpu.roll(y, IDX_DIM, axis=1)
        for hh, dup in enumerate((jnp.where(low, y, yr), jnp.where(low, yr, y))):
            hi, lo = _split_bf16(dup)
            base = (2 * c + hh) * IDX4
            iq4_ref[:, base:base + LANES] = hi
            iq4_ref[:, base + LANES:base + 2 * LANES] = lo
    raw = ui_ref[:, N_IDX_HEADS * IDX_DIM:N_IDX_HEADS * IDX_DIM + LANES]
    y = _rope64(raw, c64, s64, first_half)
    ikw_ref[...] = jnp.where(low, y, raw)
    dup = jnp.where(low, y, pltpu.roll(y, IDX_DIM, axis=1))
    hi = dup.astype(BF16).astype(F32)
    hilo = jnp.where(low, hi, dup - hi).astype(BF16)
    ik4_ref[:, 0:LANES] = hilo
    ik4_ref[:, LANES:2 * LANES] = hilo


def _rope_prep(u_main, u_idx, tabs, tr, kv_final=None):
    t = u_main.shape[0]
    nt = tabs[0].shape[0] // tr
    row = lambda w, c: pl.BlockSpec((tr, w), lambda i: (i, c))
    tab = pl.BlockSpec((tr, LANES), lambda i: (i % nt, 0))
    outs = [(BRANCH_W, BF16), (N_KV_B * HEAD_DIM_B, F32), (N_KV_B * HEAD_DIM_B, BF16), (N_KV_B * HEAD_DIM_B, BF16),
            (N_IDX_HEADS * IDX4, BF16), (IDX4, BF16), (LANES, F32), (N_HEADS_D * QK_DIM_D, F32),
            (N_HEADS_D * QK_DIM_D, F32)]
    in_specs = [row(1024, 1), row(512, 4), row(512, 5), row(512, 10), row(512, 11), row(N_IDX, 0), tab, tab, tab, tab]
    args = [u_main, u_main, u_main, u_main, u_main, u_idx, *tabs]
    out_specs = [row(w, 0) for w, _ in outs]
    out_shape = [jax.ShapeDtypeStruct((t, w), dt) for w, dt in outs]
    aliases = {}
    if kv_final is not None:
        layer, depth, prev = kv_final
        out_specs += [pl.BlockSpec((None, tr * N_KV_B, HEAD_DIM_B), lambda i: (layer, i, 0))] * 2
        out_shape += [jax.ShapeDtypeStruct((depth, t * N_KV_B, HEAD_DIM_B), F32)] * 2
        if prev is not None:
            aliases = {len(args): len(outs), len(args) + 1: len(outs) + 1}
            in_specs += [pl.BlockSpec(memory_space=pl.ANY)] * 2
            args += list(prev)
    return pl.pallas_call(
        functools.partial(_rope_kernel, n_alias=len(aliases)),
        grid=(t // tr,),
        in_specs=in_specs,
        out_specs=out_specs,
        out_shape=out_shape,
        input_output_aliases=aliases,
        compiler_params=_cparams("parallel"),
    )(*args)


def _rope_tables(pos):
    def tab(half, reps):
        inv = jnp.exp(-math.log(ROPE_THETA) * jnp.arange(half, dtype=F32) / half)
        ang = pos[:, None] * inv[None, :]
        cos, sin = jnp.cos(ang), jnp.sin(ang)
        return jnp.tile(jnp.concatenate([cos, cos], 1), (1, reps)), jnp.tile(jnp.concatenate([-sin, sin], 1), (1, reps))

    c128, s128 = tab(HEAD_DIM_B // 2, 1)
    c64, s64 = tab(IDX_DIM // 2, 2)
    return c128, s128, c64, s64


def _pool_kernel(a_ref, prev_ref, w_ref, sc_ref, o_ref, ext_s, *, S, Tc, start, nch):
    ch = pl.program_id(1)
    R = Tc * S
    PS = POOL_BUF * S
    OFF = _round_up(PS, SUBLANES)

    @pl.when(ch == 0)
    def _():
        ext_s[OFF - PS:OFF, :] = prev_ref[0]

    x = a_ref[...]
    ext_s[OFF:OFF + R, :] = x
    t_loc = lax.broadcasted_iota(I32, (R, 1), 0) // S if S > 1 else lax.broadcasted_iota(I32, (R, 1), 0)
    pos1 = start + ch * Tc + t_loc + 1
    for gi, w in enumerate(POOL_WINDOWS):
        sl = slice(gi * POOL_GROUP, (gi + 1) * POOL_GROUP)
        xs = x[:, sl]
        acc = xs
        for j in range(1, w):
            acc = acc + ext_s[OFF - j * S:OFF - j * S + R, sl]
        cnt = jnp.minimum(w, pos1).astype(F32)
        mixed = acc / cnt - xs
        y = _dot(mixed.astype(BF16), w_ref[gi])
        o_ref[:, sl] = (y * sc_ref[:, sl]).astype(o_ref.dtype)
    if nch > 1:
        ext_s[OFF - PS:OFF, :] = ext_s[OFF + R - PS:OFF + R, :]


def _pool(u_main, prev, w_pool, scale, *, S, Tc, start):
    t = u_main.shape[0]
    R = Tc * S
    G = prev.shape[0]
    nch = t // (G * R)
    PS = POOL_BUF * S
    C = BRANCH_W
    return pl.pallas_call(
        functools.partial(_pool_kernel, S=S, Tc=Tc, start=start, nch=nch),
        grid=(G, nch),
        in_specs=[pl.BlockSpec((R, C), lambda g, c: (g * nch + c, 0)),
                  pl.BlockSpec((1, PS, C), lambda g, c: (g, 0, 0)),
                  pl.BlockSpec((len(POOL_WINDOWS), POOL_GROUP, POOL_GROUP), lambda g, c: (0, 0, 0)),
                  pl.BlockSpec((1, C), lambda g, c: (0, 0))],
        out_specs=pl.BlockSpec((R, C), lambda g, c: (g * nch + c, 0)),
        out_shape=jax.ShapeDtypeStruct((t, C), BF16),
        scratch_shapes=[pltpu.VMEM((_round_up(PS, SUBLANES) + R, C), F32)],
        compiler_params=_cparams("parallel", "arbitrary"),
    )(u_main, prev, w_pool, scale)


def _rglru_kernel(cx_ref, cg_ref, prev_ref, h0_ref, wc_ref, bc_ref, wa_ref, ba_ref, wx_ref, bx_ref, lam_ref,
                  o_ref, hl_ref, ext_s, a_s, b_s, h_s, *, S, Tc, nch):
    ch = pl.program_id(1)
    R = Tc * S
    PS = (CONV_W - 1) * S
    OFF = _round_up(PS, SUBLANES)

    @pl.when(ch == 0)
    def _():
        ext_s[OFF - PS:OFF, :] = prev_ref[0]
        h_s[...] = h0_ref[0]

    x = cx_ref[...]
    ext_s[OFF:OFF + R, :] = x
    xc = bc_ref[...] + x * wc_ref[CONV_W - 1:CONV_W, :]
    for j in range(CONV_W - 1):
        k = CONV_W - 1 - j
        xc = xc + ext_s[OFF - k * S:OFF - k * S + R, :] * wc_ref[j:j + 1, :]
    xb = xc.astype(BF16)
    for n in range(LRU_BLOCKS):
        sl = slice(n * LRU_BLOCK, (n + 1) * LRU_BLOCK)
        r = jax.nn.sigmoid(_dot(xb[:, sl], wa_ref[n]) + ba_ref[:, sl])
        i = jax.nn.sigmoid(_dot(xb[:, sl], wx_ref[n]) + bx_ref[:, sl])
        lam = lam_ref[:, sl]
        softplus_neg = jnp.maximum(-lam, 0.0) + jnp.log1p(jnp.exp(-jnp.abs(lam)))
        log_a = (-LRU_C) * r * softplus_neg
        a_s[:, sl] = jnp.exp(log_a)
        th = jnp.tanh(log_a)
        b_s[:, sl] = jnp.sqrt(-2.0 * th / (1.0 - th)) * (i * xc[:, sl])

    if S % SUBLANES == 0:
        def step(t, h):
            off = pl.multiple_of(t * S, S)
            h = a_s[pl.ds(off, S), :] * h + b_s[pl.ds(off, S), :]
            b_s[pl.ds(off, S), :] = h
            return h

        h = lax.fori_loop(0, Tc, step, h_s[...])
    else:
        assert S == 1 and Tc % SUBLANES == 0
        row =lax.broadcasted_iota(I32, (SUBLANES, a_s.shape[1]), 0)

        def tile_step(i, h):
            off = pl.multiple_of(i * SUBLANES, SUBLANES)
            A = a_s[pl.ds(off, SUBLANES), :]
            B = b_s[pl.ds(off, SUBLANES), :]
            for sh in (1, 2, 4):
                keep = row >= sh
                B = A * jnp.where(keep, pltpu.roll(B, sh, axis=0), 0.0) + B
                A = A * jnp.where(keep, pltpu.roll(A, sh, axis=0), 1.0)
            hs = A * h + B
            b_s[pl.ds(off, SUBLANES), :] = hs
            return hs[SUBLANES - 1:SUBLANES, :]

        h = lax.fori_loop(0, Tc // SUBLANES, tile_step, h_s[...])
    h_s[...] = h
    hl_ref[0] = h
    o_ref[...] = (_gelu_tanh(cg_ref[...]) * b_s[...]).astype(o_ref.dtype)
    if nch > 1:
        ext_s[OFF - PS:OFF, :] = ext_s[OFF + R - PS:OFF + R, :]


def _rglru(u_main, prev, h0, lp, *, S, Tc):
    t = u_main.shape[0]
    R = Tc * S
    G = prev.shape[0]
    nch = t // (G * R)
    PS = (CONV_W - 1) * S
    C = LRU_W
    vec = pl.BlockSpec((1, C), lambda g, c: (0, 0))
    blk = pl.BlockSpec((LRU_BLOCKS, LRU_BLOCK, LRU_BLOCK), lambda g, c: (0, 0, 0))
    return pl.pallas_call(
        functools.partial(_rglru_kernel, S=S, Tc=Tc, nch=nch),
        grid=(G, nch),
        in_specs=[pl.BlockSpec((R, C), lambda g, c: (g * nch + c, 3)),
                  pl.BlockSpec((R, C), lambda g, c: (g * nch + c, 4)),
                  pl.BlockSpec((1, PS, C), lambda g, c: (g, 0, 0)),
                  pl.BlockSpec((1, S, C), lambda g, c: (g, 0, 0)),
                  pl.BlockSpec((CONV_W, C), lambda g, c: (0, 0)), vec, blk, vec, blk, vec, vec],
        out_specs=[pl.BlockSpec((R, C), lambda g, c: (g * nch + c, 0)),
                   pl.BlockSpec((1, S, C), lambda g, c: (g, 0, 0))],
        out_shape=[jax.ShapeDtypeStruct((t, C), BF16), jax.ShapeDtypeStruct((G, S, C), F32)],
        scratch_shapes=[pltpu.VMEM((_round_up(PS, SUBLANES) + R, C), F32), pltpu.VMEM((R, C), F32),
                        pltpu.VMEM((R, C), F32), pltpu.VMEM((S, C), F32)],
        compiler_params=_cparams("parallel", "arbitrary"),
    )(u_main, u_main, prev, h0, lp["w_conv_c"], lp["b_conv_c"], lp["w_rg_a"], lp["b_rg_a"], lp["w_rg_x"],
      lp["b_rg_x"], lp["lru_lambda"])


def _ret_kernel(rq_ref, rk_ref, rv_ref, dg_ref, s0_ref, gn_ref, o_ref, sn_ref, s_s, *, c, spb, cps):
    @pl.when(pl.program_id(1) == 0)
    def _():
        s_s[...] = s0_ref[...]

    ii = lax.broadcasted_iota(I32, (c, c), 0)
    jj = lax.broadcasted_iota(I32, (c, c), 1)
    dif = (ii - jj).astype(F32)
    tpos = lax.broadcasted_iota(I32, (c, 1), 0).astype(F32)
    mm = BF16 if c % 16 == 0 else F32
    for h in range(N_HEADS_D):
        lg = LOG_G[h]
        qs = slice(h * QK_DIM_D, (h + 1) * QK_DIM_D)
        vs = slice(h * V_DIM_D, (h + 1) * V_DIM_D)
        intra = jnp.where(dif >= 0.0, jnp.exp(lg * jnp.maximum(dif, 0.0)), 0.0)
        q_dec = jnp.exp(lg * (tpos + 1.0))
        k_dec = jnp.exp(lg * (c - 1.0 - tpos))
        for sq in range(spb):
            s = s_s[sq, h]
            for sub in range(cps):
                rows = slice((sq * cps + sub) * c, (sq * cps + sub + 1) * c)
                q = rq_ref[rows, qs].astype(mm)
                k = rk_ref[rows, qs]
                v = rv_ref[rows, vs].astype(mm)
                att = _dot_nt(q, k.astype(mm)) * intra
                o = _dot(att.astype(mm), v) + _dot(q, s.astype(mm)) * q_dec
                s = s * math.exp(lg * c) + _dot_tn((k * k_dec).astype(mm), v)
                mu = jnp.mean(o, axis=-1, keepdims=True)
                var = jnp.mean(jnp.square(o - mu), axis=-1, keepdims=True)
                y = (o - mu) * lax.rsqrt(var + EPS) * gn_ref[:, vs]
                dg = dg_ref[rows, vs]
                o_ref[rows, vs] = (dg * jax.nn.sigmoid(dg) * y).astype(o_ref.dtype)
            s_s[sq, h] = s
    sn_ref[...] = s_s[...]


def _retention(rq, rk, rv, dg, s0, gn, *, c, v_col, g_col, spb, cps=1, layer=None):
    t = rq.shape[0]
    st_shape = s0.shape if layer is None else s0.shape[1:]
    B = st_shape[0]
    nch = t // (B * c * cps)
    assert spb == 1 or nch == 1
    W = N_HEADS_D * QK_DIM_D
    R = spb * cps * c
    st = pl.BlockSpec((spb, N_HEADS_D, QK_DIM_D, V_DIM_D), lambda b, i: (b, 0, 0, 0))
    st_in = st if layer is None else pl.BlockSpec((None, spb, N_HEADS_D, QK_DIM_D, V_DIM_D),
                                                  lambda b, i: (layer, b, 0, 0, 0))
    return pl.pallas_call(
        functools.partial(_ret_kernel, c=c, spb=spb, cps=cps),
        grid=(B // spb, nch),
        in_specs=[pl.BlockSpec((R, W), lambda b, i: (b * nch + i, 0)),
                  pl.BlockSpec((R, W), lambda b, i: (b * nch + i, 0)),
                  pl.BlockSpec((R, BRANCH_W), lambda b, i: (b * nch + i, v_col)),
                  pl.BlockSpec((R, BRANCH_W), lambda b, i: (b * nch + i, g_col)),
                  st_in, pl.BlockSpec((1, BRANCH_W), lambda b, i: (0, 0))],
        out_specs=[pl.BlockSpec((R, BRANCH_W), lambda b, i: (b * nch + i, 0)), st],
        out_shape=[jax.ShapeDtypeStruct((t, BRANCH_W), BF16), jax.ShapeDtypeStruct(st_shape, F32)],
        scratch_shapes=[pltpu.VMEM((spb, N_HEADS_D, QK_DIM_D, V_DIM_D), F32)],
        compiler_params=_cparams("parallel", "arbitrary"),
    )(rq, rk, rv, dg, s0, gn)


def _sortable(x):
    b = lax.bitcast_convert_type(x, I32)
    return b ^ ((b >> 31) & 0x7FFFFFFF)


def _count(m):
    return jnp.sum(jnp.where(m, 1.0, 0.0), axis=1, keepdims=True)


def _kth_largest_key(keys, kf, bits):
    m = keys.shape[0]
    t0 = jnp.where(_count(keys >= 0) >= kf, 0, INT_MIN).astype(I32)

    def step(nb, shift, t):
        digit = jnp.zeros((m, 1), I32)
        for v in range(1, 2 ** nb):
            cand = t | jnp.left_shift(jnp.int32(v), shift)
            digit = digit + jnp.where(_count(keys >= cand) >= kf, 1, 0)
        return t | jnp.left_shift(digit, shift)

    nfull, rem = divmod(31, bits)
    t = lax.fori_loop(0, nfull, lambda i, t: step(bits, 31 - bits * (i + 1), t), t0)
    return step(rem, 0, t) if rem else t


def _topk_mask(keys, valid, idx, k, bits):
    m, n = keys.shape
    kf = float(k)
    thr = _kth_largest_key(keys, kf, bits)
    gt = keys > thr
    eq = (keys == thr) & valid
    need = kf - _count(gt)
    excess = jnp.max(_count(eq) - need) > 0.0
    nbits = max(1, (n - 1).bit_length())

    def first_ties():
        def idx_step(i, j):
            cand = j | jnp.left_shift(jnp.int32(1), nbits - 1 - i)
            return jnp.where(_count(eq & (idx < cand)) <= need - 1.0, cand, j)

        return lax.fori_loop(0, nbits, idx_step, jnp.zeros((m, 1), I32))

    jthr = lax.cond(excess, first_ties, lambda: jnp.full((m, 1), 2 ** nbits, I32))
    return (gt & valid) | (eq & (idx <= jthr))


Q_DIGITS = (7, 8, 8)
Q_MAX = float(2 ** sum(Q_DIGITS) - 1)


def _count_ge(b, cand):
    m, n = b.shape
    ct = jnp.broadcast_to(cand, (m, LANES)).astype(b.dtype)
    one, zero = jnp.ones((), b.dtype), jnp.zeros((), b.dtype)
    acc = jnp.where(b[:, 0:LANES] >= ct, one, zero)
    for c in range(1, n // LANES):
        acc = acc + jnp.where(b[:, c * LANES:(c + 1) * LANES] >= ct, one, zero)
    return jnp.sum(acc.astype(F32), axis=1, keepdims=True)


def _digit_search(b, kf, nbits):
    def step(i, t):
        cand = t | jnp.left_shift(jnp.int32(1), nbits - 1 - i)
        return jnp.where(_count_ge(b, cand.astype(F32)) >= kf, cand, t)

    return lax.fori_loop(0, nbits, step, jnp.zeros((b.shape[0], 1), I32)).astype(F32)


def _topk_mask_fast(score, valid, idx, k, digit_dtype):
    kf = float(k)
    lo = jnp.min(jnp.where(valid, score, jnp.inf), axis=1, keepdims=True)
    hi = jnp.max(jnp.where(valid, score, -jnp.inf), axis=1, keepdims=True)
    scale = jnp.where(hi > lo, Q_MAX / (hi - lo), 0.0)
    q = jnp.where(valid, jnp.minimum(jnp.floor((score - lo) * scale), Q_MAX), -1.0)
    w1, w0 = float(2 ** Q_DIGITS[2]), float(2 ** (Q_DIGITS[1] + Q_DIGITS[2]))
    d2 = jnp.floor(q * (1.0 / w0))
    r = q - d2 * w0
    d1 = jnp.floor(r * (1.0 / w1))
    d0 = r - d1 * w1
    b2 = d2.astype(digit_dtype)
    t2 = _digit_search(b2, kf, Q_DIGITS[0])
    k1 = kf - _count_ge(b2, t2 + 1.0)
    in2 = d2 == t2
    b1 = jnp.where(in2, d1, -1.0).astype(digit_dtype)
    t1 = _digit_search(b1, k1, Q_DIGITS[1])
    k0 = k1 - _count_ge(b1, t1 + 1.0)
    b0 = jnp.where(in2 & (d1 == t1), d0, -1.0).astype(digit_dtype)
    t0 = _digit_search(b0, k0, Q_DIGITS[2])
    thr = t2 * w0 + t1 * w1 + t0
    ambiguous = jnp.max(_count_ge(b0, t0) - k0) > 0.0
    fast = jnp.where(q >= thr, 1, 0).astype(I32)

    def exact():
        keys = _sortable(jnp.where(valid, score, -jnp.inf))
        return jnp.where(_topk_mask(keys, valid, idx, k, 1), 1, 0).astype(I32)

    return lax.cond(ambiguous, exact, lambda: fast) > 0


def _masked_attention(s, mask, v):
    s = jnp.where(mask, s, -jnp.inf)
    p = jnp.exp2(s - jnp.max(s, axis=1, keepdims=True))
    return _dot(p.astype(BF16), v) / jnp.sum(p, axis=1, keepdims=True)


def _dsa_prompt_kernel(q_ref, kb_ref, vb_ref, iq4_ref, ik4_ref, iw_ref, *rest, j0, topk):
    o_ref = rest[-1]
    nq = q_ref.shape[0]
    nk = kb_ref.shape[0]
    t0 = (j0 + pl.program_id(1)) * nq
    ik4 = ik4_ref[...]
    score = jnp.zeros((nq, nk), F32)
    for h in range(N_IDX_HEADS):
        s = _dot_nt(iq4_ref[:, h * IDX4:(h + 1) * IDX4], ik4)
        score = score + jnp.maximum(s, 0.0) * iw_ref[:, IDX_DIM + h:IDX_DIM + h + 1]
    score = score * IDX_SCALE
    kpos = lax.broadcasted_iota(I32, (nq, nk), 1)
    valid = kpos <= t0 + lax.broadcasted_iota(I32, (nq, nk), 0)
    if nk > topk and j0 * nq + 1 >= topk:
        mask = _topk_mask_fast(score, valid, kpos, topk, BF16)
    elif nk > topk:
        mask = _topk_mask(_sortable(jnp.where(valid, score, -jnp.inf)), valid, kpos, topk, 1)
    else:
        mask = valid
    for g in range(N_KV_B):
        gs = slice(g * HEAD_DIM_B, (g + 1) * HEAD_DIM_B)
        kg = kb_ref[:, gs]
        vg = vb_ref[:, gs]
        for r in range(KV_REP):
            hs = slice((g * KV_REP + r) * HEAD_DIM_B, (g * KV_REP + r + 1) * HEAD_DIM_B)
            o_ref[:, hs] = _masked_attention(_dot_nt(q_ref[:, hs], kg), mask, vg).astype(o_ref.dtype)


def _dsa_prompt(q, kb, vb, iq4, ik4, ikw, *, n_seq, n_cls, qb):
    t = q.shape[0]
    s_len = t // n_seq
    nb = s_len // qb
    nbc = nb // n_cls
    topk = min(TOPK_MAX, s_len // 4)
    q, kb, vb, iq4, ik4, ikw = (a.reshape(n_seq, s_len, a.shape[-1]) for a in (q, kb, vb, iq4, ik4, ikw))
    out = None
    for c in range(n_cls):
        nk = (c + 1) * nbc * qb
        qrow = lambda w, c=c: pl.BlockSpec((None, qb, w), lambda b, j: (b, c * nbc + j, 0))
        seq = lambda w, nk=nk: pl.BlockSpec((None, nk, w), lambda b, j: (b, 0, 0))
        prev = [] if out is None else [out]
        out = pl.pallas_call(
            functools.partial(_dsa_prompt_kernel, j0=c * nbc, topk=topk),
            grid=(n_seq, nbc),
            in_specs=[qrow(BRANCH_W), seq(N_KV_B * HEAD_DIM_B), seq(N_KV_B * HEAD_DIM_B), qrow(N_IDX_HEADS * IDX4),
                      seq(IDX4), qrow(LANES)] + [pl.BlockSpec(memory_space=pl.ANY)] * len(prev),
            out_specs=qrow(BRANCH_W),
            out_shape=jax.ShapeDtypeStruct((n_seq, s_len, BRANCH_W), BF16),
            input_output_aliases={6: 0} if prev else {},
            compiler_params=_cparams("parallel", "arbitrary"),
        )(q, kb, vb, iq4, ik4, ikw, *prev)
    return out.reshape(t, BRANCH_W)


def _dsa_sample_select_kernel(pt_ref, iq4_ref, w_ref, ikn_ref, *rest, n_pages, t_new, spb):
    del pt_ref
    idx_refs, (m_ref, sc_s) = rest[:n_pages * spb], rest[n_pages * spb:]
    P = PAGE_SIZE
    pad = P - t_new
    nk = (n_pages + 1) * P
    past = n_pages * P
    for s in range(spb):
        iq4 = iq4_ref[s]
        w = w_ref[s]

        def page_scores(ik, page, keys_on_lanes):
            hi, lo = _split_bf16(ik)
            if keys_on_lanes:
                sc = _dot(iq4, jnp.concatenate([hi, lo, hi, lo], axis=0))
            else:
                sc = _dot_nt(iq4, jnp.concatenate([hi, lo, hi, lo], axis=1))
            r = jnp.maximum(sc, 0.0) * w
            acc = r[0:t_new]
            for h in range(1, N_IDX_HEADS):
                acc = acc + r[h * t_new:(h + 1) * t_new]
            sc_s[s * t_new:(s + 1) * t_new, page * P:(page + 1) * P] = acc * IDX_SCALE

        for p in range(n_pages):
            page_scores(idx_refs[s * n_pages + p][0, 0], p, True)
        page_scores(jnp.concatenate([ikn_ref[s], jnp.zeros((pad, IDX_DIM), F32)], axis=0), n_pages, False)

    rows = spb * t_new
    kpos = lax.broadcasted_iota(I32, (rows, nk), 1)
    valid = kpos <= past + (lax.broadcasted_iota(I32, (rows, nk), 0) & (t_new - 1))
    mask = _topk_mask_fast(sc_s[...], valid, kpos, min(TOPK_MAX, (past + t_new) // 4), BF16)
    m_ref[...] = jnp.where(mask, 1, 0).astype(I32)


def _dsa_sample_attend_kernel(pt_ref, qg_ref, kn_ref, vn_ref, m_ref, *rest, n_pages, t_new, spb):
    del pt_ref
    n_in = 2 * n_pages * spb
    page_refs, (o_ref, k_s, v_s) = rest[:n_in], rest[n_in:]
    P = PAGE_SIZE
    pad = P - t_new
    nk = (n_pages + 1) * P
    past = n_pages * P
    zkv = jnp.zeros((pad, N_KV_B * HEAD_DIM_B), F32)
    for s in range(spb):
        k_refs = page_refs[2 * n_pages * s:2 * n_pages * s + n_pages]
        v_refs = page_refs[2 * n_pages * s + n_pages:2 * n_pages * (s + 1)]
        for p in range(n_pages):
            for g in range(N_KV_B):
                gs = slice(g * HEAD_DIM_B, (g + 1) * HEAD_DIM_B)
                k_s[s, p * P:(p + 1) * P, gs] = k_refs[p][0, 0, pl.ds(g, P, stride=N_KV_B), :].astype(BF16)
                v_s[s, p * P:(p + 1) * P, gs] = v_refs[p][0, 0, pl.ds(g, P, stride=N_KV_B), :].astype(BF16)
        k_s[s, past:nk, :] = jnp.concatenate([kn_ref[s], zkv], axis=0).astype(BF16)
        v_s[s, past:nk, :] = jnp.concatenate([vn_ref[s], zkv], axis=0).astype(BF16)
        m_s = jnp.concatenate([m_ref[s * t_new:(s + 1) * t_new, :]] * KV_REP, axis=0) > 0
        for g in range(N_KV_B):
            gs = slice(g * HEAD_DIM_B, (g + 1) * HEAD_DIM_B)
            o_ref[s, g] = _masked_attention(_dot_nt(qg_ref[s, g], k_s[s, :, gs]), m_s, v_s[s, :, gs]).astype(o_ref.dtype)


def _dsa_sample(qg, iq4r, wr, k_new, v_new, ik_new, cache_k, cache_v, cache_idx_k, page_table, layer, spb, spb_sel):
    B, _, rows, hd = qg.shape
    t_new = k_new.shape[1]
    assert t_new & (t_new - 1) == 0 and B % spb == 0 and B % spb_sel == 0
    n_pages = page_table.shape[1]
    kvw = N_KV_B * HEAD_DIM_B
    nk = (n_pages + 1) * PAGE_SIZE

    def per_seq(n, shape):
        return pl.BlockSpec((n,) + shape, lambda b, pt: (b,) + (0,) * len(shape))

    def page_spec(n, rows_, width, s, p):
        return pl.BlockSpec((1, 1, rows_, width), lambda b, pt: (layer, pt[b * n + s, p], 0, 0))

    sel_specs = [per_seq(spb_sel, iq4r.shape[1:]), per_seq(spb_sel, wr.shape[1:]), per_seq(spb_sel, (t_new, IDX_DIM))]
    sel_specs += [page_spec(spb_sel, IDX_DIM, PAGE_SIZE, s, p) for s in range(spb_sel) for p in range(n_pages)]
    mask = pl.pallas_call(
        functools.partial(_dsa_sample_select_kernel, n_pages=n_pages, t_new=t_new, spb=spb_sel),
        grid_spec=pltpu.PrefetchScalarGridSpec(
            num_scalar_prefetch=1, grid=(B // spb_sel,), in_specs=sel_specs,
            out_specs=pl.BlockSpec((spb_sel * t_new, nk), lambda b, pt: (b, 0)),
            scratch_shapes=[pltpu.VMEM((spb_sel * t_new, nk), F32)]),
        out_shape=jax.ShapeDtypeStruct((B * t_new, nk), I32),
        compiler_params=_cparams("arbitrary"),
    )(page_table, iq4r, wr, ik_new, *([cache_idx_k] * (n_pages * spb_sel)))

    att_specs = [per_seq(spb, qg.shape[1:]), per_seq(spb, (t_new, kvw)), per_seq(spb, (t_new, kvw)),
                 pl.BlockSpec((spb * t_new, nk), lambda b, pt: (b, 0))]
    pages = []
    for s in range(spb):
        att_specs += [page_spec(spb, PAGE_SIZE * N_KV_B, HEAD_DIM_B, s, p) for p in range(n_pages)] * 2
        pages += [cache_k] * n_pages + [cache_v] * n_pages
    return pl.pallas_call(
        functools.partial(_dsa_sample_attend_kernel, n_pages=n_pages, t_new=t_new, spb=spb),
        grid_spec=pltpu.PrefetchScalarGridSpec(
            num_scalar_prefetch=1, grid=(B // spb,), in_specs=att_specs,
            out_specs=pl.BlockSpec((spb,) + qg.shape[1:], lambda b, pt: (b, 0, 0, 0)),
            scratch_shapes=[pltpu.VMEM((spb, nk, kvw), BF16), pltpu.VMEM((spb, nk, kvw), BF16)]),
        out_shape=jax.ShapeDtypeStruct(qg.shape, BF16),
        compiler_params=_cparams("arbitrary"),
    )(page_table, qg, k_new, v_new, mask, *pages)


def _merge_kernel(xn_ref, a_ref, b_ref, c_ref, d_ref, wg0, wg1, wg2, wg3, bg0, bg1, bg2, bg3, wb_ref, o_ref):
    xn = xn_ref[...]
    acc = None
    for n, (br, wg, bg) in enumerate(zip((a_ref, b_ref, c_ref, d_ref), (wg0, wg1, wg2, wg3), (bg0, bg1, bg2, bg3))):
        term = jax.nn.sigmoid(_dot(xn, wg[...]) + bg[...]) * _dot(br[...], wb_ref[n])
        acc = term if acc is None else acc + term
    o_ref[...] = acc.astype(o_ref.dtype)


def _merge(xn, branches, w_gate, b_gate, w_branch, l, tm, tn):
    t, d = xn.shape
    nj = d // tn
    row = lambda w: pl.BlockSpec((tm, w), lambda i, j: (i, 0))
    wg = [pl.BlockSpec((None, d, tn), functools.partial(lambda i, j, n: (l, 0, n * nj + j), n=n))
          for n in range(N_BRANCH)]
    bg = [pl.BlockSpec((1, tn), functools.partial(lambda i, j, n: (0, n * nj + j), n=n)) for n in range(N_BRANCH)]
    return pl.pallas_call(
        _merge_kernel,
        grid=(t // tm, nj),
        in_specs=[row(d)] + [row(BRANCH_W)] * N_BRANCH + wg + bg
                 + [pl.BlockSpec((None, N_BRANCH, BRANCH_W, tn), lambda i, j: (l, 0, 0, j))],
        out_specs=pl.BlockSpec((tm, tn), lambda i, j: (i, j)),
        out_shape=jax.ShapeDtypeStruct((t, d), BF16),
        compiler_params=_cparams("parallel", "arbitrary"),
    )(xn, *branches, *([w_gate] * N_BRANCH), *([b_gate] * N_BRANCH), w_branch)


def _outproj_kernel(m_ref, x_ref, w_ref, g_ref, o_ref):
    o_ref[...] = x_ref[...] + _rms(_dot(m_ref[...], w_ref[...]), g_ref[...])


def _outproj(merged, x, w_o, g, l, tm):
    t, d = x.shape
    return pl.pallas_call(
        _outproj_kernel,
        grid=(t // tm,),
        in_specs=[pl.BlockSpec((tm, d), lambda i: (i, 0)), pl.BlockSpec((tm, d), lambda i: (i, 0)),
                  pl.BlockSpec((None, d, d), lambda i: (l, 0, 0)), pl.BlockSpec((1, d), lambda i: (0, 0))],
        out_specs=pl.BlockSpec((tm, d), lambda i: (i, 0)),
        out_shape=jax.ShapeDtypeStruct((t, d), F32),
        compiler_params=_cparams("parallel"),
    )(merged, x, w_o, g)


def _ffn_kernel(x_ref, gpre_ref, wug_ref, wuv_ref, wcg_ref, wcv_ref, bcg_ref, bcv_ref, wd_ref, gpost_ref,
                stg_ref, stv_ref, o_ref, ng_ref, nv_ref, h_s, acc_s, cg_s, cv_s, ext_s, *, S, tiles_per_group):
    i = pl.program_id(0)
    j = pl.program_id(1)
    tm = x_ref.shape[0]
    CR = cg_s.shape[1]

    @pl.when(j == 0)
    def _():
        h_s[...] = _rms(x_ref[...], gpre_ref[...]).astype(BF16)
        acc_s[...] = jnp.zeros_like(acc_s)

    @pl.when(i % tiles_per_group == 0)
    def _():
        cg_s[j] = stg_ref[0]
        cv_s[j] = stv_ref[0]

    h = h_s[...]

    def conv(w_ref, wc_ref, bc_ref, c_s, new_ref):
        u = _dot(h, w_ref[...])
        ext_s[0:CR, :] = c_s[j]
        ext_s[CR:CR + tm, :] = u
        y = bc_ref[...] + u * wc_ref[2:3, :]
        y = y + ext_s[CR - S:CR - S + tm, :] * wc_ref[1:2, :]
        y = y + ext_s[CR - 2 * S:CR - 2 * S + tm, :] * wc_ref[0:1, :]
        tail = ext_s[tm:tm + CR, :]
        c_s[j] = tail
        new_ref[0] = tail
        return y

    gate = conv(wug_ref, wcg_ref, bcg_ref, cg_s, ng_ref)
    val = conv(wuv_ref, wcv_ref, bcv_ref, cv_s, nv_ref)
    acc_s[...] += _dot((_gelu_tanh(gate) * val).astype(BF16), wd_ref[...])

    @pl.when(j == pl.num_programs(1) - 1)
    def _():
        o_ref[...] = x_ref[...] + _rms(acc_s[...], gpost_ref[...])


def _ffn(x, lp, state, *, S, tm, tn):
    t, d = x.shape
    G, CR, _ = state.shape
    nj = D_FF // tn
    ni = t // tm
    tiles_per_group = ni // G
    grp = lambda i, j: (i // tiles_per_group, 0, j)
    tile = lambda i, j: (i, 0, j)
    grp_v = lambda i, j: (i // tiles_per_group, 0, nj + j)
    col = lambda r: pl.BlockSpec((r, tn), lambda i, j: (0, j))
    col_v = lambda r: pl.BlockSpec((r, tn), lambda i, j: (0, nj + j))
    vec = pl.BlockSpec((1, d), lambda i, j: (0, 0))
    l = lp["layer"]
    xo, ng, nv = pl.pallas_call(
        functools.partial(_ffn_kernel, S=S, tiles_per_group=tiles_per_group),
        grid=(ni, nj),
        in_specs=[pl.BlockSpec((tm, d), lambda i, j: (i, 0)), vec,
                  pl.BlockSpec((None, d, tn), lambda i, j: (l, 0, j)),
                  pl.BlockSpec((None, d, tn), lambda i, j: (l, 0, nj + j)),
                  col(FFN_CONV_W), col_v(FFN_CONV_W), col(1), col_v(1),
                  pl.BlockSpec((None, tn, d), lambda i, j: (l, j, 0)), vec,
                  pl.BlockSpec((1, CR, tn), grp), pl.BlockSpec((1, CR, tn), grp_v)],
        out_specs=[pl.BlockSpec((tm, d), lambda i, j: (i, 0)),
                   pl.BlockSpec((1, CR, tn), tile), pl.BlockSpec((1, CR, tn), tile)],
        out_shape=[jax.ShapeDtypeStruct((t, d), F32), jax.ShapeDtypeStruct((ni, CR, D_FF), F32),
                   jax.ShapeDtypeStruct((ni, CR, D_FF), F32)],
        scratch_shapes=[pltpu.VMEM((tm, d), BF16), pltpu.VMEM((tm, d), F32), pltpu.VMEM((nj, CR, tn), F32),
                        pltpu.VMEM((nj, CR, tn), F32), pltpu.VMEM((CR + tm, tn), F32)],
        compiler_params=_cparams("arbitrary", "arbitrary"),
    )(x, lp["g_ffn_pre"], lp["w_up"], lp["w_up"], lp["w_ffn_conv"], lp["w_ffn_conv"], lp["b_ffn_conv"],
      lp["b_ffn_conv"], lp["w_down"], lp["g_ffn_post"], state, state)
    last = slice(tiles_per_group - 1, None, tiles_per_group)
    return xo, ng[last], nv[last]


def _layer_front(x, lp, tabs, kv_final=None):
    l = lp["layer"]
    u_main, xn = _inproj(x, lp["g_mix_pre"], lp["w_in_main"], l, TM_INPROJ, TN_INPROJ)
    u_idx = _inproj_split(x, lp["g_mix_pre"], lp["w_in_idx3"], l, TM_IDX, N_IDX)
    return u_main, xn, _rope_prep(u_main, u_idx, tabs, ROPE_ROWS, kv_final)


def _layer_back(x, xn, branches, lp, ffn_state, *, S):
    l = lp["layer"]
    merged = _merge(xn, branches, lp["w_gate"], lp["b_gate"], lp["w_branch"], l, TM_DENSE, TN_MERGE)
    x1 = _outproj(merged, x, lp["w_o"], lp["g_mix_post"], l, TM_DENSE)
    return _ffn(x1, lp, ffn_state, S=S, tm=TM_DENSE, tn=TN_FFN[S])


def _prep_big_weights(p):
    w_in = jnp.swapaxes(p["w_in"], 1, 2)
    o_iq = 3072
    o_cx = o_iq + N_IDX_HEADS * IDX_DIM + IDX_DIM + N_IDX_HEADS
    w_idx = jnp.pad(w_in[:, o_iq:o_cx], ((0, 0), (0, N_IDX - (o_cx - o_iq)), (0, 0)))
    hi = w_idx.astype(BF16)
    lo = (w_idx - hi.astype(F32)).astype(BF16)
    return dict(
        w_in_main=jnp.concatenate([w_in[:, :o_iq], w_in[:, o_cx:]], axis=1).astype(BF16),
        w_in_idx3=jnp.concatenate([hi, lo, hi], axis=2),
        w_branch=p["w_branch"].astype(BF16), w_gate=p["w_gate"].astype(BF16), w_o=p["w_o"].astype(BF16),
        w_up=p["w_up"].astype(BF16), w_down=p["w_down"].astype(BF16))


def _prep_layer_params(p, big, l):
    row = lambda a: a[l][None, :]
    return dict(
        big, layer=l,
        g_mix_pre=row(p["g_mix_pre"]),
        w_pool=p["w_pool"][l].astype(BF16), pool_scale=row(p["pool_scale"]),
        w_conv_c=p["w_conv_c"][l], b_conv_c=row(p["b_conv_c"]),
        w_rg_a=p["w_rg_a"][l].astype(BF16), b_rg_a=row(p["b_rg_a"]),
        w_rg_x=p["w_rg_x"][l].astype(BF16), b_rg_x=row(p["b_rg_x"]),
        lru_lambda=row(p["lru_lambda"]), ret_gn=row(p["ret_gn"]), b_gate=row(p["b_gate"]),
        g_mix_post=row(p["g_mix_post"]), g_ffn_pre=row(p["g_ffn_pre"]),
        w_ffn_conv=p["w_ffn_conv"][l], b_ffn_conv=row(p["b_ffn_conv"]), g_ffn_post=row(p["g_ffn_post"]))


def _to_time_major(a, nb, nt):
    return jnp.swapaxes(a, 0, 1).reshape((nt * nb,) + a.shape[2:])


def _to_seq_major(a, nb, nt):
    return jnp.swapaxes(a.reshape((nt, nb) + a.shape[1:]), 0, 1)


def kernel(x_prompt, x_sample, cache_k, cache_v, cache_idx_k, state_pool, state_conv, state_rglru, state_ret, state_ffn_conv, page_table, g_mix_pre, w_in, w_pool, pool_scale, w_conv_c, b_conv_c, w_rg_a, b_rg_a, w_rg_x, b_rg_x, lru_lambda, ret_gn, w_branch, w_gate, b_gate, w_o, g_mix_post, g_ffn_pre, w_up, w_ffn_conv, b_ffn_conv, w_down, g_ffn_post):
    params = dict(g_mix_pre=g_mix_pre, w_in=w_in, w_pool=w_pool, pool_scale=pool_scale, w_conv_c=w_conv_c,
                  b_conv_c=b_conv_c, w_rg_a=w_rg_a, b_rg_a=b_rg_a, w_rg_x=w_rg_x, b_rg_x=b_rg_x,
                  lru_lambda=lru_lambda, ret_gn=ret_gn, w_branch=w_branch, w_gate=w_gate, b_gate=b_gate, w_o=w_o,
                  g_mix_post=g_mix_post, g_ffn_pre=g_ffn_pre, w_up=w_up, w_ffn_conv=w_ffn_conv,
                  b_ffn_conv=b_ffn_conv, w_down=w_down, g_ffn_post=g_ffn_post)
    depth = w_in.shape[0]
    bp, seq, d = x_prompt.shape
    bs, tdec, _ = x_sample.shape
    past = page_table.shape[1] * PAGE_SIZE
    n_pool = cache_k.shape[1]
    ck = cache_k.reshape(depth, n_pool, PAGE_SIZE * N_KV_B, HEAD_DIM_B)
    cv = cache_v.reshape(depth, n_pool, PAGE_SIZE * N_KV_B, HEAD_DIM_B)
    cik_t = jnp.swapaxes(cache_idx_k, 2, 3)
    big = _prep_big_weights(params)

    tabs_p = _rope_tables(jnp.arange(seq, dtype=F32))
    tabs_s = _rope_tables(jnp.repeat(past + jnp.arange(tdec, dtype=F32), bs))
    xp = x_prompt.reshape(bp * seq, d)
    xs = _to_time_major(x_sample, bs, tdec)
    cr_p = max(SUBLANES, (FFN_CONV_W - 1))
    outs_p, outs_s = [], []
    kv_p = None
    for l in range(depth):
        lp = _prep_layer_params(params, big, l)
        u, xn, (q, _, kb, vb, iq4, ik4, ikw, rq, rk, *kv_p) = _layer_front(xp, lp, tabs_p, (l, depth, kv_p))
        o_a = _pool(u, jnp.zeros((bp, POOL_BUF, BRANCH_W), F32), lp["w_pool"], lp["pool_scale"], S=1, Tc=SEQ_CHUNK,
                    start=0)
        o_b = _dsa_prompt(q, kb, vb, iq4, ik4, ikw, n_seq=bp, n_cls=DSA_CLASSES, qb=DSA_QBLOCK)
        o_c, h_p = _rglru(u, jnp.zeros((bp, CONV_W - 1, LRU_W), F32), jnp.zeros((bp, 1, LRU_W), F32), lp, S=1,
                          Tc=SEQ_CHUNK)
        o_d, s_p = _retention(rq, rk, u, u, jnp.zeros((bp, N_HEADS_D, QK_DIM_D, V_DIM_D), F32), lp["ret_gn"],
                              c=RET_CHUNK if seq % RET_CHUNK == 0 else seq, v_col=6, g_col=7, spb=1,
                              cps=2 if seq % (2 * RET_CHUNK) == 0 else 1)
        xp, ng, nv = _layer_back(xp, xn, (o_a, o_b, o_c, o_d), lp, jnp.zeros((bp, cr_p, 2 * D_FF), F32), S=1)
        u3 = u.reshape(bp, seq, N_MAIN)
        outs_p.append((
            None, None,
            ikw.reshape(bp, seq, LANES)[:, :, :IDX_DIM], u3[:, seq - POOL_BUF:, 0:BRANCH_W],
            u3[:, seq - (CONV_W - 1):, 3072:3072 + LRU_W], h_p[:, 0], s_p,
            jnp.concatenate([ng, nv], axis=-1)[:, cr_p - (FFN_CONV_W - 1):]))
        u, xn, (q, k, kb, vb, iq4, ik4, ikw, rq, rk) = _layer_front(xs, lp, tabs_s)
        o_a = _pool(u, _to_time_major(state_pool[l], bs, POOL_BUF)[None], lp["w_pool"], lp["pool_scale"],
                    S=bs, Tc=tdec, start=past)
        o_c, h_s = _rglru(u, _to_time_major(state_conv[l], bs, CONV_W - 1)[None], state_rglru[l][None], lp,
                          S=bs, Tc=tdec)
        sm = lambda a: _to_seq_major(a, bs, tdec)
        u_sm = sm(u)
        o_d, s_s = _retention(sm(rq).reshape(bs * tdec, -1), sm(rk).reshape(bs * tdec, -1),
                              u_sm[:, :, 6144:7168].reshape(bs * tdec, -1), u_sm[:, :, 7168:8192].reshape(bs * tdec, -1),
                              state_ret, lp["ret_gn"], c=tdec, v_col=0, g_col=0, spb=RET_SAMPLE_SPB, layer=l)
        o_d = _to_time_major(o_d.reshape(bs, tdec, BRANCH_W), bs, tdec)
        qg = sm(q).reshape(bs, tdec, N_KV_B, KV_REP, HEAD_DIM_B).transpose(0, 2, 3, 1, 4)
        qg = qg.reshape(bs, N_KV_B, KV_REP * tdec, HEAD_DIM_B)
        iq4r = sm(iq4).reshape(bs, tdec, N_IDX_HEADS, IDX4).transpose(0, 2, 1, 3).reshape(bs, N_IDX_HEADS * tdec, IDX4)
        ikw_sm = sm(ikw)
        wr = ikw_sm[:, :, IDX_DIM:IDX_DIM + N_IDX_HEADS].transpose(0, 2, 1).reshape(bs, N_IDX_HEADS * tdec, 1)
        k_sm = sm(k)
        v_sm = u_sm[:, :, 2560:3072]
        o_b = _dsa_sample(qg, iq4r, wr, k_sm, v_sm, ikw_sm[:, :, :IDX_DIM], ck, cv, cik_t, page_table, l,
                          DSA_SAMPLE_SPB, DSA_SELECT_SPB)
        o_b = o_b.reshape(bs, N_KV_B, KV_REP, tdec, HEAD_DIM_B).transpose(3, 0, 1, 2, 4).reshape(tdec * bs, BRANCH_W)
        ffn_state = _to_time_major(state_ffn_conv[l], bs, FFN_CONV_W - 1)[None]
        xs, ng, nv = _layer_back(xs, xn, (o_a, o_b, o_c, o_d), lp, ffn_state, S=bs)
        ffn_new = _to_seq_major(jnp.concatenate([ng, nv], axis=-1)[0], bs, FFN_CONV_W - 1)
        a_in = u_sm[:, :, 0:BRANCH_W]
        c_x = u_sm[:, :, 3072:3072 + LRU_W]
        outs_s.append((
            k_sm.reshape(bs, tdec, N_KV_B, HEAD_DIM_B), v_sm.reshape(bs, tdec, N_KV_B, HEAD_DIM_B),
            ikw_sm[:, :, :IDX_DIM],
            jnp.concatenate([state_pool[l], a_in], axis=1)[:, tdec:],
            jnp.concatenate([state_conv[l], c_x], axis=1)[:, tdec:],
            h_s[0], s_s, ffn_new))

    stk = lambda outs, i: jnp.stack([o[i] for o in outs], axis=0)
    res = [xp.reshape(bp, seq, d), _to_seq_major(xs, bs, tdec)]
    for i in range(8):
        p_i = kv_p[i].reshape(depth, bp, seq, N_KV_B, HEAD_DIM_B) if i < 2 else stk(outs_p, i)
        res += [p_i, stk(outs_s, i)]
    return tuple(res)
```
